```python
import math
import jax
import jax.numpy as jnp
from jax import lax
import numpy as np

D_MODEL = 1024
BATCH = 8
SEQ = 4096
DEPTH = 2

F32 = jnp.float32
EPS = 1e-6
TINY = 1e-30
N_BRANCH = 4
D_BRANCH = D_MODEL // 4

A_HEADS = 4
A_DK = D_BRANCH // A_HEADS
A_DV = D_BRANCH // A_HEADS
A_CHUNK = 64
D_B = D_BRANCH
B_BLOCKS = 4
B_BW = D_B // B_BLOCKS
B_CONV = 4
LRU_C = 8.0
D_C = D_BRANCH
C_ORDER = 2
C_CONV = 3
C_EMB = 33
C_HID = 64
C_MIN_DECAY = math.log(1e-2) / 1.5
C_MAX_DECAY = math.log(1e-2) / 0.3
D_GROUPS = ((128, 1), (512, 4), (2048, 16))
D_HEADS_PER_GROUP = 4
D_HEAD_DIM = D_BRANCH // D_HEADS_PER_GROUP
D_N_HEADS = len(D_GROUPS) * D_HEADS_PER_GROUP
D_QKV = D_N_HEADS * D_HEAD_DIM
N_BUCKETS = 32
MAX_DISTANCE = 1024
NEG_BIG = -1e30
D_FF = -(-8 * D_MODEL // (3 * 256)) * 256
IN_A = 5 * D_BRANCH
IN_B = 2 * D_B
IN_C = 3 * D_C
IN_D = 3 * D_QKV
IN_WIDTH = IN_A + IN_B + IN_C + IN_D
MIXER_OFFSETS = (IN_A, IN_A + IN_B, IN_A + IN_B + IN_C)

kernel_name = "hybrid_gated_hgrn2_rglru_hyena_dilated_encoder"


def rmsnorm(x, g):
    xf = x.astype(F32)
    y = xf * lax.rsqrt(jnp.mean(xf * xf, axis=-1, keepdims=True) + EPS)
    return (y * g.astype(F32)).astype(x.dtype)


def dwconv(x, w, b, left):
    K, C = w.shape
    y = lax.conv_general_dilated(x, w[:, None, :].astype(x.dtype), window_strides=(1,),
                                 padding=[(left, K - 1 - left)],
                                 dimension_numbers=("NWC", "WIO", "NWC"),
                                 feature_group_count=C)
    return y + b.astype(x.dtype)


def hgrn2_bidir(q, f_logit_fwd, f_logit_bwd, v, lb):
    B, S, H, DK = q.shape
    DV = v.shape[-1]
    C = A_CHUNK
    nc = S // C
    lb = lb.astype(F32)

    def forget(fl):
        fl = fl.astype(F32)
        f = lb + (1.0 - lb) * jax.nn.sigmoid(fl)
        log_f = jnp.log(jnp.maximum(f, TINY))
        return log_f, (1.0 - lb) * jax.nn.sigmoid(-fl)

    lf_fwd, k_fwd = forget(f_logit_fwd)
    lf_bwd, k_bwd = forget(f_logit_bwd)
    rev = lambda t: t[:, ::-1]
    qf, vf = q.astype(F32), v.astype(F32)

    def to_chunks(a_fwd, a_bwd):
        t = jnp.stack([a_fwd, rev(a_bwd)])
        return t.reshape(2, B, nc, C, H, t.shape[-1]).transpose(2, 0, 1, 4, 3, 5)

    qs, ks = to_chunks(qf, qf), to_chunks(k_fwd, k_bwd)
    gs, vs = to_chunks(lf_fwd, lf_bwd), to_chunks(vf, vf)
    tril = jnp.tril(jnp.ones((C, C), bool))[:, :, None]

    def step(state, inp):
        qc, kc, gc, vc = inp
        b = jnp.cumsum(gc, axis=-2)
        diff = b[..., :, None, :] - b[..., None, :, :]
        decay = jnp.where(tril, jnp.exp(jnp.where(tril, diff, 0.0)), 0.0)
        scores = jnp.einsum('zbhtk,zbhtsk,zbhsk->zbhts', qc, decay, kc)
        o = (jnp.einsum('zbhts,zbhsv->zbhtv', scores, vc)
             + jnp.einsum('zbhtk,zbhkv->zbhtv', qc * jnp.exp(b), state))
        b_last = b[..., -1:, :]
        state = (state * jnp.exp(b_last)[..., 0, :, None]
                 + jnp.einsum('zbhsk,zbhsv->zbhkv', kc * jnp.exp(b_last - b), vc))
        return state, o

    s0 = jnp.zeros((2, B, H, DK, DV), F32)
    _, o = lax.scan(step, s0, (qs, ks, gs, vs))
    o = o.transpose(1, 2, 0, 4, 3, 5).reshape(2, B, S, H, DV)
    return o[0] + rev(o[1])


def lin_scan(a, u, reverse):
    def comb(e1, e2):
        a1, b1 = e1
        a2, b2 = e2
        return a1 * a2, a2 * b1 + b2
    _, h = lax.associative_scan(comb, (a, u), axis=1, reverse=reverse)
    return h


def rglru_bidir(x, wa, ba, wx, bx, lam):
    B, S, Dr = x.shape
    xf = x.astype(F32)
    xb = xf.reshape(B, S, B_BLOCKS, B_BW)

    def blockdiag(w, b):
        y = jnp.einsum('bsnc,zncd->zbsnd', xb, w.astype(F32)).reshape(2, B, S, Dr)
        return y + b.astype(F32)[:, None, None]

    r = jax.nn.sigmoid(blockdiag(wa, ba))
    ig = jax.nn.sigmoid(blockdiag(wx, bx))
    log_a = -LRU_C * r * jax.nn.softplus(-lam.astype(F32))[:, None, None]
    a = jnp.exp(log_a)
    u = jnp.sqrt(jnp.maximum(-jnp.expm1(2.0 * log_a), 0.0)) * ig * xf[None]
    return lin_scan(a[0], u[0], False) + lin_scan(a[1], u[1], True)


def hyena_filters(L, w1, b1, freq, w2, b2, w3):
    t = jnp.linspace(0.0, 1.0, L, dtype=F32)[:, None]
    bands = (C_EMB - 1) // 2
    w = 2.0 * math.pi * jnp.arange(L, dtype=F32)[:, None] / L
    fr = jnp.linspace(1e-4, bands - 1, bands, dtype=F32)[None]
    z = jnp.concatenate([t, jnp.cos(fr * w), -jnp.sin(fr * w)], axis=-1)
    freq = freq.astype(F32)
    hdn = jnp.sin(freq * (z @ w1.astype(F32) + b1.astype(F32)))
    hdn = jnp.sin(freq * (hdn @ w2.astype(F32) + b2.astype(F32)))
    hf = (hdn @ w3.astype(F32)).reshape(L, C_ORDER, 2, D_C)
    deltas = jnp.linspace(C_MIN_DECAY, C_MAX_DECAY, D_C, dtype=F32)
    hf = hf * jnp.exp(-t * jnp.abs(deltas))[:, None, None, :]
    return hf * lax.rsqrt(jnp.sum(hf * hf, axis=0, keepdims=True) + EPS)


def bidir_fftconv(u, h_fwd, h_bwd, bias):
    L = u.shape[1]
    n = 2 * L
    uf = u.astype(F32)
    U = jnp.fft.rfft(uf, n=n, axis=1)
    H = jnp.fft.rfft(h_fwd, n=n, axis=0) + jnp.conj(jnp.fft.rfft(h_bwd, n=n, axis=0))
    y = jnp.fft.irfft(U * H[None], n=n, axis=1)[:, :L]
    return y + uf * bias.astype(F32)


def t5_bucket(rel):
    half = N_BUCKETS // 2
    max_exact = half // 2
    n = np.abs(rel)
    large = max_exact + (np.log(np.maximum(n, 1) / max_exact) / math.log(MAX_DISTANCE / max_exact)
                         * (half - max_exact)).astype(np.int64)
    large = np.minimum(large, half - 1)
    return (rel > 0).astype(np.int64) * half + np.where(n < max_exact, n, large)


def banded_attention(q, k, v, bias_band, half):
    N, H, n, dh = q.shape
    W = half
    nb = -(-n // W)
    n_pad = nb * W
    qp = jnp.pad(q, ((0, 0), (0, 0), (0, n_pad - n), (0, 0))).reshape(N, H, nb, W, dh)

    def kblocks(t):
        tp = jnp.pad(t, ((0, 0), (0, 0), (W, n_pad - n + W), (0, 0))).reshape(N, H, nb + 2, W, dh)
        return jnp.concatenate([tp[:, :, :-2], tp[:, :, 1:-1], tp[:, :, 2:]], axis=3)

    kb, vb = kblocks(k), kblocks(v)
    a_idx = np.arange(W)[:, None]
    c_idx = np.arange(3 * W)[None, :]
    rel = c_idx - W - a_idx
    key_pos = (np.arange(nb)[:, None, None] - 1) * W + c_idx[None]
    valid = (np.abs(rel) <= W)[None] & (key_pos >= 0) & (key_pos < n)
    bias = bias_band[:, np.clip(rel + W, 0, 2 * W)]
    s = jnp.einsum('zhbqd,zhbkd->zhbqk', qp, kb) * (dh ** -0.5) + bias[:, None]
    s = jnp.where(valid[None, None], s, NEG_BIG)
    m = jnp.max(s, axis=-1, keepdims=True)
    p = jnp.exp(s - m)
    l = jnp.sum(p, axis=-1)
    o = jnp.einsum('zhbqk,zhbkd->zhbqd', p, vb) / l[..., None]
    lse = m[..., 0] + jnp.log(l)
    o = o.reshape(N, H, n_pad, dh)[:, :, :n]
    lse = lse.reshape(N, H, n_pad)[:, :, :n]
    return o, lse


def dilated_attention(q, k, v, rel_bias):
    B, S, _, dh = q.shape
    G = D_HEADS_PER_GROUP
    rel_bias = rel_bias.astype(F32)
    outs, lses = [], []
    for g, (win, dil) in enumerate(D_GROUPS):
        hs = slice(g * G, (g + 1) * G)
        n = S // dil

        def gather(t):
            t = t[:, :, hs].astype(F32).reshape(B, n, dil, G, dh)
            return t.transpose(0, 2, 3, 1, 4).reshape(B * dil, G, n, dh)

        half = win // (2 * dil)
        offsets = np.arange(-half, half + 1) * dil
        bias_band = rel_bias[t5_bucket(offsets)][:, hs].T
        o, lse = banded_attention(gather(q), gather(k), gather(v), bias_band, half)
        outs.append(o.reshape(B, dil, G, n, dh).transpose(0, 3, 1, 2, 4).reshape(B, S, G, dh))
        lses.append(lse.reshape(B, dil, G, n).transpose(0, 3, 1, 2).reshape(B, S, G))
    wts = jax.nn.softmax(jnp.stack(lses), axis=0)
    return jnp.sum(wts[..., None] * jnp.stack(outs), axis=0)


def setup_inputs(seed: int = 0) -> dict:
    key = jax.random.key(seed)
    ks = iter(jax.random.split(key, 40))
    L = DEPTH

    def nrm(shape, scale):
        return scale * jax.random.normal(next(ks), shape, F32)

    lam_u = jax.random.uniform(next(ks), (L, 2, D_B), F32, 0.9, 0.999)
    s = lam_u ** (1.0 / LRU_C)
    lru_lambda = jnp.log(s) - jnp.log1p(-s)
    return {
        "x": nrm((BATCH, SEQ, D_MODEL), 1.0),
        "norm1_g": 1.0 + nrm((L, D_MODEL), 0.01),
        "w_in": nrm((L, D_MODEL, IN_WIDTH), D_MODEL ** -0.5),
        "hgrn_lb_logits": nrm((L, D_BRANCH), 0.5),
        "hgrn_norm_g": 1.0 + nrm((L, D_BRANCH), 0.01),
        "lru_conv_w": nrm((L, B_CONV, D_B), B_CONV ** -0.5),
        "lru_conv_b": nrm((L, D_B), 0.02),
        "lru_wa": nrm((L, 2, B_BLOCKS, B_BW, B_BW), B_BW ** -0.5),
        "lru_ba": nrm((L, 2, D_B), 0.1),
        "lru_wx": nrm((L, 2, B_BLOCKS, B_BW, B_BW), B_BW ** -0.5),
        "lru_bx": nrm((L, 2, D_B), 0.1),
        "lru_lambda": lru_lambda,
        "hy_conv_w": nrm((L, C_CONV, 3 * D_C), C_CONV ** -0.5),
        "hy_conv_b": nrm((L, 3 * D_C), 0.02),
        "hy_w1": nrm((L, C_EMB, C_HID), C_EMB ** -0.5),
        "hy_b1": nrm((L, C_HID), 0.1),
        "hy_freq": 1.0 + nrm((L, C_HID), 0.1),
        "hy_w2": nrm((L, C_HID, C_HID), C_HID ** -0.5),
        "hy_b2": nrm((L, C_HID), 0.1),
        "hy_w3": nrm((L, C_HID, C_ORDER * 2 * D_C), C_HID ** -0.5),
        "hy_bias": nrm((L, C_ORDER, D_C), 0.1),
        "rel_bias": nrm((N_BUCKETS, D_N_HEADS), 0.2),
        "w_branch": nrm((L, N_BRANCH, D_BRANCH, D_MODEL), D_BRANCH ** -0.5),
        "w_gate": nrm((L, D_MODEL, N_BRANCH, D_MODEL), D_MODEL ** -0.5),
        "b_gate": nrm((L, N_BRANCH, D_MODEL), 0.1),
        "w_out": nrm((L, D_MODEL, D_MODEL), D_MODEL ** -0.5),
        "norm2_g": 1.0 + nrm((L, D_MODEL), 0.01),
        "w_ff1": nrm((L, D_MODEL, D_FF), D_MODEL ** -0.5),
        "w_ff3": nrm((L, D_MODEL, D_FF), D_MODEL ** -0.5),
        "w_ff2": nrm((L, D_FF, D_MODEL), D_FF ** -0.5),
        "final_g": 1.0 + nrm((D_MODEL,), 0.01),
    }


def reference(x, norm1_g, w_in, hgrn_lb_logits, hgrn_norm_g, lru_conv_w, lru_conv_b, lru_wa, lru_ba,
              lru_wx, lru_bx, lru_lambda, hy_conv_w, hy_conv_b, hy_w1, hy_b1, hy_freq, hy_w2, hy_b2,
              hy_w3, hy_bias, rel_bias, w_branch, w_gate, b_gate, w_out, norm2_g, w_ff1, w_ff3, w_ff2,
              final_g):
    B, S, _ = x.shape
    lb_soft = jax.nn.softmax(hgrn_lb_logits.astype(F32), axis=0)
    lower_bounds = jnp.cumsum(lb_soft, axis=0) - lb_soft[0]
    for l in range(DEPTH):
        h = rmsnorm(x, norm1_g[l])
        proj = h @ w_in[l]
        pA, pB, pC, pD = jnp.split(proj, MIXER_OFFSETS, axis=-1)

        qA, fA_fwd, fA_bwd, iA, gA = jnp.split(pA, 5, axis=-1)
        heads = lambda t: t.reshape(B, S, A_HEADS, -1)
        oA = hgrn2_bidir(heads(qA), heads(fA_fwd), heads(fA_bwd), heads(iA),
                         lower_bounds[l].reshape(A_HEADS, A_DK))
        oA = rmsnorm(oA, hgrn_norm_g[l].reshape(A_HEADS, A_DV)).reshape(B, S, D_BRANCH)
        yA = (oA * jax.nn.silu(gA.astype(F32))).astype(x.dtype)

        xB, gB = jnp.split(pB, 2, axis=-1)
        xB = dwconv(xB, lru_conv_w[l], lru_conv_b[l], B_CONV // 2)
        hB = rglru_bidir(xB, lru_wa[l], lru_ba[l], lru_wx[l], lru_bx[l], lru_lambda[l])
        yB = (hB * jax.nn.gelu(gB.astype(F32))).astype(x.dtype)

        uC = dwconv(pC, hy_conv_w[l], hy_conv_b[l], C_CONV // 2)
        vC, x1, x2 = jnp.split(uC, 3, axis=-1)
        filt = hyena_filters(S, hy_w1[l], hy_b1[l], hy_freq[l], hy_w2[l], hy_b2[l], hy_w3[l])
        z = x1.astype(F32) * bidir_fftconv(vC, filt[:, 0, 0], filt[:, 0, 1], hy_bias[l, 0])
        yC = (x2.astype(F32) * bidir_fftconv(z, filt[:, 1, 0], filt[:, 1, 1], hy_bias[l, 1])).astype(x.dtype)

        qD, kD, vD = [t.reshape(B, S, D_N_HEADS, D_HEAD_DIM) for t in jnp.split(pD, 3, axis=-1)]
        yD = dilated_attention(qD, kD, vD, rel_bias).reshape(B, S, D_BRANCH).astype(x.dtype)

        mixed = jnp.zeros_like(x)
        for j, y in enumerate((yA, yB, yC, yD)):
            gate = jax.nn.sigmoid(h @ w_gate[l, :, j] + b_gate[l, j])
            mixed = mixed + gate * (y @ w_branch[l, j])
        x = x + mixed @ w_out[l]

        h2 = rmsnorm(x, norm2_g[l])
        x = x + (jax.nn.silu(h2 @ w_ff1[l]) * (h2 @ w_ff3[l])) @ w_ff2[l]
    return rmsnorm(x, final_g)
```

```python
import functools
import math

import jax
import jax.numpy as jnp
import numpy as np
from jax import lax
from jax.experimental import pallas as pl
from jax.experimental.pallas import tpu as pltpu

F32 = jnp.float32
BF16 = jnp.bfloat16
HI = lax.Precision.HIGHEST

D_MODEL = 1024
DEPTH = 2
EPS = 1e-6
TINY = 1e-30
N_BRANCH = 4
D_BRANCH = 256
A_HEADS = 4
A_DK = 64
A_CHUNK = 64
B_BLOCKS = 4
B_BW = 64
B_CONV = 4
LRU_C = 8.0
C_ORDER = 2
C_CONV = 3
C_EMB = 33
C_HID = 64
C_MIN_DECAY = math.log(1e-2) / 1.5
C_MAX_DECAY = math.log(1e-2) / 0.3
D_GROUPS = ((128, 1), (512, 4), (2048, 16))
D_HEADS_PER_GROUP = 4
D_HEAD_DIM = 64
D_N_HEADS = 12
D_QKV = 768
N_BUCKETS = 32
MAX_DISTANCE = 1024
NEG_BIG = -1e30
D_FF = 2816
IN_A = 5 * D_BRANCH
IN_B = 2 * D_BRANCH
IN_C = 3 * D_BRANCH
IN_D = 3 * D_QKV
IN_WIDTH = IN_A + IN_B + IN_C + IN_D

LANES = 128
SUBLANES = 8
VMEM_LIMIT = 56 * 1024 * 1024


def _cparams(*sem):
    return pltpu.CompilerParams(dimension_semantics=sem, vmem_limit_bytes=VMEM_LIMIT)


def _const_spec(shape):
    nd = len(shape)
    return pl.BlockSpec(shape, lambda *_: (0,) * nd, pipeline_mode=pl.Buffered(1))


def _rms(x, g):
    return x * lax.rsqrt(jnp.mean(x * x, axis=-1, keepdims=True) + EPS) * g


def _sigmoid(x):
    return 1.0 / (1.0 + jnp.exp(-x))


IN_TM = 512
IN_CHUNK = 256


def _inproj_kernel(x_ref, g_ref, w_ref, oa_ref, ob_ref, oc_ref, od_ref):
    h = _rms(x_ref[...], g_ref[...]).astype(BF16)
    off = 0
    for o_ref in (oa_ref, ob_ref, oc_ref, od_ref):
        width = o_ref.shape[-1]
        for c in range(0, width, IN_CHUNK):
            o_ref[:, c:c + IN_CHUNK] = jnp.dot(h, w_ref[:, off + c:off + c + IN_CHUNK],
                                               preferred_element_type=F32)
        off += width


def _inproj(x2, g, w_bf16):
    n = x2.shape[0]
    widths = (IN_A, IN_B, IN_C, IN_D)
    return pl.pallas_call(
        _inproj_kernel,
        grid=(n // IN_TM,),
        in_specs=[pl.BlockSpec((IN_TM, D_MODEL), lambda i: (i, 0)),
                  _const_spec((1, D_MODEL)),
                  _const_spec((D_MODEL, IN_WIDTH))],
        out_specs=[pl.BlockSpec((IN_TM, w), lambda i: (i, 0)) for w in widths],
        out_shape=[jax.ShapeDtypeStruct((n, w), F32) for w in widths],
        compiler_params=_cparams("parallel"),
        name="inproj",
    )(x2, g.reshape(1, D_MODEL), w_bf16)


FF_TM = 256
FF_CHUNK = 256
FF_NCHUNK = D_FF // FF_CHUNK


def _ffn_kernel(x_ref, g_ref, w1_ref, w3_ref, w2_ref, fg_ref, o_ref, acc_ref, *, final):
    x = x_ref[...]
    h = _rms(x, g_ref[...]).astype(BF16)
    acc_ref[...] = x

    def body(c, carry):
        a = jnp.dot(h, w1_ref[c], preferred_element_type=F32)
        b = jnp.dot(h, w3_ref[c], preferred_element_type=F32)
        t = (a * _sigmoid(a) * b).astype(BF16)
        acc_ref[...] += jnp.dot(t, w2_ref[c], preferred_element_type=F32)
        return carry

    lax.fori_loop(0, FF_NCHUNK, body, 0)
    y = acc_ref[...]
    if final:
        y = _rms(y, fg_ref[...])
    o_ref[...] = y


def _ffn(x2, g, w1c, w3c, w2c, final_g, final):
    n = x2.shape[0]
    return pl.pallas_call(
        functools.partial(_ffn_kernel, final=final),
        grid=(n // FF_TM,),
        in_specs=[pl.BlockSpec((FF_TM, D_MODEL), lambda i: (i, 0)),
                  _const_spec((1, D_MODEL)),
                  _const_spec((FF_NCHUNK, D_MODEL, FF_CHUNK)),
                  _const_spec((FF_NCHUNK, D_MODEL, FF_CHUNK)),
                  _const_spec((FF_NCHUNK, FF_CHUNK, D_MODEL)),
                  _const_spec((1, D_MODEL))],
        out_specs=pl.BlockSpec((FF_TM, D_MODEL), lambda i: (i, 0)),
        out_shape=jax.ShapeDtypeStruct((n, D_MODEL), F32),
        scratch_shapes=[pltpu.VMEM((FF_TM, D_MODEL), F32)],
        compiler_params=_cparams("parallel"),
        name="ffn_final" if final else "ffn",
    )(x2, g.reshape(1, D_MODEL), w1c, w3c, w2c, final_g.reshape(1, D_MODEL))


def _ffn_weights(w1, w3, w2):
    w1c = w1.astype(BF16).reshape(D_MODEL, FF_NCHUNK, FF_CHUNK).transpose(1, 0, 2)
    w3c = w3.astype(BF16).reshape(D_MODEL, FF_NCHUNK, FF_CHUNK).transpose(1, 0, 2)
    w2c = w2.astype(BF16).reshape(FF_NCHUNK, FF_CHUNK, D_MODEL)
    return w1c, w3c, w2c


MG_TM = 256


def _head_ones():
    r = np.arange(D_BRANCH)[:, None] // A_DK
    c = np.arange(D_BRANCH)[None, :] // A_DK
    return jnp.asarray((r == c).astype(np.float32) / A_DK)


def _merge_kernel(x_ref, g1_ref, oaf_ref, oab_ref, ga_ref, hg_ref, hm_ref, yb_ref, yc_ref,
                  o0_ref, o1_ref, o2_ref, l0_ref, l1_ref, l2_ref,
                  wg_ref, bg_ref, wb_ref, wo_ref, out_ref):
    x = x_ref[...]
    h = _rms(x, g1_ref[...]).astype(BF16)
    oa = oaf_ref[...] + oab_ref[...]
    ms = jnp.dot(oa * oa, hm_ref[...], precision=HI, preferred_element_type=F32)
    ga = ga_ref[...]
    ya = oa * lax.rsqrt(ms + EPS) * hg_ref[...] * (ga * _sigmoid(ga))
    l0, l1, l2 = l0_ref[...], l1_ref[...], l2_ref[...]
    m = jnp.maximum(jnp.maximum(l0, l1), l2)
    e0, e1, e2 = jnp.exp(l0 - m), jnp.exp(l1 - m), jnp.exp(l2 - m)
    yd = (e0 * o0_ref[...] + e1 * o1_ref[...] + e2 * o2_ref[...]) / (e0 + e1 + e2)
    mixed = None
    for j, y in enumerate((ya, yb_ref[...], yc_ref[...], yd)):
        gate = _sigmoid(jnp.dot(h, wg_ref[:, j * D_MODEL:(j + 1) * D_MODEL], preferred_element_type=F32)
                        + bg_ref[:, j * D_MODEL:(j + 1) * D_MODEL])
        t = gate * jnp.dot(y.astype(BF16), wb_ref[j], preferred_element_type=F32)
        mixed = t if mixed is None else mixed + t
    out_ref[...] = x + jnp.dot(mixed.astype(BF16), wo_ref[...], preferred_element_type=F32)


def _merge(x2, g1, oa_f, oa_b, pa, hg, yb, yc, od, ld, wg, bg, wb, wo):
    n = x2.shape[0]
    tile = lambda w: pl.BlockSpec((MG_TM, w), lambda i: (i, 0))
    return pl.pallas_call(
        _merge_kernel,
        grid=(n // MG_TM,),
        in_specs=[tile(D_MODEL), _const_spec((1, D_MODEL)),
                  tile(D_BRANCH), tile(D_BRANCH),
                  pl.BlockSpec((MG_TM, D_BRANCH), lambda i: (i, 4)),
                  _const_spec((1, D_BRANCH)), _const_spec((D_BRANCH, D_BRANCH)),
                  tile(D_BRANCH), tile(D_BRANCH),
                  tile(D_BRANCH), tile(D_BRANCH), tile(D_BRANCH),
                  tile(D_BRANCH), tile(D_BRANCH), tile(D_BRANCH),
                  _const_spec((D_MODEL, N_BRANCH * D_MODEL)), _const_spec((1, N_BRANCH * D_MODEL)),
                  _const_spec((N_BRANCH, D_BRANCH, D_MODEL)), _const_spec((D_MODEL, D_MODEL))],
        out_specs=tile(D_MODEL),
        out_shape=jax.ShapeDtypeStruct((n, D_MODEL), F32),
        compiler_params=_cparams("parallel"),
        name="merge",
    )(x2, g1.reshape(1, D_MODEL), oa_f, oa_b, pa, hg.reshape(1, D_BRANCH), _head_ones(), yb, yc,
      od[0], od[1], od[2], ld[0], ld[1], ld[2], wg, bg, wb, wo)


HG_TS = 256
HG_NCH = HG_TS // A_CHUNK
HG_MID = A_CHUNK // 2


def _hgrn_kernel(q_ref, f_ref, v_ref, lb_ref, o_ref, st_ref):
    d = pl.program_id(0)
    sgn = 1 - 2 * d

    @pl.when(pl.program_id(2) == 0)
    def _():
        st_ref[...] = jnp.zeros_like(st_ref)

    lb = lb_ref[...]
    row = lax.broadcasted_iota(jnp.int32, (A_CHUNK, A_CHUNK), 0)
    col = lax.broadcasted_iota(jnp.int32, (A_CHUNK, A_CHUNK), 1)
    cum = jnp.where((row - col) * sgn >= 0, 1.0, 0.0).astype(F32)
    srow = lax.broadcasted_iota(jnp.int32, (A_HEADS * A_CHUNK, A_CHUNK), 0) % A_CHUNK
    scol = lax.broadcasted_iota(jnp.int32, (A_HEADS * A_CHUNK, A_CHUNK), 1)
    causal = (srow - scol) * sgn >= 0
    lane_head = lax.broadcasted_iota(jnp.int32, (A_CHUNK, D_BRANCH), 1) // A_DK
    hmask = [(lane_head == hh).astype(F32) for hh in range(A_HEADS)]
    blk_r = lax.broadcasted_iota(jnp.int32, (D_BRANCH, D_BRANCH), 0) // A_DK
    blk_c = lax.broadcasted_iota(jnp.int32, (D_BRANCH, D_BRANCH), 1) // A_DK
    blockdiag = (blk_r == blk_c).astype(F32)

    for j in range(HG_NCH):
        cj = j + d * (HG_NCH - 1 - 2 * j)
        off = pl.multiple_of(cj * A_CHUNK, A_CHUNK)
        q = q_ref[pl.ds(off, A_CHUNK), :]
        fl = f_ref[pl.ds(off, A_CHUNK), :]
        v = v_ref[pl.ds(off, A_CHUNK), :]
        f = lb + (1.0 - lb) * _sigmoid(fl)
        g = jnp.log(jnp.maximum(f, TINY))
        kk = (1.0 - lb) * _sigmoid(-fl)
        b = jnp.dot(cum, g, precision=HI, preferred_element_type=F32)
        bm = b[HG_MID:HG_MID + 1, :]
        blast = jnp.where(d == 0, b[A_CHUNK - 1:A_CHUNK, :], b[0:1, :])
        qt = q * jnp.exp(b - bm)
        kt = (kk * jnp.exp(bm - b)).astype(BF16)
        qs = jnp.concatenate([qt * hmask[hh] for hh in range(A_HEADS)], axis=0).astype(BF16)
        s = lax.dot_general(qs, kt, (((1,), (1,)), ((), ())), preferred_element_type=F32)
        s = jnp.where(causal, s, 0.0).astype(BF16)
        ost = jnp.dot(s, v.astype(BF16), preferred_element_type=F32)
        o = ost[0:A_CHUNK] * hmask[0]
        for hh in range(1, A_HEADS):
            o = o + ost[hh * A_CHUNK:(hh + 1) * A_CHUNK] * hmask[hh]
        st = st_ref[...]
        qe = (q * jnp.exp(b)).astype(BF16)
        o = o + lax.dot_general(qe, st.astype(BF16), (((1,), (1,)), ((), ())), preferred_element_type=F32)
        o_ref[pl.ds(off, A_CHUNK), :] = o
        kh = (kk * jnp.exp(blast - b)).astype(BF16)
        upd = jnp.dot(v.T.astype(BF16), kh, preferred_element_type=F32)
        st_ref[...] = st * jnp.exp(blast) + upd * blockdiag


def _hgrn(pa3, lb):
    bsz, seq, _ = pa3.shape
    nblk = seq // HG_TS

    def tmap(col):
        def m(d, b, i):
            return (b, i + d * (nblk - 1 - 2 * i), col(d))
        return m

    blk = (None, HG_TS, D_BRANCH)
    return pl.pallas_call(
        _hgrn_kernel,
        grid=(2, bsz, nblk),
        in_specs=[pl.BlockSpec(blk, tmap(lambda d: 0)),
                  pl.BlockSpec(blk, tmap(lambda d: 1 + d)),
                  pl.BlockSpec(blk, tmap(lambda d: 3)),
                  _const_spec((1, D_BRANCH))],
        out_specs=pl.BlockSpec((None, None, HG_TS, D_BRANCH),
                               lambda d, b, i: (d, b, i + d * (nblk - 1 - 2 * i), 0)),
        out_shape=jax.ShapeDtypeStruct((2, bsz, seq, D_BRANCH), F32),
        scratch_shapes=[pltpu.VMEM((D_BRANCH, D_BRANCH), F32)],
        compiler_params=_cparams("arbitrary", "arbitrary", "arbitrary"),
        name="hgrn2",
    )(pa3, pa3, pa3, lb.reshape(1, D_BRANCH))


RG_TB = 128
RG_PAD = SUBLANES
RG_LEFT = B_CONV // 2


def _shift_rows(x, k, fwd, fill):
    t = x.shape[0]
    row = lax.broadcasted_iota(jnp.int32, x.shape, 0)
    if fwd:
        return jnp.where(row >= k, pltpu.roll(x, k, 0), fill)
    return jnp.where(row < t - k, pltpu.roll(x, t - k, 0), fill)


def _block_scan(a, u, fwd):
    k = 1
    while k < a.shape[0]:
        u = a * _shift_rows(u, k, fwd, 0.0) + u
        a = a * _shift_rows(a, k, fwd, 1.0)
        k *= 2
    return a, u


def _gelu_tanh(x):
    return 0.5 * x * (1.0 + jnp.tanh(math.sqrt(2.0 / math.pi) * (x + 0.044715 * (x * x * x))))


def _rglru_kernel(x_ref, gt_ref, cw_ref, cb_ref, wg_ref, bg_ref, lam_ref, o_ref, xp_ref):
    seq = x_ref.shape[0]
    nblk = seq // RG_TB
    xp_ref[0:RG_PAD, :] = jnp.zeros((RG_PAD, D_BRANCH), F32)
    xp_ref[RG_PAD + seq:2 * RG_PAD + seq, :] = jnp.zeros((RG_PAD, D_BRANCH), F32)
    xp_ref[RG_PAD:RG_PAD + seq, :] = x_ref[...]
    nl = -lam_ref[...]
    sp = jnp.maximum(nl, 0.0) + jnp.log(1.0 + jnp.exp(-jnp.abs(nl)))

    def block(i, carry, dirn):
        r0 = pl.multiple_of(i * RG_TB, RG_TB)
        win = xp_ref[pl.ds(r0, RG_TB + 2 * RG_PAD), :]
        xc = cb_ref[...]
        for j in range(B_CONV):
            s0 = RG_PAD + j - RG_LEFT
            xc = xc + cw_ref[j:j + 1, :] * win[s0:s0 + RG_TB, :]
        w = wg_ref[:, dirn * 2 * D_BRANCH:(dirn + 1) * 2 * D_BRANCH]
        gates = jnp.dot(xc, w, precision=HI, preferred_element_type=F32) \
            + bg_ref[:, dirn * 2 * D_BRANCH:(dirn + 1) * 2 * D_BRANCH]
        r = _sigmoid(gates[:, :D_BRANCH])
        ig = _sigmoid(gates[:, D_BRANCH:])
        log_a = -LRU_C * r * sp[dirn:dirn + 1, :]
        a = jnp.exp(log_a)
        u = jnp.sqrt(jnp.maximum(-jnp.tanh(log_a) * (a * a + 1.0), 0.0)) * ig * xc
        acum, hloc = _block_scan(a, u, dirn == 0)
        h = hloc + acum * carry
        if dirn == 0:
            o_ref[pl.ds(r0, RG_TB), :] = h
            return h[RG_TB - 1:RG_TB, :]
        o_ref[pl.ds(r0, RG_TB), :] = (o_ref[pl.ds(r0, RG_TB), :] + h) * _gelu_tanh(gt_ref[pl.ds(r0, RG_TB), :])
        return h[0:1, :]

    zero = jnp.zeros((1, D_BRANCH), F32)
    lax.fori_loop(0, nblk, lambda i, c: block(i, c, 0), zero)
    lax.fori_loop(0, nblk, lambda i, c: block(nblk - 1 - i, c, 1), zero)


def _blockdiag(w):
    eye = jnp.eye(B_BLOCKS, dtype=w.dtype)
    return jnp.einsum('ncd,nm->ncmd', w, eye).reshape(D_BRANCH, D_BRANCH)


def _rglru(pb3, cw, cb, wa, ba, wx, bx, lam):
    bsz, seq, _ = pb3.shape
    wg = jnp.concatenate([_blockdiag(wa[0]), _blockdiag(wx[0]), _blockdiag(wa[1]), _blockdiag(wx[1])], axis=1)
    bg = jnp.concatenate([ba[0], bx[0], ba[1], bx[1]]).reshape(1, 4 * D_BRANCH)
    blk = (None, seq, D_BRANCH)
    return pl.pallas_call(
        _rglru_kernel,
        grid=(bsz,),
        in_specs=[pl.BlockSpec(blk, lambda b: (b, 0, 0)),
                  pl.BlockSpec(blk, lambda b: (b, 0, 1)),
                  _const_spec((B_CONV, D_BRANCH)), _const_spec((1, D_BRANCH)),
                  _const_spec((D_BRANCH, 4 * D_BRANCH)), _const_spec((1, 4 * D_BRANCH)),
                  _const_spec((2, D_BRANCH))],
        out_specs=pl.BlockSpec(blk, lambda b: (b, 0, 0)),
        out_shape=jax.ShapeDtypeStruct((bsz, seq, D_BRANCH), F32),
        scratch_shapes=[pltpu.VMEM((seq + 2 * RG_PAD, D_BRANCH), F32)],
        compiler_params=_cparams("parallel"),
        name="rglru",
    )(pb3, pb3, cw, cb.reshape(1, D_BRANCH), wg, bg, lam)


AT_HALF = 64
AT_TQ = 256


def _t5_bucket(rel):
    half = N_BUCKETS // 2
    max_exact = half // 2
    n = np.abs(rel)
    large = max_exact + (np.log(np.maximum(n, 1) / max_exact) / math.log(MAX_DISTANCE / max_exact)
                         * (half - max_exact)).astype(np.int64)
    large = np.minimum(large, half - 1)
    return (rel > 0).astype(np.int64) * half + np.where(n < max_exact, n, large)


def _attn_geometry(n):
    tq = min(AT_TQ, n)
    win = min(tq + 2 * AT_HALF, n)
    return tq, win, n // tq


def _attn_bias_tables(rel_bias, g, dil, n):
    tq, win, nq = _attn_geometry(n)
    hs = slice(g * D_HEADS_PER_GROUP, (g + 1) * D_HEADS_PER_GROUP)
    offsets = np.arange(-AT_HALF, AT_HALF + 1) * dil
    band = rel_bias.astype(F32)[_t5_bucket(offsets)][:, hs].T
    tables = []
    for i in sorted({0, min(1, nq - 1), nq - 1}):
        ws = int(np.clip(i * tq - AT_HALF, 0, n - win))
        rel = (ws + np.arange(win)[None, :]) - (i * tq + np.arange(tq)[:, None])
        t = band[:, np.clip(rel + AT_HALF, 0, 2 * AT_HALF)]
        tables.append(jnp.where(jnp.asarray(np.abs(rel) <= AT_HALF)[None], t, NEG_BIG))
    return jnp.stack(tables)


def _attn_kernel(q_ref, k_ref, v_ref, bias_ref, o_ref, l_ref, *, n):
    tq, win, _ = _attn_geometry(n)
    i = pl.program_id(1)
    ws = pl.multiple_of(jnp.clip(i * tq - AT_HALF, 0, n - win), AT_HALF)
    q = q_ref[...] * (D_HEAD_DIM ** -0.5)
    kw = k_ref[pl.ds(ws, win), :].astype(BF16)
    vw = v_ref[pl.ds(ws, win), :].astype(BF16)
    lane_head = lax.broadcasted_iota(jnp.int32, (tq, D_BRANCH), 1) // D_HEAD_DIM
    o_acc = jnp.zeros((tq, D_BRANCH), F32)
    l_acc = jnp.zeros((tq, D_BRANCH), F32)
    for hh in range(D_HEADS_PER_GROUP):
        hm = lane_head == hh
        qh = jnp.where(hm, q, 0.0).astype(BF16)
        s = lax.dot_general(qh, kw, (((1,), (1,)), ((), ())), preferred_element_type=F32) + bias_ref[hh]
        m = jnp.max(s, axis=-1, keepdims=True)
        p = jnp.exp(s - m)
        l = jnp.sum(p, axis=-1, keepdims=True)
        oh = jnp.dot(p.astype(BF16), vw, preferred_element_type=F32) / l
        o_acc = jnp.where(hm, oh, o_acc)
        l_acc = jnp.where(hm, m + jnp.log(l), l_acc)
    o_ref[...] = o_acc
    l_ref[...] = l_acc


def _banded_attention(arr, cols, bias, n):
    rows = arr.shape[0]
    tq, win, nq = _attn_geometry(n)
    ncase = bias.shape[0]

    def case(i):
        return jnp.minimum(jnp.where(i == nq - 1, ncase - 1, jnp.minimum(i, 1)), ncase - 1)

    out_spec = pl.BlockSpec((None, tq, D_BRANCH), lambda r, i: (r, i, 0))
    shp = jax.ShapeDtypeStruct((rows, n, D_BRANCH), F32)
    return pl.pallas_call(
        functools.partial(_attn_kernel, n=n),
        grid=(rows, nq),
        in_specs=[pl.BlockSpec((None, tq, D_BRANCH), lambda r, i: (r, i, cols[0])),
                  pl.BlockSpec((None, n, D_BRANCH), lambda r, i: (r, 0, cols[1])),
                  pl.BlockSpec((None, n, D_BRANCH), lambda r, i: (r, 0, cols[2])),
                  pl.BlockSpec((None, D_HEADS_PER_GROUP, tq, win), lambda r, i: (case(i), 0, 0, 0))],
        out_specs=[out_spec, out_spec],
        out_shape=[shp, shp],
        compiler_params=_cparams("parallel", "arbitrary"),
        name=f"attn_n{n}",
    )(arr, arr, arr, bias)


FFT_IN = 128
FFT_NT = FFT_IN // SUBLANES
HY_TB = 256
HY_PAD = SUBLANES


def _fft_tables(seq):
    n = 2 * seq
    n1 = n // FFT_IN
    half = n1 // 2
    s_in = (SUBLANES * np.arange(FFT_NT)[:, None] + np.arange(SUBLANES)[None, :]).astype(np.float64)
    k_a = np.arange(n1, dtype=np.float64)
    s_a = np.arange(half, dtype=np.float64)
    ang = 2.0 * np.pi * (s_in[:, None, :, None] * k_a[None, :, None, None] / n
                         + s_a[None, None, None, :] * k_a[None, :, None, None] / n1)
    gr, gi = np.cos(ang), -np.sin(ang)
    fwd = np.stack([np.stack([gr, -gi], axis=3), np.stack([gi, gr], axis=3)], axis=2)
    ir, ii = gr / n, -gi / n
    inv = np.stack([np.stack([ir, -ii], axis=2), np.stack([ii, ir], axis=2)], axis=1)
    inv = inv.transpose(0, 1, 5, 4, 2, 3)
    eye = jnp.eye(SUBLANES, dtype=F32)
    m_fwd = jnp.einsum('bkrcis,cd->bkrcisd', jnp.asarray(fwd, F32), eye).reshape(
        FFT_NT, n1 * 2 * SUBLANES, 2 * half * SUBLANES)
    m_inv = jnp.einsum('botckr,cd->botckrd', jnp.asarray(inv, F32), eye).reshape(
        FFT_NT, 2 * half * SUBLANES, n1 * 2 * SUBLANES)
    a = 2.0 * np.pi * np.outer(np.arange(FFT_IN), np.arange(FFT_IN)) / FFT_IN
    fr, fi = np.cos(a), -np.sin(a)
    w_fwd = jnp.asarray(np.block([[fr, -fi], [fi, fr]]), F32)
    w_inv = jnp.asarray(np.block([[fr, fi], [-fi, fr]]), F32)
    return m_fwd, m_inv, w_fwd, w_inv, n1, half


def _fft_outer_kernel(m_ref, z_ref, v_ref, *, rows_in):
    z = z_ref[...].reshape(rows_in, D_BRANCH)
    v = jnp.dot(m_ref[:, :rows_in], z, precision=HI, preferred_element_type=F32)
    v_ref[...] = v.reshape(v_ref.shape)


def _fft_outer(m_fwd, z, n1, half, complex_in):
    if complex_in:
        p = z.shape[1]
        zspec = pl.BlockSpec((2, None, half, None, SUBLANES, D_BRANCH), lambda t, q: (0, q, 0, t, 0, 0))
    else:
        p = z.shape[0]
        zspec = pl.BlockSpec((None, half, None, SUBLANES, D_BRANCH), lambda t, q: (q, 0, t, 0, 0))
    rows_in = (2 if complex_in else 1) * half * SUBLANES
    rows_out = n1 * 2 * SUBLANES
    return pl.pallas_call(
        functools.partial(_fft_outer_kernel, rows_in=rows_in),
        grid=(FFT_NT, p),
        in_specs=[pl.BlockSpec((None, rows_out, 2 * half * SUBLANES), lambda t, q: (t, 0, 0)), zspec],
        out_specs=pl.BlockSpec((None, n1, 2, None, SUBLANES, D_BRANCH), lambda t, q: (q, 0, 0, t, 0, 0)),
        out_shape=jax.ShapeDtypeStruct((p, n1, 2, FFT_NT, SUBLANES, D_BRANCH), F32),
        compiler_params=_cparams("arbitrary", "arbitrary"),
        name="fft_outer_c" if complex_in else "fft_outer_r",
    )(m_fwd, z)


def _fft_inner_kernel(w_ref, v_ref, h_ref):
    v = v_ref[...].reshape(2 * FFT_IN, D_BRANCH)
    h_ref[...] = jnp.dot(w_ref[...], v, precision=HI, preferred_element_type=F32).reshape(h_ref.shape)


def _fft_inner(w_fwd, v):
    p, n1 = v.shape[:2]
    return pl.pallas_call(
        _fft_inner_kernel,
        grid=(p, n1),
        in_specs=[_const_spec((2 * FFT_IN, 2 * FFT_IN)),
                  pl.BlockSpec((None, None, 2, FFT_NT, SUBLANES, D_BRANCH), lambda q, k: (q, k, 0, 0, 0, 0))],
        out_specs=pl.BlockSpec((None, None, 2, FFT_IN, D_BRANCH), lambda q, k: (q, k, 0, 0, 0)),
        out_shape=jax.ShapeDtypeStruct((p, n1, 2, FFT_IN, D_BRANCH), F32),
        compiler_params=_cparams("parallel", "parallel"),
        name="fft_inner",
    )(w_fwd, v)


def _fft_mid_kernel(wf_ref, wi_ref, v_ref, hf_ref, hb_ref, d_ref):
    v = v_ref[...].reshape(2 * FFT_IN, D_BRANCH)
    z = jnp.dot(wf_ref[...], v, precision=HI, preferred_element_type=F32)
    zr, zi = z[:FFT_IN], z[FFT_IN:]
    hr = hf_ref[0] + hb_ref[0]
    hi = hf_ref[1] - hb_ref[1]
    pr = zr * hr - zi * hi
    pi = zr * hi + zi * hr
    d = jnp.dot(wi_ref[...], jnp.concatenate([pr, pi], axis=0), precision=HI, preferred_element_type=F32)
    d_ref[...] = d.reshape(d_ref.shape)


def _fft_mid(w_fwd, w_inv, v, hspec, order):
    p, n1 = v.shape[:2]
    vspec = pl.BlockSpec((None, None, 2, FFT_NT, SUBLANES, D_BRANCH), lambda q, k: (q, k, 0, 0, 0, 0))
    hblk = (None, None, 2, FFT_IN, D_BRANCH)
    return pl.pallas_call(
        _fft_mid_kernel,
        grid=(p, n1),
        in_specs=[_const_spec((2 * FFT_IN, 2 * FFT_IN)), _const_spec((2 * FFT_IN, 2 * FFT_IN)), vspec,
                  pl.BlockSpec(hblk, lambda q, k: (2 * order, k, 0, 0, 0)),
                  pl.BlockSpec(hblk, lambda q, k: (2 * order + 1, k, 0, 0, 0))],
        out_specs=vspec,
        out_shape=jax.ShapeDtypeStruct(v.shape, F32),
        compiler_params=_cparams("parallel", "parallel"),
        name="fft_mid",
    )(w_fwd, w_inv, v, hspec, hspec)


def _ifft_outer_kernel(m_ref, d_ref, u_ref, x_ref, b_ref, o_ref):
    d = d_ref[...].reshape(m_ref.shape[1], D_BRANCH)
    y = jnp.dot(m_ref[...], d, precision=HI, preferred_element_type=F32).reshape(o_ref.shape)
    o_ref[...] = x_ref[...] * (y + u_ref[...] * b_ref[...])


def _ifft_outer(m_inv, d, u, ucol, x, xcol, bias, n1, half):
    p = d.shape[0]
    rows_in = n1 * 2 * SUBLANES
    io = lambda col: pl.BlockSpec((2, None, half, None, SUBLANES, D_BRANCH), lambda t, q: (0, q, 0, t, 0, col))
    return pl.pallas_call(
        _ifft_outer_kernel,
        grid=(FFT_NT, p),
        in_specs=[pl.BlockSpec((None, 2 * half * SUBLANES, rows_in), lambda t, q: (t, 0, 0)),
                  pl.BlockSpec((None, n1, 2, None, SUBLANES, D_BRANCH), lambda t, q: (q, 0, 0, t, 0, 0)),
                  io(ucol), io(xcol), _const_spec((1, D_BRANCH))],
        out_specs=io(0),
        out_shape=jax.ShapeDtypeStruct((2, p, half, FFT_NT, SUBLANES, D_BRANCH), F32),
        compiler_params=_cparams("arbitrary", "arbitrary"),
        name="ifft_outer",
    )(m_inv, d, u, x, bias.reshape(1, D_BRANCH))


def _hyconv_kernel(x_ref, w_ref, b_ref, o_ref, xp_ref):
    seq = x_ref.shape[0]
    xp_ref[0:HY_PAD, :] = jnp.zeros((HY_PAD, D_BRANCH), F32)
    xp_ref[HY_PAD + seq:2 * HY_PAD + seq, :] = jnp.zeros((HY_PAD, D_BRANCH), F32)
    xp_ref[HY_PAD:HY_PAD + seq, :] = x_ref[...]

    def body(i, carry):
        r0 = pl.multiple_of(i * HY_TB, HY_TB)
        win = xp_ref[pl.ds(r0, HY_TB + 2 * HY_PAD), :]
        y = b_ref[...]
        for j in range(C_CONV):
            s0 = HY_PAD + j - C_CONV // 2
            y = y + w_ref[j:j + 1, :] * win[s0:s0 + HY_TB, :]
        o_ref[pl.ds(r0, HY_TB), :] = y
        return carry

    lax.fori_loop(0, seq // HY_TB, body, 0)


def _hyconv(pc3, cw, cb):
    bsz, seq, width = pc3.shape
    nb = width // D_BRANCH
    blk = pl.BlockSpec((None, seq, D_BRANCH), lambda b, j: (b, 0, j))
    return pl.pallas_call(
        _hyconv_kernel,
        grid=(bsz, nb),
        in_specs=[blk, pl.BlockSpec((C_CONV, D_BRANCH), lambda b, j: (0, j)),
                  pl.BlockSpec((1, D_BRANCH), lambda b, j: (0, j))],
        out_specs=blk,
        out_shape=jax.ShapeDtypeStruct(pc3.shape, F32),
        scratch_shapes=[pltpu.VMEM((seq + 2 * HY_PAD, D_BRANCH), F32)],
        compiler_params=_cparams("parallel", "parallel"),
        name="hyconv",
    )(pc3, cw, cb.reshape(1, width))


def _hyfilt_kernel(z_ref, w1_ref, b1_ref, fr_ref, w2_ref, b2_ref, w3_ref, dec_ref, o_ref):
    seq = z_ref.shape[0]
    nblk = seq // HY_TB
    fr = fr_ref[...]

    def body(i, ss):
        r0 = pl.multiple_of(i * HY_TB, HY_TB)
        zb = z_ref[pl.ds(r0, HY_TB), :]
        h = jnp.sin(fr * (jnp.dot(zb, w1_ref[...], precision=HI, preferred_element_type=F32) + b1_ref[...]))
        h = jnp.sin(fr * (jnp.dot(h, w2_ref[...], precision=HI, preferred_element_type=F32) + b2_ref[...]))
        hf = jnp.dot(h, w3_ref[...], precision=HI, preferred_element_type=F32)
        hf = hf * jnp.exp(-zb[:, 0:1] * dec_ref[...])
        o_ref[pl.ds(r0, HY_TB), :] = hf
        return ss + jnp.sum(hf * hf, axis=0, keepdims=True)

    ss = lax.fori_loop(0, nblk, body, jnp.zeros((1, D_BRANCH), F32))
    scale = lax.rsqrt(ss + EPS)

    def norm(i, carry):
        r0 = pl.multiple_of(i * HY_TB, HY_TB)
        o_ref[pl.ds(r0, HY_TB), :] = o_ref[pl.ds(r0, HY_TB), :] * scale
        return carry

    lax.fori_loop(0, nblk, norm, 0)


def _hyfilt(seq, w1, b1, freq, w2, b2, w3):
    t = jnp.linspace(0.0, 1.0, seq, dtype=F32)[:, None]
    bands = (C_EMB - 1) // 2
    w = 2.0 * math.pi * jnp.arange(seq, dtype=F32)[:, None] / seq
    fr = jnp.linspace(1e-4, bands - 1, bands, dtype=F32)[None]
    z = jnp.concatenate([t, jnp.cos(fr * w), -jnp.sin(fr * w)], axis=-1)
    z = jnp.pad(z, ((0, 0), (0, LANES - C_EMB)))
    padm = lambda a, r, c: jnp.pad(a.astype(F32), ((0, r - a.shape[0]), (0, c - a.shape[1])))
    row = lambda a: padm(a.reshape(1, -1), 1, LANES)
    dec = jnp.abs(jnp.linspace(C_MIN_DECAY, C_MAX_DECAY, D_BRANCH, dtype=F32)).reshape(1, D_BRANCH)
    nset = C_ORDER * 2
    return pl.pallas_call(
        _hyfilt_kernel,
        grid=(nset,),
        in_specs=[_const_spec((seq, LANES)), _const_spec((LANES, LANES)), _const_spec((1, LANES)),
                  _const_spec((1, LANES)), _const_spec((LANES, LANES)), _const_spec((1, LANES)),
                  pl.BlockSpec((LANES, D_BRANCH), lambda j: (0, j)), _const_spec((1, D_BRANCH))],
        out_specs=pl.BlockSpec((None, seq, D_BRANCH), lambda j: (j, 0, 0)),
        out_shape=jax.ShapeDtypeStruct((nset, seq, D_BRANCH), F32),
        compiler_params=_cparams("parallel"),
        name="hyfilt",
    )(z, padm(w1, LANES, LANES), row(b1), row(freq), padm(w2, LANES, LANES), row(b2),
      padm(w3, LANES, nset * D_BRANCH), dec)


def _hyena(pc3, cw, cb, w1, b1, freq, w2, b2, w3, bias):
    bsz, seq, width = pc3.shape
    m_fwd, m_inv, w_fwd, w_inv, n1, half = _fft_tables(seq)
    npair = bsz // 2
    uc = _hyconv(pc3, cw, cb)
    uc6 = uc.reshape(2, npair, half, FFT_NT, SUBLANES, width)
    filt = _hyfilt(seq, w1, b1, freq, w2, b2, w3).reshape(C_ORDER * 2, half, FFT_NT, SUBLANES, D_BRANCH)
    hspec = _fft_inner(w_fwd, _fft_outer(m_fwd, filt, n1, half, False))
    src = uc6
    for order in range(C_ORDER):
        v = _fft_outer(m_fwd, src, n1, half, True)
        d = _fft_mid(w_fwd, w_inv, v, hspec, order)
        src = _ifft_outer(m_inv, d, src, 0, uc6, order + 1, bias[order], n1, half)
    return src.reshape(bsz, seq, D_BRANCH)


def _dilated_attention(pd3, rel_bias):
    bsz, seq, _ = pd3.shape
    ng = len(D_GROUPS)
    outs, lses = [], []
    for g, (_, dil) in enumerate(D_GROUPS):
        n = seq // dil
        bias = _attn_bias_tables(rel_bias, g, dil, n)
        if dil == 1:
            o, l = _banded_attention(pd3, (g, ng + g, 2 * ng + g), bias, n)
        else:
            t = jnp.concatenate([pd3[:, :, (j * ng + g) * D_BRANCH:(j * ng + g + 1) * D_BRANCH] for j in range(3)], -1)
            t = t.reshape(bsz, n, dil, 3 * D_BRANCH).transpose(0, 2, 1, 3).reshape(bsz * dil, n, 3 * D_BRANCH)
            o, l = _banded_attention(t, (0, 1, 2), bias, n)
            back = lambda a: a.reshape(bsz, dil, n, D_BRANCH).transpose(0, 2, 1, 3).reshape(bsz, seq, D_BRANCH)
            o, l = back(o), back(l)
        outs.append(o.reshape(bsz * seq, D_BRANCH))
        lses.append(l.reshape(bsz * seq, D_BRANCH))
    return outs, lses


def kernel(x, norm1_g, w_in, hgrn_lb_logits, hgrn_norm_g, lru_conv_w, lru_conv_b, lru_wa, lru_ba, lru_wx, lru_bx,
           lru_lambda, hy_conv_w, hy_conv_b, hy_w1, hy_b1, hy_freq, hy_w2, hy_b2, hy_w3, hy_bias, rel_bias,
           w_branch, w_gate, b_gate, w_out, norm2_g, w_ff1, w_ff3, w_ff2, final_g):
    bsz, seq, _ = x.shape
    n = bsz * seq
    lb_soft = jax.nn.softmax(hgrn_lb_logits.astype(F32), axis=0)
    lower_bounds = jnp.cumsum(lb_soft, axis=0) - lb_soft[0]
    x2 = x.reshape(n, D_MODEL)
    flat = lambda a: a.reshape(n, D_BRANCH)
    for l in range(DEPTH):
        pa, pb, pc, pd = _inproj(x2, norm1_g[l], w_in[l].astype(BF16))
        oa = _hgrn(pa.reshape(bsz, seq, IN_A), lower_bounds[l])
        yb = _rglru(pb.reshape(bsz, seq, IN_B), lru_conv_w[l], lru_conv_b[l], lru_wa[l], lru_ba[l],
                    lru_wx[l], lru_bx[l], lru_lambda[l])
        yc = _hyena(pc.reshape(bsz, seq, IN_C), hy_conv_w[l], hy_conv_b[l], hy_w1[l], hy_b1[l], hy_freq[l],
                    hy_w2[l], hy_b2[l], hy_w3[l], hy_bias[l])
        od, ld = _dilated_attention(pd.reshape(bsz, seq, IN_D), rel_bias)
        x2 = _merge(x2, norm1_g[l], flat(oa[0]), flat(oa[1]), pa, hgrn_norm_g[l], flat(yb), flat(yc), od, ld,
                    w_gate[l].reshape(D_MODEL, N_BRANCH * D_MODEL).astype(BF16),
                    b_gate[l].reshape(1, N_BRANCH * D_MODEL), w_branch[l].astype(BF16), w_out[l].astype(BF16))
        x2 = _ffn(x2, norm2_g[l], *_ffn_weights(w_ff1[l], w_ff3[l], w_ff2[l]), final_g, l == DEPTH - 1)
    return x2.reshape(bsz, seq, D_MODEL)
```

```python
import functools
import math

import jax
import jax.numpy as jnp
import numpy as np
from jax import lax
from jax.experimental import pallas as pl
from jax.experimental.pallas import tpu as pltpu

F32 = jnp.float32
BF16 = jnp.bfloat16
HI = lax.Precision.HIGHEST

D_MODEL = 1024
DEPTH = 2
EPS = 1e-6
TINY = 1e-30
N_BRANCH = 4
D_BRANCH = 256
A_HEADS = 4
A_DK = 64
A_CHUNK = 64
B_BLOCKS = 4
B_BW = 64
B_CONV = 4
LRU_C = 8.0
C_ORDER = 2
C_CONV = 3
C_EMB = 33
C_HID = 64
C_MIN_DECAY = math.log(1e-2) / 1.5
C_MAX_DECAY = math.log(1e-2) / 0.3
D_GROUPS = ((128, 1), (512, 4), (2048, 16))
D_HEADS_PER_GROUP = 4
D_HEAD_DIM = 64
D_N_HEADS = 12
D_QKV = 768
N_BUCKETS = 32
MAX_DISTANCE = 1024
NEG_BIG = -1e30
D_FF = 2816
IN_A = 5 * D_BRANCH
IN_B = 2 * D_BRANCH
IN_C = 3 * D_BRANCH
IN_D = 3 * D_QKV
IN_WIDTH = IN_A + IN_B + IN_C + IN_D

LANES = 128
SUBLANES = 8
VMEM_LIMIT = 56 * 1024 * 1024


def _cparams(*sem):
    return pltpu.CompilerParams(dimension_semantics=sem, vmem_limit_bytes=VMEM_LIMIT)


def _const_spec(shape):
    nd = len(shape)
    return pl.BlockSpec(shape, lambda *_: (0,) * nd, pipeline_mode=pl.Buffered(1))


def _rms(x, g):
    return x * lax.rsqrt(jnp.mean(x * x, axis=-1, keepdims=True) + EPS) * g


def _sigmoid(x):
    return 1.0 / (1.0 + jnp.exp(-x))


IN_TM = 512
IN_CHUNK = 256


def _inproj_kernel(x_ref, g_ref, w_ref, oa_ref, ob_ref, oc_ref, od_ref):
    h = _rms(x_ref[...], g_ref[...]).astype(BF16)
    off = 0
    for o_ref in (oa_ref, ob_ref, oc_ref, od_ref):
        width = o_ref.shape[-1]
        for c in range(0, width, IN_CHUNK):
            o_ref[:, c:c + IN_CHUNK] = jnp.dot(h, w_ref[:, off + c:off + c + IN_CHUNK],
                                               preferred_element_type=F32)
        off += width


def _inproj(x2, g, w_bf16):
    n = x2.shape[0]
    widths = (IN_A, IN_B, IN_C, IN_D)
    return pl.pallas_call(
        _inproj_kernel,
        grid=(n // IN_TM,),
        in_specs=[pl.BlockSpec((IN_TM, D_MODEL), lambda i: (i, 0)),
                  _const_spec((1, D_MODEL)),
                  _const_spec((D_MODEL, IN_WIDTH))],
        out_specs=[pl.BlockSpec((IN_TM, w), lambda i: (i, 0)) for w in widths],
        out_shape=[jax.ShapeDtypeStruct((n, w), F32) for w in widths],
        compiler_params=_cparams("parallel"),
        name="inproj",
    )(x2, g.reshape(1, D_MODEL), w_bf16)


FF_TM = 512
FF_CHUNK = 256
FF_NCHUNK = D_FF // FF_CHUNK


def _ffn_kernel(x_ref, g_ref, w1_ref, w3_ref, w2_ref, fg_ref, o_ref, acc_ref, *, final):
    x = x_ref[...]
    h = _rms(x, g_ref[...]).astype(BF16)
    acc_ref[...] = x

    def body(c, carry):
        a = jnp.dot(h, w1_ref[c], preferred_element_type=F32)
        b = jnp.dot(h, w3_ref[c], preferred_element_type=F32)
        t = (a * _sigmoid(a) * b).astype(BF16)
        acc_ref[...] += jnp.dot(t, w2_ref[c], preferred_element_type=F32)
        return carry

    lax.fori_loop(0, FF_NCHUNK, body, 0)
    y = acc_ref[...]
    if final:
        y = _rms(y, fg_ref[...])
    o_ref[...] = y


def _ffn(x2, g, w1c, w3c, w2c, final_g, final):
    n = x2.shape[0]
    return pl.pallas_call(
        functools.partial(_ffn_kernel, final=final),
        grid=(n // FF_TM,),
        in_specs=[pl.BlockSpec((FF_TM, D_MODEL), lambda i: (i, 0)),
                  _const_spec((1, D_MODEL)),
                  _const_spec((FF_NCHUNK, D_MODEL, FF_CHUNK)),
                  _const_spec((FF_NCHUNK, D_MODEL, FF_CHUNK)),
                  _const_spec((FF_NCHUNK, FF_CHUNK, D_MODEL)),
                  _const_spec((1, D_MODEL))],
        out_specs=pl.BlockSpec((FF_TM, D_MODEL), lambda i: (i, 0)),
        out_shape=jax.ShapeDtypeStruct((n, D_MODEL), F32),
        scratch_shapes=[pltpu.VMEM((FF_TM, D_MODEL), F32)],
        compiler_params=_cparams("parallel"),
        name="ffn_final" if final else "ffn",
    )(x2, g.reshape(1, D_MODEL), w1c, w3c, w2c, final_g.reshape(1, D_MODEL))


def _ffn_weights(w1, w3, w2):
    w1c = w1.astype(BF16).reshape(D_MODEL, FF_NCHUNK, FF_CHUNK).transpose(1, 0, 2)
    w3c = w3.astype(BF16).reshape(D_MODEL, FF_NCHUNK, FF_CHUNK).transpose(1, 0, 2)
    w2c = w2.astype(BF16).reshape(FF_NCHUNK, FF_CHUNK, D_MODEL)
    return w1c, w3c, w2c


MG_TM = 512


def _head_ones():
    r = np.arange(D_BRANCH)[:, None] // A_DK
    c = np.arange(D_BRANCH)[None, :] // A_DK
    return jnp.asarray((r == c).astype(np.float32) / A_DK)


def _merge_kernel(x_ref, g1_ref, oaf_ref, oab_ref, ga_ref, hg_ref, hm_ref, yb_ref, yc_ref,
                  o0_ref, o1_ref, o2_ref, l0_ref, l1_ref, l2_ref,
                  wg_ref, bg_ref, wb_ref, wo_ref, out_ref):
    x = x_ref[...]
    h = _rms(x, g1_ref[...]).astype(BF16)
    oa = oaf_ref[...] + oab_ref[...]
    ms = jnp.dot(oa * oa, hm_ref[...], precision=HI, preferred_element_type=F32)
    ga = ga_ref[...]
    ya = oa * lax.rsqrt(ms + EPS) * hg_ref[...] * (ga * _sigmoid(ga))
    l0, l1, l2 = l0_ref[...], l1_ref[...], l2_ref[...]
    m = jnp.maximum(jnp.maximum(l0, l1), l2)
    e0, e1, e2 = jnp.exp(l0 - m), jnp.exp(l1 - m), jnp.exp(l2 - m)
    yd = (e0 * o0_ref[...] + e1 * o1_ref[...] + e2 * o2_ref[...]) / (e0 + e1 + e2)
    mixed = None
    for j, y in enumerate((ya, yb_ref[...], yc_ref[...], yd)):
        gate = _sigmoid(jnp.dot(h, wg_ref[:, j * D_MODEL:(j + 1) * D_MODEL], preferred_element_type=F32)
                        + bg_ref[:, j * D_MODEL:(j + 1) * D_MODEL])
        t = gate * jnp.dot(y.astype(BF16), wb_ref[j], preferred_element_type=F32)
        mixed = t if mixed is None else mixed + t
    out_ref[...] = x + jnp.dot(mixed.astype(BF16), wo_ref[...], preferred_element_type=F32)


def _merge(x2, g1, oa, pa, hg, yb, yc, od, ld, wg, bg, wb, wo):
    n = x2.shape[0]
    tile = lambda w: pl.BlockSpec((MG_TM, w), lambda i: (i, 0))
    return pl.pallas_call(
        _merge_kernel,
        grid=(n // MG_TM,),
        in_specs=[tile(D_MODEL), _const_spec((1, D_MODEL)),
                  pl.BlockSpec((None, MG_TM, D_BRANCH), lambda i: (0, i, 0)),
                  pl.BlockSpec((None, MG_TM, D_BRANCH), lambda i: (1, i, 0)),
                  pl.BlockSpec((MG_TM, D_BRANCH), lambda i: (i, 4)),
                  _const_spec((1, D_BRANCH)), _const_spec((D_BRANCH, D_BRANCH)),
                  tile(D_BRANCH), tile(D_BRANCH),
                  tile(D_BRANCH), tile(D_BRANCH), tile(D_BRANCH),
                  tile(D_BRANCH), tile(D_BRANCH), tile(D_BRANCH),
                  _const_spec((D_MODEL, N_BRANCH * D_MODEL)), _const_spec((1, N_BRANCH * D_MODEL)),
                  _const_spec((N_BRANCH, D_BRANCH, D_MODEL)), _const_spec((D_MODEL, D_MODEL))],
        out_specs=tile(D_MODEL),
        out_shape=jax.ShapeDtypeStruct((n, D_MODEL), F32),
        compiler_params=_cparams("parallel"),
        name="merge",
    )(x2, g1.reshape(1, D_MODEL), oa, oa, pa, hg.reshape(1, D_BRANCH), _head_ones(), yb, yc,
      od[0], od[1], od[2], ld[0], ld[1], ld[2], wg, bg, wb, wo)


HG_TS = 256
HG_NCH = HG_TS // A_CHUNK
HG_MID = A_CHUNK // 2


def _hgrn_kernel(q_ref, f_ref, v_ref, lb_ref, o_ref, st_ref):
    d = pl.program_id(0)
    sgn = 1 - 2 * d

    @pl.when(pl.program_id(2) == 0)
    def _():
        st_ref[...] = jnp.zeros_like(st_ref)

    lb = lb_ref[...]
    row = lax.broadcasted_iota(jnp.int32, (A_CHUNK, A_CHUNK), 0)
    col = lax.broadcasted_iota(jnp.int32, (A_CHUNK, A_CHUNK), 1)
    cum = jnp.where((row - col) * sgn >= 0, 1.0, 0.0).astype(F32)
    srow = lax.broadcasted_iota(jnp.int32, (A_HEADS * A_CHUNK, A_CHUNK), 0) % A_CHUNK
    scol = lax.broadcasted_iota(jnp.int32, (A_HEADS * A_CHUNK, A_CHUNK), 1)
    causal = (srow - scol) * sgn >= 0
    lane_head = lax.broadcasted_iota(jnp.int32, (A_CHUNK, D_BRANCH), 1) // A_DK
    hmask = [(lane_head == hh).astype(F32) for hh in range(A_HEADS)]
    blk_r = lax.broadcasted_iota(jnp.int32, (D_BRANCH, D_BRANCH), 0) // A_DK
    blk_c = lax.broadcasted_iota(jnp.int32, (D_BRANCH, D_BRANCH), 1) // A_DK
    blockdiag = (blk_r == blk_c).astype(F32)

    for j in range(HG_NCH):
        cj = j + d * (HG_NCH - 1 - 2 * j)
        off = pl.multiple_of(cj * A_CHUNK, A_CHUNK)
        q = q_ref[pl.ds(off, A_CHUNK), :]
        fl = f_ref[pl.ds(off, A_CHUNK), :]
        v = v_ref[pl.ds(off, A_CHUNK), :]
        f = lb + (1.0 - lb) * _sigmoid(fl)
        g = jnp.log(jnp.maximum(f, TINY))
        kk = (1.0 - lb) * _sigmoid(-fl)
        b = jnp.dot(cum, g, precision=HI, preferred_element_type=F32)
        bm = b[HG_MID:HG_MID + 1, :]
        blast = jnp.where(d == 0, b[A_CHUNK - 1:A_CHUNK, :], b[0:1, :])
        qt = q * jnp.exp(b - bm)
        kt = (kk * jnp.exp(bm - b)).astype(BF16)
        qs = jnp.concatenate([qt * hmask[hh] for hh in range(A_HEADS)], axis=0).astype(BF16)
        s = lax.dot_general(qs, kt, (((1,), (1,)), ((), ())), preferred_element_type=F32)
        s = jnp.where(causal, s, 0.0).astype(BF16)
        ost = jnp.dot(s, v.astype(BF16), preferred_element_type=F32)
        o = ost[0:A_CHUNK] * hmask[0]
        for hh in range(1, A_HEADS):
            o = o + ost[hh * A_CHUNK:(hh + 1) * A_CHUNK] * hmask[hh]
        st = st_ref[...]
        qe = (q * jnp.exp(b)).astype(BF16)
        o = o + lax.dot_general(qe, st.astype(BF16), (((1,), (1,)), ((), ())), preferred_element_type=F32)
        o_ref[pl.ds(off, A_CHUNK), :] = o
        kh = (kk * jnp.exp(blast - b)).astype(BF16)
        upd = jnp.dot(v.T.astype(BF16), kh, preferred_element_type=F32)
        st_ref[...] = st * jnp.exp(blast) + upd * blockdiag


def _hgrn(pa3, lb):
    bsz, seq, _ = pa3.shape
    nblk = seq // HG_TS

    def tmap(col):
        def m(d, b, i):
            return (b, i + d * (nblk - 1 - 2 * i), col(d))
        return m

    blk = (None, HG_TS, D_BRANCH)
    return pl.pallas_call(
        _hgrn_kernel,
        grid=(2, bsz, nblk),
        in_specs=[pl.BlockSpec(blk, tmap(lambda d: 0)),
                  pl.BlockSpec(blk, tmap(lambda d: 1 + d)),
                  pl.BlockSpec(blk, tmap(lambda d: 3)),
                  _const_spec((1, D_BRANCH))],
        out_specs=pl.BlockSpec((None, None, HG_TS, D_BRANCH),
                               lambda d, b, i: (d, b, i + d * (nblk - 1 - 2 * i), 0)),
        out_shape=jax.ShapeDtypeStruct((2, bsz, seq, D_BRANCH), F32),
        scratch_shapes=[pltpu.VMEM((D_BRANCH, D_BRANCH), F32)],
        compiler_params=_cparams("arbitrary", "arbitrary", "arbitrary"),
        name="hgrn2",
    )(pa3, pa3, pa3, lb.reshape(1, D_BRANCH))


RG_TB = 128
RG_PAD = SUBLANES
RG_LEFT = B_CONV // 2


def _shift_rows(x, k, fwd, fill):
    t = x.shape[0]
    row = lax.broadcasted_iota(jnp.int32, x.shape, 0)
    if fwd:
        return jnp.where(row >= k, pltpu.roll(x, k, 0), fill)
    return jnp.where(row < t - k, pltpu.roll(x, t - k, 0), fill)


def _block_scan(a, u, fwd):
    k = 1
    while k < a.shape[0]:
        u = a * _shift_rows(u, k, fwd, 0.0) + u
        a = a * _shift_rows(a, k, fwd, 1.0)
        k *= 2
    return a, u


def _gelu_tanh(x):
    return 0.5 * x * (1.0 + jnp.tanh(math.sqrt(2.0 / math.pi) * (x + 0.044715 * (x * x * x))))


def _rglru_kernel(x_ref, gt_ref, cw_ref, cb_ref, wg_ref, bg_ref, lam_ref, o_ref, xp_ref):
    seq = x_ref.shape[0]
    nblk = seq // RG_TB
    xp_ref[0:RG_PAD, :] = jnp.zeros((RG_PAD, D_BRANCH), F32)
    xp_ref[RG_PAD + seq:2 * RG_PAD + seq, :] = jnp.zeros((RG_PAD, D_BRANCH), F32)
    xp_ref[RG_PAD:RG_PAD + seq, :] = x_ref[...]
    nl = -lam_ref[...]
    sp = jnp.maximum(nl, 0.0) + jnp.log(1.0 + jnp.exp(-jnp.abs(nl)))

    def block(i, carry, dirn):
        r0 = pl.multiple_of(i * RG_TB, RG_TB)
        win = xp_ref[pl.ds(r0, RG_TB + 2 * RG_PAD), :]
        xc = cb_ref[...]
        for j in range(B_CONV):
            s0 = RG_PAD + j - RG_LEFT
            xc = xc + cw_ref[j:j + 1, :] * win[s0:s0 + RG_TB, :]
        w = wg_ref[:, dirn * 2 * D_BRANCH:(dirn + 1) * 2 * D_BRANCH]
        gates = jnp.dot(xc, w, precision=HI, preferred_element_type=F32) \
            + bg_ref[:, dirn * 2 * D_BRANCH:(dirn + 1) * 2 * D_BRANCH]
        r = _sigmoid(gates[:, :D_BRANCH])
        ig = _sigmoid(gates[:, D_BRANCH:])
        log_a = -LRU_C * r * sp[dirn:dirn + 1, :]
        a = jnp.exp(log_a)
        u = jnp.sqrt(jnp.maximum(-jnp.tanh(log_a) * (a * a + 1.0), 0.0)) * ig * xc
        acum, hloc = _block_scan(a, u, dirn == 0)
        h = hloc + acum * carry
        if dirn == 0:
            o_ref[pl.ds(r0, RG_TB), :] = h
            return h[RG_TB - 1:RG_TB, :]
        o_ref[pl.ds(r0, RG_TB), :] = (o_ref[pl.ds(r0, RG_TB), :] + h) * _gelu_tanh(gt_ref[pl.ds(r0, RG_TB), :])
        return h[0:1, :]

    zero = jnp.zeros((1, D_BRANCH), F32)
    lax.fori_loop(0, nblk, lambda i, c: block(i, c, 0), zero)
    lax.fori_loop(0, nblk, lambda i, c: block(nblk - 1 - i, c, 1), zero)


def _blockdiag(w):
    eye = jnp.eye(B_BLOCKS, dtype=w.dtype)
    return jnp.einsum('ncd,nm->ncmd', w, eye).reshape(D_BRANCH, D_BRANCH)


def _rglru(pb3, cw, cb, wa, ba, wx, bx, lam):
    bsz, seq, _ = pb3.shape
    wg = jnp.concatenate([_blockdiag(wa[0]), _blockdiag(wx[0]), _blockdiag(wa[1]), _blockdiag(wx[1])], axis=1)
    bg = jnp.concatenate([ba[0], bx[0], ba[1], bx[1]]).reshape(1, 4 * D_BRANCH)
    blk = (None, seq, D_BRANCH)
    return pl.pallas_call(
        _rglru_kernel,
        grid=(bsz,),
        in_specs=[pl.BlockSpec(blk, lambda b: (b, 0, 0)),
                  pl.BlockSpec(blk, lambda b: (b, 0, 1)),
                  _const_spec((B_CONV, D_BRANCH)), _const_spec((1, D_BRANCH)),
                  _const_spec((D_BRANCH, 4 * D_BRANCH)), _const_spec((1, 4 * D_BRANCH)),
                  _const_spec((2, D_BRANCH))],
        out_specs=pl.BlockSpec(blk, lambda b: (b, 0, 0)),
        out_shape=jax.ShapeDtypeStruct((bsz, seq, D_BRANCH), F32),
        scratch_shapes=[pltpu.VMEM((seq + 2 * RG_PAD, D_BRANCH), F32)],
        compiler_params=_cparams("parallel"),
        name="rglru",
    )(pb3, pb3, cw, cb.reshape(1, D_BRANCH), wg, bg, lam)


FFT_IN = 128
FFT_NT = FFT_IN // SUBLANES
HY_TB = 256
HY_PAD = SUBLANES


def _split_np(a):
    a = np.asarray(a, np.float32)
    hi = a.astype(jnp.bfloat16)
    lo = (a - hi.astype(np.float32)).astype(jnp.bfloat16)
    return jnp.asarray(hi), jnp.asarray(lo)


def _dot3(mh, ml, x):
    xh = x.astype(BF16)
    xl = (x - xh.astype(F32)).astype(BF16)
    return (jnp.dot(mh, xh, preferred_element_type=F32) + jnp.dot(mh, xl, preferred_element_type=F32)
            + jnp.dot(ml, xh, preferred_element_type=F32))


def _fft_tables(seq):
    n = 2 * seq
    n1 = n // FFT_IN
    half = n1 // 2
    eye = np.eye(SUBLANES)
    a = 2.0 * np.pi * np.outer(np.arange(n1), np.arange(half)) / n1
    gr, gi = np.cos(a), -np.sin(a)
    blk = np.stack([np.stack([gr, -gi], axis=1), np.stack([gi, gr], axis=1)], axis=0)
    m_out = np.einsum('rkis,cd->rkcisd', blk, eye).reshape(2 * n1 * SUBLANES, 2 * half * SUBLANES)
    ir, ii = gr.T / n, -gi.T / n
    blk = np.stack([np.stack([ir, -ii], axis=1), np.stack([ii, ir], axis=1)], axis=0)
    m_inv = np.einsum('otrk,cd->otcrkd', blk, eye).reshape(2 * half * SUBLANES, 2 * n1 * SUBLANES)
    a = 2.0 * np.pi * np.outer(np.arange(FFT_IN), np.arange(FFT_IN)) / FFT_IN
    fr, fi = np.cos(a), -np.sin(a)
    w_fwd = np.block([[fr, -fi], [fi, fr]])
    w_inv = np.block([[fr, fi], [-fi, fr]])
    s_in = SUBLANES * np.arange(FFT_NT)[:, None, None] + np.arange(SUBLANES)[None, None, :]
    th = 2.0 * np.pi * s_in * np.arange(n1)[None, :, None] / n
    tw = (jnp.asarray(np.cos(th)[..., None], F32), jnp.asarray(-np.sin(th)[..., None], F32))
    return dict(m_out=_split_np(m_out), m_inv=_split_np(m_inv), w_fwd=_split_np(w_fwd), w_inv=_split_np(w_inv),
                tw=tw, n1=n1, half=half)


def _tw_spec(n1):
    return pl.BlockSpec((None, n1, SUBLANES, 1), lambda t, q: (t, 0, 0, 0))


def _fft_outer_kernel(mh_ref, ml_ref, twr_ref, twi_ref, z_ref, v_ref, *, rows_in):
    n1 = v_ref.shape[0]
    z = z_ref[...].reshape(rows_in, D_BRANCH)
    v = _dot3(mh_ref[:, :rows_in], ml_ref[:, :rows_in], z)
    vr = v[:n1 * SUBLANES].reshape(n1, SUBLANES, D_BRANCH)
    vi = v[n1 * SUBLANES:].reshape(n1, SUBLANES, D_BRANCH)
    tr, ti = twr_ref[...], twi_ref[...]
    v_ref[:, 0] = vr * tr - vi * ti
    v_ref[:, 1] = vr * ti + vi * tr


def _fft_outer(tb, z, complex_in):
    n1, half = tb["n1"], tb["half"]
    if complex_in:
        p = z.shape[1]
        zspec = pl.BlockSpec((2, None, half, None, SUBLANES, D_BRANCH), lambda t, q: (0, q, 0, t, 0, 0))
    else:
        p = z.shape[0]
        zspec = pl.BlockSpec((None, half, None, SUBLANES, D_BRANCH), lambda t, q: (q, 0, t, 0, 0))
    rows_in = (2 if complex_in else 1) * half * SUBLANES
    mshape = (2 * n1 * SUBLANES, 2 * half * SUBLANES)
    return pl.pallas_call(
        functools.partial(_fft_outer_kernel, rows_in=rows_in),
        grid=(FFT_NT, p),
        in_specs=[_const_spec(mshape), _const_spec(mshape), _tw_spec(n1), _tw_spec(n1), zspec],
        out_specs=pl.BlockSpec((None, n1, 2, None, SUBLANES, D_BRANCH), lambda t, q: (q, 0, 0, t, 0, 0)),
        out_shape=jax.ShapeDtypeStruct((p, n1, 2, FFT_NT, SUBLANES, D_BRANCH), F32),
        compiler_params=_cparams("arbitrary", "arbitrary"),
        name="fft_outer_c" if complex_in else "fft_outer_r",
    )(*tb["m_out"], *tb["tw"], z)


def _fft_inner_kernel(wh_ref, wl_ref, v_ref, h_ref):
    v = v_ref[...].reshape(2 * FFT_IN, D_BRANCH)
    h_ref[...] = _dot3(wh_ref[...], wl_ref[...], v).reshape(h_ref.shape)


def _fft_inner(tb, v):
    p, n1 = v.shape[:2]
    wspec = _const_spec((2 * FFT_IN, 2 * FFT_IN))
    return pl.pallas_call(
        _fft_inner_kernel,
        grid=(p, n1),
        in_specs=[wspec, wspec,
                  pl.BlockSpec((None, None, 2, FFT_NT, SUBLANES, D_BRANCH), lambda q, k: (q, k, 0, 0, 0, 0))],
        out_specs=pl.BlockSpec((None, None, 2, FFT_IN, D_BRANCH), lambda q, k: (q, k, 0, 0, 0)),
        out_shape=jax.ShapeDtypeStruct((p, n1, 2, FFT_IN, D_BRANCH), F32),
        compiler_params=_cparams("parallel", "parallel"),
        name="fft_inner",
    )(*tb["w_fwd"], v)


def _fft_mid_kernel(wfh_ref, wfl_ref, wih_ref, wil_ref, v_ref, hf_ref, hb_ref, d_ref):
    v = v_ref[...].reshape(2 * FFT_IN, D_BRANCH)
    z = _dot3(wfh_ref[...], wfl_ref[...], v)
    zr, zi = z[:FFT_IN], z[FFT_IN:]
    hr = hf_ref[0] + hb_ref[0]
    hi = hf_ref[1] - hb_ref[1]
    pr = zr * hr - zi * hi
    pi = zr * hi + zi * hr
    d = _dot3(wih_ref[...], wil_ref[...], jnp.concatenate([pr, pi], axis=0))
    d_ref[...] = d.reshape(d_ref.shape)


def _fft_mid(tb, v, hspec, order):
    p, n1 = v.shape[:2]
    vspec = pl.BlockSpec((None, None, 2, FFT_NT, SUBLANES, D_BRANCH), lambda q, k: (q, k, 0, 0, 0, 0))
    hblk = (None, None, 2, FFT_IN, D_BRANCH)
    wspec = _const_spec((2 * FFT_IN, 2 * FFT_IN))
    return pl.pallas_call(
        _fft_mid_kernel,
        grid=(p, n1),
        in_specs=[wspec, wspec, wspec, wspec, vspec,
                  pl.BlockSpec(hblk, lambda q, k: (2 * order, k, 0, 0, 0)),
                  pl.BlockSpec(hblk, lambda q, k: (2 * order + 1, k, 0, 0, 0))],
        out_specs=vspec,
        out_shape=jax.ShapeDtypeStruct(v.shape, F32),
        compiler_params=_cparams("parallel", "parallel"),
        name="fft_mid",
    )(*tb["w_fwd"], *tb["w_inv"], v, hspec, hspec)


def _ifft_outer_kernel(mh_ref, ml_ref, twr_ref, twi_ref, d_ref, u_ref, x_ref, b_ref, o_ref):
    n1 = d_ref.shape[0]
    dr, di = d_ref[:, 0], d_ref[:, 1]
    tr, ti = twr_ref[...], twi_ref[...]
    er = (dr * tr + di * ti).reshape(n1 * SUBLANES, D_BRANCH)
    ei = (di * tr - dr * ti).reshape(n1 * SUBLANES, D_BRANCH)
    y = _dot3(mh_ref[...], ml_ref[...], jnp.concatenate([er, ei], axis=0)).reshape(o_ref.shape)
    o_ref[...] = x_ref[...] * (y + u_ref[...] * b_ref[...])


def _ifft_outer(tb, d, u, ucol, x, xcol, bias):
    n1, half = tb["n1"], tb["half"]
    p = d.shape[0]
    mshape = (2 * half * SUBLANES, 2 * n1 * SUBLANES)
    io = lambda col: pl.BlockSpec((2, None, half, None, SUBLANES, D_BRANCH), lambda t, q: (0, q, 0, t, 0, col))
    return pl.pallas_call(
        _ifft_outer_kernel,
        grid=(FFT_NT, p),
        in_specs=[_const_spec(mshape), _const_spec(mshape), _tw_spec(n1), _tw_spec(n1),
                  pl.BlockSpec((None, n1, 2, None, SUBLANES, D_BRANCH), lambda t, q: (q, 0, 0, t, 0, 0)),
                  io(ucol), io(xcol), _const_spec((1, D_BRANCH))],
        out_specs=io(0),
        out_shape=jax.ShapeDtypeStruct((2, p, half, FFT_NT, SUBLANES, D_BRANCH), F32),
        compiler_params=_cparams("arbitrary", "arbitrary"),
        name="ifft_outer",
    )(*tb["m_inv"], *tb["tw"], d, u, x, bias.reshape(1, D_BRANCH))


def _hyconv_kernel(x_ref, w_ref, b_ref, o_ref, xp_ref):
    seq = x_ref.shape[0]
    xp_ref[0:HY_PAD, :] = jnp.zeros((HY_PAD, D_BRANCH), F32)
    xp_ref[HY_PAD + seq:2 * HY_PAD + seq, :] = jnp.zeros((HY_PAD, D_BRANCH), F32)
    xp_ref[HY_PAD:HY_PAD + seq, :] = x_ref[...]

    def body(i, carry):
        r0 = pl.multiple_of(i * HY_TB, HY_TB)
        win = xp_ref[pl.ds(r0, HY_TB + 2 * HY_PAD), :]
        y = b_ref[...]
        for j in range(C_CONV):
            s0 = HY_PAD + j - C_CONV // 2
            y = y + w_ref[j:j + 1, :] * win[s0:s0 + HY_TB, :]
        o_ref[pl.ds(r0, HY_TB), :] = y
        return carry

    lax.fori_loop(0, seq // HY_TB, body, 0)


def _hyconv(pc3, cw, cb):
    bsz, seq, width = pc3.shape
    nb = width // D_BRANCH
    blk = pl.BlockSpec((None, seq, D_BRANCH), lambda b, j: (b, 0, j))
    return pl.pallas_call(
        _hyconv_kernel,
        grid=(bsz, nb),
        in_specs=[blk, pl.BlockSpec((C_CONV, D_BRANCH), lambda b, j: (0, j)),
                  pl.BlockSpec((1, D_BRANCH), lambda b, j: (0, j))],
        out_specs=blk,
        out_shape=jax.ShapeDtypeStruct(pc3.shape, F32),
        scratch_shapes=[pltpu.VMEM((seq + 2 * HY_PAD, D_BRANCH), F32)],
        compiler_params=_cparams("parallel", "parallel"),
        name="hyconv",
    )(pc3, cw, cb.reshape(1, width))


def _hyfilt_kernel(z_ref, w1_ref, b1_ref, fr_ref, w2_ref, b2_ref, w3_ref, dec_ref, o_ref):
    seq = z_ref.shape[0]
    nblk = seq // HY_TB
    fr = fr_ref[...]

    def body(i, ss):
        r0 = pl.multiple_of(i * HY_TB, HY_TB)
        zb = z_ref[pl.ds(r0, HY_TB), :]
        h = jnp.sin(fr * (jnp.dot(zb, w1_ref[...], precision=HI, preferred_element_type=F32) + b1_ref[...]))
        h = jnp.sin(fr * (jnp.dot(h, w2_ref[...], precision=HI, preferred_element_type=F32) + b2_ref[...]))
        hf = jnp.dot(h, w3_ref[...], precision=HI, preferred_element_type=F32)
        hf = hf * jnp.exp(-zb[:, 0:1] * dec_ref[...])
        o_ref[pl.ds(r0, HY_TB), :] = hf
        return ss + jnp.sum(hf * hf, axis=0, keepdims=True)

    ss = lax.fori_loop(0, nblk, body, jnp.zeros((1, D_BRANCH), F32))
    scale = lax.rsqrt(ss + EPS)

    def norm(i, carry):
        r0 = pl.multiple_of(i * HY_TB, HY_TB)
        o_ref[pl.ds(r0, HY_TB), :] = o_ref[pl.ds(r0, HY_TB), :] * scale
        return carry

    lax.fori_loop(0, nblk, norm, 0)


def _hyfilt(seq, w1, b1, freq, w2, b2, w3):
    t = jnp.linspace(0.0, 1.0, seq, dtype=F32)[:, None]
    bands = (C_EMB - 1) // 2
    w = 2.0 * math.pi * jnp.arange(seq, dtype=F32)[:, None] / seq
    fr = jnp.linspace(1e-4, bands - 1, bands, dtype=F32)[None]
    z = jnp.concatenate([t, jnp.cos(fr * w), -jnp.sin(fr * w)], axis=-1)
    z = jnp.pad(z, ((0, 0), (0, LANES - C_EMB)))
    padm = lambda a, r, c: jnp.pad(a.astype(F32), ((0, r - a.shape[0]), (0, c - a.shape[1])))
    row = lambda a: padm(a.reshape(1, -1), 1, LANES)
    dec = jnp.abs(jnp.linspace(C_MIN_DECAY, C_MAX_DECAY, D_BRANCH, dtype=F32)).reshape(1, D_BRANCH)
    nset = C_ORDER * 2
    return pl.pallas_call(
        _hyfilt_kernel,
        grid=(nset,),
        in_specs=[_const_spec((seq, LANES)), _const_spec((LANES, LANES)), _const_spec((1, LANES)),
                  _const_spec((1, LANES)), _const_spec((LANES, LANES)), _const_spec((1, LANES)),
                  pl.BlockSpec((LANES, D_BRANCH), lambda j: (0, j)), _const_spec((1, D_BRANCH))],
        out_specs=pl.BlockSpec((None, seq, D_BRANCH), lambda j: (j, 0, 0)),
        out_shape=jax.ShapeDtypeStruct((nset, seq, D_BRANCH), F32),
        compiler_params=_cparams("parallel"),
        name="hyfilt",
    )(z, padm(w1, LANES, LANES), row(b1), row(freq), padm(w2, LANES, LANES), row(b2),
      padm(w3, LANES, nset * D_BRANCH), dec)


def _hyena(pc3, cw, cb, w1, b1, freq, w2, b2, w3, bias):
    bsz, seq, width = pc3.shape
    tb = _fft_tables(seq)
    half = tb["half"]
    npair = bsz // 2
    uc = _hyconv(pc3, cw, cb)
    uc6 = uc.reshape(2, npair, half, FFT_NT, SUBLANES, width)
    filt = _hyfilt(seq, w1, b1, freq, w2, b2, w3).reshape(C_ORDER * 2, half, FFT_NT, SUBLANES, D_BRANCH)
    hspec = _fft_inner(tb, _fft_outer(tb, filt, False))
    src = uc6
    for order in range(C_ORDER):
        d = _fft_mid(tb, _fft_outer(tb, src, True), hspec, order)
        src = _ifft_outer(tb, d, src, 0, uc6, order + 1, bias[order])
    return src.reshape(bsz, seq, D_BRANCH)


AT_HALF = 64
AT_TQ = 256


def _t5_bucket(rel):
    half = N_BUCKETS // 2
    max_exact = half // 2
    n = np.abs(rel)
    large = max_exact + (np.log(np.maximum(n, 1) / max_exact) / math.log(MAX_DISTANCE / max_exact)
                         * (half - max_exact)).astype(np.int64)
    large = np.minimum(large, half - 1)
    return (rel > 0).astype(np.int64) * half + np.where(n < max_exact, n, large)


def _attn_geometry(n):
    tq = min(AT_TQ, n)
    win = min(tq + 2 * AT_HALF, n)
    return tq, win, n // tq


def _attn_bias_tables(rel_bias, g, dil, n):
    tq, win, nq = _attn_geometry(n)
    hs = slice(g * D_HEADS_PER_GROUP, (g + 1) * D_HEADS_PER_GROUP)
    offsets = np.arange(-AT_HALF, AT_HALF + 1) * dil
    onehot = np.zeros((2 * AT_HALF + 1, N_BUCKETS), np.float32)
    onehot[np.arange(2 * AT_HALF + 1), _t5_bucket(offsets)] = 1.0
    band = jnp.dot(rel_bias.astype(F32)[:, hs].T, jnp.asarray(onehot).T, precision=HI)
    nband = 2 * AT_HALF + 1
    lv = tq + win - 1
    tables = []
    for i in sorted({0, min(1, nq - 1), nq - 1}):
        ws = int(np.clip(i * tq - AT_HALF, 0, n - win))
        lo = (tq - 1) - (ws - i * tq) - AT_HALF
        v = jnp.pad(band, ((0, 0), (lo, lv - lo - nband)), constant_values=NEG_BIG)
        flat = jnp.tile(v, (1, tq + 1))[:, tq - 1:tq - 1 + tq * (lv - 1)]
        tables.append(flat.reshape(D_HEADS_PER_GROUP, tq, lv - 1)[:, :, :win])
    return jnp.stack(tables)


def _attn_kernel(q_ref, k_ref, v_ref, bias_ref, o_ref, l_ref, *, n, dil):
    tq, win, _ = _attn_geometry(n)
    hp = pl.program_id(1)
    i = pl.program_id(2)
    ws = pl.multiple_of(jnp.clip(i * tq - AT_HALF, 0, n - win), AT_HALF)
    lane_head = lax.broadcasted_iota(jnp.int32, (tq, LANES), 1) // D_HEAD_DIM
    for r in range(dil):
        q = q_ref[pl.ds(r, tq, stride=dil), :] * (D_HEAD_DIM ** -0.5)
        kw = k_ref[pl.ds(ws * dil + r, win, stride=dil), :].astype(BF16)
        vw = v_ref[pl.ds(ws * dil + r, win, stride=dil), :].astype(BF16)
        o_acc = jnp.zeros((tq, LANES), F32)
        l_acc = jnp.zeros((tq, LANES), F32)
        for hh in range(LANES // D_HEAD_DIM):
            hm = lane_head == hh
            qh = jnp.where(hm, q, 0.0).astype(BF16)
            s = lax.dot_general(qh, kw, (((1,), (1,)), ((), ())), preferred_element_type=F32)
            s = s + bias_ref[hp * (LANES // D_HEAD_DIM) + hh]
            m = jnp.max(s, axis=-1, keepdims=True)
            p = jnp.exp(s - m)
            l = jnp.sum(p, axis=-1, keepdims=True)
            oh = jnp.dot(p.astype(BF16), vw, preferred_element_type=F32) / l
            o_acc = jnp.where(hm, oh, o_acc)
            l_acc = jnp.where(hm, m + jnp.log(l), l_acc)
        o_ref[pl.ds(r, tq, stride=dil), :] = o_acc
        l_ref[pl.ds(r, tq, stride=dil), :] = l_acc


def _banded_attention(pd3, g, dil, bias):
    bsz, seq, width = pd3.shape
    n = seq // dil
    tq, win, nq = _attn_geometry(n)
    ncase = bias.shape[0]
    rows = dil * tq
    per = D_BRANCH // LANES
    third = width // 3 // LANES

    def case(i):
        return jnp.minimum(jnp.where(i == nq - 1, ncase - 1, jnp.minimum(i, 1)), ncase - 1)

    out_spec = pl.BlockSpec((None, rows, LANES), lambda b, h, i: (b, i, h))
    shp = jax.ShapeDtypeStruct((bsz, seq, D_BRANCH), F32)
    return pl.pallas_call(
        functools.partial(_attn_kernel, n=n, dil=dil),
        grid=(bsz, per, nq),
        in_specs=[pl.BlockSpec((None, rows, LANES), lambda b, h, i: (b, i, g * per + h)),
                  pl.BlockSpec((None, seq, LANES), lambda b, h, i: (b, 0, third + g * per + h)),
                  pl.BlockSpec((None, seq, LANES), lambda b, h, i: (b, 0, 2 * third + g * per + h)),
                  pl.BlockSpec((None, D_HEADS_PER_GROUP, tq, win), lambda b, h, i: (case(i), 0, 0, 0))],
        out_specs=[out_spec, out_spec],
        out_shape=[shp, shp],
        compiler_params=_cparams("parallel", "parallel", "arbitrary"),
        name=f"attn_d{dil}",
    )(pd3, pd3, pd3, bias)


def _dilated_attention(pd3, rel_bias):
    bsz, seq, _ = pd3.shape
    outs, lses = [], []
    for g, (_, dil) in enumerate(D_GROUPS):
        o, l = _banded_attention(pd3, g, dil, _attn_bias_tables(rel_bias, g, dil, seq // dil))
        outs.append(o.reshape(bsz * seq, D_BRANCH))
        lses.append(l.reshape(bsz * seq, D_BRANCH))
    return outs, lses


def kernel(x, norm1_g, w_in, hgrn_lb_logits, hgrn_norm_g, lru_conv_w, lru_conv_b, lru_wa, lru_ba, lru_wx, lru_bx,
           lru_lambda, hy_conv_w, hy_conv_b, hy_w1, hy_b1, hy_freq, hy_w2, hy_b2, hy_w3, hy_bias, rel_bias,
           w_branch, w_gate, b_gate, w_out, norm2_g, w_ff1, w_ff3, w_ff2, final_g):
    bsz, seq, _ = x.shape
    n = bsz * seq
    lb_soft = jax.nn.softmax(hgrn_lb_logits.astype(F32), axis=0)
    lower_bounds = jnp.cumsum(lb_soft, axis=0) - lb_soft[0]
    x2 = x.reshape(n, D_MODEL)
    flat = lambda a: a.reshape(n, D_BRANCH)
    for l in range(DEPTH):
        pa, pb, pc, pd = _inproj(x2, norm1_g[l], w_in[l].astype(BF16))
        oa = _hgrn(pa.reshape(bsz, seq, IN_A), lower_bounds[l])
        yb = _rglru(pb.reshape(bsz, seq, IN_B), lru_conv_w[l], lru_conv_b[l], lru_wa[l], lru_ba[l],
                    lru_wx[l], lru_bx[l], lru_lambda[l])
        yc = _hyena(pc.reshape(bsz, seq, IN_C), hy_conv_w[l], hy_conv_b[l], hy_w1[l], hy_b1[l], hy_freq[l],
                    hy_w2[l], hy_b2[l], hy_w3[l], hy_bias[l])
        od, ld = _dilated_attention(pd.reshape(bsz, seq, IN_D), rel_bias)
        x2 = _merge(x2, norm1_g[l], oa.reshape(2, n, D_BRANCH), pa, hgrn_norm_g[l], flat(yb), flat(yc), od, ld,
                    w_gate[l].reshape(D_MODEL, N_BRANCH * D_MODEL).astype(BF16),
                    b_gate[l].reshape(1, N_BRANCH * D_MODEL), w_branch[l].astype(BF16), w_out[l].astype(BF16))
        x2 = _ffn(x2, norm2_g[l], *_ffn_weights(w_ff1[l], w_ff3[l], w_ff2[l]), final_g, l == DEPTH - 1)
    return x2.reshape(bsz, seq, D_MODEL)
```

```python
import functools
import math

import jax
import jax.numpy as jnp
import numpy as np
from jax import lax
from jax.experimental import pallas as pl
from jax.experimental.pallas import tpu as pltpu

F32 = jnp.float32
BF16 = jnp.bfloat16
HI = lax.Precision.HIGHEST

D_MODEL = 1024
DEPTH = 2
EPS = 1e-6
TINY = 1e-30
N_BRANCH = 4
D_BRANCH = 256
A_HEADS = 4
A_DK = 64
A_CHUNK = 64
B_BLOCKS = 4
B_BW = 64
B_CONV = 4
LRU_C = 8.0
C_ORDER = 2
C_CONV = 3
C_EMB = 33
C_HID = 64
C_MIN_DECAY = math.log(1e-2) / 1.5
C_MAX_DECAY = math.log(1e-2) / 0.3
D_GROUPS = ((128, 1), (512, 4), (2048, 16))
D_HEADS_PER_GROUP = 4
D_HEAD_DIM = 64
D_N_HEADS = 12
D_QKV = 768
N_BUCKETS = 32
MAX_DISTANCE = 1024
NEG_BIG = -1e30
D_FF = 2816
IN_A = 5 * D_BRANCH
IN_B = 2 * D_BRANCH
IN_C = 3 * D_BRANCH
IN_D = 3 * D_QKV
IN_WIDTH = IN_A + IN_B + IN_C + IN_D

LANES = 128
SUBLANES = 8
VMEM_LIMIT = 56 * 1024 * 1024


def _cparams(*sem):
    return pltpu.CompilerParams(dimension_semantics=sem, vmem_limit_bytes=VMEM_LIMIT)


def _const_spec(shape):
    nd = len(shape)
    return pl.BlockSpec(shape, lambda *_: (0,) * nd, pipeline_mode=pl.Buffered(1))


def _rms(x, g):
    return x * lax.rsqrt(jnp.mean(x * x, axis=-1, keepdims=True) + EPS) * g


def _sigmoid(x):
    return 1.0 / (1.0 + jnp.exp(-x))


IN_TM = 512
IN_CHUNK = 256


def _inproj_kernel(x_ref, g_ref, w_ref, oa_ref, ob_ref, oc_ref, od_ref):
    h = _rms(x_ref[...], g_ref[...]).astype(BF16)
    off = 0
    for o_ref in (oa_ref, ob_ref, oc_ref, od_ref):
        width = o_ref.shape[-1]
        for c in range(0, width, IN_CHUNK):
            o_ref[:, c:c + IN_CHUNK] = jnp.dot(h, w_ref[:, off + c:off + c + IN_CHUNK],
                                               preferred_element_type=F32)
        off += width


def _inproj(x2, g, w_bf16):
    n = x2.shape[0]
    widths = (IN_A, IN_B, IN_C, IN_D)
    return pl.pallas_call(
        _inproj_kernel,
        grid=(n // IN_TM,),
        in_specs=[pl.BlockSpec((IN_TM, D_MODEL), lambda i: (i, 0)),
                  _const_spec((1, D_MODEL)),
                  _const_spec((D_MODEL, IN_WIDTH))],
        out_specs=[pl.BlockSpec((IN_TM, w), lambda i: (i, 0)) for w in widths],
        out_shape=[jax.ShapeDtypeStruct((n, w), F32) for w in widths],
        compiler_params=_cparams("parallel"),
        name="inproj",
    )(x2, g.reshape(1, D_MODEL), w_bf16)


FF_TM = 512
FF_CHUNK = 256
FF_NCHUNK = D_FF // FF_CHUNK


def _ffn_kernel(x_ref, g_ref, w1_ref, w3_ref, w2_ref, fg_ref, o_ref, acc_ref, *, final):
    x = x_ref[...]
    h = _rms(x, g_ref[...]).astype(BF16)
    acc_ref[...] = x

    def body(c, carry):
        a = jnp.dot(h, w1_ref[c], preferred_element_type=F32)
        b = jnp.dot(h, w3_ref[c], preferred_element_type=F32)
        t = (a * _sigmoid(a) * b).astype(BF16)
        acc_ref[...] += jnp.dot(t, w2_ref[c], preferred_element_type=F32)
        return carry

    lax.fori_loop(0, FF_NCHUNK, body, 0)
    y = acc_ref[...]
    if final:
        y = _rms(y, fg_ref[...])
    o_ref[...] = y


def _ffn(x2, g, w1c, w3c, w2c, final_g, final):
    n = x2.shape[0]
    return pl.pallas_call(
        functools.partial(_ffn_kernel, final=final),
        grid=(n // FF_TM,),
        in_specs=[pl.BlockSpec((FF_TM, D_MODEL), lambda i: (i, 0)),
                  _const_spec((1, D_MODEL)),
                  _const_spec((FF_NCHUNK, D_MODEL, FF_CHUNK)),
                  _const_spec((FF_NCHUNK, D_MODEL, FF_CHUNK)),
                  _const_spec((FF_NCHUNK, FF_CHUNK, D_MODEL)),
                  _const_spec((1, D_MODEL))],
        out_specs=pl.BlockSpec((FF_TM, D_MODEL), lambda i: (i, 0)),
        out_shape=jax.ShapeDtypeStruct((n, D_MODEL), F32),
        scratch_shapes=[pltpu.VMEM((FF_TM, D_MODEL), F32)],
        compiler_params=_cparams("parallel"),
        name="ffn_final" if final else "ffn",
    )(x2, g.reshape(1, D_MODEL), w1c, w3c, w2c, final_g.reshape(1, D_MODEL))


def _ffn_weights(w1, w3, w2):
    w1c = w1.astype(BF16).reshape(D_MODEL, FF_NCHUNK, FF_CHUNK).transpose(1, 0, 2)
    w3c = w3.astype(BF16).reshape(D_MODEL, FF_NCHUNK, FF_CHUNK).transpose(1, 0, 2)
    w2c = w2.astype(BF16).reshape(FF_NCHUNK, FF_CHUNK, D_MODEL)
    return w1c, w3c, w2c


MG_TM = 512


def _head_ones():
    r = np.arange(D_BRANCH)[:, None] // A_DK
    c = np.arange(D_BRANCH)[None, :] // A_DK
    return jnp.asarray((r == c).astype(np.float32) / A_DK)


def _merge_kernel(x_ref, g1_ref, oaf_ref, oab_ref, ga_ref, hg_ref, hm_ref, yb_ref, yc_ref,
                  o0_ref, o1_ref, o2_ref, l0_ref, l1_ref, l2_ref,
                  wg_ref, bg_ref, wb_ref, wo_ref, out_ref):
    x = x_ref[...]
    h = _rms(x, g1_ref[...]).astype(BF16)
    oa = oaf_ref[...] + oab_ref[...]
    ms = jnp.dot(oa * oa, hm_ref[...], precision=HI, preferred_element_type=F32)
    ga = ga_ref[...]
    ya = oa * lax.rsqrt(ms + EPS) * hg_ref[...] * (ga * _sigmoid(ga))
    l0, l1, l2 = l0_ref[...], l1_ref[...], l2_ref[...]
    m = jnp.maximum(jnp.maximum(l0, l1), l2)
    e0, e1, e2 = jnp.exp(l0 - m), jnp.exp(l1 - m), jnp.exp(l2 - m)
    yd = (e0 * o0_ref[...] + e1 * o1_ref[...] + e2 * o2_ref[...]) / (e0 + e1 + e2)
    mixed = None
    for j, y in enumerate((ya, yb_ref[...], yc_ref[...], yd)):
        gate = _sigmoid(jnp.dot(h, wg_ref[:, j * D_MODEL:(j + 1) * D_MODEL], preferred_element_type=F32)
                        + bg_ref[:, j * D_MODEL:(j + 1) * D_MODEL])
        t = gate * jnp.dot(y.astype(BF16), wb_ref[j], preferred_element_type=F32)
        mixed = t if mixed is None else mixed + t
    out_ref[...] = x + jnp.dot(mixed.astype(BF16), wo_ref[...], preferred_element_type=F32)


def _merge(x2, g1, oa_f, oa_b, pa, hg, yb, yc, od, ld, wg, bg, wb, wo):
    n = x2.shape[0]
    tile = lambda w: pl.BlockSpec((MG_TM, w), lambda i: (i, 0))
    return pl.pallas_call(
        _merge_kernel,
        grid=(n // MG_TM,),
        in_specs=[tile(D_MODEL), _const_spec((1, D_MODEL)),
                  tile(D_BRANCH), tile(D_BRANCH),
                  pl.BlockSpec((MG_TM, D_BRANCH), lambda i: (i, 4)),
                  _const_spec((1, D_BRANCH)), _const_spec((D_BRANCH, D_BRANCH)),
                  tile(D_BRANCH), tile(D_BRANCH),
                  tile(D_BRANCH), tile(D_BRANCH), tile(D_BRANCH),
                  tile(D_BRANCH), tile(D_BRANCH), tile(D_BRANCH),
                  _const_spec((D_MODEL, N_BRANCH * D_MODEL)), _const_spec((1, N_BRANCH * D_MODEL)),
                  _const_spec((N_BRANCH, D_BRANCH, D_MODEL)), _const_spec((D_MODEL, D_MODEL))],
        out_specs=tile(D_MODEL),
        out_shape=jax.ShapeDtypeStruct((n, D_MODEL), F32),
        compiler_params=_cparams("parallel"),
        name="merge",
    )(x2, g1.reshape(1, D_MODEL), oa_f, oa_b, pa, hg.reshape(1, D_BRANCH), _head_ones(), yb, yc,
      od[0], od[1], od[2], ld[0], ld[1], ld[2], wg, bg, wb, wo)


HG_TS = 512
HG_NCH = HG_TS // A_CHUNK
HG_MID = A_CHUNK // 2


def _hgrn_chunk(q, fl, v, lb, st, fwd):
    row = lax.broadcasted_iota(jnp.int32, (A_CHUNK, A_CHUNK), 0)
    col = lax.broadcasted_iota(jnp.int32, (A_CHUNK, A_CHUNK), 1)
    cum = ((row >= col) if fwd else (row <= col)).astype(F32)
    srow = lax.broadcasted_iota(jnp.int32, (A_HEADS * A_CHUNK, A_CHUNK), 0) % A_CHUNK
    scol = lax.broadcasted_iota(jnp.int32, (A_HEADS * A_CHUNK, A_CHUNK), 1)
    causal = (srow >= scol) if fwd else (srow <= scol)
    lane_head = lax.broadcasted_iota(jnp.int32, (A_CHUNK, D_BRANCH), 1) // A_DK
    hmask = [(lane_head == hh).astype(F32) for hh in range(A_HEADS)]
    blk_r = lax.broadcasted_iota(jnp.int32, (D_BRANCH, D_BRANCH), 0) // A_DK
    blk_c = lax.broadcasted_iota(jnp.int32, (D_BRANCH, D_BRANCH), 1) // A_DK
    blockdiag = (blk_r == blk_c).astype(F32)

    f = lb + (1.0 - lb) * _sigmoid(fl)
    g = jnp.log(jnp.maximum(f, TINY))
    kk = (1.0 - lb) * _sigmoid(-fl)
    b = jnp.dot(cum, g, precision=HI, preferred_element_type=F32)
    bm = b[HG_MID:HG_MID + 1, :]
    blast = b[A_CHUNK - 1:A_CHUNK, :] if fwd else b[0:1, :]
    qt = q * jnp.exp(b - bm)
    kt = (kk * jnp.exp(bm - b)).astype(BF16)
    qs = jnp.concatenate([qt * hmask[hh] for hh in range(A_HEADS)], axis=0).astype(BF16)
    s = lax.dot_general(qs, kt, (((1,), (1,)), ((), ())), preferred_element_type=F32)
    s = jnp.where(causal, s, 0.0).astype(BF16)
    ost = jnp.dot(s, v.astype(BF16), preferred_element_type=F32)
    o = ost[0:A_CHUNK] * hmask[0]
    for hh in range(1, A_HEADS):
        o = o + ost[hh * A_CHUNK:(hh + 1) * A_CHUNK] * hmask[hh]
    qe = (q * jnp.exp(b)).astype(BF16)
    o = o + lax.dot_general(qe, st.astype(BF16), (((1,), (1,)), ((), ())), preferred_element_type=F32)
    kh = (kk * jnp.exp(blast - b)).astype(BF16)
    upd = jnp.dot(v.T.astype(BF16), kh, preferred_element_type=F32)
    return o, st * jnp.exp(blast) + upd * blockdiag


def _hgrn_kernel(qf_ref, ff_ref, vf_ref, qb_ref, fb_ref, vb_ref, lb_ref, of_ref, ob_ref, st_ref):
    @pl.when(pl.program_id(1) == 0)
    def _():
        st_ref[...] = jnp.zeros_like(st_ref)

    lb = lb_ref[...]

    def body(j, carry):
        offf = pl.multiple_of(j * A_CHUNK, A_CHUNK)
        offb = pl.multiple_of((HG_NCH - 1 - j) * A_CHUNK, A_CHUNK)
        o, st = _hgrn_chunk(qf_ref[pl.ds(offf, A_CHUNK), :], ff_ref[pl.ds(offf, A_CHUNK), :],
                            vf_ref[pl.ds(offf, A_CHUNK), :], lb, st_ref[0], True)
        of_ref[pl.ds(offf, A_CHUNK), :] = o
        st_ref[0] = st
        o, st = _hgrn_chunk(qb_ref[pl.ds(offb, A_CHUNK), :], fb_ref[pl.ds(offb, A_CHUNK), :],
                            vb_ref[pl.ds(offb, A_CHUNK), :], lb, st_ref[1], False)
        ob_ref[pl.ds(offb, A_CHUNK), :] = o
        st_ref[1] = st
        return carry

    lax.fori_loop(0, HG_NCH, body, 0)


def _hgrn(pa3, lb):
    bsz, seq, _ = pa3.shape
    nblk = seq // HG_TS
    blk = (None, HG_TS, D_BRANCH)
    up = lambda col: pl.BlockSpec(blk, lambda b, i: (b, i, col))
    down = lambda col: pl.BlockSpec(blk, lambda b, i: (b, nblk - 1 - i, col))
    shp = jax.ShapeDtypeStruct((bsz, seq, D_BRANCH), F32)
    return pl.pallas_call(
        _hgrn_kernel,
        grid=(bsz, nblk),
        in_specs=[up(0), up(1), up(3), down(0), down(2), down(3), _const_spec((1, D_BRANCH))],
        out_specs=[up(0), down(0)],
        out_shape=[shp, shp],
        scratch_shapes=[pltpu.VMEM((2, D_BRANCH, D_BRANCH), F32)],
        compiler_params=_cparams("parallel", "arbitrary"),
        name="hgrn2",
    )(pa3, pa3, pa3, pa3, pa3, pa3, lb.reshape(1, D_BRANCH))


RG_TB = 128
RG_PAD = SUBLANES
RG_LEFT = B_CONV // 2


def _dot3_rhs(x, wh, wl):
    xh = x.astype(BF16)
    xl = (x - xh.astype(F32)).astype(BF16)
    return (jnp.dot(xh, wh, preferred_element_type=F32) + jnp.dot(xl, wh, preferred_element_type=F32)
            + jnp.dot(xh, wl, preferred_element_type=F32))


def _group_scan(a, u, fwd):
    row = lax.broadcasted_iota(jnp.int32, a.shape, 1)
    k = 1
    while k < SUBLANES:
        if fwd:
            keep = row >= k
            us, as_ = pltpu.roll(u, k, 1), pltpu.roll(a, k, 1)
        else:
            keep = row < SUBLANES - k
            us, as_ = pltpu.roll(u, SUBLANES - k, 1), pltpu.roll(a, SUBLANES - k, 1)
        u = a * jnp.where(keep, us, 0.0) + u
        a = a * jnp.where(keep, as_, 1.0)
        k *= 2
    return a, u


def _block_scan(a, u, carry, fwd):
    t = a.shape[0]
    ngrp = t // SUBLANES
    ag, ug = _group_scan(a.reshape(ngrp, SUBLANES, D_BRANCH), u.reshape(ngrp, SUBLANES, D_BRANCH), fwd)
    hs = [None] * ngrp
    for g in (range(ngrp) if fwd else range(ngrp - 1, -1, -1)):
        h = ug[g] + ag[g] * carry
        hs[g] = h
        carry = h[SUBLANES - 1:SUBLANES, :] if fwd else h[0:1, :]
    return jnp.concatenate(hs, axis=0), carry


def _gelu_tanh(x):
    return 0.5 * x * (1.0 + jnp.tanh(math.sqrt(2.0 / math.pi) * (x + 0.044715 * (x * x * x))))


def _rglru_kernel(x_ref, gt_ref, cw_ref, cb_ref, wh_ref, wl_ref, bg_ref, lam_ref, o_ref, xp_ref, xc_ref):
    seq = x_ref.shape[0]
    nblk = seq // RG_TB
    xp_ref[0:RG_PAD, :] = jnp.zeros((RG_PAD, D_BRANCH), F32)
    xp_ref[RG_PAD + seq:2 * RG_PAD + seq, :] = jnp.zeros((RG_PAD, D_BRANCH), F32)
    xp_ref[RG_PAD:RG_PAD + seq, :] = x_ref[...]
    nl = -lam_ref[...]
    sp = jnp.maximum(nl, 0.0) + jnp.log(1.0 + jnp.exp(-jnp.abs(nl)))

    def block(i, carry, dirn):
        r0 = pl.multiple_of(i * RG_TB, RG_TB)
        if dirn == 0:
            win = xp_ref[pl.ds(r0, RG_TB + 2 * RG_PAD), :]
            xc = cb_ref[...]
            for j in range(B_CONV):
                s0 = RG_PAD + j - RG_LEFT
                xc = xc + cw_ref[j:j + 1, :] * win[s0:s0 + RG_TB, :]
            xc_ref[pl.ds(r0, RG_TB), :] = xc
        else:
            xc = xc_ref[pl.ds(r0, RG_TB), :]
        cols = slice(dirn * 2 * D_BRANCH, (dirn + 1) * 2 * D_BRANCH)
        gates = _dot3_rhs(xc, wh_ref[:, cols], wl_ref[:, cols]) + bg_ref[:, cols]
        r = _sigmoid(gates[:, :D_BRANCH])
        ig = _sigmoid(gates[:, D_BRANCH:])
        log_a = -LRU_C * r * sp[dirn:dirn + 1, :]
        a = jnp.exp(log_a)
        u = jnp.sqrt(jnp.maximum(-jnp.tanh(log_a) * (a * a + 1.0), 0.0)) * ig * xc
        h, carry = _block_scan(a, u, carry, dirn == 0)
        if dirn == 0:
            o_ref[pl.ds(r0, RG_TB), :] = h
        else:
            o_ref[pl.ds(r0, RG_TB), :] = (o_ref[pl.ds(r0, RG_TB), :] + h) * _gelu_tanh(gt_ref[pl.ds(r0, RG_TB), :])
        return carry

    zero = jnp.zeros((1, D_BRANCH), F32)
    lax.fori_loop(0, nblk, lambda i, c: block(i, c, 0), zero)
    lax.fori_loop(0, nblk, lambda i, c: block(nblk - 1 - i, c, 1), zero)


def _blockdiag(w):
    eye = jnp.eye(B_BLOCKS, dtype=w.dtype)
    return jnp.einsum('ncd,nm->ncmd', w, eye).reshape(D_BRANCH, D_BRANCH)


def _rglru(pb3, cw, cb, wa, ba, wx, bx, lam):
    bsz, seq, _ = pb3.shape
    wg = jnp.concatenate([_blockdiag(wa[0]), _blockdiag(wx[0]), _blockdiag(wa[1]), _blockdiag(wx[1])], axis=1)
    bg = jnp.concatenate([ba[0], bx[0], ba[1], bx[1]]).reshape(1, 4 * D_BRANCH)
    wh = wg.astype(BF16)
    wl = (wg - wh.astype(F32)).astype(BF16)
    blk = (None, seq, D_BRANCH)
    return pl.pallas_call(
        _rglru_kernel,
        grid=(bsz,),
        in_specs=[pl.BlockSpec(blk, lambda b: (b, 0, 0)),
                  pl.BlockSpec(blk, lambda b: (b, 0, 1)),
                  _const_spec((B_CONV, D_BRANCH)), _const_spec((1, D_BRANCH)),
                  _const_spec((D_BRANCH, 4 * D_BRANCH)), _const_spec((D_BRANCH, 4 * D_BRANCH)),
                  _const_spec((1, 4 * D_BRANCH)), _const_spec((2, D_BRANCH))],
        out_specs=pl.BlockSpec(blk, lambda b: (b, 0, 0)),
        out_shape=jax.ShapeDtypeStruct((bsz, seq, D_BRANCH), F32),
        scratch_shapes=[pltpu.VMEM((seq + 2 * RG_PAD, D_BRANCH), F32), pltpu.VMEM((seq, D_BRANCH), F32)],
        compiler_params=_cparams("parallel"),
        name="rglru",
    )(pb3, pb3, cw, cb.reshape(1, D_BRANCH), wh, wl, bg, lam)


FFT_IN = 128
FFT_NT = FFT_IN // SUBLANES
HY_TB = 256
HY_PAD = SUBLANES


def _split_np(a):
    a = np.asarray(a, np.float32)
    hi = a.astype(jnp.bfloat16)
    lo = (a - hi.astype(np.float32)).astype(jnp.bfloat16)
    return jnp.asarray(hi), jnp.asarray(lo)


def _dot3(mh, ml, x):
    xh = x.astype(BF16)
    xl = (x - xh.astype(F32)).astype(BF16)
    return (jnp.dot(mh, xh, preferred_element_type=F32) + jnp.dot(mh, xl, preferred_element_type=F32)
            + jnp.dot(ml, xh, preferred_element_type=F32))


def _fft_tables(seq):
    n = 2 * seq
    n1 = n // FFT_IN
    half = n1 // 2
    eye = np.eye(SUBLANES)
    a = 2.0 * np.pi * np.outer(np.arange(n1), np.arange(half)) / n1
    gr, gi = np.cos(a), -np.sin(a)
    blk = np.stack([np.stack([gr, -gi], axis=1), np.stack([gi, gr], axis=1)], axis=0)
    m_out = np.einsum('rkis,cd->rkcisd', blk, eye).reshape(2 * n1 * SUBLANES, 2 * half * SUBLANES)
    ir, ii = gr.T / n, -gi.T / n
    blk = np.stack([np.stack([ir, -ii], axis=1), np.stack([ii, ir], axis=1)], axis=0)
    m_inv = np.einsum('otrk,cd->otcrkd', blk, eye).reshape(2 * half * SUBLANES, 2 * n1 * SUBLANES)
    a = 2.0 * np.pi * np.outer(np.arange(FFT_IN), np.arange(FFT_IN)) / FFT_IN
    fr, fi = np.cos(a), -np.sin(a)
    w_fwd = np.block([[fr, -fi], [fi, fr]])
    w_inv = np.block([[fr, fi], [-fi, fr]])
    s_in = SUBLANES * np.arange(FFT_NT)[:, None, None] + np.arange(SUBLANES)[None, None, :]
    th = 2.0 * np.pi * s_in * np.arange(n1)[None, :, None] / n
    tw = (jnp.asarray(np.cos(th)[..., None], F32), jnp.asarray(-np.sin(th)[..., None], F32))
    return dict(m_out=_split_np(m_out), m_inv=_split_np(m_inv), w_fwd=_split_np(w_fwd), w_inv=_split_np(w_inv),
                tw=tw, n1=n1, half=half)


def _tw_spec(n1):
    return pl.BlockSpec((None, n1, SUBLANES, 1), lambda t, q: (t, 0, 0, 0))


def _fft_outer_kernel(mh_ref, ml_ref, twr_ref, twi_ref, z_ref, v_ref, *, rows_in):
    n1 = v_ref.shape[0]
    z = z_ref[...].reshape(rows_in, D_BRANCH)
    v = _dot3(mh_ref[:, :rows_in], ml_ref[:, :rows_in], z)
    vr = v[:n1 * SUBLANES].reshape(n1, SUBLANES, D_BRANCH)
    vi = v[n1 * SUBLANES:].reshape(n1, SUBLANES, D_BRANCH)
    tr, ti = twr_ref[...], twi_ref[...]
    v_ref[:, 0] = vr * tr - vi * ti
    v_ref[:, 1] = vr * ti + vi * tr


def _fft_outer(tb, z, complex_in):
    n1, half = tb["n1"], tb["half"]
    if complex_in:
        p = z.shape[1]
        zspec = pl.BlockSpec((2, None, half, None, SUBLANES, D_BRANCH), lambda t, q: (0, q, 0, t, 0, 0))
    else:
        p = z.shape[0]
        zspec = pl.BlockSpec((None, half, None, SUBLANES, D_BRANCH), lambda t, q: (q, 0, t, 0, 0))
    rows_in = (2 if complex_in else 1) * half * SUBLANES
    mshape = (2 * n1 * SUBLANES, 2 * half * SUBLANES)
    return pl.pallas_call(
        functools.partial(_fft_outer_kernel, rows_in=rows_in),
        grid=(FFT_NT, p),
        in_specs=[_const_spec(mshape), _const_spec(mshape), _tw_spec(n1), _tw_spec(n1), zspec],
        out_specs=pl.BlockSpec((None, n1, 2, None, SUBLANES, D_BRANCH), lambda t, q: (q, 0, 0, t, 0, 0)),
        out_shape=jax.ShapeDtypeStruct((p, n1, 2, FFT_NT, SUBLANES, D_BRANCH), F32),
        compiler_params=_cparams("arbitrary", "arbitrary"),
        name="fft_outer_c" if complex_in else "fft_outer_r",
    )(*tb["m_out"], *tb["tw"], z)


FFT_KB = 4


def _fft_filt_kernel(wh_ref, wl_ref, vf_ref, vb_ref, h_ref):
    for kb in range(FFT_KB):
        zf = _dot3(wh_ref[...], wl_ref[...], vf_ref[kb].reshape(2 * FFT_IN, D_BRANCH))
        zb = _dot3(wh_ref[...], wl_ref[...], vb_ref[kb].reshape(2 * FFT_IN, D_BRANCH))
        h_ref[kb, 0] = zf[:FFT_IN] + zb[:FFT_IN]
        h_ref[kb, 1] = zf[FFT_IN:] - zb[FFT_IN:]


def _fft_filt(tb, v):
    n1 = v.shape[1]
    wspec = _const_spec((2 * FFT_IN, 2 * FFT_IN))
    vblk = (None, FFT_KB, 2, FFT_NT, SUBLANES, D_BRANCH)
    return pl.pallas_call(
        _fft_filt_kernel,
        grid=(C_ORDER, n1 // FFT_KB),
        in_specs=[wspec, wspec,
                  pl.BlockSpec(vblk, lambda o, k: (2 * o, k, 0, 0, 0, 0)),
                  pl.BlockSpec(vblk, lambda o, k: (2 * o + 1, k, 0, 0, 0, 0))],
        out_specs=pl.BlockSpec((None, FFT_KB, 2, FFT_IN, D_BRANCH), lambda o, k: (o, k, 0, 0, 0)),
        out_shape=jax.ShapeDtypeStruct((C_ORDER, n1, 2, FFT_IN, D_BRANCH), F32),
        compiler_params=_cparams("parallel", "parallel"),
        name="fft_filt",
    )(*tb["w_fwd"], v, v)


def _fft_mid_kernel(wfh_ref, wfl_ref, wih_ref, wil_ref, v_ref, h_ref, d_ref):
    for kb in range(FFT_KB):
        z = _dot3(wfh_ref[...], wfl_ref[...], v_ref[kb].reshape(2 * FFT_IN, D_BRANCH))
        zr, zi = z[:FFT_IN], z[FFT_IN:]
        hr, hi = h_ref[kb, 0], h_ref[kb, 1]
        pr = zr * hr - zi * hi
        pi = zr * hi + zi * hr
        d = _dot3(wih_ref[...], wil_ref[...], jnp.concatenate([pr, pi], axis=0))
        d_ref[kb] = d.reshape(d_ref.shape[1:])


def _fft_mid(tb, v, hspec, order):
    p, n1 = v.shape[:2]
    vspec = pl.BlockSpec((None, FFT_KB, 2, FFT_NT, SUBLANES, D_BRANCH), lambda k, q: (q, k, 0, 0, 0, 0))
    wspec = _const_spec((2 * FFT_IN, 2 * FFT_IN))
    return pl.pallas_call(
        _fft_mid_kernel,
        grid=(n1 // FFT_KB, p),
        in_specs=[wspec, wspec, wspec, wspec, vspec,
                  pl.BlockSpec((None, FFT_KB, 2, FFT_IN, D_BRANCH), lambda k, q: (order, k, 0, 0, 0))],
        out_specs=vspec,
        out_shape=jax.ShapeDtypeStruct(v.shape, F32),
        compiler_params=_cparams("parallel", "arbitrary"),
        name="fft_mid",
    )(*tb["w_fwd"], *tb["w_inv"], v, hspec)


def _ifft_outer_kernel(mh_ref, ml_ref, twr_ref, twi_ref, d_ref, u_ref, x_ref, b_ref, o_ref):
    n1 = d_ref.shape[0]
    dr, di = d_ref[:, 0], d_ref[:, 1]
    tr, ti = twr_ref[...], twi_ref[...]
    er = (dr * tr + di * ti).reshape(n1 * SUBLANES, D_BRANCH)
    ei = (di * tr - dr * ti).reshape(n1 * SUBLANES, D_BRANCH)
    y = _dot3(mh_ref[...], ml_ref[...], jnp.concatenate([er, ei], axis=0)).reshape(o_ref.shape)
    o_ref[...] = x_ref[...] * (y + u_ref[...] * b_ref[...])


def _ifft_outer(tb, d, u, ucol, x, xcol, bias):
    n1, half = tb["n1"], tb["half"]
    p = d.shape[0]
    mshape = (2 * half * SUBLANES, 2 * n1 * SUBLANES)
    io = lambda col: pl.BlockSpec((2, None, half, None, SUBLANES, D_BRANCH), lambda t, q: (0, q, 0, t, 0, col))
    return pl.pallas_call(
        _ifft_outer_kernel,
        grid=(FFT_NT, p),
        in_specs=[_const_spec(mshape), _const_spec(mshape), _tw_spec(n1), _tw_spec(n1),
                  pl.BlockSpec((None, n1, 2, None, SUBLANES, D_BRANCH), lambda t, q: (q, 0, 0, t, 0, 0)),
                  io(ucol), io(xcol), _const_spec((1, D_BRANCH))],
        out_specs=io(0),
        out_shape=jax.ShapeDtypeStruct((2, p, half, FFT_NT, SUBLANES, D_BRANCH), F32),
        compiler_params=_cparams("arbitrary", "arbitrary"),
        name="ifft_outer",
    )(*tb["m_inv"], *tb["tw"], d, u, x, bias.reshape(1, D_BRANCH))


def _hyconv_kernel(x_ref, w_ref, b_ref, o_ref, xp_ref):
    seq = x_ref.shape[0]
    xp_ref[0:HY_PAD, :] = jnp.zeros((HY_PAD, D_BRANCH), F32)
    xp_ref[HY_PAD + seq:2 * HY_PAD + seq, :] = jnp.zeros((HY_PAD, D_BRANCH), F32)
    xp_ref[HY_PAD:HY_PAD + seq, :] = x_ref[...]

    def body(i, carry):
        r0 = pl.multiple_of(i * HY_TB, HY_TB)
        win = xp_ref[pl.ds(r0, HY_TB + 2 * HY_PAD), :]
        y = b_ref[...]
        for j in range(C_CONV):
            s0 = HY_PAD + j - C_CONV // 2
            y = y + w_ref[j:j + 1, :] * win[s0:s0 + HY_TB, :]
        o_ref[pl.ds(r0, HY_TB), :] = y
        return carry

    lax.fori_loop(0, seq // HY_TB, body, 0)


def _hyconv(pc3, cw, cb):
    bsz, seq, width = pc3.shape
    nb = width // D_BRANCH
    blk = pl.BlockSpec((None, seq, D_BRANCH), lambda b, j: (b, 0, j))
    return pl.pallas_call(
        _hyconv_kernel,
        grid=(bsz, nb),
        in_specs=[blk, pl.BlockSpec((C_CONV, D_BRANCH), lambda b, j: (0, j)),
                  pl.BlockSpec((1, D_BRANCH), lambda b, j: (0, j))],
        out_specs=blk,
        out_shape=jax.ShapeDtypeStruct(pc3.shape, F32),
        scratch_shapes=[pltpu.VMEM((seq + 2 * HY_PAD, D_BRANCH), F32)],
        compiler_params=_cparams("parallel", "parallel"),
        name="hyconv",
    )(pc3, cw, cb.reshape(1, width))


def _hyfilt_kernel(z_ref, w1_ref, b1_ref, fr_ref, w2_ref, b2_ref, w3_ref, dec_ref, o_ref, h_ref):
    seq = z_ref.shape[0]
    nblk = seq // HY_TB
    fr = fr_ref[...]

    @pl.when(pl.program_id(0) == 0)
    def _():
        def hidden(i, carry):
            r0 = pl.multiple_of(i * HY_TB, HY_TB)
            zb = z_ref[pl.ds(r0, HY_TB), :]
            h = jnp.sin(fr * (jnp.dot(zb, w1_ref[...], precision=HI, preferred_element_type=F32) + b1_ref[...]))
            h = jnp.sin(fr * (jnp.dot(h, w2_ref[...], precision=HI, preferred_element_type=F32) + b2_ref[...]))
            h_ref[pl.ds(r0, HY_TB), :] = h
            return carry

        lax.fori_loop(0, nblk, hidden, 0)

    def body(i, ss):
        r0 = pl.multiple_of(i * HY_TB, HY_TB)
        hf = jnp.dot(h_ref[pl.ds(r0, HY_TB), :], w3_ref[...], precision=HI, preferred_element_type=F32)
        hf = hf * jnp.exp(-z_ref[pl.ds(r0, HY_TB), 0:1] * dec_ref[...])
        o_ref[pl.ds(r0, HY_TB), :] = hf
        return ss + jnp.sum(hf * hf, axis=0, keepdims=True)

    ss = lax.fori_loop(0, nblk, body, jnp.zeros((1, D_BRANCH), F32))
    scale = lax.rsqrt(ss + EPS)

    def norm(i, carry):
        r0 = pl.multiple_of(i * HY_TB, HY_TB)
        o_ref[pl.ds(r0, HY_TB), :] = o_ref[pl.ds(r0, HY_TB), :] * scale
        return carry

    lax.fori_loop(0, nblk, norm, 0)


def _hyfilt(seq, w1, b1, freq, w2, b2, w3):
    t = jnp.linspace(0.0, 1.0, seq, dtype=F32)[:, None]
    bands = (C_EMB - 1) // 2
    w = 2.0 * math.pi * jnp.arange(seq, dtype=F32)[:, None] / seq
    fr = jnp.linspace(1e-4, bands - 1, bands, dtype=F32)[None]
    z = jnp.concatenate([t, jnp.cos(fr * w), -jnp.sin(fr * w)], axis=-1)
    z = jnp.pad(z, ((0, 0), (0, LANES - C_EMB)))
    padm = lambda a, r, c: jnp.pad(a.astype(F32), ((0, r - a.shape[0]), (0, c - a.shape[1])))
    row = lambda a: padm(a.reshape(1, -1), 1, LANES)
    dec = jnp.abs(jnp.linspace(C_MIN_DECAY, C_MAX_DECAY, D_BRANCH, dtype=F32)).reshape(1, D_BRANCH)
    nset = C_ORDER * 2
    return pl.pallas_call(
        _hyfilt_kernel,
        grid=(nset,),
        in_specs=[_const_spec((seq, LANES)), _const_spec((LANES, LANES)), _const_spec((1, LANES)),
                  _const_spec((1, LANES)), _const_spec((LANES, LANES)), _const_spec((1, LANES)),
                  pl.BlockSpec((LANES, D_BRANCH), lambda j: (0, j)), _const_spec((1, D_BRANCH))],
        out_specs=pl.BlockSpec((None, seq, D_BRANCH), lambda j: (j, 0, 0)),
        out_shape=jax.ShapeDtypeStruct((nset, seq, D_BRANCH), F32),
        scratch_shapes=[pltpu.VMEM((seq, LANES), F32)],
        compiler_params=_cparams("arbitrary"),
        name="hyfilt",
    )(z, padm(w1, LANES, LANES), row(b1), row(freq), padm(w2, LANES, LANES), row(b2),
      padm(w3, LANES, nset * D_BRANCH), dec)


def _hyena(pc3, cw, cb, w1, b1, freq, w2, b2, w3, bias):
    bsz, seq, width = pc3.shape
    tb = _fft_tables(seq)
    half = tb["half"]
    npair = bsz // 2
    uc = _hyconv(pc3, cw, cb)
    uc6 = uc.reshape(2, npair, half, FFT_NT, SUBLANES, width)
    filt = _hyfilt(seq, w1, b1, freq, w2, b2, w3).reshape(C_ORDER * 2, half, FFT_NT, SUBLANES, D_BRANCH)
    hspec = _fft_filt(tb, _fft_outer(tb, filt, False))
    src = uc6
    for order in range(C_ORDER):
        d = _fft_mid(tb, _fft_outer(tb, src, True), hspec, order)
        src = _ifft_outer(tb, d, src, 0, uc6, order + 1, bias[order])
    return src.reshape(bsz, seq, D_BRANCH)


AT_HALF = 64
AT_TQ = 256


def _t5_bucket(rel):
    half = N_BUCKETS // 2
    max_exact = half // 2
    n = np.abs(rel)
    large = max_exact + (np.log(np.maximum(n, 1) / max_exact) / math.log(MAX_DISTANCE / max_exact)
                         * (half - max_exact)).astype(np.int64)
    large = np.minimum(large, half - 1)
    return (rel > 0).astype(np.int64) * half + np.where(n < max_exact, n, large)


def _attn_geometry(n):
    tq = min(AT_TQ, n)
    win = min(tq + 2 * AT_HALF, n)
    return tq, win, n // tq


def _attn_bias_tables(rel_bias, g, dil, n):
    tq, win, nq = _attn_geometry(n)
    hs = slice(g * D_HEADS_PER_GROUP, (g + 1) * D_HEADS_PER_GROUP)
    offsets = np.arange(-AT_HALF, AT_HALF + 1) * dil
    onehot = np.zeros((2 * AT_HALF + 1, N_BUCKETS), np.float32)
    onehot[np.arange(2 * AT_HALF + 1), _t5_bucket(offsets)] = 1.0
    band = jnp.dot(rel_bias.astype(F32)[:, hs].T, jnp.asarray(onehot).T, precision=HI)
    nband = 2 * AT_HALF + 1
    lv = tq + win - 1
    tables = []
    for i in sorted({0, min(1, nq - 1), nq - 1}):
        ws = int(np.clip(i * tq - AT_HALF, 0, n - win))
        lo = (tq - 1) - (ws - i * tq) - AT_HALF
        v = jnp.pad(band, ((0, 0), (lo, lv - lo - nband)), constant_values=NEG_BIG)
        flat = jnp.tile(v, (1, tq + 1))[:, tq - 1:tq - 1 + tq * (lv - 1)]
        tables.append(flat.reshape(D_HEADS_PER_GROUP, tq, lv - 1)[:, :, :win])
    return jnp.stack(tables)


def _attn_kernel(q_ref, k_ref, v_ref, bias_ref, o_ref, l_ref, *, n, dil):
    tq, win, _ = _attn_geometry(n)
    width = q_ref.shape[-1]
    nh = width // D_HEAD_DIM
    hp = pl.program_id(1)
    i = pl.program_id(2)
    ws = pl.multiple_of(jnp.clip(i * tq - AT_HALF, 0, n - win), AT_HALF)
    lane_head = lax.broadcasted_iota(jnp.int32, (tq, width), 1) // D_HEAD_DIM
    for r in range(dil):
        q = q_ref[pl.ds(r, tq, stride=dil), :] * (D_HEAD_DIM ** -0.5)
        kw = k_ref[pl.ds(ws * dil + r, win, stride=dil), :].astype(BF16)
        vw = v_ref[pl.ds(ws * dil + r, win, stride=dil), :].astype(BF16)
        o_acc = jnp.zeros((tq, width), F32)
        l_acc = jnp.zeros((tq, width), F32)
        for hh in range(nh):
            hm = lane_head == hh
            qh = jnp.where(hm, q, 0.0).astype(BF16)
            s = lax.dot_general(qh, kw, (((1,), (1,)), ((), ())), preferred_element_type=F32)
            s = s + bias_ref[hp * nh + hh]
            m = jnp.max(s, axis=-1, keepdims=True)
            p = jnp.exp(s - m)
            l = jnp.sum(p, axis=-1, keepdims=True)
            oh = jnp.dot(p.astype(BF16), vw, preferred_element_type=F32) / l
            o_acc = jnp.where(hm, oh, o_acc)
            l_acc = jnp.where(hm, m + jnp.log(l), l_acc)
        o_ref[pl.ds(r, tq, stride=dil), :] = o_acc
        l_ref[pl.ds(r, tq, stride=dil), :] = l_acc


def _banded_attention(pd3, g, dil, bias):
    bsz, seq, width = pd3.shape
    n = seq // dil
    tq, win, nq = _attn_geometry(n)
    ncase = bias.shape[0]
    rows = dil * tq
    bw = LANES if dil > 1 else D_BRANCH
    per = D_BRANCH // bw
    third = width // 3 // bw

    def case(i):
        return jnp.minimum(jnp.where(i == nq - 1, ncase - 1, jnp.minimum(i, 1)), ncase - 1)

    out_spec = pl.BlockSpec((None, rows, bw), lambda b, h, i: (b, i, h))
    shp = jax.ShapeDtypeStruct((bsz, seq, D_BRANCH), F32)
    return pl.pallas_call(
        functools.partial(_attn_kernel, n=n, dil=dil),
        grid=(bsz, per, nq),
        in_specs=[pl.BlockSpec((None, rows, bw), lambda b, h, i: (b, i, g * per + h)),
                  pl.BlockSpec((None, seq, bw), lambda b, h, i: (b, 0, third + g * per + h)),
                  pl.BlockSpec((None, seq, bw), lambda b, h, i: (b, 0, 2 * third + g * per + h)),
                  pl.BlockSpec((None, D_HEADS_PER_GROUP, tq, win), lambda b, h, i: (case(i), 0, 0, 0))],
        out_specs=[out_spec, out_spec],
        out_shape=[shp, shp],
        compiler_params=_cparams("parallel", "parallel", "arbitrary"),
        name=f"attn_d{dil}",
    )(pd3, pd3, pd3, bias)


def _dilated_attention(pd3, rel_bias):
    bsz, seq, _ = pd3.shape
    outs, lses = [], []
    for g, (_, dil) in enumerate(D_GROUPS):
        o, l = _banded_attention(pd3, g, dil, _attn_bias_tables(rel_bias, g, dil, seq // dil))
        outs.append(o.reshape(bsz * seq, D_BRANCH))
        lses.append(l.reshape(bsz * seq, D_BRANCH))
    return outs, lses


def kernel(x, norm1_g, w_in, hgrn_lb_logits, hgrn_norm_g, lru_conv_w, lru_conv_b, lru_wa, lru_ba, lru_wx, lru_bx,
           lru_lambda, hy_conv_w, hy_conv_b, hy_w1, hy_b1, hy_freq, hy_w2, hy_b2, hy_w3, hy_bias, rel_bias,
           w_branch, w_gate, b_gate, w_out, norm2_g, w_ff1, w_ff3, w_ff2, final_g):
    bsz, seq, _ = x.shape
    n = bsz * seq
    lb_soft = jax.nn.softmax(hgrn_lb_logits.astype(F32), axis=0)
    lower_bounds = jnp.cumsum(lb_soft, axis=0) - lb_soft[0]
    x2 = x.reshape(n, D_MODEL)
    flat = lambda a: a.reshape(n, D_BRANCH)
    for l in range(DEPTH):
        pa, pb, pc, pd = _inproj(x2, norm1_g[l], w_in[l].astype(BF16))
        oa_f, oa_b = _hgrn(pa.reshape(bsz, seq, IN_A), lower_bounds[l])
        yb = _rglru(pb.reshape(bsz, seq, IN_B), lru_conv_w[l], lru_conv_b[l], lru_wa[l], lru_ba[l],
                    lru_wx[l], lru_bx[l], lru_lambda[l])
        yc = _hyena(pc.reshape(bsz, seq, IN_C), hy_conv_w[l], hy_conv_b[l], hy_w1[l], hy_b1[l], hy_freq[l],
                    hy_w2[l], hy_b2[l], hy_w3[l], hy_bias[l])
        od, ld = _dilated_attention(pd.reshape(bsz, seq, IN_D), rel_bias)
        x2 = _merge(x2, norm1_g[l], flat(oa_f), flat(oa_b), pa, hgrn_norm_g[l], flat(yb), flat(yc), od, ld,
                    w_gate[l].reshape(D_MODEL, N_BRANCH * D_MODEL).astype(BF16),
                    b_gate[l].reshape(1, N_BRANCH * D_MODEL), w_branch[l].astype(BF16), w_out[l].astype(BF16))
        x2 = _ffn(x2, norm2_g[l], *_ffn_weights(w_ff1[l], w_ff3[l], w_ff2[l]), final_g, l == DEPTH - 1)
    return x2.reshape(bsz, seq, D_MODEL)
```

```python
import functools
import math

import jax
import jax.numpy as jnp
import numpy as np
from jax import lax
from jax.experimental import pallas as pl
from jax.experimental.pallas import tpu as pltpu

F32 = jnp.float32
BF16 = jnp.bfloat16
HI = lax.Precision.HIGHEST

D_MODEL = 1024
DEPTH = 2
EPS = 1e-6
TINY = 1e-30
N_BRANCH = 4
D_BRANCH = 256
A_HEADS = 4
A_DK = 64
A_CHUNK = 64
B_BLOCKS = 4
B_BW = 64
B_CONV = 4
LRU_C = 8.0
C_ORDER = 2
C_CONV = 3
C_EMB = 33
C_HID = 64
C_MIN_DECAY = math.log(1e-2) / 1.5
C_MAX_DECAY = math.log(1e-2) / 0.3
D_GROUPS = ((128, 1), (512, 4), (2048, 16))
D_HEADS_PER_GROUP = 4
D_HEAD_DIM = 64
D_N_HEADS = 12
D_QKV = 768
N_BUCKETS = 32
MAX_DISTANCE = 1024
NEG_BIG = -1e30
D_FF = 2816
IN_A = 5 * D_BRANCH
IN_B = 2 * D_BRANCH
IN_C = 3 * D_BRANCH
IN_D = 3 * D_QKV
IN_WIDTH = IN_A + IN_B + IN_C + IN_D

LANES = 128
SUBLANES = 8
VMEM_LIMIT = 56 * 1024 * 1024


def _cparams(*sem):
    return pltpu.CompilerParams(dimension_semantics=sem, vmem_limit_bytes=VMEM_LIMIT)


def _const_spec(shape):
    nd = len(shape)
    return pl.BlockSpec(shape, lambda *_: (0,) * nd, pipeline_mode=pl.Buffered(1))


def _rms(x, g):
    return x * lax.rsqrt(jnp.mean(x * x, axis=-1, keepdims=True) + EPS) * g


def _sigmoid(x):
    return 1.0 / (1.0 + jnp.exp(-x))


IN_TM = 512
IN_CHUNK = 256


def _inproj_kernel(x_ref, g_ref, w_ref, oa_ref, ob_ref, oc_ref, od_ref):
    h = _rms(x_ref[...], g_ref[...]).astype(BF16)
    off = 0
    for o_ref in (oa_ref, ob_ref, oc_ref, od_ref):
        width = o_ref.shape[-1]
        for c in range(0, width, IN_CHUNK):
            o_ref[:, c:c + IN_CHUNK] = jnp.dot(h, w_ref[:, off + c:off + c + IN_CHUNK],
                                               preferred_element_type=F32)
        off += width


def _inproj(x2, g, w_bf16):
    n = x2.shape[0]
    widths = (IN_A, IN_B, IN_C, IN_D)
    return pl.pallas_call(
        _inproj_kernel,
        grid=(n // IN_TM,),
        in_specs=[pl.BlockSpec((IN_TM, D_MODEL), lambda i: (i, 0)),
                  _const_spec((1, D_MODEL)),
                  _const_spec((D_MODEL, IN_WIDTH))],
        out_specs=[pl.BlockSpec((IN_TM, w), lambda i: (i, 0)) for w in widths],
        out_shape=[jax.ShapeDtypeStruct((n, w), F32) for w in widths],
        compiler_params=_cparams("parallel"),
        name="inproj",
    )(x2, g.reshape(1, D_MODEL), w_bf16)


FF_TM = 512
FF_CHUNK = 256
FF_NCHUNK = D_FF // FF_CHUNK


def _ffn_kernel(x_ref, g_ref, w1_ref, w3_ref, w2_ref, fg_ref, o_ref, acc_ref, *, final):
    x = x_ref[...]
    h = _rms(x, g_ref[...]).astype(BF16)
    acc_ref[...] = x

    def body(c, carry):
        a = jnp.dot(h, w1_ref[c], preferred_element_type=F32)
        b = jnp.dot(h, w3_ref[c], preferred_element_type=F32)
        t = (a * _sigmoid(a) * b).astype(BF16)
        acc_ref[...] += jnp.dot(t, w2_ref[c], preferred_element_type=F32)
        return carry

    lax.fori_loop(0, FF_NCHUNK, body, 0)
    y = acc_ref[...]
    if final:
        y = _rms(y, fg_ref[...])
    o_ref[...] = y


def _ffn(x2, g, w1c, w3c, w2c, final_g, final):
    n = x2.shape[0]
    return pl.pallas_call(
        functools.partial(_ffn_kernel, final=final),
        grid=(n // FF_TM,),
        in_specs=[pl.BlockSpec((FF_TM, D_MODEL), lambda i: (i, 0)),
                  _const_spec((1, D_MODEL)),
                  _const_spec((FF_NCHUNK, D_MODEL, FF_CHUNK)),
                  _const_spec((FF_NCHUNK, D_MODEL, FF_CHUNK)),
                  _const_spec((FF_NCHUNK, FF_CHUNK, D_MODEL)),
                  _const_spec((1, D_MODEL))],
        out_specs=pl.BlockSpec((FF_TM, D_MODEL), lambda i: (i, 0)),
        out_shape=jax.ShapeDtypeStruct((n, D_MODEL), F32),
        scratch_shapes=[pltpu.VMEM((FF_TM, D_MODEL), F32)],
        compiler_params=_cparams("parallel"),
        name="ffn_final" if final else "ffn",
    )(x2, g.reshape(1, D_MODEL), w1c, w3c, w2c, final_g.reshape(1, D_MODEL))


def _ffn_weights(w1, w3, w2):
    w1c = w1.astype(BF16).reshape(D_MODEL, FF_NCHUNK, FF_CHUNK).transpose(1, 0, 2)
    w3c = w3.astype(BF16).reshape(D_MODEL, FF_NCHUNK, FF_CHUNK).transpose(1, 0, 2)
    w2c = w2.astype(BF16).reshape(FF_NCHUNK, FF_CHUNK, D_MODEL)
    return w1c, w3c, w2c


MG_TM = 512


def _head_ones():
    r = np.arange(D_BRANCH)[:, None] // A_DK
    c = np.arange(D_BRANCH)[None, :] // A_DK
    return jnp.asarray((r == c).astype(np.float32) / A_DK)


def _merge_kernel(x_ref, g1_ref, oaf_ref, oab_ref, ga_ref, hg_ref, hm_ref, yb_ref, yc_ref,
                  o0_ref, o1_ref, o2_ref, l0_ref, l1_ref, l2_ref,
                  wg_ref, bg_ref, wb_ref, wo_ref, out_ref):
    x = x_ref[...]
    h = _rms(x, g1_ref[...]).astype(BF16)
    oa = oaf_ref[...] + oab_ref[...]
    ms = jnp.dot(oa * oa, hm_ref[...], precision=HI, preferred_element_type=F32)
    ga = ga_ref[...]
    ya = oa * lax.rsqrt(ms + EPS) * hg_ref[...] * (ga * _sigmoid(ga))
    l0, l1, l2 = l0_ref[...], l1_ref[...], l2_ref[...]
    m = jnp.maximum(jnp.maximum(l0, l1), l2)
    e0, e1, e2 = jnp.exp(l0 - m), jnp.exp(l1 - m), jnp.exp(l2 - m)
    yd = (e0 * o0_ref[...] + e1 * o1_ref[...] + e2 * o2_ref[...]) / (e0 + e1 + e2)
    mixed = None
    for j, y in enumerate((ya, yb_ref[...], yc_ref[...], yd)):
        gate = _sigmoid(jnp.dot(h, wg_ref[:, j * D_MODEL:(j + 1) * D_MODEL], preferred_element_type=F32)
                        + bg_ref[:, j * D_MODEL:(j + 1) * D_MODEL])
        t = gate * jnp.dot(y.astype(BF16), wb_ref[j], preferred_element_type=F32)
        mixed = t if mixed is None else mixed + t
    out_ref[...] = x + jnp.dot(mixed.astype(BF16), wo_ref[...], preferred_element_type=F32)


def _merge(x2, g1, oa_f, oa_b, pa, hg, yb, yc, od, ld, wg, bg, wb, wo):
    n = x2.shape[0]
    tile = lambda w: pl.BlockSpec((MG_TM, w), lambda i: (i, 0))
    return pl.pallas_call(
        _merge_kernel,
        grid=(n // MG_TM,),
        in_specs=[tile(D_MODEL), _const_spec((1, D_MODEL)),
                  tile(D_BRANCH), tile(D_BRANCH),
                  pl.BlockSpec((MG_TM, D_BRANCH), lambda i: (i, 4)),
                  _const_spec((1, D_BRANCH)), _const_spec((D_BRANCH, D_BRANCH)),
                  tile(D_BRANCH), tile(D_BRANCH),
                  tile(D_BRANCH), tile(D_BRANCH), tile(D_BRANCH),
                  tile(D_BRANCH), tile(D_BRANCH), tile(D_BRANCH),
                  _const_spec((D_MODEL, N_BRANCH * D_MODEL)), _const_spec((1, N_BRANCH * D_MODEL)),
                  _const_spec((N_BRANCH, D_BRANCH, D_MODEL)), _const_spec((D_MODEL, D_MODEL))],
        out_specs=tile(D_MODEL),
        out_shape=jax.ShapeDtypeStruct((n, D_MODEL), F32),
        compiler_params=_cparams("parallel"),
        name="merge",
    )(x2, g1.reshape(1, D_MODEL), oa_f, oa_b, pa, hg.reshape(1, D_BRANCH), _head_ones(), yb, yc,
      od[0], od[1], od[2], ld[0], ld[1], ld[2], wg, bg, wb, wo)


HG_TS = 512
HG_NCH = HG_TS // A_CHUNK
HG_MID = A_CHUNK // 2


def _hgrn_cum_matrices():
    r = np.arange(HG_TS)[:, None]
    c = np.arange(HG_TS)[None, :]
    same = (r // A_CHUNK) == (c // A_CHUNK)
    return (jnp.asarray((same & (r >= c)).astype(np.float32), BF16),
            jnp.asarray((same & (r <= c)).astype(np.float32), BF16))


def _hgrn_block(q, fl, v, lb, cum, st, fwd):
    srow = lax.broadcasted_iota(jnp.int32, (A_HEADS * A_CHUNK, A_CHUNK), 0) % A_CHUNK
    scol = lax.broadcasted_iota(jnp.int32, (A_HEADS * A_CHUNK, A_CHUNK), 1)
    causal = (srow >= scol) if fwd else (srow <= scol)
    lane_head = lax.broadcasted_iota(jnp.int32, (A_CHUNK, D_BRANCH), 1) // A_DK
    hmask = [(lane_head == hh).astype(F32) for hh in range(A_HEADS)]
    blk_r = lax.broadcasted_iota(jnp.int32, (D_BRANCH, D_BRANCH), 0) // A_DK
    blk_c = lax.broadcasted_iota(jnp.int32, (D_BRANCH, D_BRANCH), 1) // A_DK
    blockdiag = (blk_r == blk_c).astype(F32)

    f = lb + (1.0 - lb) * _sigmoid(fl)
    g = jnp.log(jnp.maximum(f, TINY))
    kk = (1.0 - lb) * _sigmoid(-fl)
    g1 = g.astype(BF16)
    r1 = g - g1.astype(F32)
    g2 = r1.astype(BF16)
    g3 = (r1 - g2.astype(F32)).astype(BF16)
    b = (jnp.dot(cum, g1, preferred_element_type=F32) + jnp.dot(cum, g2, preferred_element_type=F32)
         + jnp.dot(cum, g3, preferred_element_type=F32))
    rows = lambda c: slice(c * A_CHUNK, (c + 1) * A_CHUNK)
    last = (A_CHUNK - 1) if fwd else 0
    bcast = lambda r0: jnp.concatenate(
        [jnp.broadcast_to(b[c * A_CHUNK + r0:c * A_CHUNK + r0 + 1, :], (A_CHUNK, D_BRANCH)) for c in range(HG_NCH)], 0)
    bm = bcast(HG_MID)
    bl = bcast(last)
    qt = q * jnp.exp(b - bm)
    kt = (kk * jnp.exp(bm - b)).astype(BF16)
    qe = (q * jnp.exp(b)).astype(BF16)
    kh = (kk * jnp.exp(bl - b)).astype(BF16)
    vb = v.astype(BF16)
    o_intra, upd = [], []
    for c in range(HG_NCH):
        qs = jnp.concatenate([qt[rows(c)] * hmask[hh] for hh in range(A_HEADS)], axis=0).astype(BF16)
        s = lax.dot_general(qs, kt[rows(c)], (((1,), (1,)), ((), ())), preferred_element_type=F32)
        s = jnp.where(causal, s, 0.0).astype(BF16)
        ost = jnp.dot(s, vb[rows(c)], preferred_element_type=F32)
        o = ost[0:A_CHUNK] * hmask[0]
        for hh in range(1, A_HEADS):
            o = o + ost[hh * A_CHUNK:(hh + 1) * A_CHUNK] * hmask[hh]
        o_intra.append(o)
        upd.append(jnp.dot(v[rows(c)].T.astype(BF16), kh[rows(c)], preferred_element_type=F32) * blockdiag)
    outs = [None] * HG_NCH
    for c in (range(HG_NCH) if fwd else range(HG_NCH - 1, -1, -1)):
        outs[c] = o_intra[c] + lax.dot_general(qe[rows(c)], st.astype(BF16), (((1,), (1,)), ((), ())),
                                               preferred_element_type=F32)
        st = st * jnp.exp(b[c * A_CHUNK + last:c * A_CHUNK + last + 1, :]) + upd[c]
    return jnp.concatenate(outs, axis=0), st


def _hgrn_kernel(qf_ref, ff_ref, vf_ref, qb_ref, fb_ref, vb_ref, lb_ref, cf_ref, cb_ref, of_ref, ob_ref, st_ref):
    @pl.when(pl.program_id(1) == 0)
    def _():
        st_ref[...] = jnp.zeros_like(st_ref)

    lb = lb_ref[...]
    o, st = _hgrn_block(qf_ref[...], ff_ref[...], vf_ref[...], lb, cf_ref[...], st_ref[0], True)
    of_ref[...] = o
    st_ref[0] = st
    o, st = _hgrn_block(qb_ref[...], fb_ref[...], vb_ref[...], lb, cb_ref[...], st_ref[1], False)
    ob_ref[...] = o
    st_ref[1] = st


def _hgrn(pa3, lb):
    bsz, seq, _ = pa3.shape
    nblk = seq // HG_TS
    blk = (None, HG_TS, D_BRANCH)
    up = lambda col: pl.BlockSpec(blk, lambda b, i: (b, i, col))
    down = lambda col: pl.BlockSpec(blk, lambda b, i: (b, nblk - 1 - i, col))
    shp = jax.ShapeDtypeStruct((bsz, seq, D_BRANCH), F32)
    cum_f, cum_b = _hgrn_cum_matrices()
    return pl.pallas_call(
        _hgrn_kernel,
        grid=(bsz, nblk),
        in_specs=[up(0), up(1), up(3), down(0), down(2), down(3), _const_spec((1, D_BRANCH)),
                  _const_spec((HG_TS, HG_TS)), _const_spec((HG_TS, HG_TS))],
        out_specs=[up(0), down(0)],
        out_shape=[shp, shp],
        scratch_shapes=[pltpu.VMEM((2, D_BRANCH, D_BRANCH), F32)],
        compiler_params=_cparams("parallel", "arbitrary"),
        name="hgrn2",
    )(pa3, pa3, pa3, pa3, pa3, pa3, lb.reshape(1, D_BRANCH), cum_f, cum_b)


RG_TB = 128
RG_PAD = SUBLANES
RG_LEFT = B_CONV // 2


def _dot3_rhs(x, wh, wl):
    xh = x.astype(BF16)
    xl = (x - xh.astype(F32)).astype(BF16)
    return (jnp.dot(xh, wh, preferred_element_type=F32) + jnp.dot(xl, wh, preferred_element_type=F32)
            + jnp.dot(xh, wl, preferred_element_type=F32))


def _group_scan(a, u, fwd):
    row = lax.broadcasted_iota(jnp.int32, a.shape, 1)
    k = 1
    while k < SUBLANES:
        if fwd:
            keep = row >= k
            us, as_ = pltpu.roll(u, k, 1), pltpu.roll(a, k, 1)
        else:
            keep = row < SUBLANES - k
            us, as_ = pltpu.roll(u, SUBLANES - k, 1), pltpu.roll(a, SUBLANES - k, 1)
        u = a * jnp.where(keep, us, 0.0) + u
        a = a * jnp.where(keep, as_, 1.0)
        k *= 2
    return a, u


def _block_scan(a, u, carry, fwd):
    t = a.shape[0]
    ngrp = t // SUBLANES
    ag, ug = _group_scan(a.reshape(ngrp, SUBLANES, D_BRANCH), u.reshape(ngrp, SUBLANES, D_BRANCH), fwd)
    hs = [None] * ngrp
    for g in (range(ngrp) if fwd else range(ngrp - 1, -1, -1)):
        h = ug[g] + ag[g] * carry
        hs[g] = h
        carry = h[SUBLANES - 1:SUBLANES, :] if fwd else h[0:1, :]
    return jnp.concatenate(hs, axis=0), carry


def _gelu_tanh(x):
    return 0.5 * x * (1.0 + jnp.tanh(math.sqrt(2.0 / math.pi) * (x + 0.044715 * (x * x * x))))


def _rglru_kernel(x_ref, gt_ref, cw_ref, cb_ref, wh_ref, wl_ref, bg_ref, lam_ref, o_ref, xp_ref, xc_ref):
    seq = x_ref.shape[0]
    nblk = seq // RG_TB
    xp_ref[0:RG_PAD, :] = jnp.zeros((RG_PAD, D_BRANCH), F32)
    xp_ref[RG_PAD + seq:2 * RG_PAD + seq, :] = jnp.zeros((RG_PAD, D_BRANCH), F32)
    xp_ref[RG_PAD:RG_PAD + seq, :] = x_ref[...]
    nl = -lam_ref[...]
    sp = jnp.maximum(nl, 0.0) + jnp.log(1.0 + jnp.exp(-jnp.abs(nl)))

    def block(i, carry, dirn):
        r0 = pl.multiple_of(i * RG_TB, RG_TB)
        if dirn == 0:
            win = xp_ref[pl.ds(r0, RG_TB + 2 * RG_PAD), :]
            xc = cb_ref[...]
            for j in range(B_CONV):
                s0 = RG_PAD + j - RG_LEFT
                xc = xc + cw_ref[j:j + 1, :] * win[s0:s0 + RG_TB, :]
            xc_ref[pl.ds(r0, RG_TB), :] = xc
        else:
            xc = xc_ref[pl.ds(r0, RG_TB), :]
        cols = slice(dirn * 2 * D_BRANCH, (dirn + 1) * 2 * D_BRANCH)
        gates = _dot3_rhs(xc, wh_ref[:, cols], wl_ref[:, cols]) + bg_ref[:, cols]
        r = _sigmoid(gates[:, :D_BRANCH])
        ig = _sigmoid(gates[:, D_BRANCH:])
        log_a = -LRU_C * r * sp[dirn:dirn + 1, :]
        a = jnp.exp(log_a)
        u = jnp.sqrt(jnp.maximum(-jnp.tanh(log_a) * (a * a + 1.0), 0.0)) * ig * xc
        h, carry = _block_scan(a, u, carry, dirn == 0)
        if dirn == 0:
            o_ref[pl.ds(r0, RG_TB), :] = h
        else:
            o_ref[pl.ds(r0, RG_TB), :] = (o_ref[pl.ds(r0, RG_TB), :] + h) * _gelu_tanh(gt_ref[pl.ds(r0, RG_TB), :])
        return carry

    zero = jnp.zeros((1, D_BRANCH), F32)
    lax.fori_loop(0, nblk, lambda i, c: block(i, c, 0), zero)
    lax.fori_loop(0, nblk, lambda i, c: block(nblk - 1 - i, c, 1), zero)


def _blockdiag(w):
    eye = jnp.eye(B_BLOCKS, dtype=w.dtype)
    return jnp.einsum('ncd,nm->ncmd', w, eye).reshape(D_BRANCH, D_BRANCH)


def _rglru(pb3, cw, cb, wa, ba, wx, bx, lam):
    bsz, seq, _ = pb3.shape
    wg = jnp.concatenate([_blockdiag(wa[0]), _blockdiag(wx[0]), _blockdiag(wa[1]), _blockdiag(wx[1])], axis=1)
    bg = jnp.concatenate([ba[0], bx[0], ba[1], bx[1]]).reshape(1, 4 * D_BRANCH)
    wh = wg.astype(BF16)
    wl = (wg - wh.astype(F32)).astype(BF16)
    blk = (None, seq, D_BRANCH)
    return pl.pallas_call(
        _rglru_kernel,
        grid=(bsz,),
        in_specs=[pl.BlockSpec(blk, lambda b: (b, 0, 0)),
                  pl.BlockSpec(blk, lambda b: (b, 0, 1)),
                  _const_spec((B_CONV, D_BRANCH)), _const_spec((1, D_BRANCH)),
                  _const_spec((D_BRANCH, 4 * D_BRANCH)), _const_spec((D_BRANCH, 4 * D_BRANCH)),
                  _const_spec((1, 4 * D_BRANCH)), _const_spec((2, D_BRANCH))],
        out_specs=pl.BlockSpec(blk, lambda b: (b, 0, 0)),
        out_shape=jax.ShapeDtypeStruct((bsz, seq, D_BRANCH), F32),
        scratch_shapes=[pltpu.VMEM((seq + 2 * RG_PAD, D_BRANCH), F32), pltpu.VMEM((seq, D_BRANCH), F32)],
        compiler_params=_cparams("parallel"),
        name="rglru",
    )(pb3, pb3, cw, cb.reshape(1, D_BRANCH), wh, wl, bg, lam)


FFT_IN = 128
FFT_NT = FFT_IN // SUBLANES
HY_TB = 256
HY_PAD = SUBLANES


def _split_np(a):
    a = np.asarray(a, np.float32)
    hi = a.astype(jnp.bfloat16)
    lo = (a - hi.astype(np.float32)).astype(jnp.bfloat16)
    return jnp.asarray(hi), jnp.asarray(lo)


def _dot3(mh, ml, x):
    xh = x.astype(BF16)
    xl = (x - xh.astype(F32)).astype(BF16)
    return (jnp.dot(mh, xh, preferred_element_type=F32) + jnp.dot(mh, xl, preferred_element_type=F32)
            + jnp.dot(ml, xh, preferred_element_type=F32))


def _fft_tables(seq):
    n = 2 * seq
    n1 = n // FFT_IN
    half = n1 // 2
    eye = np.eye(SUBLANES)
    a = 2.0 * np.pi * np.outer(np.arange(n1), np.arange(half)) / n1
    gr, gi = np.cos(a), -np.sin(a)
    blk = np.stack([np.stack([gr, -gi], axis=1), np.stack([gi, gr], axis=1)], axis=0)
    m_out = np.einsum('rkis,cd->rkcisd', blk, eye).reshape(2 * n1 * SUBLANES, 2 * half * SUBLANES)
    ir, ii = gr.T / n, -gi.T / n
    blk = np.stack([np.stack([ir, -ii], axis=1), np.stack([ii, ir], axis=1)], axis=0)
    m_inv = np.einsum('otrk,cd->otcrkd', blk, eye).reshape(2 * half * SUBLANES, 2 * n1 * SUBLANES)
    a = 2.0 * np.pi * np.outer(np.arange(FFT_IN), np.arange(FFT_IN)) / FFT_IN
    fr, fi = np.cos(a), -np.sin(a)
    w_fwd = np.block([[fr, -fi], [fi, fr]])
    w_inv = np.block([[fr, fi], [-fi, fr]])
    s_in = SUBLANES * np.arange(FFT_NT)[:, None, None] + np.arange(SUBLANES)[None, None, :]
    th = 2.0 * np.pi * s_in * np.arange(n1)[None, :, None] / n
    tw = (jnp.asarray(np.cos(th)[..., None], F32), jnp.asarray(-np.sin(th)[..., None], F32))
    return dict(m_out=_split_np(m_out), m_inv=_split_np(m_inv), w_fwd=_split_np(w_fwd), w_inv=_split_np(w_inv),
                tw=tw, n1=n1, half=half)


def _tw_spec(n1):
    return pl.BlockSpec((None, n1, SUBLANES, 1), lambda t, q: (t, 0, 0, 0))


def _fft_outer_kernel(mh_ref, ml_ref, twr_ref, twi_ref, z_ref, v_ref, *, rows_in):
    n1 = v_ref.shape[0]
    z = z_ref[...].reshape(rows_in, D_BRANCH)
    v = _dot3(mh_ref[:, :rows_in], ml_ref[:, :rows_in], z)
    vr = v[:n1 * SUBLANES].reshape(n1, SUBLANES, D_BRANCH)
    vi = v[n1 * SUBLANES:].reshape(n1, SUBLANES, D_BRANCH)
    tr, ti = twr_ref[...], twi_ref[...]
    v_ref[:, 0] = vr * tr - vi * ti
    v_ref[:, 1] = vr * ti + vi * tr


def _fft_outer(tb, z, complex_in):
    n1, half = tb["n1"], tb["half"]
    if complex_in:
        p = z.shape[1]
        zspec = pl.BlockSpec((2, None, half, None, SUBLANES, D_BRANCH), lambda t, q: (0, q, 0, t, 0, 0))
    else:
        p = z.shape[0]
        zspec = pl.BlockSpec((None, half, None, SUBLANES, D_BRANCH), lambda t, q: (q, 0, t, 0, 0))
    rows_in = (2 if complex_in else 1) * half * SUBLANES
    mshape = (2 * n1 * SUBLANES, 2 * half * SUBLANES)
    return pl.pallas_call(
        functools.partial(_fft_outer_kernel, rows_in=rows_in),
        grid=(FFT_NT, p),
        in_specs=[_const_spec(mshape), _const_spec(mshape), _tw_spec(n1), _tw_spec(n1), zspec],
        out_specs=pl.BlockSpec((None, n1, 2, None, SUBLANES, D_BRANCH), lambda t, q: (q, 0, 0, t, 0, 0)),
        out_shape=jax.ShapeDtypeStruct((p, n1, 2, FFT_NT, SUBLANES, D_BRANCH), F32),
        compiler_params=_cparams("arbitrary", "arbitrary"),
        name="fft_outer_c" if complex_in else "fft_outer_r",
    )(*tb["m_out"], *tb["tw"], z)


FFT_KB = 4


def _fft_filt_kernel(wh_ref, wl_ref, vf_ref, vb_ref, h_ref):
    for kb in range(FFT_KB):
        zf = _dot3(wh_ref[...], wl_ref[...], vf_ref[kb].reshape(2 * FFT_IN, D_BRANCH))
        zb = _dot3(wh_ref[...], wl_ref[...], vb_ref[kb].reshape(2 * FFT_IN, D_BRANCH))
        h_ref[kb, 0] = zf[:FFT_IN] + zb[:FFT_IN]
        h_ref[kb, 1] = zf[FFT_IN:] - zb[FFT_IN:]


def _fft_filt(tb, v):
    n1 = v.shape[1]
    wspec = _const_spec((2 * FFT_IN, 2 * FFT_IN))
    vblk = (None, FFT_KB, 2, FFT_NT, SUBLANES, D_BRANCH)
    return pl.pallas_call(
        _fft_filt_kernel,
        grid=(C_ORDER, n1 // FFT_KB),
        in_specs=[wspec, wspec,
                  pl.BlockSpec(vblk, lambda o, k: (2 * o, k, 0, 0, 0, 0)),
                  pl.BlockSpec(vblk, lambda o, k: (2 * o + 1, k, 0, 0, 0, 0))],
        out_specs=pl.BlockSpec((None, FFT_KB, 2, FFT_IN, D_BRANCH), lambda o, k: (o, k, 0, 0, 0)),
        out_shape=jax.ShapeDtypeStruct((C_ORDER, n1, 2, FFT_IN, D_BRANCH), F32),
        compiler_params=_cparams("parallel", "parallel"),
        name="fft_filt",
    )(*tb["w_fwd"], v, v)


def _fft_mid_kernel(wfh_ref, wfl_ref, wih_ref, wil_ref, v_ref, h_ref, d_ref):
    for kb in range(FFT_KB):
        z = _dot3(wfh_ref[...], wfl_ref[...], v_ref[kb].reshape(2 * FFT_IN, D_BRANCH))
        zr, zi = z[:FFT_IN], z[FFT_IN:]
        hr, hi = h_ref[kb, 0], h_ref[kb, 1]
        pr = zr * hr - zi * hi
        pi = zr * hi + zi * hr
        d = _dot3(wih_ref[...], wil_ref[...], jnp.concatenate([pr, pi], axis=0))
        d_ref[kb] = d.reshape(d_ref.shape[1:])


def _fft_mid(tb, v, hspec, order):
    p, n1 = v.shape[:2]
    vspec = pl.BlockSpec((None, FFT_KB, 2, FFT_NT, SUBLANES, D_BRANCH), lambda k, q: (q, k, 0, 0, 0, 0))
    wspec = _const_spec((2 * FFT_IN, 2 * FFT_IN))
    return pl.pallas_call(
        _fft_mid_kernel,
        grid=(n1 // FFT_KB, p),
        in_specs=[wspec, wspec, wspec, wspec, vspec,
                  pl.BlockSpec((None, FFT_KB, 2, FFT_IN, D_BRANCH), lambda k, q: (order, k, 0, 0, 0))],
        out_specs=vspec,
        out_shape=jax.ShapeDtypeStruct(v.shape, F32),
        compiler_params=_cparams("parallel", "arbitrary"),
        name="fft_mid",
    )(*tb["w_fwd"], *tb["w_inv"], v, hspec)


def _ifft_outer_kernel(mh_ref, ml_ref, twr_ref, twi_ref, d_ref, u_ref, x_ref, b_ref, o_ref):
    n1 = d_ref.shape[0]
    dr, di = d_ref[:, 0], d_ref[:, 1]
    tr, ti = twr_ref[...], twi_ref[...]
    er = (dr * tr + di * ti).reshape(n1 * SUBLANES, D_BRANCH)
    ei = (di * tr - dr * ti).reshape(n1 * SUBLANES, D_BRANCH)
    y = _dot3(mh_ref[...], ml_ref[...], jnp.concatenate([er, ei], axis=0)).reshape(o_ref.shape)
    o_ref[...] = x_ref[...] * (y + u_ref[...] * b_ref[...])


def _ifft_outer(tb, d, u, ucol, x, xcol, bias):
    n1, half = tb["n1"], tb["half"]
    p = d.shape[0]
    mshape = (2 * half * SUBLANES, 2 * n1 * SUBLANES)
    io = lambda col: pl.BlockSpec((2, None, half, None, SUBLANES, D_BRANCH), lambda t, q: (0, q, 0, t, 0, col))
    return pl.pallas_call(
        _ifft_outer_kernel,
        grid=(FFT_NT, p),
        in_specs=[_const_spec(mshape), _const_spec(mshape), _tw_spec(n1), _tw_spec(n1),
                  pl.BlockSpec((None, n1, 2, None, SUBLANES, D_BRANCH), lambda t, q: (q, 0, 0, t, 0, 0)),
                  io(ucol), io(xcol), _const_spec((1, D_BRANCH))],
        out_specs=io(0),
        out_shape=jax.ShapeDtypeStruct((2, p, half, FFT_NT, SUBLANES, D_BRANCH), F32),
        compiler_params=_cparams("arbitrary", "arbitrary"),
        name="ifft_outer",
    )(*tb["m_inv"], *tb["tw"], d, u, x, bias.reshape(1, D_BRANCH))


def _hyconv_kernel(x_ref, w_ref, b_ref, o_ref, xp_ref):
    seq = x_ref.shape[0]
    xp_ref[0:HY_PAD, :] = jnp.zeros((HY_PAD, D_BRANCH), F32)
    xp_ref[HY_PAD + seq:2 * HY_PAD + seq, :] = jnp.zeros((HY_PAD, D_BRANCH), F32)
    xp_ref[HY_PAD:HY_PAD + seq, :] = x_ref[...]

    def body(i, carry):
        r0 = pl.multiple_of(i * HY_TB, HY_TB)
        win = xp_ref[pl.ds(r0, HY_TB + 2 * HY_PAD), :]
        y = b_ref[...]
        for j in range(C_CONV):
            s0 = HY_PAD + j - C_CONV // 2
            y = y + w_ref[j:j + 1, :] * win[s0:s0 + HY_TB, :]
        o_ref[pl.ds(r0, HY_TB), :] = y
        return carry

    lax.fori_loop(0, seq // HY_TB, body, 0)


def _hyconv(pc3, cw, cb):
    bsz, seq, width = pc3.shape
    nb = width // D_BRANCH
    blk = pl.BlockSpec((None, seq, D_BRANCH), lambda b, j: (b, 0, j))
    return pl.pallas_call(
        _hyconv_kernel,
        grid=(bsz, nb),
        in_specs=[blk, pl.BlockSpec((C_CONV, D_BRANCH), lambda b, j: (0, j)),
                  pl.BlockSpec((1, D_BRANCH), lambda b, j: (0, j))],
        out_specs=blk,
        out_shape=jax.ShapeDtypeStruct(pc3.shape, F32),
        scratch_shapes=[pltpu.VMEM((seq + 2 * HY_PAD, D_BRANCH), F32)],
        compiler_params=_cparams("parallel", "parallel"),
        name="hyconv",
    )(pc3, cw, cb.reshape(1, width))


def _hyfilt_kernel(z_ref, w1_ref, b1_ref, fr_ref, w2_ref, b2_ref, w3_ref, dec_ref, o_ref, h_ref):
    seq = z_ref.shape[0]
    nblk = seq // HY_TB
    fr = fr_ref[...]

    @pl.when(pl.program_id(0) == 0)
    def _():
        def hidden(i, carry):
            r0 = pl.multiple_of(i * HY_TB, HY_TB)
            zb = z_ref[pl.ds(r0, HY_TB), :]
            h = jnp.sin(fr * (jnp.dot(zb, w1_ref[...], precision=HI, preferred_element_type=F32) + b1_ref[...]))
            h = jnp.sin(fr * (jnp.dot(h, w2_ref[...], precision=HI, preferred_element_type=F32) + b2_ref[...]))
            h_ref[pl.ds(r0, HY_TB), :] = h
            return carry

        lax.fori_loop(0, nblk, hidden, 0)

    def body(i, ss):
        r0 = pl.multiple_of(i * HY_TB, HY_TB)
        hf = jnp.dot(h_ref[pl.ds(r0, HY_TB), :], w3_ref[...], precision=HI, preferred_element_type=F32)
        hf = hf * jnp.exp(-z_ref[pl.ds(r0, HY_TB), 0:1] * dec_ref[...])
        o_ref[pl.ds(r0, HY_TB), :] = hf
        return ss + jnp.sum(hf * hf, axis=0, keepdims=True)

    ss = lax.fori_loop(0, nblk, body, jnp.zeros((1, D_BRANCH), F32))
    scale = lax.rsqrt(ss + EPS)

    def norm(i, carry):
        r0 = pl.multiple_of(i * HY_TB, HY_TB)
        o_ref[pl.ds(r0, HY_TB), :] = o_ref[pl.ds(r0, HY_TB), :] * scale
        return carry

    lax.fori_loop(0, nblk, norm, 0)


def _hyfilt(seq, w1, b1, freq, w2, b2, w3):
    t = jnp.linspace(0.0, 1.0, seq, dtype=F32)[:, None]
    bands = (C_EMB - 1) // 2
    w = 2.0 * math.pi * jnp.arange(seq, dtype=F32)[:, None] / seq
    fr = jnp.linspace(1e-4, bands - 1, bands, dtype=F32)[None]
    z = jnp.concatenate([t, jnp.cos(fr * w), -jnp.sin(fr * w)], axis=-1)
    z = jnp.pad(z, ((0, 0), (0, LANES - C_EMB)))
    padm = lambda a, r, c: jnp.pad(a.astype(F32), ((0, r - a.shape[0]), (0, c - a.shape[1])))
    row = lambda a: padm(a.reshape(1, -1), 1, LANES)
    dec = jnp.abs(jnp.linspace(C_MIN_DECAY, C_MAX_DECAY, D_BRANCH, dtype=F32)).reshape(1, D_BRANCH)
    nset = C_ORDER * 2
    return pl.pallas_call(
        _hyfilt_kernel,
        grid=(nset,),
        in_specs=[_const_spec((seq, LANES)), _const_spec((LANES, LANES)), _const_spec((1, LANES)),
                  _const_spec((1, LANES)), _const_spec((LANES, LANES)), _const_spec((1, LANES)),
                  pl.BlockSpec((LANES, D_BRANCH), lambda j: (0, j)), _const_spec((1, D_BRANCH))],
        out_specs=pl.BlockSpec((None, seq, D_BRANCH), lambda j: (j, 0, 0)),
        out_shape=jax.ShapeDtypeStruct((nset, seq, D_BRANCH), F32),
        scratch_shapes=[pltpu.VMEM((seq, LANES), F32)],
        compiler_params=_cparams("arbitrary"),
        name="hyfilt",
    )(z, padm(w1, LANES, LANES), row(b1), row(freq), padm(w2, LANES, LANES), row(b2),
      padm(w3, LANES, nset * D_BRANCH), dec)


def _hyena(pc3, cw, cb, w1, b1, freq, w2, b2, w3, bias):
    bsz, seq, width = pc3.shape
    tb = _fft_tables(seq)
    half = tb["half"]
    npair = bsz // 2
    uc = _hyconv(pc3, cw, cb)
    uc6 = uc.reshape(2, npair, half, FFT_NT, SUBLANES, width)
    filt = _hyfilt(seq, w1, b1, freq, w2, b2, w3).reshape(C_ORDER * 2, half, FFT_NT, SUBLANES, D_BRANCH)
    hspec = _fft_filt(tb, _fft_outer(tb, filt, False))
    src = uc6
    for order in range(C_ORDER):
        d = _fft_mid(tb, _fft_outer(tb, src, True), hspec, order)
        src = _ifft_outer(tb, d, src, 0, uc6, order + 1, bias[order])
    return src.reshape(bsz, seq, D_BRANCH)


AT_HALF = 64
AT_TQ = 256
AT_SUB = 128


def _t5_bucket(rel):
    half = N_BUCKETS // 2
    max_exact = half // 2
    n = np.abs(rel)
    large = max_exact + (np.log(np.maximum(n, 1) / max_exact) / math.log(MAX_DISTANCE / max_exact)
                         * (half - max_exact)).astype(np.int64)
    large = np.minimum(large, half - 1)
    return (rel > 0).astype(np.int64) * half + np.where(n < max_exact, n, large)


def _attn_geometry(n):
    tq = min(AT_TQ, n)
    sub = min(AT_SUB, tq)
    win = min(sub + 2 * AT_HALF, n)
    return tq, sub, win, n // tq, n // sub


def _attn_bias_tables(rel_bias, g, dil, n):
    _, sub, win, _, nsb = _attn_geometry(n)
    hs = slice(g * D_HEADS_PER_GROUP, (g + 1) * D_HEADS_PER_GROUP)
    offsets = np.arange(-AT_HALF, AT_HALF + 1) * dil
    onehot = np.zeros((2 * AT_HALF + 1, N_BUCKETS), np.float32)
    onehot[np.arange(2 * AT_HALF + 1), _t5_bucket(offsets)] = 1.0
    band = jnp.dot(rel_bias.astype(F32)[:, hs].T, jnp.asarray(onehot).T, precision=HI)
    nband = 2 * AT_HALF + 1
    lv = sub + win - 1
    tables = []
    for i in sorted({0, min(1, nsb - 1), nsb - 1}):
        ws = int(np.clip(i * sub - AT_HALF, 0, n - win))
        lo = (sub - 1) - (ws - i * sub) - AT_HALF
        v = jnp.pad(band, ((0, 0), (lo, lv - lo - nband)), constant_values=NEG_BIG)
        flat = jnp.tile(v, (1, sub + 1))[:, sub - 1:sub - 1 + sub * (lv - 1)]
        tables.append(flat.reshape(D_HEADS_PER_GROUP, sub, lv - 1)[:, :, :win])
    return jnp.stack(tables)


def _attn_kernel(q_ref, k_ref, v_ref, bias_ref, o_ref, l_ref, *, n, dil):
    tq, sub, win, _, nsb = _attn_geometry(n)
    ncase = bias_ref.shape[0]
    width = q_ref.shape[-1]
    nh = width // D_HEAD_DIM
    hp = pl.program_id(1)
    lane_head = lax.broadcasted_iota(jnp.int32, (sub, width), 1) // D_HEAD_DIM
    for j in range(tq // sub):
        sidx = pl.program_id(2) * (tq // sub) + j
        case = jnp.minimum(jnp.where(sidx == nsb - 1, ncase - 1, jnp.minimum(sidx, 1)), ncase - 1)
        ws = pl.multiple_of(jnp.clip(sidx * sub - AT_HALF, 0, n - win), AT_HALF)
        for r in range(dil):
            q = q_ref[pl.ds(j * sub * dil + r, sub, stride=dil), :] * (D_HEAD_DIM ** -0.5)
            kw = k_ref[pl.ds(ws * dil + r, win, stride=dil), :].astype(BF16)
            vw = v_ref[pl.ds(ws * dil + r, win, stride=dil), :].astype(BF16)
            o_acc = jnp.zeros((sub, width), F32)
            l_acc = jnp.zeros((sub, width), F32)
            for hh in range(nh):
                hm = lane_head == hh
                qh = jnp.where(hm, q, 0.0).astype(BF16)
                s = lax.dot_general(qh, kw, (((1,), (1,)), ((), ())), preferred_element_type=F32)
                s = s + bias_ref[case, hp * nh + hh]
                m = jnp.max(s, axis=-1, keepdims=True)
                p = jnp.exp(s - m)
                l = jnp.sum(p, axis=-1, keepdims=True)
                oh = jnp.dot(p.astype(BF16), vw, preferred_element_type=F32) / l
                o_acc = jnp.where(hm, oh, o_acc)
                l_acc = jnp.where(hm, m + jnp.log(l), l_acc)
            o_ref[pl.ds(j * sub * dil + r, sub, stride=dil), :] = o_acc
            l_ref[pl.ds(j * sub * dil + r, sub, stride=dil), :] = l_acc


def _banded_attention(pd3, g, dil, bias):
    bsz, seq, width = pd3.shape
    n = seq // dil
    tq, sub, win, nq, _ = _attn_geometry(n)
    rows = dil * tq
    bw = LANES if dil > 1 else D_BRANCH
    per = D_BRANCH // bw
    third = width // 3 // bw
    out_spec = pl.BlockSpec((None, rows, bw), lambda b, h, i: (b, i, h))
    shp = jax.ShapeDtypeStruct((bsz, seq, D_BRANCH), F32)
    return pl.pallas_call(
        functools.partial(_attn_kernel, n=n, dil=dil),
        grid=(bsz, per, nq),
        in_specs=[pl.BlockSpec((None, rows, bw), lambda b, h, i: (b, i, g * per + h)),
                  pl.BlockSpec((None, seq, bw), lambda b, h, i: (b, 0, third + g * per + h)),
                  pl.BlockSpec((None, seq, bw), lambda b, h, i: (b, 0, 2 * third + g * per + h)),
                  _const_spec(bias.shape)],
        out_specs=[out_spec, out_spec],
        out_shape=[shp, shp],
        compiler_params=_cparams("parallel", "parallel", "arbitrary"),
        name=f"attn_d{dil}",
    )(pd3, pd3, pd3, bias)


def _dilated_attention(pd3, rel_bias):
    bsz, seq, _ = pd3.shape
    outs, lses = [], []
    for g, (_, dil) in enumerate(D_GROUPS):
        o, l = _banded_attention(pd3, g, dil, _attn_bias_tables(rel_bias, g, dil, seq // dil))
        outs.append(o.reshape(bsz * seq, D_BRANCH))
        lses.append(l.reshape(bsz * seq, D_BRANCH))
    return outs, lses


def kernel(x, norm1_g, w_in, hgrn_lb_logits, hgrn_norm_g, lru_conv_w, lru_conv_b, lru_wa, lru_ba, lru_wx, lru_bx,
           lru_lambda, hy_conv_w, hy_conv_b, hy_w1, hy_b1, hy_freq, hy_w2, hy_b2, hy_w3, hy_bias, rel_bias,
           w_branch, w_gate, b_gate, w_out, norm2_g, w_ff1, w_ff3, w_ff2, final_g):
    bsz, seq, _ = x.shape
    n = bsz * seq
    lb_soft = jax.nn.softmax(hgrn_lb_logits.astype(F32), axis=0)
    lower_bounds = jnp.cumsum(lb_soft, axis=0) - lb_soft[0]
    x2 = x.reshape(n, D_MODEL)
    flat = lambda a: a.reshape(n, D_BRANCH)
    for l in range(DEPTH):
        pa, pb, pc, pd = _inproj(x2, norm1_g[l], w_in[l].astype(BF16))
        oa_f, oa_b = _hgrn(pa.reshape(bsz, seq, IN_A), lower_bounds[l])
        yb = _rglru(pb.reshape(bsz, seq, IN_B), lru_conv_w[l], lru_conv_b[l], lru_wa[l], lru_ba[l],
                    lru_wx[l], lru_bx[l], lru_lambda[l])
        yc = _hyena(pc.reshape(bsz, seq, IN_C), hy_conv_w[l], hy_conv_b[l], hy_w1[l], hy_b1[l], hy_freq[l],
                    hy_w2[l], hy_b2[l], hy_w3[l], hy_bias[l])
        od, ld = _dilated_attention(pd.reshape(bsz, seq, IN_D), rel_bias)
        x2 = _merge(x2, norm1_g[l], flat(oa_f), flat(oa_b), pa, hgrn_norm_g[l], flat(yb), flat(yc), od, ld,
                    w_gate[l].reshape(D_MODEL, N_BRANCH * D_MODEL).astype(BF16),
                    b_gate[l].reshape(1, N_BRANCH * D_MODEL), w_branch[l].astype(BF16), w_out[l].astype(BF16))
        x2 = _ffn(x2, norm2_g[l], *_ffn_weights(w_ff1[l], w_ff3[l], w_ff2[l]), final_g, l == DEPTH - 1)
    return x2.reshape(bsz, seq, D_MODEL)
```

```python
import functools
import math

import jax
import jax.numpy as jnp
import numpy as np
from jax import lax
from jax.experimental import pallas as pl
from jax.experimental.pallas import tpu as pltpu

F32 = jnp.float32
BF16 = jnp.bfloat16
HI = lax.Precision.HIGHEST

D_MODEL = 1024
DEPTH = 2
EPS = 1e-6
TINY = 1e-30
N_BRANCH = 4
D_BRANCH = 256
A_HEADS = 4
A_DK = 64
A_CHUNK = 64
B_BLOCKS = 4
B_BW = 64
B_CONV = 4
LRU_C = 8.0
C_ORDER = 2
C_CONV = 3
C_EMB = 33
C_HID = 64
C_MIN_DECAY = math.log(1e-2) / 1.5
C_MAX_DECAY = math.log(1e-2) / 0.3
D_GROUPS = ((128, 1), (512, 4), (2048, 16))
D_HEADS_PER_GROUP = 4
D_HEAD_DIM = 64
D_N_HEADS = 12
D_QKV = 768
N_BUCKETS = 32
MAX_DISTANCE = 1024
NEG_BIG = -1e30
D_FF = 2816
IN_A = 5 * D_BRANCH
IN_B = 2 * D_BRANCH
IN_C = 3 * D_BRANCH
IN_D = 3 * D_QKV
IN_WIDTH = IN_A + IN_B + IN_C + IN_D

LANES = 128
SUBLANES = 8
VMEM_LIMIT = 56 * 1024 * 1024


def _cparams(*sem):
    return pltpu.CompilerParams(dimension_semantics=sem, vmem_limit_bytes=VMEM_LIMIT)


def _const_spec(shape):
    nd = len(shape)
    return pl.BlockSpec(shape, lambda *_: (0,) * nd, pipeline_mode=pl.Buffered(1))


def _rms(x, g):
    return x * lax.rsqrt(jnp.mean(x * x, axis=-1, keepdims=True) + EPS) * g


def _sigmoid(x):
    return 1.0 / (1.0 + jnp.exp(-x))


IN_TM = 512
IN_CHUNK = 256


def _inproj_kernel(x_ref, g_ref, w_ref, oa_ref, ob_ref, oc_ref, od_ref):
    h = _rms(x_ref[...], g_ref[...]).astype(BF16)
    off = 0
    for o_ref in (oa_ref, ob_ref, oc_ref, od_ref):
        width = o_ref.shape[-1]
        for c in range(0, width, IN_CHUNK):
            o_ref[:, c:c + IN_CHUNK] = jnp.dot(h, w_ref[:, off + c:off + c + IN_CHUNK],
                                               preferred_element_type=F32)
        off += width


def _inproj(x2, g, w_bf16):
    n = x2.shape[0]
    widths = (IN_A, IN_B, IN_C, IN_D)
    return pl.pallas_call(
        _inproj_kernel,
        grid=(n // IN_TM,),
        in_specs=[pl.BlockSpec((IN_TM, D_MODEL), lambda i: (i, 0)),
                  _const_spec((1, D_MODEL)),
                  _const_spec((D_MODEL, IN_WIDTH))],
        out_specs=[pl.BlockSpec((IN_TM, w), lambda i: (i, 0)) for w in widths],
        out_shape=[jax.ShapeDtypeStruct((n, w), F32) for w in widths],
        compiler_params=_cparams("parallel"),
        name="inproj",
    )(x2, g.reshape(1, D_MODEL), w_bf16)


FF_TM = 1024
FF_CHUNK = 256
FF_NCHUNK = D_FF // FF_CHUNK


def _ffn_kernel(x_ref, g_ref, w1_ref, w3_ref, w2_ref, fg_ref, o_ref, acc_ref, *, final):
    x = x_ref[...]
    h = _rms(x, g_ref[...]).astype(BF16)
    acc_ref[...] = x

    def body(c, carry):
        a = jnp.dot(h, w1_ref[c], preferred_element_type=F32)
        b = jnp.dot(h, w3_ref[c], preferred_element_type=F32)
        t = (a * _sigmoid(a) * b).astype(BF16)
        acc_ref[...] += jnp.dot(t, w2_ref[c], preferred_element_type=F32)
        return carry

    lax.fori_loop(0, FF_NCHUNK, body, 0)
    y = acc_ref[...]
    if final:
        y = _rms(y, fg_ref[...])
    o_ref[...] = y


def _ffn(x2, g, w1c, w3c, w2c, final_g, final):
    n = x2.shape[0]
    return pl.pallas_call(
        functools.partial(_ffn_kernel, final=final),
        grid=(n // FF_TM,),
        in_specs=[pl.BlockSpec((FF_TM, D_MODEL), lambda i: (i, 0)),
                  _const_spec((1, D_MODEL)),
                  _const_spec((FF_NCHUNK, D_MODEL, FF_CHUNK)),
                  _const_spec((FF_NCHUNK, D_MODEL, FF_CHUNK)),
                  _const_spec((FF_NCHUNK, FF_CHUNK, D_MODEL)),
                  _const_spec((1, D_MODEL))],
        out_specs=pl.BlockSpec((FF_TM, D_MODEL), lambda i: (i, 0)),
        out_shape=jax.ShapeDtypeStruct((n, D_MODEL), F32),
        scratch_shapes=[pltpu.VMEM((FF_TM, D_MODEL), F32)],
        compiler_params=_cparams("parallel"),
        name="ffn_final" if final else "ffn",
    )(x2, g.reshape(1, D_MODEL), w1c, w3c, w2c, final_g.reshape(1, D_MODEL))


def _ffn_weights(w1, w3, w2):
    w1c = w1.astype(BF16).reshape(D_MODEL, FF_NCHUNK, FF_CHUNK).transpose(1, 0, 2)
    w3c = w3.astype(BF16).reshape(D_MODEL, FF_NCHUNK, FF_CHUNK).transpose(1, 0, 2)
    w2c = w2.astype(BF16).reshape(FF_NCHUNK, FF_CHUNK, D_MODEL)
    return w1c, w3c, w2c


MG_TM = 512


def _head_ones():
    r = np.arange(D_BRANCH)[:, None] // A_DK
    c = np.arange(D_BRANCH)[None, :] // A_DK
    return jnp.asarray((r == c).astype(np.float32) / A_DK)


def _merge_kernel(x_ref, g1_ref, oaf_ref, oab_ref, ga_ref, hg_ref, hm_ref, yb_ref, yc_ref,
                  o0_ref, o1_ref, o2_ref, l0_ref, l1_ref, l2_ref,
                  wg_ref, bg_ref, wb_ref, wo_ref, out_ref):
    x = x_ref[...]
    h = _rms(x, g1_ref[...]).astype(BF16)
    oa = oaf_ref[...] + oab_ref[...]
    ms = jnp.dot(oa * oa, hm_ref[...], precision=HI, preferred_element_type=F32)
    ga = ga_ref[...]
    ya = oa * lax.rsqrt(ms + EPS) * hg_ref[...] * (ga * _sigmoid(ga))
    l0, l1, l2 = l0_ref[...], l1_ref[...], l2_ref[...]
    m = jnp.maximum(jnp.maximum(l0, l1), l2)
    e0, e1, e2 = jnp.exp(l0 - m), jnp.exp(l1 - m), jnp.exp(l2 - m)
    yd = (e0 * o0_ref[...] + e1 * o1_ref[...] + e2 * o2_ref[...]) / (e0 + e1 + e2)
    mixed = None
    for j, y in enumerate((ya, yb_ref[...], yc_ref[...], yd)):
        gate = _sigmoid(jnp.dot(h, wg_ref[:, j * D_MODEL:(j + 1) * D_MODEL], preferred_element_type=F32)
                        + bg_ref[:, j * D_MODEL:(j + 1) * D_MODEL])
        t = gate * jnp.dot(y.astype(BF16), wb_ref[j], preferred_element_type=F32)
        mixed = t if mixed is None else mixed + t
    out_ref[...] = x + jnp.dot(mixed.astype(BF16), wo_ref[...], preferred_element_type=F32)


def _merge(x2, g1, oa_f, oa_b, pa, hg, yb, yc, od, ld, wg, bg, wb, wo):
    n = x2.shape[0]
    tile = lambda w: pl.BlockSpec((MG_TM, w), lambda i: (i, 0))
    return pl.pallas_call(
        _merge_kernel,
        grid=(n // MG_TM,),
        in_specs=[tile(D_MODEL), _const_spec((1, D_MODEL)),
                  tile(D_BRANCH), tile(D_BRANCH),
                  pl.BlockSpec((MG_TM, D_BRANCH), lambda i: (i, 4)),
                  _const_spec((1, D_BRANCH)), _const_spec((D_BRANCH, D_BRANCH)),
                  tile(D_BRANCH), tile(D_BRANCH),
                  tile(D_BRANCH), tile(D_BRANCH), tile(D_BRANCH),
                  tile(D_BRANCH), tile(D_BRANCH), tile(D_BRANCH),
                  _const_spec((D_MODEL, N_BRANCH * D_MODEL)), _const_spec((1, N_BRANCH * D_MODEL)),
                  _const_spec((N_BRANCH, D_BRANCH, D_MODEL)), _const_spec((D_MODEL, D_MODEL))],
        out_specs=tile(D_MODEL),
        out_shape=jax.ShapeDtypeStruct((n, D_MODEL), F32),
        compiler_params=_cparams("parallel"),
        name="merge",
    )(x2, g1.reshape(1, D_MODEL), oa_f, oa_b, pa, hg.reshape(1, D_BRANCH), _head_ones(), yb, yc,
      od[0], od[1], od[2], ld[0], ld[1], ld[2], wg, bg, wb, wo)


HG_TS = 512
HG_NCH = HG_TS // A_CHUNK
HG_MID = A_CHUNK // 2


def _hgrn_cum_matrices():
    r = np.arange(HG_TS)[:, None]
    c = np.arange(HG_TS)[None, :]
    same = (r // A_CHUNK) == (c // A_CHUNK)
    return (jnp.asarray((same & (r >= c)).astype(np.float32), BF16),
            jnp.asarray((same & (r <= c)).astype(np.float32), BF16))


def _hgrn_block(q, fl, v, lb, cum, st, fwd):
    srow = lax.broadcasted_iota(jnp.int32, (A_HEADS * A_CHUNK, A_CHUNK), 0) % A_CHUNK
    scol = lax.broadcasted_iota(jnp.int32, (A_HEADS * A_CHUNK, A_CHUNK), 1)
    causal = (srow >= scol) if fwd else (srow <= scol)
    lane_head = lax.broadcasted_iota(jnp.int32, (A_CHUNK, D_BRANCH), 1) // A_DK
    hmask = [(lane_head == hh).astype(F32) for hh in range(A_HEADS)]
    blk_r = lax.broadcasted_iota(jnp.int32, (D_BRANCH, D_BRANCH), 0) // A_DK
    blk_c = lax.broadcasted_iota(jnp.int32, (D_BRANCH, D_BRANCH), 1) // A_DK
    blockdiag = (blk_r == blk_c).astype(F32)

    f = lb + (1.0 - lb) * _sigmoid(fl)
    g = jnp.log(jnp.maximum(f, TINY))
    kk = (1.0 - lb) * _sigmoid(-fl)
    g1 = g.astype(BF16)
    r1 = g - g1.astype(F32)
    g2 = r1.astype(BF16)
    g3 = (r1 - g2.astype(F32)).astype(BF16)
    b = (jnp.dot(cum, g1, preferred_element_type=F32) + jnp.dot(cum, g2, preferred_element_type=F32)
         + jnp.dot(cum, g3, preferred_element_type=F32))
    rows = lambda c: slice(c * A_CHUNK, (c + 1) * A_CHUNK)
    last = (A_CHUNK - 1) if fwd else 0
    bcast = lambda r0: jnp.concatenate(
        [jnp.broadcast_to(b[c * A_CHUNK + r0:c * A_CHUNK + r0 + 1, :], (A_CHUNK, D_BRANCH)) for c in range(HG_NCH)], 0)
    bm = bcast(HG_MID)
    bl = bcast(last)
    qt = q * jnp.exp(b - bm)
    kt = (kk * jnp.exp(bm - b)).astype(BF16)
    qe = (q * jnp.exp(b)).astype(BF16)
    kh = (kk * jnp.exp(bl - b)).astype(BF16)
    vb = v.astype(BF16)
    o_intra, upd = [], []
    for c in range(HG_NCH):
        qs = jnp.concatenate([qt[rows(c)] * hmask[hh] for hh in range(A_HEADS)], axis=0).astype(BF16)
        s = lax.dot_general(qs, kt[rows(c)], (((1,), (1,)), ((), ())), preferred_element_type=F32)
        s = jnp.where(causal, s, 0.0).astype(BF16)
        ost = jnp.dot(s, vb[rows(c)], preferred_element_type=F32)
        o = ost[0:A_CHUNK] * hmask[0]
        for hh in range(1, A_HEADS):
            o = o + ost[hh * A_CHUNK:(hh + 1) * A_CHUNK] * hmask[hh]
        o_intra.append(o)
        upd.append(jnp.dot(v[rows(c)].T.astype(BF16), kh[rows(c)], preferred_element_type=F32) * blockdiag)
    outs = [None] * HG_NCH
    for c in (range(HG_NCH) if fwd else range(HG_NCH - 1, -1, -1)):
        outs[c] = o_intra[c] + lax.dot_general(qe[rows(c)], st.astype(BF16), (((1,), (1,)), ((), ())),
                                               preferred_element_type=F32)
        st = st * jnp.exp(b[c * A_CHUNK + last:c * A_CHUNK + last + 1, :]) + upd[c]
    return jnp.concatenate(outs, axis=0), st


def _hgrn_kernel(qf_ref, ff_ref, vf_ref, qb_ref, fb_ref, vb_ref, lb_ref, cf_ref, cb_ref, of_ref, ob_ref, st_ref):
    @pl.when(pl.program_id(1) == 0)
    def _():
        st_ref[...] = jnp.zeros_like(st_ref)

    lb = lb_ref[...]
    o, st = _hgrn_block(qf_ref[...], ff_ref[...], vf_ref[...], lb, cf_ref[...], st_ref[0], True)
    of_ref[...] = o
    st_ref[0] = st
    o, st = _hgrn_block(qb_ref[...], fb_ref[...], vb_ref[...], lb, cb_ref[...], st_ref[1], False)
    ob_ref[...] = o
    st_ref[1] = st


def _hgrn(pa3, lb):
    bsz, seq, _ = pa3.shape
    nblk = seq // HG_TS
    blk = (None, HG_TS, D_BRANCH)
    up = lambda col: pl.BlockSpec(blk, lambda b, i: (b, i, col))
    down = lambda col: pl.BlockSpec(blk, lambda b, i: (b, nblk - 1 - i, col))
    shp = jax.ShapeDtypeStruct((bsz, seq, D_BRANCH), F32)
    cum_f, cum_b = _hgrn_cum_matrices()
    return pl.pallas_call(
        _hgrn_kernel,
        grid=(bsz, nblk),
        in_specs=[up(0), up(1), up(3), down(0), down(2), down(3), _const_spec((1, D_BRANCH)),
                  _const_spec((HG_TS, HG_TS)), _const_spec((HG_TS, HG_TS))],
        out_specs=[up(0), down(0)],
        out_shape=[shp, shp],
        scratch_shapes=[pltpu.VMEM((2, D_BRANCH, D_BRANCH), F32)],
        compiler_params=_cparams("parallel", "arbitrary"),
        name="hgrn2",
    )(pa3, pa3, pa3, pa3, pa3, pa3, lb.reshape(1, D_BRANCH), cum_f, cum_b)


RG_TB = 128
RG_PAD = SUBLANES
RG_LEFT = B_CONV // 2


def _dot3_rhs(x, wh, wl):
    xh = x.astype(BF16)
    xl = (x - xh.astype(F32)).astype(BF16)
    return (jnp.dot(xh, wh, preferred_element_type=F32) + jnp.dot(xl, wh, preferred_element_type=F32)
            + jnp.dot(xh, wl, preferred_element_type=F32))


def _group_scan(a, u, fwd):
    row = lax.broadcasted_iota(jnp.int32, a.shape, 1)
    k = 1
    while k < SUBLANES:
        if fwd:
            keep = row >= k
            us, as_ = pltpu.roll(u, k, 1), pltpu.roll(a, k, 1)
        else:
            keep = row < SUBLANES - k
            us, as_ = pltpu.roll(u, SUBLANES - k, 1), pltpu.roll(a, SUBLANES - k, 1)
        u = a * jnp.where(keep, us, 0.0) + u
        a = a * jnp.where(keep, as_, 1.0)
        k *= 2
    return a, u


def _block_scan(a, u, carry, fwd):
    t = a.shape[0]
    ngrp = t // SUBLANES
    ag, ug = _group_scan(a.reshape(ngrp, SUBLANES, D_BRANCH), u.reshape(ngrp, SUBLANES, D_BRANCH), fwd)
    hs = [None] * ngrp
    for g in (range(ngrp) if fwd else range(ngrp - 1, -1, -1)):
        h = ug[g] + ag[g] * carry
        hs[g] = h
        carry = h[SUBLANES - 1:SUBLANES, :] if fwd else h[0:1, :]
    return jnp.concatenate(hs, axis=0), carry


def _gelu_tanh(x):
    return 0.5 * x * (1.0 + jnp.tanh(math.sqrt(2.0 / math.pi) * (x + 0.044715 * (x * x * x))))


def _rglru_kernel(x_ref, gt_ref, cw_ref, cb_ref, wh_ref, wl_ref, bg_ref, lam_ref, o_ref, xp_ref, xc_ref):
    seq = x_ref.shape[0]
    nblk = seq // RG_TB
    xp_ref[0:RG_PAD, :] = jnp.zeros((RG_PAD, D_BRANCH), F32)
    xp_ref[RG_PAD + seq:2 * RG_PAD + seq, :] = jnp.zeros((RG_PAD, D_BRANCH), F32)
    xp_ref[RG_PAD:RG_PAD + seq, :] = x_ref[...]
    nl = -lam_ref[...]
    sp = jnp.maximum(nl, 0.0) + jnp.log(1.0 + jnp.exp(-jnp.abs(nl)))

    def block(i, carry, dirn):
        r0 = pl.multiple_of(i * RG_TB, RG_TB)
        if dirn == 0:
            win = xp_ref[pl.ds(r0, RG_TB + 2 * RG_PAD), :]
            xc = cb_ref[...]
            for j in range(B_CONV):
                s0 = RG_PAD + j - RG_LEFT
                xc = xc + cw_ref[j:j + 1, :] * win[s0:s0 + RG_TB, :]
            xc_ref[pl.ds(r0, RG_TB), :] = xc
        else:
            xc = xc_ref[pl.ds(r0, RG_TB), :]
        cols = slice(dirn * 2 * D_BRANCH, (dirn + 1) * 2 * D_BRANCH)
        gates = _dot3_rhs(xc, wh_ref[:, cols], wl_ref[:, cols]) + bg_ref[:, cols]
        r = _sigmoid(gates[:, :D_BRANCH])
        ig = _sigmoid(gates[:, D_BRANCH:])
        log_a = -LRU_C * r * sp[dirn:dirn + 1, :]
        a = jnp.exp(log_a)
        u = jnp.sqrt(jnp.maximum(-jnp.tanh(log_a) * (a * a + 1.0), 0.0)) * ig * xc
        h, carry = _block_scan(a, u, carry, dirn == 0)
        if dirn == 0:
            o_ref[pl.ds(r0, RG_TB), :] = h
        else:
            o_ref[pl.ds(r0, RG_TB), :] = (o_ref[pl.ds(r0, RG_TB), :] + h) * _gelu_tanh(gt_ref[pl.ds(r0, RG_TB), :])
        return carry

    zero = jnp.zeros((1, D_BRANCH), F32)
    lax.fori_loop(0, nblk, lambda i, c: block(i, c, 0), zero)
    lax.fori_loop(0, nblk, lambda i, c: block(nblk - 1 - i, c, 1), zero)


def _blockdiag(w):
    eye = jnp.eye(B_BLOCKS, dtype=w.dtype)
    return jnp.einsum('ncd,nm->ncmd', w, eye).reshape(D_BRANCH, D_BRANCH)


def _rglru(pb3, cw, cb, wa, ba, wx, bx, lam):
    bsz, seq, _ = pb3.shape
    wg = jnp.concatenate([_blockdiag(wa[0]), _blockdiag(wx[0]), _blockdiag(wa[1]), _blockdiag(wx[1])], axis=1)
    bg = jnp.concatenate([ba[0], bx[0], ba[1], bx[1]]).reshape(1, 4 * D_BRANCH)
    wh = wg.astype(BF16)
    wl = (wg - wh.astype(F32)).astype(BF16)
    blk = (None, seq, D_BRANCH)
    return pl.pallas_call(
        _rglru_kernel,
        grid=(bsz,),
        in_specs=[pl.BlockSpec(blk, lambda b: (b, 0, 0)),
                  pl.BlockSpec(blk, lambda b: (b, 0, 1)),
                  _const_spec((B_CONV, D_BRANCH)), _const_spec((1, D_BRANCH)),
                  _const_spec((D_BRANCH, 4 * D_BRANCH)), _const_spec((D_BRANCH, 4 * D_BRANCH)),
                  _const_spec((1, 4 * D_BRANCH)), _const_spec((2, D_BRANCH))],
        out_specs=pl.BlockSpec(blk, lambda b: (b, 0, 0)),
        out_shape=jax.ShapeDtypeStruct((bsz, seq, D_BRANCH), F32),
        scratch_shapes=[pltpu.VMEM((seq + 2 * RG_PAD, D_BRANCH), F32), pltpu.VMEM((seq, D_BRANCH), F32)],
        compiler_params=_cparams("parallel"),
        name="rglru",
    )(pb3, pb3, cw, cb.reshape(1, D_BRANCH), wh, wl, bg, lam)


FFT_IN = 128
FFT_NT = FFT_IN // SUBLANES
HY_TB = 256
HY_PAD = SUBLANES


def _split_np(a):
    a = np.asarray(a, np.float32)
    hi = a.astype(jnp.bfloat16)
    lo = (a - hi.astype(np.float32)).astype(jnp.bfloat16)
    return jnp.asarray(hi), jnp.asarray(lo)


def _dot3(mh, ml, x):
    xh = x.astype(BF16)
    xl = (x - xh.astype(F32)).astype(BF16)
    return (jnp.dot(mh, xh, preferred_element_type=F32) + jnp.dot(mh, xl, preferred_element_type=F32)
            + jnp.dot(ml, xh, preferred_element_type=F32))


def _fft_tables(seq):
    n = 2 * seq
    n1 = n // FFT_IN
    half = n1 // 2
    eye = np.eye(SUBLANES)
    a = 2.0 * np.pi * np.outer(np.arange(n1), np.arange(half)) / n1
    gr, gi = np.cos(a), -np.sin(a)
    blk = np.stack([np.stack([gr, -gi], axis=1), np.stack([gi, gr], axis=1)], axis=0)
    m_out = np.einsum('rkis,cd->rkcisd', blk, eye).reshape(2 * n1 * SUBLANES, 2 * half * SUBLANES)
    ir, ii = gr.T / n, -gi.T / n
    blk = np.stack([np.stack([ir, -ii], axis=1), np.stack([ii, ir], axis=1)], axis=0)
    m_inv = np.einsum('otrk,cd->otcrkd', blk, eye).reshape(2 * half * SUBLANES, 2 * n1 * SUBLANES)
    a = 2.0 * np.pi * np.outer(np.arange(FFT_IN), np.arange(FFT_IN)) / FFT_IN
    fr, fi = np.cos(a), -np.sin(a)
    w_fwd = np.block([[fr, -fi], [fi, fr]])
    w_inv = np.block([[fr, fi], [-fi, fr]])
    s_in = SUBLANES * np.arange(FFT_NT)[:, None, None] + np.arange(SUBLANES)[None, None, :]
    th = 2.0 * np.pi * s_in * np.arange(n1)[None, :, None] / n
    tw = (jnp.asarray(np.cos(th)[..., None], F32), jnp.asarray(-np.sin(th)[..., None], F32))
    return dict(m_out=_split_np(m_out), m_inv=_split_np(m_inv), w_fwd=_split_np(w_fwd), w_inv=_split_np(w_inv),
                tw=tw, n1=n1, half=half)


def _tw_spec(n1):
    return pl.BlockSpec((None, n1, SUBLANES, 1), lambda t, q: (t, 0, 0, 0))


def _fft_outer_kernel(mh_ref, ml_ref, twr_ref, twi_ref, z_ref, v_ref, *, rows_in):
    n1 = v_ref.shape[0]
    z = z_ref[...].reshape(rows_in, D_BRANCH)
    v = _dot3(mh_ref[:, :rows_in], ml_ref[:, :rows_in], z)
    vr = v[:n1 * SUBLANES].reshape(n1, SUBLANES, D_BRANCH)
    vi = v[n1 * SUBLANES:].reshape(n1, SUBLANES, D_BRANCH)
    tr, ti = twr_ref[...], twi_ref[...]
    v_ref[:, 0] = vr * tr - vi * ti
    v_ref[:, 1] = vr * ti + vi * tr


def _fft_outer(tb, z, complex_in):
    n1, half = tb["n1"], tb["half"]
    if complex_in:
        p = z.shape[1]
        zspec = pl.BlockSpec((2, None, half, None, SUBLANES, D_BRANCH), lambda t, q: (0, q, 0, t, 0, 0))
    else:
        p = z.shape[0]
        zspec = pl.BlockSpec((None, half, None, SUBLANES, D_BRANCH), lambda t, q: (q, 0, t, 0, 0))
    rows_in = (2 if complex_in else 1) * half * SUBLANES
    mshape = (2 * n1 * SUBLANES, 2 * half * SUBLANES)
    return pl.pallas_call(
        functools.partial(_fft_outer_kernel, rows_in=rows_in),
        grid=(FFT_NT, p),
        in_specs=[_const_spec(mshape), _const_spec(mshape), _tw_spec(n1), _tw_spec(n1), zspec],
        out_specs=pl.BlockSpec((None, n1, 2, None, SUBLANES, D_BRANCH), lambda t, q: (q, 0, 0, t, 0, 0)),
        out_shape=jax.ShapeDtypeStruct((p, n1, 2, FFT_NT, SUBLANES, D_BRANCH), F32),
        compiler_params=_cparams("arbitrary", "arbitrary"),
        name="fft_outer_c" if complex_in else "fft_outer_r",
    )(*tb["m_out"], *tb["tw"], z)


FFT_KB = 4


def _fft_filt_kernel(wh_ref, wl_ref, vf_ref, vb_ref, h_ref):
    for kb in range(FFT_KB):
        zf = _dot3(wh_ref[...], wl_ref[...], vf_ref[kb].reshape(2 * FFT_IN, D_BRANCH))
        zb = _dot3(wh_ref[...], wl_ref[...], vb_ref[kb].reshape(2 * FFT_IN, D_BRANCH))
        h_ref[kb, 0] = zf[:FFT_IN] + zb[:FFT_IN]
        h_ref[kb, 1] = zf[FFT_IN:] - zb[FFT_IN:]


def _fft_filt(tb, v):
    n1 = v.shape[1]
    wspec = _const_spec((2 * FFT_IN, 2 * FFT_IN))
    vblk = (None, FFT_KB, 2, FFT_NT, SUBLANES, D_BRANCH)
    return pl.pallas_call(
        _fft_filt_kernel,
        grid=(C_ORDER, n1 // FFT_KB),
        in_specs=[wspec, wspec,
                  pl.BlockSpec(vblk, lambda o, k: (2 * o, k, 0, 0, 0, 0)),
                  pl.BlockSpec(vblk, lambda o, k: (2 * o + 1, k, 0, 0, 0, 0))],
        out_specs=pl.BlockSpec((None, FFT_KB, 2, FFT_IN, D_BRANCH), lambda o, k: (o, k, 0, 0, 0)),
        out_shape=jax.ShapeDtypeStruct((C_ORDER, n1, 2, FFT_IN, D_BRANCH), F32),
        compiler_params=_cparams("parallel", "parallel"),
        name="fft_filt",
    )(*tb["w_fwd"], v, v)


def _fft_mid_kernel(wfh_ref, wfl_ref, wih_ref, wil_ref, v_ref, h_ref, d_ref):
    for kb in range(FFT_KB):
        z = _dot3(wfh_ref[...], wfl_ref[...], v_ref[kb].reshape(2 * FFT_IN, D_BRANCH))
        zr, zi = z[:FFT_IN], z[FFT_IN:]
        hr, hi = h_ref[kb, 0], h_ref[kb, 1]
        pr = zr * hr - zi * hi
        pi = zr * hi + zi * hr
        d = _dot3(wih_ref[...], wil_ref[...], jnp.concatenate([pr, pi], axis=0))
        d_ref[kb] = d.reshape(d_ref.shape[1:])


def _fft_mid(tb, v, hspec, order):
    p, n1 = v.shape[:2]
    vspec = pl.BlockSpec((None, FFT_KB, 2, FFT_NT, SUBLANES, D_BRANCH), lambda k, q: (q, k, 0, 0, 0, 0))
    wspec = _const_spec((2 * FFT_IN, 2 * FFT_IN))
    return pl.pallas_call(
        _fft_mid_kernel,
        grid=(n1 // FFT_KB, p),
        in_specs=[wspec, wspec, wspec, wspec, vspec,
                  pl.BlockSpec((None, FFT_KB, 2, FFT_IN, D_BRANCH), lambda k, q: (order, k, 0, 0, 0))],
        out_specs=vspec,
        out_shape=jax.ShapeDtypeStruct(v.shape, F32),
        compiler_params=_cparams("parallel", "arbitrary"),
        name="fft_mid",
    )(*tb["w_fwd"], *tb["w_inv"], v, hspec)


def _ifft_outer_kernel(mh_ref, ml_ref, twr_ref, twi_ref, d_ref, u_ref, x_ref, b_ref, o_ref):
    n1 = d_ref.shape[0]
    dr, di = d_ref[:, 0], d_ref[:, 1]
    tr, ti = twr_ref[...], twi_ref[...]
    er = (dr * tr + di * ti).reshape(n1 * SUBLANES, D_BRANCH)
    ei = (di * tr - dr * ti).reshape(n1 * SUBLANES, D_BRANCH)
    y = _dot3(mh_ref[...], ml_ref[...], jnp.concatenate([er, ei], axis=0)).reshape(o_ref.shape)
    o_ref[...] = x_ref[...] * (y + u_ref[...] * b_ref[...])


def _ifft_outer(tb, d, u, ucol, x, xcol, bias):
    n1, half = tb["n1"], tb["half"]
    p = d.shape[0]
    mshape = (2 * half * SUBLANES, 2 * n1 * SUBLANES)
    io = lambda col: pl.BlockSpec((2, None, half, None, SUBLANES, D_BRANCH), lambda t, q: (0, q, 0, t, 0, col))
    return pl.pallas_call(
        _ifft_outer_kernel,
        grid=(FFT_NT, p),
        in_specs=[_const_spec(mshape), _const_spec(mshape), _tw_spec(n1), _tw_spec(n1),
                  pl.BlockSpec((None, n1, 2, None, SUBLANES, D_BRANCH), lambda t, q: (q, 0, 0, t, 0, 0)),
                  io(ucol), io(xcol), _const_spec((1, D_BRANCH))],
        out_specs=io(0),
        out_shape=jax.ShapeDtypeStruct((2, p, half, FFT_NT, SUBLANES, D_BRANCH), F32),
        compiler_params=_cparams("arbitrary", "arbitrary"),
        name="ifft_outer",
    )(*tb["m_inv"], *tb["tw"], d, u, x, bias.reshape(1, D_BRANCH))


def _hyconv_kernel(x_ref, w_ref, b_ref, o_ref, xp_ref):
    seq = x_ref.shape[0]
    xp_ref[0:HY_PAD, :] = jnp.zeros((HY_PAD, D_BRANCH), F32)
    xp_ref[HY_PAD + seq:2 * HY_PAD + seq, :] = jnp.zeros((HY_PAD, D_BRANCH), F32)
    xp_ref[HY_PAD:HY_PAD + seq, :] = x_ref[...]

    def body(i, carry):
        r0 = pl.multiple_of(i * HY_TB, HY_TB)
        win = xp_ref[pl.ds(r0, HY_TB + 2 * HY_PAD), :]
        y = b_ref[...]
        for j in range(C_CONV):
            s0 = HY_PAD + j - C_CONV // 2
            y = y + w_ref[j:j + 1, :] * win[s0:s0 + HY_TB, :]
        o_ref[pl.ds(r0, HY_TB), :] = y
        return carry

    lax.fori_loop(0, seq // HY_TB, body, 0)


def _hyconv(pc3, cw, cb):
    bsz, seq, width = pc3.shape
    nb = width // D_BRANCH
    blk = pl.BlockSpec((None, seq, D_BRANCH), lambda b, j: (b, 0, j))
    return pl.pallas_call(
        _hyconv_kernel,
        grid=(bsz, nb),
        in_specs=[blk, pl.BlockSpec((C_CONV, D_BRANCH), lambda b, j: (0, j)),
                  pl.BlockSpec((1, D_BRANCH), lambda b, j: (0, j))],
        out_specs=blk,
        out_shape=jax.ShapeDtypeStruct(pc3.shape, F32),
        scratch_shapes=[pltpu.VMEM((seq + 2 * HY_PAD, D_BRANCH), F32)],
        compiler_params=_cparams("parallel", "parallel"),
        name="hyconv",
    )(pc3, cw, cb.reshape(1, width))


def _hyfilt_kernel(z_ref, w1_ref, b1_ref, fr_ref, w2_ref, b2_ref, w3_ref, dec_ref, o_ref, h_ref):
    seq = z_ref.shape[0]
    nblk = seq // HY_TB
    fr = fr_ref[...]

    @pl.when(pl.program_id(0) == 0)
    def _():
        def hidden(i, carry):
            r0 = pl.multiple_of(i * HY_TB, HY_TB)
            zb = z_ref[pl.ds(r0, HY_TB), :]
            h = jnp.sin(fr * (jnp.dot(zb, w1_ref[...], precision=HI, preferred_element_type=F32) + b1_ref[...]))
            h = jnp.sin(fr * (jnp.dot(h, w2_ref[...], precision=HI, preferred_element_type=F32) + b2_ref[...]))
            h_ref[pl.ds(r0, HY_TB), :] = h
            return carry

        lax.fori_loop(0, nblk, hidden, 0)

    def body(i, ss):
        r0 = pl.multiple_of(i * HY_TB, HY_TB)
        hf = jnp.dot(h_ref[pl.ds(r0, HY_TB), :], w3_ref[...], precision=HI, preferred_element_type=F32)
        hf = hf * jnp.exp(-z_ref[pl.ds(r0, HY_TB), 0:1] * dec_ref[...])
        o_ref[pl.ds(r0, HY_TB), :] = hf
        return ss + jnp.sum(hf * hf, axis=0, keepdims=True)

    ss = lax.fori_loop(0, nblk, body, jnp.zeros((1, D_BRANCH), F32))
    scale = lax.rsqrt(ss + EPS)

    def norm(i, carry):
        r0 = pl.multiple_of(i * HY_TB, HY_TB)
        o_ref[pl.ds(r0, HY_TB), :] = o_ref[pl.ds(r0, HY_TB), :] * scale
        return carry

    lax.fori_loop(0, nblk, norm, 0)


def _hyfilt(seq, w1, b1, freq, w2, b2, w3):
    t = jnp.linspace(0.0, 1.0, seq, dtype=F32)[:, None]
    bands = (C_EMB - 1) // 2
    w = 2.0 * math.pi * jnp.arange(seq, dtype=F32)[:, None] / seq
    fr = jnp.linspace(1e-4, bands - 1, bands, dtype=F32)[None]
    z = jnp.concatenate([t, jnp.cos(fr * w), -jnp.sin(fr * w)], axis=-1)
    z = jnp.pad(z, ((0, 0), (0, LANES - C_EMB)))
    padm = lambda a, r, c: jnp.pad(a.astype(F32), ((0, r - a.shape[0]), (0, c - a.shape[1])))
    row = lambda a: padm(a.reshape(1, -1), 1, LANES)
    dec = jnp.abs(jnp.linspace(C_MIN_DECAY, C_MAX_DECAY, D_BRANCH, dtype=F32)).reshape(1, D_BRANCH)
    nset = C_ORDER * 2
    return pl.pallas_call(
        _hyfilt_kernel,
        grid=(nset,),
        in_specs=[_const_spec((seq, LANES)), _const_spec((LANES, LANES)), _const_spec((1, LANES)),
                  _const_spec((1, LANES)), _const_spec((LANES, LANES)), _const_spec((1, LANES)),
                  pl.BlockSpec((LANES, D_BRANCH), lambda j: (0, j)), _const_spec((1, D_BRANCH))],
        out_specs=pl.BlockSpec((None, seq, D_BRANCH), lambda j: (j, 0, 0)),
        out_shape=jax.ShapeDtypeStruct((nset, seq, D_BRANCH), F32),
        scratch_shapes=[pltpu.VMEM((seq, LANES), F32)],
        compiler_params=_cparams("arbitrary"),
        name="hyfilt",
    )(z, padm(w1, LANES, LANES), row(b1), row(freq), padm(w2, LANES, LANES), row(b2),
      padm(w3, LANES, nset * D_BRANCH), dec)


def _hyena(pc3, cw, cb, w1, b1, freq, w2, b2, w3, bias):
    bsz, seq, width = pc3.shape
    tb = _fft_tables(seq)
    half = tb["half"]
    npair = bsz // 2
    uc = _hyconv(pc3, cw, cb)
    uc6 = uc.reshape(2, npair, half, FFT_NT, SUBLANES, width)
    filt = _hyfilt(seq, w1, b1, freq, w2, b2, w3).reshape(C_ORDER * 2, half, FFT_NT, SUBLANES, D_BRANCH)
    hspec = _fft_filt(tb, _fft_outer(tb, filt, False))
    src = uc6
    for order in range(C_ORDER):
        d = _fft_mid(tb, _fft_outer(tb, src, True), hspec, order)
        src = _ifft_outer(tb, d, src, 0, uc6, order + 1, bias[order])
    return src.reshape(bsz, seq, D_BRANCH)


AT_HALF = 64
AT_TQ = 512
AT_SUB = 128


def _t5_bucket(rel):
    half = N_BUCKETS // 2
    max_exact = half // 2
    n = np.abs(rel)
    large = max_exact + (np.log(np.maximum(n, 1) / max_exact) / math.log(MAX_DISTANCE / max_exact)
                         * (half - max_exact)).astype(np.int64)
    large = np.minimum(large, half - 1)
    return (rel > 0).astype(np.int64) * half + np.where(n < max_exact, n, large)


def _attn_geometry(n):
    tq = min(AT_TQ, n)
    sub = min(AT_SUB, tq)
    win = min(sub + 2 * AT_HALF, n)
    return tq, sub, win, n // tq, n // sub


def _attn_bias_tables(rel_bias, g, dil, n):
    _, sub, win, _, nsb = _attn_geometry(n)
    hs = slice(g * D_HEADS_PER_GROUP, (g + 1) * D_HEADS_PER_GROUP)
    offsets = np.arange(-AT_HALF, AT_HALF + 1) * dil
    onehot = np.zeros((2 * AT_HALF + 1, N_BUCKETS), np.float32)
    onehot[np.arange(2 * AT_HALF + 1), _t5_bucket(offsets)] = 1.0
    band = jnp.dot(rel_bias.astype(F32)[:, hs].T, jnp.asarray(onehot).T, precision=HI)
    nband = 2 * AT_HALF + 1
    lv = sub + win - 1
    tables = []
    for i in sorted({0, min(1, nsb - 1), nsb - 1}):
        ws = int(np.clip(i * sub - AT_HALF, 0, n - win))
        lo = (sub - 1) - (ws - i * sub) - AT_HALF
        v = jnp.pad(band, ((0, 0), (lo, lv - lo - nband)), constant_values=NEG_BIG)
        flat = jnp.tile(v, (1, sub + 1))[:, sub - 1:sub - 1 + sub * (lv - 1)]
        tables.append(flat.reshape(D_HEADS_PER_GROUP, sub, lv - 1)[:, :, :win])
    return jnp.stack(tables)


def _attn_kernel(q_ref, k_ref, v_ref, bias_ref, o_ref, l_ref, *, n, dil):
    tq, sub, win, _, nsb = _attn_geometry(n)
    ncase = bias_ref.shape[0]
    width = q_ref.shape[-1]
    nh = width // D_HEAD_DIM
    hp = pl.program_id(1)
    lane_head = lax.broadcasted_iota(jnp.int32, (sub, width), 1) // D_HEAD_DIM
    hmask = [lane_head == hh for hh in range(nh)]
    whole = win == n
    for r in range(dil):
        if whole:
            kw = k_ref[pl.ds(r, win, stride=dil), :].astype(BF16)
            vw = v_ref[pl.ds(r, win, stride=dil), :].astype(BF16)
        for j in range(tq // sub):
            sidx = pl.program_id(2) * (tq // sub) + j
            case = jnp.minimum(jnp.where(sidx == nsb - 1, ncase - 1, jnp.minimum(sidx, 1)), ncase - 1)
            bias = bias_ref[case, pl.ds(hp * nh, nh)].reshape(nh * sub, win)
            if not whole:
                ws = pl.multiple_of(jnp.clip(sidx * sub - AT_HALF, 0, n - win), AT_HALF)
                kw = k_ref[pl.ds(ws * dil + r, win, stride=dil), :].astype(BF16)
                vw = v_ref[pl.ds(ws * dil + r, win, stride=dil), :].astype(BF16)
            q = q_ref[pl.ds(j * sub * dil + r, sub, stride=dil), :] * (D_HEAD_DIM ** -0.5)
            qs = jnp.concatenate([jnp.where(hmask[hh], q, 0.0) for hh in range(nh)], axis=0).astype(BF16)
            s = lax.dot_general(qs, kw, (((1,), (1,)), ((), ())), preferred_element_type=F32) + bias
            m = jnp.max(s, axis=-1, keepdims=True)
            p = jnp.exp(s - m)
            l = jnp.sum(p, axis=-1, keepdims=True)
            o_all = jnp.dot(p.astype(BF16), vw, preferred_element_type=F32) / l
            lse = m + jnp.log(l)
            o_acc = jnp.zeros((sub, width), F32)
            l_acc = jnp.zeros((sub, width), F32)
            for hh in range(nh):
                o_acc = jnp.where(hmask[hh], o_all[hh * sub:(hh + 1) * sub], o_acc)
                l_acc = jnp.where(hmask[hh], lse[hh * sub:(hh + 1) * sub], l_acc)
            o_ref[pl.ds(j * sub * dil + r, sub, stride=dil), :] = o_acc
            l_ref[pl.ds(j * sub * dil + r, sub, stride=dil), :] = l_acc


def _banded_attention(pd3, g, dil, bias):
    bsz, seq, width = pd3.shape
    n = seq // dil
    tq, sub, win, nq, _ = _attn_geometry(n)
    rows = dil * tq
    bw = LANES if dil > 1 else D_BRANCH
    per = D_BRANCH // bw
    third = width // 3 // bw
    out_spec = pl.BlockSpec((None, rows, bw), lambda b, h, i: (b, i, h))
    shp = jax.ShapeDtypeStruct((bsz, seq, D_BRANCH), F32)
    return pl.pallas_call(
        functools.partial(_attn_kernel, n=n, dil=dil),
        grid=(bsz, per, nq),
        in_specs=[pl.BlockSpec((None, rows, bw), lambda b, h, i: (b, i, g * per + h)),
                  pl.BlockSpec((None, seq, bw), lambda b, h, i: (b, 0, third + g * per + h)),
                  pl.BlockSpec((None, seq, bw), lambda b, h, i: (b, 0, 2 * third + g * per + h)),
                  _const_spec(bias.shape)],
        out_specs=[out_spec, out_spec],
        out_shape=[shp, shp],
        compiler_params=_cparams("parallel", "parallel", "arbitrary"),
        name=f"attn_d{dil}",
    )(pd3, pd3, pd3, bias)


def _dilated_attention(pd3, rel_bias):
    bsz, seq, _ = pd3.shape
    outs, lses = [], []
    for g, (_, dil) in enumerate(D_GROUPS):
        o, l = _banded_attention(pd3, g, dil, _attn_bias_tables(rel_bias, g, dil, seq // dil))
        outs.append(o.reshape(bsz * seq, D_BRANCH))
        lses.append(l.reshape(bsz * seq, D_BRANCH))
    return outs, lses


def kernel(x, norm1_g, w_in, hgrn_lb_logits, hgrn_norm_g, lru_conv_w, lru_conv_b, lru_wa, lru_ba, lru_wx, lru_bx,
           lru_lambda, hy_conv_w, hy_conv_b, hy_w1, hy_b1, hy_freq, hy_w2, hy_b2, hy_w3, hy_bias, rel_bias,
           w_branch, w_gate, b_gate, w_out, norm2_g, w_ff1, w_ff3, w_ff2, final_g):
    bsz, seq, _ = x.shape
    n = bsz * seq
    lb_soft = jax.nn.softmax(hgrn_lb_logits.astype(F32), axis=0)
    lower_bounds = jnp.cumsum(lb_soft, axis=0) - lb_soft[0]
    x2 = x.reshape(n, D_MODEL)
    flat = lambda a: a.reshape(n, D_BRANCH)
    for l in range(DEPTH):
        pa, pb, pc, pd = _inproj(x2, norm1_g[l], w_in[l].astype(BF16))
        oa_f, oa_b = _hgrn(pa.reshape(bsz, seq, IN_A), lower_bounds[l])
        yb = _rglru(pb.reshape(bsz, seq, IN_B), lru_conv_w[l], lru_conv_b[l], lru_wa[l], lru_ba[l],
                    lru_wx[l], lru_bx[l], lru_lambda[l])
        yc = _hyena(pc.reshape(bsz, seq, IN_C), hy_conv_w[l], hy_conv_b[l], hy_w1[l], hy_b1[l], hy_freq[l],
                    hy_w2[l], hy_b2[l], hy_w3[l], hy_bias[l])
        od, ld = _dilated_attention(pd.reshape(bsz, seq, IN_D), rel_bias)
        x2 = _merge(x2, norm1_g[l], flat(oa_f), flat(oa_b), pa, hgrn_norm_g[l], flat(yb), flat(yc), od, ld,
                    w_gate[l].reshape(D_MODEL, N_BRANCH * D_MODEL).astype(BF16),
                    b_gate[l].reshape(1, N_BRANCH * D_MODEL), w_branch[l].astype(BF16), w_out[l].astype(BF16))
        x2 = _ffn(x2, norm2_g[l], *_ffn_weights(w_ff1[l], w_ff3[l], w_ff2[l]), final_g, l == DEPTH - 1)
    return x2.reshape(bsz, seq, D_MODEL)
```

```python
import functools
import math

import jax
import jax.numpy as jnp
import numpy as np
from jax import lax
from jax.experimental import pallas as pl
from jax.experimental.pallas import tpu as pltpu

F32 = jnp.float32
BF16 = jnp.bfloat16
HI = lax.Precision.HIGHEST

D_MODEL = 1024
DEPTH = 2
EPS = 1e-6
TINY = 1e-30
N_BRANCH = 4
D_BRANCH = 256
A_HEADS = 4
A_DK = 64
A_CHUNK = 64
B_BLOCKS = 4
B_BW = 64
B_CONV = 4
LRU_C = 8.0
C_ORDER = 2
C_CONV = 3
C_EMB = 33
C_HID = 64
C_MIN_DECAY = math.log(1e-2) / 1.5
C_MAX_DECAY = math.log(1e-2) / 0.3
D_GROUPS = ((128, 1), (512, 4), (2048, 16))
D_HEADS_PER_GROUP = 4
D_HEAD_DIM = 64
D_N_HEADS = 12
D_QKV = 768
N_BUCKETS = 32
MAX_DISTANCE = 1024
NEG_BIG = -1e30
D_FF = 2816
IN_A = 5 * D_BRANCH
IN_B = 2 * D_BRANCH
IN_C = 3 * D_BRANCH
IN_D = 3 * D_QKV
IN_WIDTH = IN_A + IN_B + IN_C + IN_D

LANES = 128
SUBLANES = 8
VMEM_LIMIT = 56 * 1024 * 1024


def _cparams(*sem):
    return pltpu.CompilerParams(dimension_semantics=sem, vmem_limit_bytes=VMEM_LIMIT)


def _const_spec(shape):
    nd = len(shape)
    return pl.BlockSpec(shape, lambda *_: (0,) * nd, pipeline_mode=pl.Buffered(1))


def _rms(x, g):
    return x * lax.rsqrt(jnp.mean(x * x, axis=-1, keepdims=True) + EPS) * g


def _sigmoid(x):
    return 1.0 / (1.0 + jnp.exp(-x))


IN_TM = 512
IN_CHUNK = 256


def _inproj_kernel(x_ref, g_ref, w_ref, oa_ref, ob_ref, oc_ref, od_ref):
    h = _rms(x_ref[...], g_ref[...]).astype(BF16)
    off = 0
    for o_ref in (oa_ref, ob_ref, oc_ref, od_ref):
        width = o_ref.shape[-1]
        for c in range(0, width, IN_CHUNK):
            o_ref[:, c:c + IN_CHUNK] = jnp.dot(h, w_ref[:, off + c:off + c + IN_CHUNK],
                                               preferred_element_type=F32)
        off += width


def _inproj(x2, g, w_bf16):
    n = x2.shape[0]
    widths = (IN_A, IN_B, IN_C, IN_D)
    return pl.pallas_call(
        _inproj_kernel,
        grid=(n // IN_TM,),
        in_specs=[pl.BlockSpec((IN_TM, D_MODEL), lambda i: (i, 0)),
                  _const_spec((1, D_MODEL)),
                  _const_spec((D_MODEL, IN_WIDTH))],
        out_specs=[pl.BlockSpec((IN_TM, w), lambda i: (i, 0)) for w in widths],
        out_shape=[jax.ShapeDtypeStruct((n, w), F32) for w in widths],
        compiler_params=_cparams("parallel"),
        name="inproj",
    )(x2, g.reshape(1, D_MODEL), w_bf16)


FF_TM = 1024
FF_CHUNK = 256
FF_NCHUNK = D_FF // FF_CHUNK


def _ffn_kernel(x_ref, g_ref, w1_ref, w3_ref, w2_ref, fg_ref, o_ref, acc_ref, *, final):
    x = x_ref[...]
    h = _rms(x, g_ref[...]).astype(BF16)
    acc_ref[...] = x

    def body(c, carry):
        a = jnp.dot(h, w1_ref[c], preferred_element_type=F32)
        b = jnp.dot(h, w3_ref[c], preferred_element_type=F32)
        t = (a * _sigmoid(a) * b).astype(BF16)
        acc_ref[...] += jnp.dot(t, w2_ref[c], preferred_element_type=F32)
        return carry

    lax.fori_loop(0, FF_NCHUNK, body, 0)
    y = acc_ref[...]
    if final:
        y = _rms(y, fg_ref[...])
    o_ref[...] = y


def _ffn(x2, g, w1c, w3c, w2c, final_g, final):
    n = x2.shape[0]
    return pl.pallas_call(
        functools.partial(_ffn_kernel, final=final),
        grid=(n // FF_TM,),
        in_specs=[pl.BlockSpec((FF_TM, D_MODEL), lambda i: (i, 0)),
                  _const_spec((1, D_MODEL)),
                  _const_spec((FF_NCHUNK, D_MODEL, FF_CHUNK)),
                  _const_spec((FF_NCHUNK, D_MODEL, FF_CHUNK)),
                  _const_spec((FF_NCHUNK, FF_CHUNK, D_MODEL)),
                  _const_spec((1, D_MODEL))],
        out_specs=pl.BlockSpec((FF_TM, D_MODEL), lambda i: (i, 0)),
        out_shape=jax.ShapeDtypeStruct((n, D_MODEL), F32),
        scratch_shapes=[pltpu.VMEM((FF_TM, D_MODEL), F32)],
        compiler_params=_cparams("parallel"),
        name="ffn_final" if final else "ffn",
    )(x2, g.reshape(1, D_MODEL), w1c, w3c, w2c, final_g.reshape(1, D_MODEL))


def _ffn_weights(w1, w3, w2):
    w1c = w1.astype(BF16).reshape(D_MODEL, FF_NCHUNK, FF_CHUNK).transpose(1, 0, 2)
    w3c = w3.astype(BF16).reshape(D_MODEL, FF_NCHUNK, FF_CHUNK).transpose(1, 0, 2)
    w2c = w2.astype(BF16).reshape(FF_NCHUNK, FF_CHUNK, D_MODEL)
    return w1c, w3c, w2c


MG_TM = 512


def _head_ones():
    r = np.arange(D_BRANCH)[:, None] // A_DK
    c = np.arange(D_BRANCH)[None, :] // A_DK
    return jnp.asarray((r == c).astype(np.float32) / A_DK)


def _merge_kernel(x_ref, g1_ref, oaf_ref, oab_ref, ga_ref, hg_ref, hm_ref, yb_ref, yc_ref,
                  o0_ref, o1_ref, o2_ref, l0_ref, l1_ref, l2_ref,
                  wg_ref, bg_ref, wb_ref, wo_ref, out_ref):
    x = x_ref[...]
    h = _rms(x, g1_ref[...]).astype(BF16)
    oa = oaf_ref[...] + oab_ref[...]
    ms = jnp.dot(oa * oa, hm_ref[...], precision=HI, preferred_element_type=F32)
    ga = ga_ref[...]
    ya = oa * lax.rsqrt(ms + EPS) * hg_ref[...] * (ga * _sigmoid(ga))
    l0, l1, l2 = l0_ref[...], l1_ref[...], l2_ref[...]
    m = jnp.maximum(jnp.maximum(l0, l1), l2)
    e0, e1, e2 = jnp.exp(l0 - m), jnp.exp(l1 - m), jnp.exp(l2 - m)
    yd = (e0 * o0_ref[...] + e1 * o1_ref[...] + e2 * o2_ref[...]) / (e0 + e1 + e2)
    mixed = None
    for j, y in enumerate((ya, yb_ref[...], yc_ref[...], yd)):
        gate = _sigmoid(jnp.dot(h, wg_ref[:, j * D_MODEL:(j + 1) * D_MODEL], preferred_element_type=F32)
                        + bg_ref[:, j * D_MODEL:(j + 1) * D_MODEL])
        t = gate * jnp.dot(y.astype(BF16), wb_ref[j], preferred_element_type=F32)
        mixed = t if mixed is None else mixed + t
    out_ref[...] = x + jnp.dot(mixed.astype(BF16), wo_ref[...], preferred_element_type=F32)


def _merge(x2, g1, oa_f, oa_b, pa, hg, yb, yc, od, ld, wg, bg, wb, wo):
    n = x2.shape[0]
    tile = lambda w: pl.BlockSpec((MG_TM, w), lambda i: (i, 0))
    return pl.pallas_call(
        _merge_kernel,
        grid=(n // MG_TM,),
        in_specs=[tile(D_MODEL), _const_spec((1, D_MODEL)),
                  tile(D_BRANCH), tile(D_BRANCH),
                  pl.BlockSpec((MG_TM, D_BRANCH), lambda i: (i, 4)),
                  _const_spec((1, D_BRANCH)), _const_spec((D_BRANCH, D_BRANCH)),
                  tile(D_BRANCH), tile(D_BRANCH),
                  tile(D_BRANCH), tile(D_BRANCH), tile(D_BRANCH),
                  tile(D_BRANCH), tile(D_BRANCH), tile(D_BRANCH),
                  _const_spec((D_MODEL, N_BRANCH * D_MODEL)), _const_spec((1, N_BRANCH * D_MODEL)),
                  _const_spec((N_BRANCH, D_BRANCH, D_MODEL)), _const_spec((D_MODEL, D_MODEL))],
        out_specs=tile(D_MODEL),
        out_shape=jax.ShapeDtypeStruct((n, D_MODEL), F32),
        compiler_params=_cparams("parallel"),
        name="merge",
    )(x2, g1.reshape(1, D_MODEL), oa_f, oa_b, pa, hg.reshape(1, D_BRANCH), _head_ones(), yb, yc,
      od[0], od[1], od[2], ld[0], ld[1], ld[2], wg, bg, wb, wo)


HG_TS = 256
HG_NCH = HG_TS // A_CHUNK
HG_MID = A_CHUNK // 2


def _hgrn_cum_matrices():
    r = np.arange(HG_TS)[:, None]
    c = np.arange(HG_TS)[None, :]
    same = (r // A_CHUNK) == (c // A_CHUNK)
    return (jnp.asarray((same & (r >= c)).astype(np.float32), BF16),
            jnp.asarray((same & (r <= c)).astype(np.float32), BF16))


def _hgrn_block(q, fl, v, lb, cum, st, fwd):
    srow = lax.broadcasted_iota(jnp.int32, (A_HEADS * A_CHUNK, A_CHUNK), 0) % A_CHUNK
    scol = lax.broadcasted_iota(jnp.int32, (A_HEADS * A_CHUNK, A_CHUNK), 1)
    causal = (srow >= scol) if fwd else (srow <= scol)
    lane_head = lax.broadcasted_iota(jnp.int32, (A_CHUNK, D_BRANCH), 1) // A_DK
    hmask = [(lane_head == hh).astype(F32) for hh in range(A_HEADS)]
    blk_r = lax.broadcasted_iota(jnp.int32, (D_BRANCH, D_BRANCH), 0) // A_DK
    blk_c = lax.broadcasted_iota(jnp.int32, (D_BRANCH, D_BRANCH), 1) // A_DK
    blockdiag = (blk_r == blk_c).astype(F32)

    sg = _sigmoid(fl)
    f = lb + (1.0 - lb) * sg
    g = jnp.log(jnp.maximum(f, TINY))
    kk = (1.0 - lb) * (1.0 - sg)
    g1 = g.astype(BF16)
    r1 = g - g1.astype(F32)
    g2 = r1.astype(BF16)
    g3 = (r1 - g2.astype(F32)).astype(BF16)
    b = (jnp.dot(cum, g1, preferred_element_type=F32) + jnp.dot(cum, g2, preferred_element_type=F32)
         + jnp.dot(cum, g3, preferred_element_type=F32))
    rows = lambda c: slice(c * A_CHUNK, (c + 1) * A_CHUNK)
    last = (A_CHUNK - 1) if fwd else 0
    bcast = lambda r0: jnp.concatenate(
        [jnp.broadcast_to(b[c * A_CHUNK + r0:c * A_CHUNK + r0 + 1, :], (A_CHUNK, D_BRANCH)) for c in range(HG_NCH)], 0)
    bm = bcast(HG_MID)
    bl = bcast(last)
    qt = q * jnp.exp(b - bm)
    kt = (kk * jnp.exp(bm - b)).astype(BF16)
    qe = (q * jnp.exp(b)).astype(BF16)
    kh = (kk * jnp.exp(bl - b)).astype(BF16)
    vb = v.astype(BF16)
    o_intra, upd = [], []
    for c in range(HG_NCH):
        qs = jnp.concatenate([qt[rows(c)] * hmask[hh] for hh in range(A_HEADS)], axis=0).astype(BF16)
        s = lax.dot_general(qs, kt[rows(c)], (((1,), (1,)), ((), ())), preferred_element_type=F32)
        s = jnp.where(causal, s, 0.0).astype(BF16)
        ost = jnp.dot(s, vb[rows(c)], preferred_element_type=F32)
        o = ost[0:A_CHUNK] * hmask[0]
        for hh in range(1, A_HEADS):
            o = o + ost[hh * A_CHUNK:(hh + 1) * A_CHUNK] * hmask[hh]
        o_intra.append(o)
        upd.append(jnp.dot(v[rows(c)].T.astype(BF16), kh[rows(c)], preferred_element_type=F32) * blockdiag)
    outs = [None] * HG_NCH
    for c in (range(HG_NCH) if fwd else range(HG_NCH - 1, -1, -1)):
        outs[c] = o_intra[c] + lax.dot_general(qe[rows(c)], st.astype(BF16), (((1,), (1,)), ((), ())),
                                               preferred_element_type=F32)
        st = st * jnp.exp(b[c * A_CHUNK + last:c * A_CHUNK + last + 1, :]) + upd[c]
    return jnp.concatenate(outs, axis=0), st


def _hgrn_kernel(qf_ref, ff_ref, vf_ref, qb_ref, fb_ref, vb_ref, lb_ref, cf_ref, cb_ref, of_ref, ob_ref, st_ref):
    @pl.when(pl.program_id(1) == 0)
    def _():
        st_ref[...] = jnp.zeros_like(st_ref)

    lb = lb_ref[...]
    o, st = _hgrn_block(qf_ref[...], ff_ref[...], vf_ref[...], lb, cf_ref[...], st_ref[0], True)
    of_ref[...] = o
    st_ref[0] = st
    o, st = _hgrn_block(qb_ref[...], fb_ref[...], vb_ref[...], lb, cb_ref[...], st_ref[1], False)
    ob_ref[...] = o
    st_ref[1] = st


def _hgrn(pa3, lb):
    bsz, seq, _ = pa3.shape
    nblk = seq // HG_TS
    blk = (None, HG_TS, D_BRANCH)
    up = lambda col: pl.BlockSpec(blk, lambda b, i: (b, i, col))
    down = lambda col: pl.BlockSpec(blk, lambda b, i: (b, nblk - 1 - i, col))
    shp = jax.ShapeDtypeStruct((bsz, seq, D_BRANCH), F32)
    cum_f, cum_b = _hgrn_cum_matrices()
    return pl.pallas_call(
        _hgrn_kernel,
        grid=(bsz, nblk),
        in_specs=[up(0), up(1), up(3), down(0), down(2), down(3), _const_spec((1, D_BRANCH)),
                  _const_spec((HG_TS, HG_TS)), _const_spec((HG_TS, HG_TS))],
        out_specs=[up(0), down(0)],
        out_shape=[shp, shp],
        scratch_shapes=[pltpu.VMEM((2, D_BRANCH, D_BRANCH), F32)],
        compiler_params=_cparams("parallel", "arbitrary"),
        name="hgrn2",
    )(pa3, pa3, pa3, pa3, pa3, pa3, lb.reshape(1, D_BRANCH), cum_f, cum_b)


RG_TB = 128
RG_PAD = SUBLANES
RG_LEFT = B_CONV // 2


def _dot3_rhs(x, wh, wl):
    xh = x.astype(BF16)
    xl = (x - xh.astype(F32)).astype(BF16)
    return (jnp.dot(xh, wh, preferred_element_type=F32) + jnp.dot(xl, wh, preferred_element_type=F32)
            + jnp.dot(xh, wl, preferred_element_type=F32))


def _group_scan(a, u, fwd):
    row = lax.broadcasted_iota(jnp.int32, a.shape, 1)
    k = 1
    while k < SUBLANES:
        if fwd:
            keep = row >= k
            us, as_ = pltpu.roll(u, k, 1), pltpu.roll(a, k, 1)
        else:
            keep = row < SUBLANES - k
            us, as_ = pltpu.roll(u, SUBLANES - k, 1), pltpu.roll(a, SUBLANES - k, 1)
        u = a * jnp.where(keep, us, 0.0) + u
        a = a * jnp.where(keep, as_, 1.0)
        k *= 2
    return a, u


def _block_scan(a, u, carry, fwd):
    t = a.shape[0]
    ngrp = t // SUBLANES
    ag, ug = _group_scan(a.reshape(ngrp, SUBLANES, D_BRANCH), u.reshape(ngrp, SUBLANES, D_BRANCH), fwd)
    hs = [None] * ngrp
    for g in (range(ngrp) if fwd else range(ngrp - 1, -1, -1)):
        h = ug[g] + ag[g] * carry
        hs[g] = h
        carry = h[SUBLANES - 1:SUBLANES, :] if fwd else h[0:1, :]
    return jnp.concatenate(hs, axis=0), carry


def _gelu_tanh(x):
    return 0.5 * x * (1.0 + jnp.tanh(math.sqrt(2.0 / math.pi) * (x + 0.044715 * (x * x * x))))


def _rglru_kernel(x_ref, gt_ref, cw_ref, cb_ref, wh_ref, wl_ref, bg_ref, lam_ref, o_ref, xp_ref, xc_ref):
    seq = x_ref.shape[0]
    nblk = seq // RG_TB
    xp_ref[0:RG_PAD, :] = jnp.zeros((RG_PAD, D_BRANCH), F32)
    xp_ref[RG_PAD + seq:2 * RG_PAD + seq, :] = jnp.zeros((RG_PAD, D_BRANCH), F32)
    xp_ref[RG_PAD:RG_PAD + seq, :] = x_ref[...]
    nl = -lam_ref[...]
    sp = jnp.maximum(nl, 0.0) + jnp.log(1.0 + jnp.exp(-jnp.abs(nl)))

    def block(i, carry, dirn):
        r0 = pl.multiple_of(i * RG_TB, RG_TB)
        if dirn == 0:
            win = xp_ref[pl.ds(r0, RG_TB + 2 * RG_PAD), :]
            xc = cb_ref[...]
            for j in range(B_CONV):
                s0 = RG_PAD + j - RG_LEFT
                xc = xc + cw_ref[j:j + 1, :] * win[s0:s0 + RG_TB, :]
            xc_ref[pl.ds(r0, RG_TB), :] = xc
        else:
            xc = xc_ref[pl.ds(r0, RG_TB), :]
        cols = slice(dirn * 2 * D_BRANCH, (dirn + 1) * 2 * D_BRANCH)
        gates = _dot3_rhs(xc, wh_ref[:, cols], wl_ref[:, cols]) + bg_ref[:, cols]
        r = _sigmoid(gates[:, :D_BRANCH])
        ig = _sigmoid(gates[:, D_BRANCH:])
        log_a = -LRU_C * r * sp[dirn:dirn + 1, :]
        a = jnp.exp(log_a)
        u = jnp.sqrt(jnp.maximum(-jnp.tanh(log_a) * (a * a + 1.0), 0.0)) * ig * xc
        h, carry = _block_scan(a, u, carry, dirn == 0)
        if dirn == 0:
            o_ref[pl.ds(r0, RG_TB), :] = h
        else:
            o_ref[pl.ds(r0, RG_TB), :] = (o_ref[pl.ds(r0, RG_TB), :] + h) * _gelu_tanh(gt_ref[pl.ds(r0, RG_TB), :])
        return carry

    zero = jnp.zeros((1, D_BRANCH), F32)
    lax.fori_loop(0, nblk, lambda i, c: block(i, c, 0), zero)
    lax.fori_loop(0, nblk, lambda i, c: block(nblk - 1 - i, c, 1), zero)


def _blockdiag(w):
    eye = jnp.eye(B_BLOCKS, dtype=w.dtype)
    return jnp.einsum('ncd,nm->ncmd', w, eye).reshape(D_BRANCH, D_BRANCH)


def _rglru(pb3, cw, cb, wa, ba, wx, bx, lam):
    bsz, seq, _ = pb3.shape
    wg = jnp.concatenate([_blockdiag(wa[0]), _blockdiag(wx[0]), _blockdiag(wa[1]), _blockdiag(wx[1])], axis=1)
    bg = jnp.concatenate([ba[0], bx[0], ba[1], bx[1]]).reshape(1, 4 * D_BRANCH)
    wh = wg.astype(BF16)
    wl = (wg - wh.astype(F32)).astype(BF16)
    blk = (None, seq, D_BRANCH)
    return pl.pallas_call(
        _rglru_kernel,
        grid=(bsz,),
        in_specs=[pl.BlockSpec(blk, lambda b: (b, 0, 0)),
                  pl.BlockSpec(blk, lambda b: (b, 0, 1)),
                  _const_spec((B_CONV, D_BRANCH)), _const_spec((1, D_BRANCH)),
                  _const_spec((D_BRANCH, 4 * D_BRANCH)), _const_spec((D_BRANCH, 4 * D_BRANCH)),
                  _const_spec((1, 4 * D_BRANCH)), _const_spec((2, D_BRANCH))],
        out_specs=pl.BlockSpec(blk, lambda b: (b, 0, 0)),
        out_shape=jax.ShapeDtypeStruct((bsz, seq, D_BRANCH), F32),
        scratch_shapes=[pltpu.VMEM((seq + 2 * RG_PAD, D_BRANCH), F32), pltpu.VMEM((seq, D_BRANCH), F32)],
        compiler_params=_cparams("parallel"),
        name="rglru",
    )(pb3, pb3, cw, cb.reshape(1, D_BRANCH), wh, wl, bg, lam)


FFT_IN = 128
FFT_NT = FFT_IN // SUBLANES
FFT_ACT_PASSES = 3
HY_TB = 256
HY_PAD = SUBLANES


def _split_np(a):
    a = np.asarray(a, np.float32)
    hi = a.astype(jnp.bfloat16)
    lo = (a - hi.astype(np.float32)).astype(jnp.bfloat16)
    return jnp.asarray(hi), jnp.asarray(lo)


def _dot3(mh, ml, x):
    xh = x.astype(BF16)
    xl = (x - xh.astype(F32)).astype(BF16)
    return (jnp.dot(mh, xh, preferred_element_type=F32) + jnp.dot(mh, xl, preferred_element_type=F32)
            + jnp.dot(ml, xh, preferred_element_type=F32))


def _dotp(mh, ml, x, passes):
    if passes == 1:
        return jnp.dot(mh, x.astype(BF16), preferred_element_type=F32)
    return _dot3(mh, ml, x)


def _fft_tables(seq):
    n = 2 * seq
    n1 = n // FFT_IN
    half = n1 // 2
    eye = np.eye(SUBLANES)
    a = 2.0 * np.pi * np.outer(np.arange(n1), np.arange(half)) / n1
    gr, gi = np.cos(a), -np.sin(a)
    blk = np.stack([np.stack([gr, -gi], axis=1), np.stack([gi, gr], axis=1)], axis=0)
    m_out = np.einsum('rkis,cd->rkcisd', blk, eye).reshape(2 * n1 * SUBLANES, 2 * half * SUBLANES)
    ir, ii = gr.T / n, -gi.T / n
    blk = np.stack([np.stack([ir, -ii], axis=1), np.stack([ii, ir], axis=1)], axis=0)
    m_inv = np.einsum('otrk,cd->otcrkd', blk, eye).reshape(2 * half * SUBLANES, 2 * n1 * SUBLANES)
    a = 2.0 * np.pi * np.outer(np.arange(FFT_IN), np.arange(FFT_IN)) / FFT_IN
    fr, fi = np.cos(a), -np.sin(a)
    w_fwd = np.block([[fr, -fi], [fi, fr]])
    w_inv = np.block([[fr, fi], [-fi, fr]])
    s_in = SUBLANES * np.arange(FFT_NT)[:, None, None] + np.arange(SUBLANES)[None, None, :]
    th = 2.0 * np.pi * s_in * np.arange(n1)[None, :, None] / n
    tw = (jnp.asarray(np.cos(th)[..., None], F32), jnp.asarray(-np.sin(th)[..., None], F32))
    return dict(m_out=_split_np(m_out), m_inv=_split_np(m_inv), w_fwd=_split_np(w_fwd), w_inv=_split_np(w_inv),
                tw=tw, n1=n1, half=half)


FFT_PP = 2


def _tw_spec(n1):
    return pl.BlockSpec((None, n1, SUBLANES, 1), lambda t, q: (t, 0, 0, 0))


def _fft_outer_kernel(mh_ref, ml_ref, twr_ref, twi_ref, z_ref, v_ref, *, complex_in, passes):
    n1 = v_ref.shape[1]
    tr, ti = twr_ref[...], twi_ref[...]
    for pp in range(FFT_PP):
        zz = z_ref[:, pp] if complex_in else z_ref[pp]
        rows_in = math.prod(zz.shape[:-1])
        v = _dotp(mh_ref[:, :rows_in], ml_ref[:, :rows_in], zz.reshape(rows_in, D_BRANCH), passes)
        vr = v[:n1 * SUBLANES].reshape(n1, SUBLANES, D_BRANCH)
        vi = v[n1 * SUBLANES:].reshape(n1, SUBLANES, D_BRANCH)
        v_ref[pp, :, 0] = vr * tr - vi * ti
        v_ref[pp, :, 1] = vr * ti + vi * tr


def _fft_outer(tb, z, complex_in, passes):
    n1, half = tb["n1"], tb["half"]
    if complex_in:
        p = z.shape[1]
        zspec = pl.BlockSpec((2, FFT_PP, half, None, SUBLANES, D_BRANCH), lambda t, q: (0, q, 0, t, 0, 0))
    else:
        p = z.shape[0]
        zspec = pl.BlockSpec((FFT_PP, half, None, SUBLANES, D_BRANCH), lambda t, q: (q, 0, t, 0, 0))
    mshape = (2 * n1 * SUBLANES, 2 * half * SUBLANES)
    return pl.pallas_call(
        functools.partial(_fft_outer_kernel, complex_in=complex_in, passes=passes),
        grid=(FFT_NT, p // FFT_PP),
        in_specs=[_const_spec(mshape), _const_spec(mshape), _tw_spec(n1), _tw_spec(n1), zspec],
        out_specs=pl.BlockSpec((FFT_PP, n1, 2, None, SUBLANES, D_BRANCH), lambda t, q: (q, 0, 0, t, 0, 0)),
        out_shape=jax.ShapeDtypeStruct((p, n1, 2, FFT_NT, SUBLANES, D_BRANCH), F32),
        compiler_params=_cparams("arbitrary", "arbitrary"),
        name="fft_outer_c" if complex_in else "fft_outer_r",
    )(*tb["m_out"], *tb["tw"], z)


FFT_KB = 8


def _fft_filt_kernel(wh_ref, wl_ref, vf_ref, vb_ref, h_ref):
    for kb in range(FFT_KB):
        zf = _dot3(wh_ref[...], wl_ref[...], vf_ref[kb].reshape(2 * FFT_IN, D_BRANCH))
        zb = _dot3(wh_ref[...], wl_ref[...], vb_ref[kb].reshape(2 * FFT_IN, D_BRANCH))
        h_ref[kb, 0] = zf[:FFT_IN] + zb[:FFT_IN]
        h_ref[kb, 1] = zf[FFT_IN:] - zb[FFT_IN:]


def _fft_filt(tb, v):
    n1 = v.shape[1]
    wspec = _const_spec((2 * FFT_IN, 2 * FFT_IN))
    vblk = (None, FFT_KB, 2, FFT_NT, SUBLANES, D_BRANCH)
    return pl.pallas_call(
        _fft_filt_kernel,
        grid=(C_ORDER, n1 // FFT_KB),
        in_specs=[wspec, wspec,
                  pl.BlockSpec(vblk, lambda o, k: (2 * o, k, 0, 0, 0, 0)),
                  pl.BlockSpec(vblk, lambda o, k: (2 * o + 1, k, 0, 0, 0, 0))],
        out_specs=pl.BlockSpec((None, FFT_KB, 2, FFT_IN, D_BRANCH), lambda o, k: (o, k, 0, 0, 0)),
        out_shape=jax.ShapeDtypeStruct((C_ORDER, n1, 2, FFT_IN, D_BRANCH), F32),
        compiler_params=_cparams("parallel", "parallel"),
        name="fft_filt",
    )(*tb["w_fwd"], v, v)


def _fft_mid_kernel(wfh_ref, wfl_ref, wih_ref, wil_ref, v_ref, h_ref, d_ref):
    for kb in range(FFT_KB):
        z = _dotp(wfh_ref[...], wfl_ref[...], v_ref[kb].reshape(2 * FFT_IN, D_BRANCH), FFT_ACT_PASSES)
        zr, zi = z[:FFT_IN], z[FFT_IN:]
        hr, hi = h_ref[kb, 0], h_ref[kb, 1]
        pr = zr * hr - zi * hi
        pi = zr * hi + zi * hr
        d = _dotp(wih_ref[...], wil_ref[...], jnp.concatenate([pr, pi], axis=0), FFT_ACT_PASSES)
        d_ref[kb] = d.reshape(d_ref.shape[1:])


def _fft_mid(tb, v, hspec, order):
    p, n1 = v.shape[:2]
    vspec = pl.BlockSpec((None, FFT_KB, 2, FFT_NT, SUBLANES, D_BRANCH), lambda k, q: (q, k, 0, 0, 0, 0))
    wspec = _const_spec((2 * FFT_IN, 2 * FFT_IN))
    return pl.pallas_call(
        _fft_mid_kernel,
        grid=(n1 // FFT_KB, p),
        in_specs=[wspec, wspec, wspec, wspec, vspec,
                  pl.BlockSpec((None, FFT_KB, 2, FFT_IN, D_BRANCH), lambda k, q: (order, k, 0, 0, 0))],
        out_specs=vspec,
        out_shape=jax.ShapeDtypeStruct(v.shape, F32),
        compiler_params=_cparams("parallel", "arbitrary"),
        name="fft_mid",
    )(*tb["w_fwd"], *tb["w_inv"], v, hspec)


def _ifft_outer_kernel(mh_ref, ml_ref, twr_ref, twi_ref, d_ref, u_ref, x_ref, b_ref, o_ref):
    n1 = d_ref.shape[1]
    tr, ti = twr_ref[...], twi_ref[...]
    for pp in range(FFT_PP):
        dr, di = d_ref[pp, :, 0], d_ref[pp, :, 1]
        er = (dr * tr + di * ti).reshape(n1 * SUBLANES, D_BRANCH)
        ei = (di * tr - dr * ti).reshape(n1 * SUBLANES, D_BRANCH)
        y = _dotp(mh_ref[...], ml_ref[...], jnp.concatenate([er, ei], axis=0), FFT_ACT_PASSES)
        y = y.reshape((2,) + o_ref.shape[2:])
        o_ref[:, pp] = x_ref[:, pp] * (y + u_ref[:, pp] * b_ref[...])


def _ifft_outer(tb, d, u, ucol, x, xcol, bias):
    n1, half = tb["n1"], tb["half"]
    p = d.shape[0]
    mshape = (2 * half * SUBLANES, 2 * n1 * SUBLANES)
    io = lambda col: pl.BlockSpec((2, FFT_PP, half, None, SUBLANES, D_BRANCH), lambda t, q: (0, q, 0, t, 0, col))
    return pl.pallas_call(
        _ifft_outer_kernel,
        grid=(FFT_NT, p // FFT_PP),
        in_specs=[_const_spec(mshape), _const_spec(mshape), _tw_spec(n1), _tw_spec(n1),
                  pl.BlockSpec((FFT_PP, n1, 2, None, SUBLANES, D_BRANCH), lambda t, q: (q, 0, 0, t, 0, 0)),
                  io(ucol), io(xcol), _const_spec((1, D_BRANCH))],
        out_specs=io(0),
        out_shape=jax.ShapeDtypeStruct((2, p, half, FFT_NT, SUBLANES, D_BRANCH), F32),
        compiler_params=_cparams("arbitrary", "arbitrary"),
        name="ifft_outer",
    )(*tb["m_inv"], *tb["tw"], d, u, x, bias.reshape(1, D_BRANCH))


def _hyconv_kernel(x_ref, w_ref, b_ref, o_ref, xp_ref):
    seq = x_ref.shape[0]
    xp_ref[0:HY_PAD, :] = jnp.zeros((HY_PAD, D_BRANCH), F32)
    xp_ref[HY_PAD + seq:2 * HY_PAD + seq, :] = jnp.zeros((HY_PAD, D_BRANCH), F32)
    xp_ref[HY_PAD:HY_PAD + seq, :] = x_ref[...]

    def body(i, carry):
        r0 = pl.multiple_of(i * HY_TB, HY_TB)
        win = xp_ref[pl.ds(r0, HY_TB + 2 * HY_PAD), :]
        y = b_ref[...]
        for j in range(C_CONV):
            s0 = HY_PAD + j - C_CONV // 2
            y = y + w_ref[j:j + 1, :] * win[s0:s0 + HY_TB, :]
        o_ref[pl.ds(r0, HY_TB), :] = y
        return carry

    lax.fori_loop(0, seq // HY_TB, body, 0)


def _hyconv(pc3, cw, cb):
    bsz, seq, width = pc3.shape
    nb = width // D_BRANCH
    blk = pl.BlockSpec((None, seq, D_BRANCH), lambda b, j: (b, 0, j))
    return pl.pallas_call(
        _hyconv_kernel,
        grid=(bsz, nb),
        in_specs=[blk, pl.BlockSpec((C_CONV, D_BRANCH), lambda b, j: (0, j)),
                  pl.BlockSpec((1, D_BRANCH), lambda b, j: (0, j))],
        out_specs=blk,
        out_shape=jax.ShapeDtypeStruct(pc3.shape, F32),
        scratch_shapes=[pltpu.VMEM((seq + 2 * HY_PAD, D_BRANCH), F32)],
        compiler_params=_cparams("parallel", "parallel"),
        name="hyconv",
    )(pc3, cw, cb.reshape(1, width))


def _hyfilt_kernel(z_ref, w1_ref, b1_ref, fr_ref, w2_ref, b2_ref, w3_ref, dec_ref, o_ref, h_ref):
    seq = z_ref.shape[0]
    nblk = seq // HY_TB
    fr = fr_ref[...]

    @pl.when(pl.program_id(0) == 0)
    def _():
        def hidden(i, carry):
            r0 = pl.multiple_of(i * HY_TB, HY_TB)
            zb = z_ref[pl.ds(r0, HY_TB), :]
            h = jnp.sin(fr * (jnp.dot(zb, w1_ref[...], precision=HI, preferred_element_type=F32) + b1_ref[...]))
            h = jnp.sin(fr * (jnp.dot(h, w2_ref[...], precision=HI, preferred_element_type=F32) + b2_ref[...]))
            h_ref[pl.ds(r0, HY_TB), :] = h
            return carry

        lax.fori_loop(0, nblk, hidden, 0)

    def body(i, ss):
        r0 = pl.multiple_of(i * HY_TB, HY_TB)
        hf = jnp.dot(h_ref[pl.ds(r0, HY_TB), :], w3_ref[...], precision=HI, preferred_element_type=F32)
        hf = hf * jnp.exp(-z_ref[pl.ds(r0, HY_TB), 0:1] * dec_ref[...])
        o_ref[pl.ds(r0, HY_TB), :] = hf
        return ss + jnp.sum(hf * hf, axis=0, keepdims=True)

    ss = lax.fori_loop(0, nblk, body, jnp.zeros((1, D_BRANCH), F32))
    scale = lax.rsqrt(ss + EPS)

    def norm(i, carry):
        r0 = pl.multiple_of(i * HY_TB, HY_TB)
        o_ref[pl.ds(r0, HY_TB), :] = o_ref[pl.ds(r0, HY_TB), :] * scale
        return carry

    lax.fori_loop(0, nblk, norm, 0)


def _hyfilt(seq, w1, b1, freq, w2, b2, w3):
    t = jnp.linspace(0.0, 1.0, seq, dtype=F32)[:, None]
    bands = (C_EMB - 1) // 2
    w = 2.0 * math.pi * jnp.arange(seq, dtype=F32)[:, None] / seq
    fr = jnp.linspace(1e-4, bands - 1, bands, dtype=F32)[None]
    z = jnp.concatenate([t, jnp.cos(fr * w), -jnp.sin(fr * w)], axis=-1)
    z = jnp.pad(z, ((0, 0), (0, LANES - C_EMB)))
    padm = lambda a, r, c: jnp.pad(a.astype(F32), ((0, r - a.shape[0]), (0, c - a.shape[1])))
    row = lambda a: padm(a.reshape(1, -1), 1, LANES)
    dec = jnp.abs(jnp.linspace(C_MIN_DECAY, C_MAX_DECAY, D_BRANCH, dtype=F32)).reshape(1, D_BRANCH)
    nset = C_ORDER * 2
    return pl.pallas_call(
        _hyfilt_kernel,
        grid=(nset,),
        in_specs=[_const_spec((seq, LANES)), _const_spec((LANES, LANES)), _const_spec((1, LANES)),
                  _const_spec((1, LANES)), _const_spec((LANES, LANES)), _const_spec((1, LANES)),
                  pl.BlockSpec((LANES, D_BRANCH), lambda j: (0, j)), _const_spec((1, D_BRANCH))],
        out_specs=pl.BlockSpec((None, seq, D_BRANCH), lambda j: (j, 0, 0)),
        out_shape=jax.ShapeDtypeStruct((nset, seq, D_BRANCH), F32),
        scratch_shapes=[pltpu.VMEM((seq, LANES), F32)],
        compiler_params=_cparams("arbitrary"),
        name="hyfilt",
    )(z, padm(w1, LANES, LANES), row(b1), row(freq), padm(w2, LANES, LANES), row(b2),
      padm(w3, LANES, nset * D_BRANCH), dec)


def _hyena(pc3, cw, cb, w1, b1, freq, w2, b2, w3, bias):
    bsz, seq, width = pc3.shape
    tb = _fft_tables(seq)
    half = tb["half"]
    npair = bsz // 2
    uc = _hyconv(pc3, cw, cb)
    uc6 = uc.reshape(2, npair, half, FFT_NT, SUBLANES, width)
    filt = _hyfilt(seq, w1, b1, freq, w2, b2, w3).reshape(C_ORDER * 2, half, FFT_NT, SUBLANES, D_BRANCH)
    hspec = _fft_filt(tb, _fft_outer(tb, filt, False, 3))
    src = uc6
    for order in range(C_ORDER):
        d = _fft_mid(tb, _fft_outer(tb, src, True, FFT_ACT_PASSES), hspec, order)
        src = _ifft_outer(tb, d, src, 0, uc6, order + 1, bias[order])
    return src.reshape(bsz, seq, D_BRANCH)


AT_HALF = 64
AT_TQ = 512
AT_SUB = 128


def _t5_bucket(rel):
    half = N_BUCKETS // 2
    max_exact = half // 2
    n = np.abs(rel)
    large = max_exact + (np.log(np.maximum(n, 1) / max_exact) / math.log(MAX_DISTANCE / max_exact)
                         * (half - max_exact)).astype(np.int64)
    large = np.minimum(large, half - 1)
    return (rel > 0).astype(np.int64) * half + np.where(n < max_exact, n, large)


def _attn_geometry(n):
    tq = min(AT_TQ, n)
    sub = min(AT_SUB, tq)
    win = min(sub + 2 * AT_HALF, n)
    return tq, sub, win, n // tq, n // sub


def _attn_bias_tables(rel_bias, g, dil, n):
    _, sub, win, _, nsb = _attn_geometry(n)
    hs = slice(g * D_HEADS_PER_GROUP, (g + 1) * D_HEADS_PER_GROUP)
    offsets = np.arange(-AT_HALF, AT_HALF + 1) * dil
    onehot = np.zeros((2 * AT_HALF + 1, N_BUCKETS), np.float32)
    onehot[np.arange(2 * AT_HALF + 1), _t5_bucket(offsets)] = 1.0
    band = jnp.dot(rel_bias.astype(F32)[:, hs].T, jnp.asarray(onehot).T, precision=HI)
    nband = 2 * AT_HALF + 1
    lv = sub + win - 1
    tables = []
    for i in sorted({0, min(1, nsb - 1), nsb - 1}):
        ws = int(np.clip(i * sub - AT_HALF, 0, n - win))
        lo = (sub - 1) - (ws - i * sub) - AT_HALF
        v = jnp.pad(band, ((0, 0), (lo, lv - lo - nband)), constant_values=NEG_BIG)
        flat = jnp.tile(v, (1, sub + 1))[:, sub - 1:sub - 1 + sub * (lv - 1)]
        tables.append(flat.reshape(D_HEADS_PER_GROUP, sub, lv - 1)[:, :, :win])
    return jnp.stack(tables)


def _attn_kernel(q_ref, k_ref, v_ref, bias_ref, o_ref, l_ref, *, n, dil):
    tq, sub, win, _, nsb = _attn_geometry(n)
    ncase = bias_ref.shape[0]
    width = q_ref.shape[-1]
    nh = width // D_HEAD_DIM
    hp = pl.program_id(1)
    lane_head = lax.broadcasted_iota(jnp.int32, (sub, width), 1) // D_HEAD_DIM
    hmask = [lane_head == hh for hh in range(nh)]
    whole = win == n
    for r in range(dil):
        if whole:
            kw = k_ref[pl.ds(r, win, stride=dil), :].astype(BF16)
            vw = v_ref[pl.ds(r, win, stride=dil), :].astype(BF16)
        for j in range(tq // sub):
            sidx = pl.program_id(2) * (tq // sub) + j
            case = jnp.minimum(jnp.where(sidx == nsb - 1, ncase - 1, jnp.minimum(sidx, 1)), ncase - 1)
            bias = bias_ref[case, pl.ds(hp * nh, nh)].reshape(nh * sub, win)
            if not whole:
                ws = pl.multiple_of(jnp.clip(sidx * sub - AT_HALF, 0, n - win), AT_HALF)
                kw = k_ref[pl.ds(ws * dil + r, win, stride=dil), :].astype(BF16)
                vw = v_ref[pl.ds(ws * dil + r, win, stride=dil), :].astype(BF16)
            q = q_ref[pl.ds(j * sub * dil + r, sub, stride=dil), :] * (D_HEAD_DIM ** -0.5)
            qs = jnp.concatenate([jnp.where(hmask[hh], q, 0.0) for hh in range(nh)], axis=0).astype(BF16)
            s = lax.dot_general(qs, kw, (((1,), (1,)), ((), ())), preferred_element_type=F32) + bias
            m = jnp.max(s, axis=-1, keepdims=True)
            p = jnp.exp(s - m)
            l = jnp.sum(p, axis=-1, keepdims=True)
            o_all = jnp.dot(p.astype(BF16), vw, preferred_element_type=F32) / l
            lse = m + jnp.log(l)
            o_acc = jnp.zeros((sub, width), F32)
            l_acc = jnp.zeros((sub, width), F32)
            for hh in range(nh):
                o_acc = jnp.where(hmask[hh], o_all[hh * sub:(hh + 1) * sub], o_acc)
                l_acc = jnp.where(hmask[hh], lse[hh * sub:(hh + 1) * sub], l_acc)
            o_ref[pl.ds(j * sub * dil + r, sub, stride=dil), :] = o_acc
            l_ref[pl.ds(j * sub * dil + r, sub, stride=dil), :] = l_acc


def _banded_attention(pd3, g, dil, bias):
    bsz, seq, width = pd3.shape
    n = seq // dil
    tq, sub, win, nq, _ = _attn_geometry(n)
    rows = dil * tq
    bw = LANES if dil > 1 else D_BRANCH
    per = D_BRANCH // bw
    third = width // 3 // bw
    out_spec = pl.BlockSpec((None, rows, bw), lambda b, h, i: (b, i, h))
    shp = jax.ShapeDtypeStruct((bsz, seq, D_BRANCH), F32)
    return pl.pallas_call(
        functools.partial(_attn_kernel, n=n, dil=dil),
        grid=(bsz, per, nq),
        in_specs=[pl.BlockSpec((None, rows, bw), lambda b, h, i: (b, i, g * per + h)),
                  pl.BlockSpec((None, seq, bw), lambda b, h, i: (b, 0, third + g * per + h)),
                  pl.BlockSpec((None, seq, bw), lambda b, h, i: (b, 0, 2 * third + g * per + h)),
                  _const_spec(bias.shape)],
        out_specs=[out_spec, out_spec],
        out_shape=[shp, shp],
        compiler_params=_cparams("parallel", "parallel", "arbitrary"),
        name=f"attn_d{dil}",
    )(pd3, pd3, pd3, bias)


def _dilated_attention(pd3, rel_bias):
    bsz, seq, _ = pd3.shape
    outs, lses = [], []
    for g, (_, dil) in enumerate(D_GROUPS):
        o, l = _banded_attention(pd3, g, dil, _attn_bias_tables(rel_bias, g, dil, seq // dil))
        outs.append(o.reshape(bsz * seq, D_BRANCH))
        lses.append(l.reshape(bsz * seq, D_BRANCH))
    return outs, lses


def kernel(x, norm1_g, w_in, hgrn_lb_logits, hgrn_norm_g, lru_conv_w, lru_conv_b, lru_wa, lru_ba, lru_wx, lru_bx,
           lru_lambda, hy_conv_w, hy_conv_b, hy_w1, hy_b1, hy_freq, hy_w2, hy_b2, hy_w3, hy_bias, rel_bias,
           w_branch, w_gate, b_gate, w_out, norm2_g, w_ff1, w_ff3, w_ff2, final_g):
    bsz, seq, _ = x.shape
    n = bsz * seq
    lb_soft = jax.nn.softmax(hgrn_lb_logits.astype(F32), axis=0)
    lower_bounds = jnp.cumsum(lb_soft, axis=0) - lb_soft[0]
    x2 = x.reshape(n, D_MODEL)
    flat = lambda a: a.reshape(n, D_BRANCH)
    for l in range(DEPTH):
        pa, pb, pc, pd = _inproj(x2, norm1_g[l], w_in[l].astype(BF16))
        oa_f, oa_b = _hgrn(pa.reshape(bsz, seq, IN_A), lower_bounds[l])
        yb = _rglru(pb.reshape(bsz, seq, IN_B), lru_conv_w[l], lru_conv_b[l], lru_wa[l], lru_ba[l],
                    lru_wx[l], lru_bx[l], lru_lambda[l])
        yc = _hyena(pc.reshape(bsz, seq, IN_C), hy_conv_w[l], hy_conv_b[l], hy_w1[l], hy_b1[l], hy_freq[l],
                    hy_w2[l], hy_b2[l], hy_w3[l], hy_bias[l])
        od, ld = _dilated_attention(pd.reshape(bsz, seq, IN_D), rel_bias)
        x2 = _merge(x2, norm1_g[l], flat(oa_f), flat(oa_b), pa, hgrn_norm_g[l], flat(yb), flat(yc), od, ld,
                    w_gate[l].reshape(D_MODEL, N_BRANCH * D_MODEL).astype(BF16),
                    b_gate[l].reshape(1, N_BRANCH * D_MODEL), w_branch[l].astype(BF16), w_out[l].astype(BF16))
        x2 = _ffn(x2, norm2_g[l], *_ffn_weights(w_ff1[l], w_ff3[l], w_ff2[l]), final_g, l == DEPTH - 1)
    return x2.reshape(bsz, seq, D_MODEL)
```

```python
import functools
import math

import jax
import jax.numpy as jnp
import numpy as np
from jax import lax
from jax.experimental import pallas as pl
from jax.experimental.pallas import tpu as pltpu

F32 = jnp.float32
BF16 = jnp.bfloat16
HI = lax.Precision.HIGHEST

D_MODEL = 1024
DEPTH = 2
EPS = 1e-6
TINY = 1e-30
N_BRANCH = 4
D_BRANCH = 256
A_HEADS = 4
A_DK = 64
A_CHUNK = 64
B_BLOCKS = 4
B_BW = 64
B_CONV = 4
LRU_C = 8.0
C_ORDER = 2
C_CONV = 3
C_EMB = 33
C_HID = 64
C_MIN_DECAY = math.log(1e-2) / 1.5
C_MAX_DECAY = math.log(1e-2) / 0.3
D_GROUPS = ((128, 1), (512, 4), (2048, 16))
D_HEADS_PER_GROUP = 4
D_HEAD_DIM = 64
D_N_HEADS = 12
D_QKV = 768
N_BUCKETS = 32
MAX_DISTANCE = 1024
NEG_BIG = -1e30
D_FF = 2816
IN_A = 5 * D_BRANCH
IN_B = 2 * D_BRANCH
IN_C = 3 * D_BRANCH
IN_D = 3 * D_QKV
IN_WIDTH = IN_A + IN_B + IN_C + IN_D

LANES = 128
SUBLANES = 8
VMEM_LIMIT = 56 * 1024 * 1024


def _cparams(*sem):
    return pltpu.CompilerParams(dimension_semantics=sem, vmem_limit_bytes=VMEM_LIMIT)


def _const_spec(shape):
    nd = len(shape)
    return pl.BlockSpec(shape, lambda *_: (0,) * nd, pipeline_mode=pl.Buffered(1))


def _rms(x, g):
    return x * lax.rsqrt(jnp.mean(x * x, axis=-1, keepdims=True) + EPS) * g


def _sigmoid(x):
    return 1.0 / (1.0 + jnp.exp(-x))


IN_TM = 512
IN_CHUNK = 256


def _inproj_kernel(x_ref, g_ref, w_ref, oa_ref, ob_ref, oc_ref, od_ref):
    h = _rms(x_ref[...], g_ref[...]).astype(BF16)
    off = 0
    for o_ref in (oa_ref, ob_ref, oc_ref, od_ref):
        width = o_ref.shape[-1]
        for c in range(0, width, IN_CHUNK):
            o_ref[:, c:c + IN_CHUNK] = jnp.dot(h, w_ref[:, off + c:off + c + IN_CHUNK],
                                               preferred_element_type=F32)
        off += width


def _inproj(x2, g, w_bf16):
    n = x2.shape[0]
    widths = (IN_A, IN_B, IN_C, IN_D)
    return pl.pallas_call(
        _inproj_kernel,
        grid=(n // IN_TM,),
        in_specs=[pl.BlockSpec((IN_TM, D_MODEL), lambda i: (i, 0)),
                  _const_spec((1, D_MODEL)),
                  _const_spec((D_MODEL, IN_WIDTH))],
        out_specs=[pl.BlockSpec((IN_TM, w), lambda i: (i, 0)) for w in widths],
        out_shape=[jax.ShapeDtypeStruct((n, w), F32) for w in widths],
        compiler_params=_cparams("parallel"),
        name="inproj",
    )(x2, g.reshape(1, D_MODEL), w_bf16)


FF_TM = 1024
FF_CHUNK = 256
FF_NCHUNK = D_FF // FF_CHUNK


def _ffn_kernel(x_ref, g_ref, w1_ref, w3_ref, w2_ref, fg_ref, o_ref, acc_ref, *, final):
    x = x_ref[...]
    h = _rms(x, g_ref[...]).astype(BF16)
    acc_ref[...] = x

    def body(c, carry):
        a = jnp.dot(h, w1_ref[c], preferred_element_type=F32)
        b = jnp.dot(h, w3_ref[c], preferred_element_type=F32)
        t = (a * _sigmoid(a) * b).astype(BF16)
        acc_ref[...] += jnp.dot(t, w2_ref[c], preferred_element_type=F32)
        return carry

    lax.fori_loop(0, FF_NCHUNK, body, 0)
    y = acc_ref[...]
    if final:
        y = _rms(y, fg_ref[...])
    o_ref[...] = y


def _ffn(x2, g, w1c, w3c, w2c, final_g, final):
    n = x2.shape[0]
    return pl.pallas_call(
        functools.partial(_ffn_kernel, final=final),
        grid=(n // FF_TM,),
        in_specs=[pl.BlockSpec((FF_TM, D_MODEL), lambda i: (i, 0)),
                  _const_spec((1, D_MODEL)),
                  _const_spec((FF_NCHUNK, D_MODEL, FF_CHUNK)),
                  _const_spec((FF_NCHUNK, D_MODEL, FF_CHUNK)),
                  _const_spec((FF_NCHUNK, FF_CHUNK, D_MODEL)),
                  _const_spec((1, D_MODEL))],
        out_specs=pl.BlockSpec((FF_TM, D_MODEL), lambda i: (i, 0)),
        out_shape=jax.ShapeDtypeStruct((n, D_MODEL), F32),
        scratch_shapes=[pltpu.VMEM((FF_TM, D_MODEL), F32)],
        compiler_params=_cparams("parallel"),
        name="ffn_final" if final else "ffn",
    )(x2, g.reshape(1, D_MODEL), w1c, w3c, w2c, final_g.reshape(1, D_MODEL))


def _ffn_weights(w1, w3, w2):
    w1c = w1.astype(BF16).reshape(D_MODEL, FF_NCHUNK, FF_CHUNK).transpose(1, 0, 2)
    w3c = w3.astype(BF16).reshape(D_MODEL, FF_NCHUNK, FF_CHUNK).transpose(1, 0, 2)
    w2c = w2.astype(BF16).reshape(FF_NCHUNK, FF_CHUNK, D_MODEL)
    return w1c, w3c, w2c


MG_TM = 512


def _head_ones():
    r = np.arange(D_BRANCH)[:, None] // A_DK
    c = np.arange(D_BRANCH)[None, :] // A_DK
    return jnp.asarray((r == c).astype(np.float32) / A_DK)


def _merge_kernel(x_ref, g1_ref, oaf_ref, oab_ref, ga_ref, hg_ref, hm_ref, yb_ref, yc_ref,
                  o0_ref, o1_ref, o2_ref, l0_ref, l1_ref, l2_ref,
                  wg_ref, bg_ref, wb_ref, wo_ref, out_ref):
    x = x_ref[...]
    h = _rms(x, g1_ref[...]).astype(BF16)
    oa = oaf_ref[...] + oab_ref[...]
    ms = jnp.dot(oa * oa, hm_ref[...], precision=HI, preferred_element_type=F32)
    ga = ga_ref[...]
    ya = oa * lax.rsqrt(ms + EPS) * hg_ref[...] * (ga * _sigmoid(ga))
    l0, l1, l2 = l0_ref[...], l1_ref[...], l2_ref[...]
    m = jnp.maximum(jnp.maximum(l0, l1), l2)
    e0, e1, e2 = jnp.exp(l0 - m), jnp.exp(l1 - m), jnp.exp(l2 - m)
    yd = (e0 * o0_ref[...] + e1 * o1_ref[...] + e2 * o2_ref[...]) / (e0 + e1 + e2)
    mixed = None
    for j, y in enumerate((ya, yb_ref[...], yc_ref[...], yd)):
        gate = _sigmoid(jnp.dot(h, wg_ref[:, j * D_MODEL:(j + 1) * D_MODEL], preferred_element_type=F32)
                        + bg_ref[:, j * D_MODEL:(j + 1) * D_MODEL])
        t = gate * jnp.dot(y.astype(BF16), wb_ref[j], preferred_element_type=F32)
        mixed = t if mixed is None else mixed + t
    out_ref[...] = x + jnp.dot(mixed.astype(BF16), wo_ref[...], preferred_element_type=F32)


def _merge(x2, g1, oa_f, oa_b, pa, hg, yb, yc, od, ld, wg, bg, wb, wo):
    n = x2.shape[0]
    tile = lambda w: pl.BlockSpec((MG_TM, w), lambda i: (i, 0))
    return pl.pallas_call(
        _merge_kernel,
        grid=(n // MG_TM,),
        in_specs=[tile(D_MODEL), _const_spec((1, D_MODEL)),
                  tile(D_BRANCH), tile(D_BRANCH),
                  pl.BlockSpec((MG_TM, D_BRANCH), lambda i: (i, 4)),
                  _const_spec((1, D_BRANCH)), _const_spec((D_BRANCH, D_BRANCH)),
                  tile(D_BRANCH), tile(D_BRANCH),
                  tile(D_BRANCH), tile(D_BRANCH), tile(D_BRANCH),
                  tile(D_BRANCH), tile(D_BRANCH), tile(D_BRANCH),
                  _const_spec((D_MODEL, N_BRANCH * D_MODEL)), _const_spec((1, N_BRANCH * D_MODEL)),
                  _const_spec((N_BRANCH, D_BRANCH, D_MODEL)), _const_spec((D_MODEL, D_MODEL))],
        out_specs=tile(D_MODEL),
        out_shape=jax.ShapeDtypeStruct((n, D_MODEL), F32),
        compiler_params=_cparams("parallel"),
        name="merge",
    )(x2, g1.reshape(1, D_MODEL), oa_f, oa_b, pa, hg.reshape(1, D_BRANCH), _head_ones(), yb, yc,
      od[0], od[1], od[2], ld[0], ld[1], ld[2], wg, bg, wb, wo)


HG_TS = 256
HG_NCH = HG_TS // A_CHUNK
HG_MID = A_CHUNK // 2


def _hgrn_cum_matrices():
    r = np.arange(HG_TS)[:, None]
    c = np.arange(HG_TS)[None, :]
    same = (r // A_CHUNK) == (c // A_CHUNK)
    return (jnp.asarray((same & (r >= c)).astype(np.float32), BF16),
            jnp.asarray((same & (r <= c)).astype(np.float32), BF16))


def _hgrn_block(q, fl, v, lb, cum, st, fwd):
    srow = lax.broadcasted_iota(jnp.int32, (A_HEADS * A_CHUNK, A_CHUNK), 0) % A_CHUNK
    scol = lax.broadcasted_iota(jnp.int32, (A_HEADS * A_CHUNK, A_CHUNK), 1)
    causal = (srow >= scol) if fwd else (srow <= scol)
    lane_head = lax.broadcasted_iota(jnp.int32, (A_CHUNK, D_BRANCH), 1) // A_DK
    hmask = [(lane_head == hh).astype(F32) for hh in range(A_HEADS)]
    blk_r = lax.broadcasted_iota(jnp.int32, (D_BRANCH, D_BRANCH), 0) // A_DK
    blk_c = lax.broadcasted_iota(jnp.int32, (D_BRANCH, D_BRANCH), 1) // A_DK
    blockdiag = (blk_r == blk_c).astype(F32)

    sg = _sigmoid(fl)
    f = lb + (1.0 - lb) * sg
    g = jnp.log(jnp.maximum(f, TINY))
    kk = (1.0 - lb) * (1.0 - sg)
    g1 = g.astype(BF16)
    r1 = g - g1.astype(F32)
    g2 = r1.astype(BF16)
    g3 = (r1 - g2.astype(F32)).astype(BF16)
    b = (jnp.dot(cum, g1, preferred_element_type=F32) + jnp.dot(cum, g2, preferred_element_type=F32)
         + jnp.dot(cum, g3, preferred_element_type=F32))
    rows = lambda c: slice(c * A_CHUNK, (c + 1) * A_CHUNK)
    last = (A_CHUNK - 1) if fwd else 0
    bcast = lambda r0: jnp.concatenate(
        [jnp.broadcast_to(b[c * A_CHUNK + r0:c * A_CHUNK + r0 + 1, :], (A_CHUNK, D_BRANCH)) for c in range(HG_NCH)], 0)
    bm = bcast(HG_MID)
    bl = bcast(last)
    qt = q * jnp.exp(b - bm)
    kt = (kk * jnp.exp(bm - b)).astype(BF16)
    qe = (q * jnp.exp(b)).astype(BF16)
    kh = (kk * jnp.exp(bl - b)).astype(BF16)
    vb = v.astype(BF16)
    o_intra, upd = [], []
    for c in range(HG_NCH):
        qs = jnp.concatenate([qt[rows(c)] * hmask[hh] for hh in range(A_HEADS)], axis=0).astype(BF16)
        s = lax.dot_general(qs, kt[rows(c)], (((1,), (1,)), ((), ())), preferred_element_type=F32)
        s = jnp.where(causal, s, 0.0).astype(BF16)
        ost = jnp.dot(s, vb[rows(c)], preferred_element_type=F32)
        o = ost[0:A_CHUNK] * hmask[0]
        for hh in range(1, A_HEADS):
            o = o + ost[hh * A_CHUNK:(hh + 1) * A_CHUNK] * hmask[hh]
        o_intra.append(o)
        upd.append(jnp.dot(v[rows(c)].T.astype(BF16), kh[rows(c)], preferred_element_type=F32) * blockdiag)
    outs = [None] * HG_NCH
    for c in (range(HG_NCH) if fwd else range(HG_NCH - 1, -1, -1)):
        outs[c] = o_intra[c] + lax.dot_general(qe[rows(c)], st.astype(BF16), (((1,), (1,)), ((), ())),
                                               preferred_element_type=F32)
        st = st * jnp.exp(b[c * A_CHUNK + last:c * A_CHUNK + last + 1, :]) + upd[c]
    return jnp.concatenate(outs, axis=0), st


def _hgrn_kernel(qf_ref, ff_ref, vf_ref, qb_ref, fb_ref, vb_ref, lb_ref, cf_ref, cb_ref, of_ref, ob_ref, st_ref):
    @pl.when(pl.program_id(1) == 0)
    def _():
        st_ref[...] = jnp.zeros_like(st_ref)

    lb = lb_ref[...]
    o, st = _hgrn_block(qf_ref[...], ff_ref[...], vf_ref[...], lb, cf_ref[...], st_ref[0], True)
    of_ref[...] = o
    st_ref[0] = st
    o, st = _hgrn_block(qb_ref[...], fb_ref[...], vb_ref[...], lb, cb_ref[...], st_ref[1], False)
    ob_ref[...] = o
    st_ref[1] = st


def _hgrn(pa3, lb):
    bsz, seq, _ = pa3.shape
    nblk = seq // HG_TS
    blk = (None, HG_TS, D_BRANCH)
    up = lambda col: pl.BlockSpec(blk, lambda b, i: (b, i, col))
    down = lambda col: pl.BlockSpec(blk, lambda b, i: (b, nblk - 1 - i, col))
    shp = jax.ShapeDtypeStruct((bsz, seq, D_BRANCH), F32)
    cum_f, cum_b = _hgrn_cum_matrices()
    return pl.pallas_call(
        _hgrn_kernel,
        grid=(bsz, nblk),
        in_specs=[up(0), up(1), up(3), down(0), down(2), down(3), _const_spec((1, D_BRANCH)),
                  _const_spec((HG_TS, HG_TS)), _const_spec((HG_TS, HG_TS))],
        out_specs=[up(0), down(0)],
        out_shape=[shp, shp],
        scratch_shapes=[pltpu.VMEM((2, D_BRANCH, D_BRANCH), F32)],
        compiler_params=_cparams("parallel", "arbitrary"),
        name="hgrn2",
    )(pa3, pa3, pa3, pa3, pa3, pa3, lb.reshape(1, D_BRANCH), cum_f, cum_b)


RG_TB = 128
RG_PAD = SUBLANES
RG_LEFT = B_CONV // 2


def _dot3_rhs(x, wh, wl):
    xh = x.astype(BF16)
    xl = (x - xh.astype(F32)).astype(BF16)
    return (jnp.dot(xh, wh, preferred_element_type=F32) + jnp.dot(xl, wh, preferred_element_type=F32)
            + jnp.dot(xh, wl, preferred_element_type=F32))


def _group_scan(a, u, fwd):
    row = lax.broadcasted_iota(jnp.int32, a.shape, 1)
    k = 1
    while k < SUBLANES:
        if fwd:
            keep = row >= k
            us, as_ = pltpu.roll(u, k, 1), pltpu.roll(a, k, 1)
        else:
            keep = row < SUBLANES - k
            us, as_ = pltpu.roll(u, SUBLANES - k, 1), pltpu.roll(a, SUBLANES - k, 1)
        u = a * jnp.where(keep, us, 0.0) + u
        a = a * jnp.where(keep, as_, 1.0)
        k *= 2
    return a, u


def _block_scan(a, u, carry, fwd):
    t = a.shape[0]
    ngrp = t // SUBLANES
    ag, ug = _group_scan(a.reshape(ngrp, SUBLANES, D_BRANCH), u.reshape(ngrp, SUBLANES, D_BRANCH), fwd)
    hs = [None] * ngrp
    for g in (range(ngrp) if fwd else range(ngrp - 1, -1, -1)):
        h = ug[g] + ag[g] * carry
        hs[g] = h
        carry = h[SUBLANES - 1:SUBLANES, :] if fwd else h[0:1, :]
    return jnp.concatenate(hs, axis=0), carry


def _gelu_tanh(x):
    return 0.5 * x * (1.0 + jnp.tanh(math.sqrt(2.0 / math.pi) * (x + 0.044715 * (x * x * x))))


def _rglru_kernel(x_ref, gt_ref, cw_ref, cb_ref, wh_ref, wl_ref, bg_ref, lam_ref, o_ref, xp_ref, xc_ref):
    seq = x_ref.shape[0]
    nblk = seq // RG_TB
    xp_ref[0:RG_PAD, :] = jnp.zeros((RG_PAD, D_BRANCH), F32)
    xp_ref[RG_PAD + seq:2 * RG_PAD + seq, :] = jnp.zeros((RG_PAD, D_BRANCH), F32)
    xp_ref[RG_PAD:RG_PAD + seq, :] = x_ref[...]
    nl = -lam_ref[...]
    sp = jnp.maximum(nl, 0.0) + jnp.log(1.0 + jnp.exp(-jnp.abs(nl)))

    def block(i, carry, dirn):
        r0 = pl.multiple_of(i * RG_TB, RG_TB)
        if dirn == 0:
            win = xp_ref[pl.ds(r0, RG_TB + 2 * RG_PAD), :]
            xc = cb_ref[...]
            for j in range(B_CONV):
                s0 = RG_PAD + j - RG_LEFT
                xc = xc + cw_ref[j:j + 1, :] * win[s0:s0 + RG_TB, :]
            xc_ref[pl.ds(r0, RG_TB), :] = xc
        else:
            xc = xc_ref[pl.ds(r0, RG_TB), :]
        cols = slice(dirn * 2 * D_BRANCH, (dirn + 1) * 2 * D_BRANCH)
        gates = _dot3_rhs(xc, wh_ref[:, cols], wl_ref[:, cols]) + bg_ref[:, cols]
        r = _sigmoid(gates[:, :D_BRANCH])
        ig = _sigmoid(gates[:, D_BRANCH:])
        log_a = -LRU_C * r * sp[dirn:dirn + 1, :]
        a = jnp.exp(log_a)
        u = jnp.sqrt(jnp.maximum(-jnp.tanh(log_a) * (a * a + 1.0), 0.0)) * ig * xc
        h, carry = _block_scan(a, u, carry, dirn == 0)
        if dirn == 0:
            o_ref[pl.ds(r0, RG_TB), :] = h
        else:
            o_ref[pl.ds(r0, RG_TB), :] = (o_ref[pl.ds(r0, RG_TB), :] + h) * _gelu_tanh(gt_ref[pl.ds(r0, RG_TB), :])
        return carry

    zero = jnp.zeros((1, D_BRANCH), F32)
    lax.fori_loop(0, nblk, lambda i, c: block(i, c, 0), zero)
    lax.fori_loop(0, nblk, lambda i, c: block(nblk - 1 - i, c, 1), zero)


def _blockdiag(w):
    eye = jnp.eye(B_BLOCKS, dtype=w.dtype)
    return jnp.einsum('ncd,nm->ncmd', w, eye).reshape(D_BRANCH, D_BRANCH)


def _rglru(pb3, cw, cb, wa, ba, wx, bx, lam):
    bsz, seq, _ = pb3.shape
    wg = jnp.concatenate([_blockdiag(wa[0]), _blockdiag(wx[0]), _blockdiag(wa[1]), _blockdiag(wx[1])], axis=1)
    bg = jnp.concatenate([ba[0], bx[0], ba[1], bx[1]]).reshape(1, 4 * D_BRANCH)
    wh = wg.astype(BF16)
    wl = (wg - wh.astype(F32)).astype(BF16)
    blk = (None, seq, D_BRANCH)
    return pl.pallas_call(
        _rglru_kernel,
        grid=(bsz,),
        in_specs=[pl.BlockSpec(blk, lambda b: (b, 0, 0)),
                  pl.BlockSpec(blk, lambda b: (b, 0, 1)),
                  _const_spec((B_CONV, D_BRANCH)), _const_spec((1, D_BRANCH)),
                  _const_spec((D_BRANCH, 4 * D_BRANCH)), _const_spec((D_BRANCH, 4 * D_BRANCH)),
                  _const_spec((1, 4 * D_BRANCH)), _const_spec((2, D_BRANCH))],
        out_specs=pl.BlockSpec(blk, lambda b: (b, 0, 0)),
        out_shape=jax.ShapeDtypeStruct((bsz, seq, D_BRANCH), F32),
        scratch_shapes=[pltpu.VMEM((seq + 2 * RG_PAD, D_BRANCH), F32), pltpu.VMEM((seq, D_BRANCH), F32)],
        compiler_params=_cparams("parallel"),
        name="rglru",
    )(pb3, pb3, cw, cb.reshape(1, D_BRANCH), wh, wl, bg, lam)


FFT_IN = 128
FFT_NT = FFT_IN // SUBLANES
FFT_ACT_PASSES = 1
HY_TB = 256
HY_PAD = SUBLANES


def _split_np(a):
    a = np.asarray(a, np.float32)
    hi = a.astype(jnp.bfloat16)
    lo = (a - hi.astype(np.float32)).astype(jnp.bfloat16)
    return jnp.asarray(hi), jnp.asarray(lo)


def _dot3(mh, ml, x):
    xh = x.astype(BF16)
    xl = (x - xh.astype(F32)).astype(BF16)
    return (jnp.dot(mh, xh, preferred_element_type=F32) + jnp.dot(mh, xl, preferred_element_type=F32)
            + jnp.dot(ml, xh, preferred_element_type=F32))


def _dotp(mh, ml, x, passes):
    if passes == 1:
        return jnp.dot(mh, x.astype(BF16), preferred_element_type=F32)
    return _dot3(mh, ml, x)


def _fft_tables(seq):
    n = 2 * seq
    n1 = n // FFT_IN
    half = n1 // 2
    eye = np.eye(SUBLANES)
    a = 2.0 * np.pi * np.outer(np.arange(n1), np.arange(half)) / n1
    gr, gi = np.cos(a), -np.sin(a)
    blk = np.stack([np.stack([gr, -gi], axis=1), np.stack([gi, gr], axis=1)], axis=0)
    m_out = np.einsum('rkis,cd->rkcisd', blk, eye).reshape(2 * n1 * SUBLANES, 2 * half * SUBLANES)
    ir, ii = gr.T / n, -gi.T / n
    blk = np.stack([np.stack([ir, -ii], axis=1), np.stack([ii, ir], axis=1)], axis=0)
    m_inv = np.einsum('otrk,cd->otcrkd', blk, eye).reshape(2 * half * SUBLANES, 2 * n1 * SUBLANES)
    a = 2.0 * np.pi * np.outer(np.arange(FFT_IN), np.arange(FFT_IN)) / FFT_IN
    fr, fi = np.cos(a), -np.sin(a)
    w_fwd = np.block([[fr, -fi], [fi, fr]])
    w_inv = np.block([[fr, fi], [-fi, fr]])
    s_in = SUBLANES * np.arange(FFT_NT)[:, None, None] + np.arange(SUBLANES)[None, None, :]
    th = 2.0 * np.pi * s_in * np.arange(n1)[None, :, None] / n
    tw = (jnp.asarray(np.cos(th)[..., None], F32), jnp.asarray(-np.sin(th)[..., None], F32))
    return dict(m_out=_split_np(m_out), m_inv=_split_np(m_inv), w_fwd=_split_np(w_fwd), w_inv=_split_np(w_inv),
                tw=tw, n1=n1, half=half)


FFT_PP = 2


def _tw_spec(n1):
    return pl.BlockSpec((None, n1, SUBLANES, 1), lambda t, q: (t, 0, 0, 0))


def _fft_outer_kernel(mh_ref, ml_ref, twr_ref, twi_ref, z_ref, v_ref, *, complex_in, passes):
    n1 = v_ref.shape[1]
    tr, ti = twr_ref[...], twi_ref[...]
    for pp in range(FFT_PP):
        zz = z_ref[:, pp] if complex_in else z_ref[pp]
        rows_in = math.prod(zz.shape[:-1])
        v = _dotp(mh_ref[:, :rows_in], ml_ref[:, :rows_in], zz.reshape(rows_in, D_BRANCH), passes)
        vr = v[:n1 * SUBLANES].reshape(n1, SUBLANES, D_BRANCH)
        vi = v[n1 * SUBLANES:].reshape(n1, SUBLANES, D_BRANCH)
        v_ref[pp, :, 0] = vr * tr - vi * ti
        v_ref[pp, :, 1] = vr * ti + vi * tr


def _fft_outer(tb, z, complex_in, passes):
    n1, half = tb["n1"], tb["half"]
    if complex_in:
        p = z.shape[1]
        zspec = pl.BlockSpec((2, FFT_PP, half, None, SUBLANES, D_BRANCH), lambda t, q: (0, q, 0, t, 0, 0))
    else:
        p = z.shape[0]
        zspec = pl.BlockSpec((FFT_PP, half, None, SUBLANES, D_BRANCH), lambda t, q: (q, 0, t, 0, 0))
    mshape = (2 * n1 * SUBLANES, 2 * half * SUBLANES)
    return pl.pallas_call(
        functools.partial(_fft_outer_kernel, complex_in=complex_in, passes=passes),
        grid=(FFT_NT, p // FFT_PP),
        in_specs=[_const_spec(mshape), _const_spec(mshape), _tw_spec(n1), _tw_spec(n1), zspec],
        out_specs=pl.BlockSpec((FFT_PP, n1, 2, None, SUBLANES, D_BRANCH), lambda t, q: (q, 0, 0, t, 0, 0)),
        out_shape=jax.ShapeDtypeStruct((p, n1, 2, FFT_NT, SUBLANES, D_BRANCH), F32),
        compiler_params=_cparams("arbitrary", "arbitrary"),
        name="fft_outer_c" if complex_in else "fft_outer_r",
    )(*tb["m_out"], *tb["tw"], z)


FFT_KB = 8


def _fft_filt_kernel(wh_ref, wl_ref, vf_ref, vb_ref, h_ref):
    for kb in range(FFT_KB):
        zf = _dot3(wh_ref[...], wl_ref[...], vf_ref[kb].reshape(2 * FFT_IN, D_BRANCH))
        zb = _dot3(wh_ref[...], wl_ref[...], vb_ref[kb].reshape(2 * FFT_IN, D_BRANCH))
        h_ref[kb, 0] = zf[:FFT_IN] + zb[:FFT_IN]
        h_ref[kb, 1] = zf[FFT_IN:] - zb[FFT_IN:]


def _fft_filt(tb, v):
    n1 = v.shape[1]
    wspec = _const_spec((2 * FFT_IN, 2 * FFT_IN))
    vblk = (None, FFT_KB, 2, FFT_NT, SUBLANES, D_BRANCH)
    return pl.pallas_call(
        _fft_filt_kernel,
        grid=(C_ORDER, n1 // FFT_KB),
        in_specs=[wspec, wspec,
                  pl.BlockSpec(vblk, lambda o, k: (2 * o, k, 0, 0, 0, 0)),
                  pl.BlockSpec(vblk, lambda o, k: (2 * o + 1, k, 0, 0, 0, 0))],
        out_specs=pl.BlockSpec((None, FFT_KB, 2, FFT_IN, D_BRANCH), lambda o, k: (o, k, 0, 0, 0)),
        out_shape=jax.ShapeDtypeStruct((C_ORDER, n1, 2, FFT_IN, D_BRANCH), F32),
        compiler_params=_cparams("parallel", "parallel"),
        name="fft_filt",
    )(*tb["w_fwd"], v, v)


def _fft_mid_kernel(wfh_ref, wfl_ref, wih_ref, wil_ref, v_ref, h_ref, d_ref):
    for kb in range(FFT_KB):
        z = _dotp(wfh_ref[...], wfl_ref[...], v_ref[kb].reshape(2 * FFT_IN, D_BRANCH), FFT_ACT_PASSES)
        zr, zi = z[:FFT_IN], z[FFT_IN:]
        hr, hi = h_ref[kb, 0], h_ref[kb, 1]
        pr = zr * hr - zi * hi
        pi = zr * hi + zi * hr
        d = _dotp(wih_ref[...], wil_ref[...], jnp.concatenate([pr, pi], axis=0), FFT_ACT_PASSES)
        d_ref[kb] = d.reshape(d_ref.shape[1:])


def _fft_mid(tb, v, hspec, order):
    p, n1 = v.shape[:2]
    vspec = pl.BlockSpec((None, FFT_KB, 2, FFT_NT, SUBLANES, D_BRANCH), lambda k, q: (q, k, 0, 0, 0, 0))
    wspec = _const_spec((2 * FFT_IN, 2 * FFT_IN))
    return pl.pallas_call(
        _fft_mid_kernel,
        grid=(n1 // FFT_KB, p),
        in_specs=[wspec, wspec, wspec, wspec, vspec,
                  pl.BlockSpec((None, FFT_KB, 2, FFT_IN, D_BRANCH), lambda k, q: (order, k, 0, 0, 0))],
        out_specs=vspec,
        out_shape=jax.ShapeDtypeStruct(v.shape, F32),
        compiler_params=_cparams("parallel", "arbitrary"),
        name="fft_mid",
    )(*tb["w_fwd"], *tb["w_inv"], v, hspec)


def _ifft_outer_kernel(mh_ref, ml_ref, twr_ref, twi_ref, d_ref, u_ref, x_ref, b_ref, o_ref):
    n1 = d_ref.shape[1]
    tr, ti = twr_ref[...], twi_ref[...]
    for pp in range(FFT_PP):
        dr, di = d_ref[pp, :, 0], d_ref[pp, :, 1]
        er = (dr * tr + di * ti).reshape(n1 * SUBLANES, D_BRANCH)
        ei = (di * tr - dr * ti).reshape(n1 * SUBLANES, D_BRANCH)
        y = _dotp(mh_ref[...], ml_ref[...], jnp.concatenate([er, ei], axis=0), FFT_ACT_PASSES)
        y = y.reshape((2,) + o_ref.shape[2:])
        o_ref[:, pp] = x_ref[:, pp] * (y + u_ref[:, pp] * b_ref[...])


def _ifft_outer(tb, d, u, ucol, x, xcol, bias):
    n1, half = tb["n1"], tb["half"]
    p = d.shape[0]
    mshape = (2 * half * SUBLANES, 2 * n1 * SUBLANES)
    io = lambda col: pl.BlockSpec((2, FFT_PP, half, None, SUBLANES, D_BRANCH), lambda t, q: (0, q, 0, t, 0, col))
    return pl.pallas_call(
        _ifft_outer_kernel,
        grid=(FFT_NT, p // FFT_PP),
        in_specs=[_const_spec(mshape), _const_spec(mshape), _tw_spec(n1), _tw_spec(n1),
                  pl.BlockSpec((FFT_PP, n1, 2, None, SUBLANES, D_BRANCH), lambda t, q: (q, 0, 0, t, 0, 0)),
                  io(ucol), io(xcol), _const_spec((1, D_BRANCH))],
        out_specs=io(0),
        out_shape=jax.ShapeDtypeStruct((2, p, half, FFT_NT, SUBLANES, D_BRANCH), F32),
        compiler_params=_cparams("arbitrary", "arbitrary"),
        name="ifft_outer",
    )(*tb["m_inv"], *tb["tw"], d, u, x, bias.reshape(1, D_BRANCH))


def _hyconv_kernel(x_ref, w_ref, b_ref, o_ref, xp_ref):
    seq = x_ref.shape[0]
    xp_ref[0:HY_PAD, :] = jnp.zeros((HY_PAD, D_BRANCH), F32)
    xp_ref[HY_PAD + seq:2 * HY_PAD + seq, :] = jnp.zeros((HY_PAD, D_BRANCH), F32)
    xp_ref[HY_PAD:HY_PAD + seq, :] = x_ref[...]

    def body(i, carry):
        r0 = pl.multiple_of(i * HY_TB, HY_TB)
        win = xp_ref[pl.ds(r0, HY_TB + 2 * HY_PAD), :]
        y = b_ref[...]
        for j in range(C_CONV):
            s0 = HY_PAD + j - C_CONV // 2
            y = y + w_ref[j:j + 1, :] * win[s0:s0 + HY_TB, :]
        o_ref[pl.ds(r0, HY_TB), :] = y
        return carry

    lax.fori_loop(0, seq // HY_TB, body, 0)


def _hyconv(pc3, cw, cb):
    bsz, seq, width = pc3.shape
    nb = width // D_BRANCH
    blk = pl.BlockSpec((None, seq, D_BRANCH), lambda b, j: (b, 0, j))
    return pl.pallas_call(
        _hyconv_kernel,
        grid=(bsz, nb),
        in_specs=[blk, pl.BlockSpec((C_CONV, D_BRANCH), lambda b, j: (0, j)),
                  pl.BlockSpec((1, D_BRANCH), lambda b, j: (0, j))],
        out_specs=blk,
        out_shape=jax.ShapeDtypeStruct(pc3.shape, F32),
        scratch_shapes=[pltpu.VMEM((seq + 2 * HY_PAD, D_BRANCH), F32)],
        compiler_params=_cparams("parallel", "parallel"),
        name="hyconv",
    )(pc3, cw, cb.reshape(1, width))


def _hyfilt_kernel(z_ref, w1_ref, b1_ref, fr_ref, w2_ref, b2_ref, w3_ref, dec_ref, o_ref, h_ref):
    seq = z_ref.shape[0]
    nblk = seq // HY_TB
    fr = fr_ref[...]

    @pl.when(pl.program_id(0) == 0)
    def _():
        def hidden(i, carry):
            r0 = pl.multiple_of(i * HY_TB, HY_TB)
            zb = z_ref[pl.ds(r0, HY_TB), :]
            h = jnp.sin(fr * (jnp.dot(zb, w1_ref[...], precision=HI, preferred_element_type=F32) + b1_ref[...]))
            h = jnp.sin(fr * (jnp.dot(h, w2_ref[...], precision=HI, preferred_element_type=F32) + b2_ref[...]))
            h_ref[pl.ds(r0, HY_TB), :] = h
            return carry

        lax.fori_loop(0, nblk, hidden, 0)

    def body(i, ss):
        r0 = pl.multiple_of(i * HY_TB, HY_TB)
        hf = jnp.dot(h_ref[pl.ds(r0, HY_TB), :], w3_ref[...], precision=HI, preferred_element_type=F32)
        hf = hf * jnp.exp(-z_ref[pl.ds(r0, HY_TB), 0:1] * dec_ref[...])
        o_ref[pl.ds(r0, HY_TB), :] = hf
        return ss + jnp.sum(hf * hf, axis=0, keepdims=True)

    ss = lax.fori_loop(0, nblk, body, jnp.zeros((1, D_BRANCH), F32))
    scale = lax.rsqrt(ss + EPS)

    def norm(i, carry):
        r0 = pl.multiple_of(i * HY_TB, HY_TB)
        o_ref[pl.ds(r0, HY_TB), :] = o_ref[pl.ds(r0, HY_TB), :] * scale
        return carry

    lax.fori_loop(0, nblk, norm, 0)


def _hyfilt(seq, w1, b1, freq, w2, b2, w3):
    t = jnp.linspace(0.0, 1.0, seq, dtype=F32)[:, None]
    bands = (C_EMB - 1) // 2
    w = 2.0 * math.pi * jnp.arange(seq, dtype=F32)[:, None] / seq
    fr = jnp.linspace(1e-4, bands - 1, bands, dtype=F32)[None]
    z = jnp.concatenate([t, jnp.cos(fr * w), -jnp.sin(fr * w)], axis=-1)
    z = jnp.pad(z, ((0, 0), (0, LANES - C_EMB)))
    padm = lambda a, r, c: jnp.pad(a.astype(F32), ((0, r - a.shape[0]), (0, c - a.shape[1])))
    row = lambda a: padm(a.reshape(1, -1), 1, LANES)
    dec = jnp.abs(jnp.linspace(C_MIN_DECAY, C_MAX_DECAY, D_BRANCH, dtype=F32)).reshape(1, D_BRANCH)
    nset = C_ORDER * 2
    return pl.pallas_call(
        _hyfilt_kernel,
        grid=(nset,),
        in_specs=[_const_spec((seq, LANES)), _const_spec((LANES, LANES)), _const_spec((1, LANES)),
                  _const_spec((1, LANES)), _const_spec((LANES, LANES)), _const_spec((1, LANES)),
                  pl.BlockSpec((LANES, D_BRANCH), lambda j: (0, j)), _const_spec((1, D_BRANCH))],
        out_specs=pl.BlockSpec((None, seq, D_BRANCH), lambda j: (j, 0, 0)),
        out_shape=jax.ShapeDtypeStruct((nset, seq, D_BRANCH), F32),
        scratch_shapes=[pltpu.VMEM((seq, LANES), F32)],
        compiler_params=_cparams("arbitrary"),
        name="hyfilt",
    )(z, padm(w1, LANES, LANES), row(b1), row(freq), padm(w2, LANES, LANES), row(b2),
      padm(w3, LANES, nset * D_BRANCH), dec)


def _hyena(pc3, cw, cb, w1, b1, freq, w2, b2, w3, bias):
    bsz, seq, width = pc3.shape
    tb = _fft_tables(seq)
    half = tb["half"]
    npair = bsz // 2
    uc = _hyconv(pc3, cw, cb)
    uc6 = uc.reshape(2, npair, half, FFT_NT, SUBLANES, width)
    filt = _hyfilt(seq, w1, b1, freq, w2, b2, w3).reshape(C_ORDER * 2, half, FFT_NT, SUBLANES, D_BRANCH)
    hspec = _fft_filt(tb, _fft_outer(tb, filt, False, 3))
    src = uc6
    for order in range(C_ORDER):
        d = _fft_mid(tb, _fft_outer(tb, src, True, FFT_ACT_PASSES), hspec, order)
        src = _ifft_outer(tb, d, src, 0, uc6, order + 1, bias[order])
    return src.reshape(bsz, seq, D_BRANCH)


AT_HALF = 64
AT_TQ = 512
AT_SUB = 128


def _t5_bucket(rel):
    half = N_BUCKETS // 2
    max_exact = half // 2
    n = np.abs(rel)
    large = max_exact + (np.log(np.maximum(n, 1) / max_exact) / math.log(MAX_DISTANCE / max_exact)
                         * (half - max_exact)).astype(np.int64)
    large = np.minimum(large, half - 1)
    return (rel > 0).astype(np.int64) * half + np.where(n < max_exact, n, large)


def _attn_geometry(n):
    tq = min(AT_TQ, n)
    sub = min(AT_SUB, tq)
    win = min(sub + 2 * AT_HALF, n)
    return tq, sub, win, n // tq, n // sub


def _attn_bias_tables(rel_bias, g, dil, n):
    _, sub, win, _, nsb = _attn_geometry(n)
    hs = slice(g * D_HEADS_PER_GROUP, (g + 1) * D_HEADS_PER_GROUP)
    offsets = np.arange(-AT_HALF, AT_HALF + 1) * dil
    onehot = np.zeros((2 * AT_HALF + 1, N_BUCKETS), np.float32)
    onehot[np.arange(2 * AT_HALF + 1), _t5_bucket(offsets)] = 1.0
    band = jnp.dot(rel_bias.astype(F32)[:, hs].T, jnp.asarray(onehot).T, precision=HI)
    nband = 2 * AT_HALF + 1
    lv = sub + win - 1
    tables = []
    for i in sorted({0, min(1, nsb - 1), nsb - 1}):
        ws = int(np.clip(i * sub - AT_HALF, 0, n - win))
        lo = (sub - 1) - (ws - i * sub) - AT_HALF
        v = jnp.pad(band, ((0, 0), (lo, lv - lo - nband)), constant_values=NEG_BIG)
        flat = jnp.tile(v, (1, sub + 1))[:, sub - 1:sub - 1 + sub * (lv - 1)]
        tables.append(flat.reshape(D_HEADS_PER_GROUP, sub, lv - 1)[:, :, :win])
    return jnp.stack(tables)


def _attn_kernel(q_ref, k_ref, v_ref, bias_ref, o_ref, l_ref, *, n, dil):
    tq, sub, win, _, nsb = _attn_geometry(n)
    ncase = bias_ref.shape[0]
    width = q_ref.shape[-1]
    nh = width // D_HEAD_DIM
    hp = pl.program_id(1)
    lane_head = lax.broadcasted_iota(jnp.int32, (sub, width), 1) // D_HEAD_DIM
    hmask = [lane_head == hh for hh in range(nh)]
    whole = win == n
    for r in range(dil):
        if whole:
            kw = k_ref[pl.ds(r, win, stride=dil), :].astype(BF16)
            vw = v_ref[pl.ds(r, win, stride=dil), :].astype(BF16)
        for j in range(tq // sub):
            sidx = pl.program_id(2) * (tq // sub) + j
            case = jnp.minimum(jnp.where(sidx == nsb - 1, ncase - 1, jnp.minimum(sidx, 1)), ncase - 1)
            bias = bias_ref[case, pl.ds(hp * nh, nh)].reshape(nh * sub, win)
            if not whole:
                ws = pl.multiple_of(jnp.clip(sidx * sub - AT_HALF, 0, n - win), AT_HALF)
                kw = k_ref[pl.ds(ws * dil + r, win, stride=dil), :].astype(BF16)
                vw = v_ref[pl.ds(ws * dil + r, win, stride=dil), :].astype(BF16)
            q = q_ref[pl.ds(j * sub * dil + r, sub, stride=dil), :] * (D_HEAD_DIM ** -0.5)
            qs = jnp.concatenate([jnp.where(hmask[hh], q, 0.0) for hh in range(nh)], axis=0).astype(BF16)
            s = lax.dot_general(qs, kw, (((1,), (1,)), ((), ())), preferred_element_type=F32) + bias
            m = jnp.max(s, axis=-1, keepdims=True)
            p = jnp.exp(s - m)
            l = jnp.sum(p, axis=-1, keepdims=True)
            o_all = jnp.dot(p.astype(BF16), vw, preferred_element_type=F32) / l
            lse = m + jnp.log(l)
            o_acc = jnp.zeros((sub, width), F32)
            l_acc = jnp.zeros((sub, width), F32)
            for hh in range(nh):
                o_acc = jnp.where(hmask[hh], o_all[hh * sub:(hh + 1) * sub], o_acc)
                l_acc = jnp.where(hmask[hh], lse[hh * sub:(hh + 1) * sub], l_acc)
            o_ref[pl.ds(j * sub * dil + r, sub, stride=dil), :] = o_acc
            l_ref[pl.ds(j * sub * dil + r, sub, stride=dil), :] = l_acc


def _banded_attention(pd3, g, dil, bias):
    bsz, seq, width = pd3.shape
    n = seq // dil
    tq, sub, win, nq, _ = _attn_geometry(n)
    rows = dil * tq
    bw = LANES if dil > 1 else D_BRANCH
    per = D_BRANCH // bw
    third = width // 3 // bw
    out_spec = pl.BlockSpec((None, rows, bw), lambda b, h, i: (b, i, h))
    shp = jax.ShapeDtypeStruct((bsz, seq, D_BRANCH), F32)
    return pl.pallas_call(
        functools.partial(_attn_kernel, n=n, dil=dil),
        grid=(bsz, per, nq),
        in_specs=[pl.BlockSpec((None, rows, bw), lambda b, h, i: (b, i, g * per + h)),
                  pl.BlockSpec((None, seq, bw), lambda b, h, i: (b, 0, third + g * per + h)),
                  pl.BlockSpec((None, seq, bw), lambda b, h, i: (b, 0, 2 * third + g * per + h)),
                  _const_spec(bias.shape)],
        out_specs=[out_spec, out_spec],
        out_shape=[shp, shp],
        compiler_params=_cparams("parallel", "parallel", "arbitrary"),
        name=f"attn_d{dil}",
    )(pd3, pd3, pd3, bias)


def _dilated_attention(pd3, rel_bias):
    bsz, seq, _ = pd3.shape
    outs, lses = [], []
    for g, (_, dil) in enumerate(D_GROUPS):
        o, l = _banded_attention(pd3, g, dil, _attn_bias_tables(rel_bias, g, dil, seq // dil))
        outs.append(o.reshape(bsz * seq, D_BRANCH))
        lses.append(l.reshape(bsz * seq, D_BRANCH))
    return outs, lses


def kernel(x, norm1_g, w_in, hgrn_lb_logits, hgrn_norm_g, lru_conv_w, lru_conv_b, lru_wa, lru_ba, lru_wx, lru_bx,
           lru_lambda, hy_conv_w, hy_conv_b, hy_w1, hy_b1, hy_freq, hy_w2, hy_b2, hy_w3, hy_bias, rel_bias,
           w_branch, w_gate, b_gate, w_out, norm2_g, w_ff1, w_ff3, w_ff2, final_g):
    bsz, seq, _ = x.shape
    n = bsz * seq
    lb_soft = jax.nn.softmax(hgrn_lb_logits.astype(F32), axis=0)
    lower_bounds = jnp.cumsum(lb_soft, axis=0) - lb_soft[0]
    x2 = x.reshape(n, D_MODEL)
    flat = lambda a: a.reshape(n, D_BRANCH)
    for l in range(DEPTH):
        pa, pb, pc, pd = _inproj(x2, norm1_g[l], w_in[l].astype(BF16))
        oa_f, oa_b = _hgrn(pa.reshape(bsz, seq, IN_A), lower_bounds[l])
        yb = _rglru(pb.reshape(bsz, seq, IN_B), lru_conv_w[l], lru_conv_b[l], lru_wa[l], lru_ba[l],
                    lru_wx[l], lru_bx[l], lru_lambda[l])
        yc = _hyena(pc.reshape(bsz, seq, IN_C), hy_conv_w[l], hy_conv_b[l], hy_w1[l], hy_b1[l], hy_freq[l],
                    hy_w2[l], hy_b2[l], hy_w3[l], hy_bias[l])
        od, ld = _dilated_attention(pd.reshape(bsz, seq, IN_D), rel_bias)
        x2 = _merge(x2, norm1_g[l], flat(oa_f), flat(oa_b), pa, hgrn_norm_g[l], flat(yb), flat(yc), od, ld,
                    w_gate[l].reshape(D_MODEL, N_BRANCH * D_MODEL).astype(BF16),
                    b_gate[l].reshape(1, N_BRANCH * D_MODEL), w_branch[l].astype(BF16), w_out[l].astype(BF16))
        x2 = _ffn(x2, norm2_g[l], *_ffn_weights(w_ff1[l], w_ff3[l], w_ff2[l]), final_g, l == DEPTH - 1)
    return x2.reshape(bsz, seq, D_MODEL)
```

```python
import functools
import math

import jax
import jax.numpy as jnp
import numpy as np
from jax import lax
from jax.experimental import pallas as pl
from jax.experimental.pallas import tpu as pltpu

F32 = jnp.float32
BF16 = jnp.bfloat16
HI = lax.Precision.HIGHEST

D_MODEL = 1024
DEPTH = 2
EPS = 1e-6
TINY = 1e-30
N_BRANCH = 4
D_BRANCH = 256
A_HEADS = 4
A_DK = 64
A_CHUNK = 64
B_BLOCKS = 4
B_BW = 64
B_CONV = 4
LRU_C = 8.0
C_ORDER = 2
C_CONV = 3
C_EMB = 33
C_HID = 64
C_MIN_DECAY = math.log(1e-2) / 1.5
C_MAX_DECAY = math.log(1e-2) / 0.3
D_GROUPS = ((128, 1), (512, 4), (2048, 16))
D_HEADS_PER_GROUP = 4
D_HEAD_DIM = 64
D_N_HEADS = 12
D_QKV = 768
N_BUCKETS = 32
MAX_DISTANCE = 1024
NEG_BIG = -1e30
D_FF = 2816
IN_A = 5 * D_BRANCH
IN_B = 2 * D_BRANCH
IN_C = 3 * D_BRANCH
IN_D = 3 * D_QKV
IN_WIDTH = IN_A + IN_B + IN_C + IN_D

LANES = 128
SUBLANES = 8
VMEM_LIMIT = 56 * 1024 * 1024


def _cparams(*sem):
    return pltpu.CompilerParams(dimension_semantics=sem, vmem_limit_bytes=VMEM_LIMIT)


def _const_spec(shape):
    nd = len(shape)
    return pl.BlockSpec(shape, lambda *_: (0,) * nd, pipeline_mode=pl.Buffered(1))


def _rms(x, g):
    return x * lax.rsqrt(jnp.mean(x * x, axis=-1, keepdims=True) + EPS) * g


def _sigmoid(x):
    return 1.0 / (1.0 + jnp.exp(-x))


IN_TM = 512
IN_CHUNK = 256


def _inproj_kernel(x_ref, g_ref, w_ref, oa_ref, ob_ref, oc_ref, od_ref):
    h = _rms(x_ref[...], g_ref[...]).astype(BF16)
    off = 0
    for o_ref in (oa_ref, ob_ref, oc_ref, od_ref):
        width = o_ref.shape[-1]
        for c in range(0, width, IN_CHUNK):
            o_ref[:, c:c + IN_CHUNK] = jnp.dot(h, w_ref[:, off + c:off + c + IN_CHUNK],
                                               preferred_element_type=F32)
        off += width


def _inproj(x2, g, w_bf16):
    n = x2.shape[0]
    widths = (IN_A, IN_B, IN_C, IN_D)
    return pl.pallas_call(
        _inproj_kernel,
        grid=(n // IN_TM,),
        in_specs=[pl.BlockSpec((IN_TM, D_MODEL), lambda i: (i, 0)),
                  _const_spec((1, D_MODEL)),
                  _const_spec((D_MODEL, IN_WIDTH))],
        out_specs=[pl.BlockSpec((IN_TM, w), lambda i: (i, 0)) for w in widths],
        out_shape=[jax.ShapeDtypeStruct((n, w), F32) for w in widths],
        compiler_params=_cparams("parallel"),
        name="inproj",
    )(x2, g.reshape(1, D_MODEL), w_bf16)


FF_TM = 1024
FF_CHUNK = 256
FF_NCHUNK = D_FF // FF_CHUNK


def _ffn_kernel(x_ref, g_ref, w1_ref, w3_ref, w2_ref, fg_ref, o_ref, acc_ref, *, final):
    x = x_ref[...]
    h = _rms(x, g_ref[...]).astype(BF16)
    acc_ref[...] = x

    def body(c, carry):
        a = jnp.dot(h, w1_ref[c], preferred_element_type=F32)
        b = jnp.dot(h, w3_ref[c], preferred_element_type=F32)
        t = (a * _sigmoid(a) * b).astype(BF16)
        acc_ref[...] += jnp.dot(t, w2_ref[c], preferred_element_type=F32)
        return carry

    lax.fori_loop(0, FF_NCHUNK, body, 0)
    y = acc_ref[...]
    if final:
        y = _rms(y, fg_ref[...])
    o_ref[...] = y


def _ffn(x2, g, w1c, w3c, w2c, final_g, final):
    n = x2.shape[0]
    return pl.pallas_call(
        functools.partial(_ffn_kernel, final=final),
        grid=(n // FF_TM,),
        in_specs=[pl.BlockSpec((FF_TM, D_MODEL), lambda i: (i, 0)),
                  _const_spec((1, D_MODEL)),
                  _const_spec((FF_NCHUNK, D_MODEL, FF_CHUNK)),
                  _const_spec((FF_NCHUNK, D_MODEL, FF_CHUNK)),
                  _const_spec((FF_NCHUNK, FF_CHUNK, D_MODEL)),
                  _const_spec((1, D_MODEL))],
        out_specs=pl.BlockSpec((FF_TM, D_MODEL), lambda i: (i, 0)),
        out_shape=jax.ShapeDtypeStruct((n, D_MODEL), F32),
        scratch_shapes=[pltpu.VMEM((FF_TM, D_MODEL), F32)],
        compiler_params=_cparams("parallel"),
        name="ffn_final" if final else "ffn",
    )(x2, g.reshape(1, D_MODEL), w1c, w3c, w2c, final_g.reshape(1, D_MODEL))


def _ffn_weights(w1, w3, w2):
    w1c = w1.astype(BF16).reshape(D_MODEL, FF_NCHUNK, FF_CHUNK).transpose(1, 0, 2)
    w3c = w3.astype(BF16).reshape(D_MODEL, FF_NCHUNK, FF_CHUNK).transpose(1, 0, 2)
    w2c = w2.astype(BF16).reshape(FF_NCHUNK, FF_CHUNK, D_MODEL)
    return w1c, w3c, w2c


MG_TM = 512


def _head_ones():
    r = np.arange(D_BRANCH)[:, None] // A_DK
    c = np.arange(D_BRANCH)[None, :] // A_DK
    return jnp.asarray((r == c).astype(np.float32) / A_DK)


def _merge_kernel(x_ref, g1_ref, oaf_ref, oab_ref, ga_ref, hg_ref, hm_ref, yb_ref, yc_ref,
                  o0_ref, o1_ref, o2_ref, l0_ref, l1_ref, l2_ref,
                  wg_ref, bg_ref, wb_ref, wo_ref, out_ref):
    x = x_ref[...]
    h = _rms(x, g1_ref[...]).astype(BF16)
    oa = oaf_ref[...] + oab_ref[...]
    ms = jnp.dot(oa * oa, hm_ref[...], precision=HI, preferred_element_type=F32)
    ga = ga_ref[...]
    ya = oa * lax.rsqrt(ms + EPS) * hg_ref[...] * (ga * _sigmoid(ga))
    l0, l1, l2 = l0_ref[...], l1_ref[...], l2_ref[...]
    m = jnp.maximum(jnp.maximum(l0, l1), l2)
    e0, e1, e2 = jnp.exp(l0 - m), jnp.exp(l1 - m), jnp.exp(l2 - m)
    yd = (e0 * o0_ref[...] + e1 * o1_ref[...] + e2 * o2_ref[...]) / (e0 + e1 + e2)
    mixed = None
    for j, y in enumerate((ya, yb_ref[...], yc_ref[...], yd)):
        gate = _sigmoid(jnp.dot(h, wg_ref[:, j * D_MODEL:(j + 1) * D_MODEL], preferred_element_type=F32)
                        + bg_ref[:, j * D_MODEL:(j + 1) * D_MODEL])
        t = gate * jnp.dot(y.astype(BF16), wb_ref[j], preferred_element_type=F32)
        mixed = t if mixed is None else mixed + t
    out_ref[...] = x + jnp.dot(mixed.astype(BF16), wo_ref[...], preferred_element_type=F32)


def _merge(x2, g1, oa_f, oa_b, pa, hg, yb, yc, od, ld, wg, bg, wb, wo):
    n = x2.shape[0]
    tile = lambda w: pl.BlockSpec((MG_TM, w), lambda i: (i, 0))
    return pl.pallas_call(
        _merge_kernel,
        grid=(n // MG_TM,),
        in_specs=[tile(D_MODEL), _const_spec((1, D_MODEL)),
                  tile(D_BRANCH), tile(D_BRANCH),
                  pl.BlockSpec((MG_TM, D_BRANCH), lambda i: (i, 4)),
                  _const_spec((1, D_BRANCH)), _const_spec((D_BRANCH, D_BRANCH)),
                  tile(D_BRANCH), tile(D_BRANCH),
                  tile(D_BRANCH), tile(D_BRANCH), tile(D_BRANCH),
                  tile(D_BRANCH), tile(D_BRANCH), tile(D_BRANCH),
                  _const_spec((D_MODEL, N_BRANCH * D_MODEL)), _const_spec((1, N_BRANCH * D_MODEL)),
                  _const_spec((N_BRANCH, D_BRANCH, D_MODEL)), _const_spec((D_MODEL, D_MODEL))],
        out_specs=tile(D_MODEL),
        out_shape=jax.ShapeDtypeStruct((n, D_MODEL), F32),
        compiler_params=_cparams("parallel"),
        name="merge",
    )(x2, g1.reshape(1, D_MODEL), oa_f, oa_b, pa, hg.reshape(1, D_BRANCH), _head_ones(), yb, yc,
      od[0], od[1], od[2], ld[0], ld[1], ld[2], wg, bg, wb, wo)


HG_TS = 256
HG_NCH = HG_TS // A_CHUNK
HG_MID = A_CHUNK // 2


def _hgrn_tables():
    r = np.arange(HG_TS)[:, None]
    c = np.arange(HG_TS)[None, :]
    same = (r // A_CHUNK) == (c // A_CHUNK)
    rr = np.arange(HG_NCH * A_HEADS * A_CHUNK)[:, None]
    same_s = (rr // (A_HEADS * A_CHUNK)) == (c // A_CHUNK)
    t, sidx = rr % A_CHUNK, c % A_CHUNK
    out = []
    for fwd in (True, False):
        order = (r >= c) if fwd else (r <= c)
        order_s = (t >= sidx) if fwd else (t <= sidx)
        out.append((jnp.asarray((same & order).astype(np.float32), BF16),
                    jnp.asarray((same_s & order_s).astype(np.float32), F32)))
    return out


def _hgrn_prep(q, fl, v, lb, cum, smask, fwd):
    lane_head = lax.broadcasted_iota(jnp.int32, (A_CHUNK, D_BRANCH), 1) // A_DK
    hmask = [(lane_head == hh).astype(F32) for hh in range(A_HEADS)]
    blk_r = lax.broadcasted_iota(jnp.int32, (D_BRANCH, D_BRANCH), 0) // A_DK
    blk_c = lax.broadcasted_iota(jnp.int32, (D_BRANCH, D_BRANCH), 1) // A_DK
    blockdiag = (blk_r == blk_c).astype(F32)
    col_chunk = lax.broadcasted_iota(jnp.int32, (D_BRANCH, HG_TS), 1) // A_CHUNK

    sg = _sigmoid(fl)
    f = lb + (1.0 - lb) * sg
    g = jnp.log(jnp.maximum(f, TINY))
    kk = (1.0 - lb) * (1.0 - sg)
    g1 = g.astype(BF16)
    r1 = g - g1.astype(F32)
    g2 = r1.astype(BF16)
    g3 = (r1 - g2.astype(F32)).astype(BF16)
    b = (jnp.dot(cum, g1, preferred_element_type=F32) + jnp.dot(cum, g2, preferred_element_type=F32)
         + jnp.dot(cum, g3, preferred_element_type=F32))
    rows = lambda c: slice(c * A_CHUNK, (c + 1) * A_CHUNK)
    last = (A_CHUNK - 1) if fwd else 0
    bcast = lambda r0: jnp.concatenate(
        [jnp.broadcast_to(b[c * A_CHUNK + r0:c * A_CHUNK + r0 + 1, :], (A_CHUNK, D_BRANCH)) for c in range(HG_NCH)], 0)
    bm = bcast(HG_MID)
    bl = bcast(last)
    qt = q * jnp.exp(b - bm)
    kt = (kk * jnp.exp(bm - b)).astype(BF16)
    qe = (q * jnp.exp(b)).astype(BF16)
    kh = (kk * jnp.exp(bl - b)).astype(BF16)
    qs = jnp.concatenate([qt[rows(c)] * hmask[hh] for c in range(HG_NCH) for hh in range(A_HEADS)],
                         axis=0).astype(BF16)
    s = lax.dot_general(qs, kt, (((1,), (1,)), ((), ())), preferred_element_type=F32)
    s = (s * smask).astype(BF16)
    ost = jnp.dot(s, v.astype(BF16), preferred_element_type=F32)
    o_intra = []
    for c in range(HG_NCH):
        base = c * A_HEADS * A_CHUNK
        o = ost[base:base + A_CHUNK] * hmask[0]
        for hh in range(1, A_HEADS):
            o = o + ost[base + hh * A_CHUNK:base + (hh + 1) * A_CHUNK] * hmask[hh]
        o_intra.append(o)
    vt = v.T
    lhs = jnp.concatenate([jnp.where(col_chunk == c, vt, 0.0) for c in range(HG_NCH)], axis=0).astype(BF16)
    updall = jnp.dot(lhs, kh, preferred_element_type=F32)
    upd = [updall[c * D_BRANCH:(c + 1) * D_BRANCH] * blockdiag for c in range(HG_NCH)]
    decay = [jnp.exp(b[c * A_CHUNK + last:c * A_CHUNK + last + 1, :]) for c in range(HG_NCH)]
    return o_intra, upd, qe, decay


def _hgrn_scan(o_intra, upd, qe, decay, st, fwd):
    rows = lambda c: slice(c * A_CHUNK, (c + 1) * A_CHUNK)
    outs = [None] * HG_NCH
    for c in (range(HG_NCH) if fwd else range(HG_NCH - 1, -1, -1)):
        outs[c] = o_intra[c] + lax.dot_general(qe[rows(c)], st.astype(BF16), (((1,), (1,)), ((), ())),
                                               preferred_element_type=F32)
        st = st * decay[c] + upd[c]
    return jnp.concatenate(outs, axis=0), st


def _hgrn_kernel(qf_ref, ff_ref, vf_ref, qb_ref, fb_ref, vb_ref, lb_ref, cf_ref, mf_ref, cb_ref, mb_ref,
                 of_ref, ob_ref, sf_ref, sb_ref):
    @pl.when(pl.program_id(1) == 0)
    def _():
        sf_ref[...] = jnp.zeros_like(sf_ref)
        sb_ref[...] = jnp.zeros_like(sb_ref)

    lb = lb_ref[...]
    pf = _hgrn_prep(qf_ref[...], ff_ref[...], vf_ref[...], lb, cf_ref[...], mf_ref[...], True)
    pb = _hgrn_prep(qb_ref[...], fb_ref[...], vb_ref[...], lb, cb_ref[...], mb_ref[...], False)
    o_f, st_f = _hgrn_scan(*pf, sf_ref[...], True)
    o_b, st_b = _hgrn_scan(*pb, sb_ref[...], False)
    of_ref[...] = o_f
    ob_ref[...] = o_b
    sf_ref[...] = st_f
    sb_ref[...] = st_b


def _hgrn(pa3, lb):
    bsz, seq, _ = pa3.shape
    nblk = seq // HG_TS
    blk = (None, HG_TS, D_BRANCH)
    up = lambda col: pl.BlockSpec(blk, lambda b, i: (b, i, col))
    down = lambda col: pl.BlockSpec(blk, lambda b, i: (b, nblk - 1 - i, col))
    shp = jax.ShapeDtypeStruct((bsz, seq, D_BRANCH), F32)
    (cum_f, sm_f), (cum_b, sm_b) = _hgrn_tables()
    mshape = (HG_NCH * A_HEADS * A_CHUNK, HG_TS)
    return pl.pallas_call(
        _hgrn_kernel,
        grid=(bsz, nblk),
        in_specs=[up(0), up(1), up(3), down(0), down(2), down(3), _const_spec((1, D_BRANCH)),
                  _const_spec((HG_TS, HG_TS)), _const_spec(mshape), _const_spec((HG_TS, HG_TS)), _const_spec(mshape)],
        out_specs=[up(0), down(0)],
        out_shape=[shp, shp],
        scratch_shapes=[pltpu.VMEM((D_BRANCH, D_BRANCH), F32), pltpu.VMEM((D_BRANCH, D_BRANCH), F32)],
        compiler_params=_cparams("parallel", "arbitrary"),
        name="hgrn2",
    )(pa3, pa3, pa3, pa3, pa3, pa3, lb.reshape(1, D_BRANCH), cum_f, sm_f, cum_b, sm_b)


RG_TB = 128
RG_PAD = SUBLANES
RG_LEFT = B_CONV // 2


def _dot3_rhs(x, wh, wl):
    xh = x.astype(BF16)
    xl = (x - xh.astype(F32)).astype(BF16)
    return (jnp.dot(xh, wh, preferred_element_type=F32) + jnp.dot(xl, wh, preferred_element_type=F32)
            + jnp.dot(xh, wl, preferred_element_type=F32))


def _group_scan(a, u, fwd):
    row = lax.broadcasted_iota(jnp.int32, a.shape, 1)
    k = 1
    while k < SUBLANES:
        if fwd:
            keep = row >= k
            us, as_ = pltpu.roll(u, k, 1), pltpu.roll(a, k, 1)
        else:
            keep = row < SUBLANES - k
            us, as_ = pltpu.roll(u, SUBLANES - k, 1), pltpu.roll(a, SUBLANES - k, 1)
        u = a * jnp.where(keep, us, 0.0) + u
        a = a * jnp.where(keep, as_, 1.0)
        k *= 2
    return a, u


def _block_scan(a, u, carry, fwd):
    t = a.shape[0]
    ngrp = t // SUBLANES
    ag, ug = _group_scan(a.reshape(ngrp, SUBLANES, D_BRANCH), u.reshape(ngrp, SUBLANES, D_BRANCH), fwd)
    hs = [None] * ngrp
    for g in (range(ngrp) if fwd else range(ngrp - 1, -1, -1)):
        h = ug[g] + ag[g] * carry
        hs[g] = h
        carry = h[SUBLANES - 1:SUBLANES, :] if fwd else h[0:1, :]
    return jnp.concatenate(hs, axis=0), carry


def _gelu_tanh(x):
    return 0.5 * x * (1.0 + jnp.tanh(math.sqrt(2.0 / math.pi) * (x + 0.044715 * (x * x * x))))


def _rglru_kernel(x_ref, gt_ref, cw_ref, cb_ref, wh_ref, wl_ref, bg_ref, lam_ref, o_ref, xp_ref, xc_ref):
    seq = x_ref.shape[0]
    nblk = seq // RG_TB
    xp_ref[0:RG_PAD, :] = jnp.zeros((RG_PAD, D_BRANCH), F32)
    xp_ref[RG_PAD + seq:2 * RG_PAD + seq, :] = jnp.zeros((RG_PAD, D_BRANCH), F32)
    xp_ref[RG_PAD:RG_PAD + seq, :] = x_ref[...]
    nl = -lam_ref[...]
    sp = jnp.maximum(nl, 0.0) + jnp.log(1.0 + jnp.exp(-jnp.abs(nl)))

    def block(i, carry, dirn):
        r0 = pl.multiple_of(i * RG_TB, RG_TB)
        if dirn == 0:
            win = xp_ref[pl.ds(r0, RG_TB + 2 * RG_PAD), :]
            xc = cb_ref[...]
            for j in range(B_CONV):
                s0 = RG_PAD + j - RG_LEFT
                xc = xc + cw_ref[j:j + 1, :] * win[s0:s0 + RG_TB, :]
            xc_ref[pl.ds(r0, RG_TB), :] = xc
        else:
            xc = xc_ref[pl.ds(r0, RG_TB), :]
        cols = slice(dirn * 2 * D_BRANCH, (dirn + 1) * 2 * D_BRANCH)
        gates = _dot3_rhs(xc, wh_ref[:, cols], wl_ref[:, cols]) + bg_ref[:, cols]
        r = _sigmoid(gates[:, :D_BRANCH])
        ig = _sigmoid(gates[:, D_BRANCH:])
        log_a = -LRU_C * r * sp[dirn:dirn + 1, :]
        a = jnp.exp(log_a)
        u = jnp.sqrt(jnp.maximum(-jnp.tanh(log_a) * (a * a + 1.0), 0.0)) * ig * xc
        h, carry = _block_scan(a, u, carry, dirn == 0)
        if dirn == 0:
            o_ref[pl.ds(r0, RG_TB), :] = h
        else:
            o_ref[pl.ds(r0, RG_TB), :] = (o_ref[pl.ds(r0, RG_TB), :] + h) * _gelu_tanh(gt_ref[pl.ds(r0, RG_TB), :])
        return carry

    zero = jnp.zeros((1, D_BRANCH), F32)
    lax.fori_loop(0, nblk, lambda i, c: block(i, c, 0), zero)
    lax.fori_loop(0, nblk, lambda i, c: block(nblk - 1 - i, c, 1), zero)


def _blockdiag(w):
    eye = jnp.eye(B_BLOCKS, dtype=w.dtype)
    return jnp.einsum('ncd,nm->ncmd', w, eye).reshape(D_BRANCH, D_BRANCH)


def _rglru(pb3, cw, cb, wa, ba, wx, bx, lam):
    bsz, seq, _ = pb3.shape
    wg = jnp.concatenate([_blockdiag(wa[0]), _blockdiag(wx[0]), _blockdiag(wa[1]), _blockdiag(wx[1])], axis=1)
    bg = jnp.concatenate([ba[0], bx[0], ba[1], bx[1]]).reshape(1, 4 * D_BRANCH)
    wh = wg.astype(BF16)
    wl = (wg - wh.astype(F32)).astype(BF16)
    blk = (None, seq, D_BRANCH)
    return pl.pallas_call(
        _rglru_kernel,
        grid=(bsz,),
        in_specs=[pl.BlockSpec(blk, lambda b: (b, 0, 0)),
                  pl.BlockSpec(blk, lambda b: (b, 0, 1)),
                  _const_spec((B_CONV, D_BRANCH)), _const_spec((1, D_BRANCH)),
                  _const_spec((D_BRANCH, 4 * D_BRANCH)), _const_spec((D_BRANCH, 4 * D_BRANCH)),
                  _const_spec((1, 4 * D_BRANCH)), _const_spec((2, D_BRANCH))],
        out_specs=pl.BlockSpec(blk, lambda b: (b, 0, 0)),
        out_shape=jax.ShapeDtypeStruct((bsz, seq, D_BRANCH), F32),
        scratch_shapes=[pltpu.VMEM((seq + 2 * RG_PAD, D_BRANCH), F32), pltpu.VMEM((seq, D_BRANCH), F32)],
        compiler_params=_cparams("parallel"),
        name="rglru",
    )(pb3, pb3, cw, cb.reshape(1, D_BRANCH), wh, wl, bg, lam)


FFT_IN = 128
FFT_NT = FFT_IN // SUBLANES
FFT_ACT_PASSES = 1
HY_TB = 256
HY_PAD = SUBLANES


def _split_np(a):
    a = np.asarray(a, np.float32)
    hi = a.astype(jnp.bfloat16)
    lo = (a - hi.astype(np.float32)).astype(jnp.bfloat16)
    return jnp.asarray(hi), jnp.asarray(lo)


def _dot3(mh, ml, x):
    xh = x.astype(BF16)
    xl = (x - xh.astype(F32)).astype(BF16)
    return (jnp.dot(mh, xh, preferred_element_type=F32) + jnp.dot(mh, xl, preferred_element_type=F32)
            + jnp.dot(ml, xh, preferred_element_type=F32))


def _dotp(mh, ml, x, passes):
    if passes == 1:
        return jnp.dot(mh, x.astype(BF16), preferred_element_type=F32)
    return _dot3(mh, ml, x)


def _fft_tables(seq):
    n = 2 * seq
    n1 = n // FFT_IN
    half = n1 // 2
    eye = np.eye(SUBLANES)
    a = 2.0 * np.pi * np.outer(np.arange(n1), np.arange(half)) / n1
    gr, gi = np.cos(a), -np.sin(a)
    blk = np.stack([np.stack([gr, -gi], axis=1), np.stack([gi, gr], axis=1)], axis=0)
    m_out = np.einsum('rkis,cd->rkcisd', blk, eye).reshape(2 * n1 * SUBLANES, 2 * half * SUBLANES)
    ir, ii = gr.T / n, -gi.T / n
    blk = np.stack([np.stack([ir, -ii], axis=1), np.stack([ii, ir], axis=1)], axis=0)
    m_inv = np.einsum('otrk,cd->otcrkd', blk, eye).reshape(2 * half * SUBLANES, 2 * n1 * SUBLANES)
    a = 2.0 * np.pi * np.outer(np.arange(FFT_IN), np.arange(FFT_IN)) / FFT_IN
    fr, fi = np.cos(a), -np.sin(a)
    w_fwd = np.block([[fr, -fi], [fi, fr]])
    w_inv = np.block([[fr, fi], [-fi, fr]])
    s_in = SUBLANES * np.arange(FFT_NT)[:, None, None] + np.arange(SUBLANES)[None, None, :]
    th = 2.0 * np.pi * s_in * np.arange(n1)[None, :, None] / n
    tw = (jnp.asarray(np.cos(th)[..., None], F32), jnp.asarray(-np.sin(th)[..., None], F32))
    return dict(m_out=_split_np(m_out), m_inv=_split_np(m_inv), w_fwd=_split_np(w_fwd), w_inv=_split_np(w_inv),
                tw=tw, n1=n1, half=half)


FFT_PP = 2


def _tw_spec(n1):
    return pl.BlockSpec((None, n1, SUBLANES, 1), lambda t, q: (t, 0, 0, 0))


def _fft_outer_kernel(mh_ref, ml_ref, twr_ref, twi_ref, z_ref, v_ref, *, complex_in, passes):
    n1 = v_ref.shape[1]
    tr, ti = twr_ref[...], twi_ref[...]
    for pp in range(FFT_PP):
        zz = z_ref[:, pp] if complex_in else z_ref[pp]
        rows_in = math.prod(zz.shape[:-1])
        v = _dotp(mh_ref[:, :rows_in], ml_ref[:, :rows_in], zz.reshape(rows_in, D_BRANCH), passes)
        vr = v[:n1 * SUBLANES].reshape(n1, SUBLANES, D_BRANCH)
        vi = v[n1 * SUBLANES:].reshape(n1, SUBLANES, D_BRANCH)
        v_ref[pp, :, 0] = vr * tr - vi * ti
        v_ref[pp, :, 1] = vr * ti + vi * tr


def _fft_outer(tb, z, complex_in, passes):
    n1, half = tb["n1"], tb["half"]
    if complex_in:
        p = z.shape[1]
        zspec = pl.BlockSpec((2, FFT_PP, half, None, SUBLANES, D_BRANCH), lambda t, q: (0, q, 0, t, 0, 0))
    else:
        p = z.shape[0]
        zspec = pl.BlockSpec((FFT_PP, half, None, SUBLANES, D_BRANCH), lambda t, q: (q, 0, t, 0, 0))
    mshape = (2 * n1 * SUBLANES, 2 * half * SUBLANES)
    return pl.pallas_call(
        functools.partial(_fft_outer_kernel, complex_in=complex_in, passes=passes),
        grid=(FFT_NT, p // FFT_PP),
        in_specs=[_const_spec(mshape), _const_spec(mshape), _tw_spec(n1), _tw_spec(n1), zspec],
        out_specs=pl.BlockSpec((FFT_PP, n1, 2, None, SUBLANES, D_BRANCH), lambda t, q: (q, 0, 0, t, 0, 0)),
        out_shape=jax.ShapeDtypeStruct((p, n1, 2, FFT_NT, SUBLANES, D_BRANCH), F32),
        compiler_params=_cparams("arbitrary", "arbitrary"),
        name="fft_outer_c" if complex_in else "fft_outer_r",
    )(*tb["m_out"], *tb["tw"], z)


FFT_KB = 8


def _fft_filt_kernel(wh_ref, wl_ref, vf_ref, vb_ref, h_ref):
    for kb in range(FFT_KB):
        zf = _dot3(wh_ref[...], wl_ref[...], vf_ref[kb].reshape(2 * FFT_IN, D_BRANCH))
        zb = _dot3(wh_ref[...], wl_ref[...], vb_ref[kb].reshape(2 * FFT_IN, D_BRANCH))
        h_ref[kb, 0] = zf[:FFT_IN] + zb[:FFT_IN]
        h_ref[kb, 1] = zf[FFT_IN:] - zb[FFT_IN:]


def _fft_filt(tb, v):
    n1 = v.shape[1]
    wspec = _const_spec((2 * FFT_IN, 2 * FFT_IN))
    vblk = (None, FFT_KB, 2, FFT_NT, SUBLANES, D_BRANCH)
    return pl.pallas_call(
        _fft_filt_kernel,
        grid=(C_ORDER, n1 // FFT_KB),
        in_specs=[wspec, wspec,
                  pl.BlockSpec(vblk, lambda o, k: (2 * o, k, 0, 0, 0, 0)),
                  pl.BlockSpec(vblk, lambda o, k: (2 * o + 1, k, 0, 0, 0, 0))],
        out_specs=pl.BlockSpec((None, FFT_KB, 2, FFT_IN, D_BRANCH), lambda o, k: (o, k, 0, 0, 0)),
        out_shape=jax.ShapeDtypeStruct((C_ORDER, n1, 2, FFT_IN, D_BRANCH), F32),
        compiler_params=_cparams("parallel", "parallel"),
        name="fft_filt",
    )(*tb["w_fwd"], v, v)


def _fft_mid_kernel(wfh_ref, wfl_ref, wih_ref, wil_ref, v_ref, h_ref, d_ref):
    for kb in range(FFT_KB):
        z = _dotp(wfh_ref[...], wfl_ref[...], v_ref[kb].reshape(2 * FFT_IN, D_BRANCH), FFT_ACT_PASSES)
        zr, zi = z[:FFT_IN], z[FFT_IN:]
        hr, hi = h_ref[kb, 0], h_ref[kb, 1]
        pr = zr * hr - zi * hi
        pi = zr * hi + zi * hr
        d = _dotp(wih_ref[...], wil_ref[...], jnp.concatenate([pr, pi], axis=0), FFT_ACT_PASSES)
        d_ref[kb] = d.reshape(d_ref.shape[1:])


def _fft_mid(tb, v, hspec, order):
    p, n1 = v.shape[:2]
    vspec = pl.BlockSpec((None, FFT_KB, 2, FFT_NT, SUBLANES, D_BRANCH), lambda k, q: (q, k, 0, 0, 0, 0))
    wspec = _const_spec((2 * FFT_IN, 2 * FFT_IN))
    return pl.pallas_call(
        _fft_mid_kernel,
        grid=(n1 // FFT_KB, p),
        in_specs=[wspec, wspec, wspec, wspec, vspec,
                  pl.BlockSpec((None, FFT_KB, 2, FFT_IN, D_BRANCH), lambda k, q: (order, k, 0, 0, 0))],
        out_specs=vspec,
        out_shape=jax.ShapeDtypeStruct(v.shape, F32),
        compiler_params=_cparams("parallel", "arbitrary"),
        name="fft_mid",
    )(*tb["w_fwd"], *tb["w_inv"], v, hspec)


def _ifft_outer_kernel(mh_ref, ml_ref, twr_ref, twi_ref, d_ref, u_ref, x_ref, b_ref, o_ref):
    n1 = d_ref.shape[1]
    tr, ti = twr_ref[...], twi_ref[...]
    for pp in range(FFT_PP):
        dr, di = d_ref[pp, :, 0], d_ref[pp, :, 1]
        er = (dr * tr + di * ti).reshape(n1 * SUBLANES, D_BRANCH)
        ei = (di * tr - dr * ti).reshape(n1 * SUBLANES, D_BRANCH)
        y = _dotp(mh_ref[...], ml_ref[...], jnp.concatenate([er, ei], axis=0), FFT_ACT_PASSES)
        y = y.reshape((2,) + o_ref.shape[2:])
        o_ref[:, pp] = x_ref[:, pp] * (y + u_ref[:, pp] * b_ref[...])


def _ifft_outer(tb, d, u, ucol, x, xcol, bias):
    n1, half = tb["n1"], tb["half"]
    p = d.shape[0]
    mshape = (2 * half * SUBLANES, 2 * n1 * SUBLANES)
    io = lambda col: pl.BlockSpec((2, FFT_PP, half, None, SUBLANES, D_BRANCH), lambda t, q: (0, q, 0, t, 0, col))
    return pl.pallas_call(
        _ifft_outer_kernel,
        grid=(FFT_NT, p // FFT_PP),
        in_specs=[_const_spec(mshape), _const_spec(mshape), _tw_spec(n1), _tw_spec(n1),
                  pl.BlockSpec((FFT_PP, n1, 2, None, SUBLANES, D_BRANCH), lambda t, q: (q, 0, 0, t, 0, 0)),
                  io(ucol), io(xcol), _const_spec((1, D_BRANCH))],
        out_specs=io(0),
        out_shape=jax.ShapeDtypeStruct((2, p, half, FFT_NT, SUBLANES, D_BRANCH), F32),
        compiler_params=_cparams("arbitrary", "arbitrary"),
        name="ifft_outer",
    )(*tb["m_inv"], *tb["tw"], d, u, x, bias.reshape(1, D_BRANCH))


def _hyconv_kernel(x_ref, w_ref, b_ref, o_ref, xp_ref):
    seq = x_ref.shape[0]
    xp_ref[0:HY_PAD, :] = jnp.zeros((HY_PAD, D_BRANCH), F32)
    xp_ref[HY_PAD + seq:2 * HY_PAD + seq, :] = jnp.zeros((HY_PAD, D_BRANCH), F32)
    xp_ref[HY_PAD:HY_PAD + seq, :] = x_ref[...]

    def body(i, carry):
        r0 = pl.multiple_of(i * HY_TB, HY_TB)
        win = xp_ref[pl.ds(r0, HY_TB + 2 * HY_PAD), :]
        y = b_ref[...]
        for j in range(C_CONV):
            s0 = HY_PAD + j - C_CONV // 2
            y = y + w_ref[j:j + 1, :] * win[s0:s0 + HY_TB, :]
        o_ref[pl.ds(r0, HY_TB), :] = y
        return carry

    lax.fori_loop(0, seq // HY_TB, body, 0)


def _hyconv(pc3, cw, cb):
    bsz, seq, width = pc3.shape
    nb = width // D_BRANCH
    blk = pl.BlockSpec((None, seq, D_BRANCH), lambda b, j: (b, 0, j))
    return pl.pallas_call(
        _hyconv_kernel,
        grid=(bsz, nb),
        in_specs=[blk, pl.BlockSpec((C_CONV, D_BRANCH), lambda b, j: (0, j)),
                  pl.BlockSpec((1, D_BRANCH), lambda b, j: (0, j))],
        out_specs=blk,
        out_shape=jax.ShapeDtypeStruct(pc3.shape, F32),
        scratch_shapes=[pltpu.VMEM((seq + 2 * HY_PAD, D_BRANCH), F32)],
        compiler_params=_cparams("parallel", "parallel"),
        name="hyconv",
    )(pc3, cw, cb.reshape(1, width))


def _hyfilt_kernel(z_ref, w1_ref, b1_ref, fr_ref, w2_ref, b2_ref, w3_ref, dec_ref, o_ref, h_ref):
    seq = z_ref.shape[0]
    nblk = seq // HY_TB
    fr = fr_ref[...]

    @pl.when(pl.program_id(0) == 0)
    def _():
        def hidden(i, carry):
            r0 = pl.multiple_of(i * HY_TB, HY_TB)
            zb = z_ref[pl.ds(r0, HY_TB), :]
            h = jnp.sin(fr * (jnp.dot(zb, w1_ref[...], precision=HI, preferred_element_type=F32) + b1_ref[...]))
            h = jnp.sin(fr * (jnp.dot(h, w2_ref[...], precision=HI, preferred_element_type=F32) + b2_ref[...]))
            h_ref[pl.ds(r0, HY_TB), :] = h
            return carry

        lax.fori_loop(0, nblk, hidden, 0)

    def body(i, ss):
        r0 = pl.multiple_of(i * HY_TB, HY_TB)
        hf = jnp.dot(h_ref[pl.ds(r0, HY_TB), :], w3_ref[...], precision=HI, preferred_element_type=F32)
        hf = hf * jnp.exp(-z_ref[pl.ds(r0, HY_TB), 0:1] * dec_ref[...])
        o_ref[pl.ds(r0, HY_TB), :] = hf
        return ss + jnp.sum(hf * hf, axis=0, keepdims=True)

    ss = lax.fori_loop(0, nblk, body, jnp.zeros((1, D_BRANCH), F32))
    scale = lax.rsqrt(ss + EPS)

    def norm(i, carry):
        r0 = pl.multiple_of(i * HY_TB, HY_TB)
        o_ref[pl.ds(r0, HY_TB), :] = o_ref[pl.ds(r0, HY_TB), :] * scale
        return carry

    lax.fori_loop(0, nblk, norm, 0)


def _hyfilt(seq, w1, b1, freq, w2, b2, w3):
    t = jnp.linspace(0.0, 1.0, seq, dtype=F32)[:, None]
    bands = (C_EMB - 1) // 2
    w = 2.0 * math.pi * jnp.arange(seq, dtype=F32)[:, None] / seq
    fr = jnp.linspace(1e-4, bands - 1, bands, dtype=F32)[None]
    z = jnp.concatenate([t, jnp.cos(fr * w), -jnp.sin(fr * w)], axis=-1)
    z = jnp.pad(z, ((0, 0), (0, LANES - C_EMB)))
    padm = lambda a, r, c: jnp.pad(a.astype(F32), ((0, r - a.shape[0]), (0, c - a.shape[1])))
    row = lambda a: padm(a.reshape(1, -1), 1, LANES)
    dec = jnp.abs(jnp.linspace(C_MIN_DECAY, C_MAX_DECAY, D_BRANCH, dtype=F32)).reshape(1, D_BRANCH)
    nset = C_ORDER * 2
    return pl.pallas_call(
        _hyfilt_kernel,
        grid=(nset,),
        in_specs=[_const_spec((seq, LANES)), _const_spec((LANES, LANES)), _const_spec((1, LANES)),
                  _const_spec((1, LANES)), _const_spec((LANES, LANES)), _const_spec((1, LANES)),
                  pl.BlockSpec((LANES, D_BRANCH), lambda j: (0, j)), _const_spec((1, D_BRANCH))],
        out_specs=pl.BlockSpec((None, seq, D_BRANCH), lambda j: (j, 0, 0)),
        out_shape=jax.ShapeDtypeStruct((nset, seq, D_BRANCH), F32),
        scratch_shapes=[pltpu.VMEM((seq, LANES), F32)],
        compiler_params=_cparams("arbitrary"),
        name="hyfilt",
    )(z, padm(w1, LANES, LANES), row(b1), row(freq), padm(w2, LANES, LANES), row(b2),
      padm(w3, LANES, nset * D_BRANCH), dec)


def _hyena(pc3, cw, cb, w1, b1, freq, w2, b2, w3, bias):
    bsz, seq, width = pc3.shape
    tb = _fft_tables(seq)
    half = tb["half"]
    npair = bsz // 2
    uc = _hyconv(pc3, cw, cb)
    uc6 = uc.reshape(2, npair, half, FFT_NT, SUBLANES, width)
    filt = _hyfilt(seq, w1, b1, freq, w2, b2, w3).reshape(C_ORDER * 2, half, FFT_NT, SUBLANES, D_BRANCH)
    hspec = _fft_filt(tb, _fft_outer(tb, filt, False, 3))
    src = uc6
    for order in range(C_ORDER):
        d = _fft_mid(tb, _fft_outer(tb, src, True, FFT_ACT_PASSES), hspec, order)
        src = _ifft_outer(tb, d, src, 0, uc6, order + 1, bias[order])
    return src.reshape(bsz, seq, D_BRANCH)


AT_HALF = 64
AT_TQ = 512
AT_SUB = 128


def _t5_bucket(rel):
    half = N_BUCKETS // 2
    max_exact = half // 2
    n = np.abs(rel)
    large = max_exact + (np.log(np.maximum(n, 1) / max_exact) / math.log(MAX_DISTANCE / max_exact)
                         * (half - max_exact)).astype(np.int64)
    large = np.minimum(large, half - 1)
    return (rel > 0).astype(np.int64) * half + np.where(n < max_exact, n, large)


def _attn_geometry(n):
    tq = min(AT_TQ, n)
    sub = min(AT_SUB, tq)
    if sub + 2 * AT_HALF >= n:
        sub = tq
    win = min(sub + 2 * AT_HALF, n)
    return tq, sub, win, n // tq, n // sub


def _attn_bias_tables(rel_bias, g, dil, n):
    _, sub, win, _, nsb = _attn_geometry(n)
    hs = slice(g * D_HEADS_PER_GROUP, (g + 1) * D_HEADS_PER_GROUP)
    offsets = np.arange(-AT_HALF, AT_HALF + 1) * dil
    onehot = np.zeros((2 * AT_HALF + 1, N_BUCKETS), np.float32)
    onehot[np.arange(2 * AT_HALF + 1), _t5_bucket(offsets)] = 1.0
    band = jnp.dot(rel_bias.astype(F32)[:, hs].T, jnp.asarray(onehot).T, precision=HI)
    nband = 2 * AT_HALF + 1
    lv = sub + win - 1
    tables = []
    for i in sorted({0, min(1, nsb - 1), nsb - 1}):
        ws = int(np.clip(i * sub - AT_HALF, 0, n - win))
        lo = (sub - 1) - (ws - i * sub) - AT_HALF
        v = jnp.pad(band, ((0, 0), (lo, lv - lo - nband)), constant_values=NEG_BIG)
        flat = jnp.tile(v, (1, sub + 1))[:, sub - 1:sub - 1 + sub * (lv - 1)]
        tables.append(flat.reshape(D_HEADS_PER_GROUP, sub, lv - 1)[:, :, :win])
    return jnp.stack(tables)


def _attn_kernel(q_ref, k_ref, v_ref, bias_ref, o_ref, l_ref, *, n, dil):
    tq, sub, win, _, nsb = _attn_geometry(n)
    ncase = bias_ref.shape[0]
    width = q_ref.shape[-1]
    nh = width // D_HEAD_DIM
    hp = pl.program_id(1)
    lane_head = lax.broadcasted_iota(jnp.int32, (sub, width), 1) // D_HEAD_DIM
    hmask = [lane_head == hh for hh in range(nh)]
    whole = win == n
    for r in range(dil):
        if whole:
            kw = k_ref[pl.ds(r, win, stride=dil), :].astype(BF16)
            vw = v_ref[pl.ds(r, win, stride=dil), :].astype(BF16)
        for j in range(tq // sub):
            sidx = pl.program_id(2) * (tq // sub) + j
            case = jnp.minimum(jnp.where(sidx == nsb - 1, ncase - 1, jnp.minimum(sidx, 1)), ncase - 1)
            bias = bias_ref[case, pl.ds(hp * nh, nh)].reshape(nh * sub, win)
            if not whole:
                ws = pl.multiple_of(jnp.clip(sidx * sub - AT_HALF, 0, n - win), AT_HALF)
                kw = k_ref[pl.ds(ws * dil + r, win, stride=dil), :].astype(BF16)
                vw = v_ref[pl.ds(ws * dil + r, win, stride=dil), :].astype(BF16)
            q = q_ref[pl.ds(j * sub * dil + r, sub, stride=dil), :] * (D_HEAD_DIM ** -0.5)
            qs = jnp.concatenate([jnp.where(hmask[hh], q, 0.0) for hh in range(nh)], axis=0).astype(BF16)
            s = lax.dot_general(qs, kw, (((1,), (1,)), ((), ())), preferred_element_type=F32) + bias
            m = jnp.max(s, axis=-1, keepdims=True)
            p = jnp.exp(s - m)
            l = jnp.sum(p, axis=-1, keepdims=True)
            o_all = jnp.dot(p.astype(BF16), vw, preferred_element_type=F32) / l
            lse = m + jnp.log(l)
            o_acc = jnp.zeros((sub, width), F32)
            l_acc = jnp.zeros((sub, width), F32)
            for hh in range(nh):
                o_acc = jnp.where(hmask[hh], o_all[hh * sub:(hh + 1) * sub], o_acc)
                l_acc = jnp.where(hmask[hh], lse[hh * sub:(hh + 1) * sub], l_acc)
            o_ref[pl.ds(j * sub * dil + r, sub, stride=dil), :] = o_acc
            l_ref[pl.ds(j * sub * dil + r, sub, stride=dil), :] = l_acc


def _banded_attention(pd3, g, dil, bias):
    bsz, seq, width = pd3.shape
    n = seq // dil
    tq, sub, win, nq, _ = _attn_geometry(n)
    rows = dil * tq
    bw = LANES if dil > 1 else D_BRANCH
    per = D_BRANCH // bw
    third = width // 3 // bw
    out_spec = pl.BlockSpec((None, rows, bw), lambda b, h, i: (b, i, h))
    shp = jax.ShapeDtypeStruct((bsz, seq, D_BRANCH), F32)
    return pl.pallas_call(
        functools.partial(_attn_kernel, n=n, dil=dil),
        grid=(bsz, per, nq),
        in_specs=[pl.BlockSpec((None, rows, bw), lambda b, h, i: (b, i, g * per + h)),
                  pl.BlockSpec((None, seq, bw), lambda b, h, i: (b, 0, third + g * per + h)),
                  pl.BlockSpec((None, seq, bw), lambda b, h, i: (b, 0, 2 * third + g * per + h)),
                  _const_spec(bias.shape)],
        out_specs=[out_spec, out_spec],
        out_shape=[shp, shp],
        compiler_params=_cparams("parallel", "parallel", "arbitrary"),
        name=f"attn_d{dil}",
    )(pd3, pd3, pd3, bias)


def _dilated_attention(pd3, rel_bias):
    bsz, seq, _ = pd3.shape
    outs, lses = [], []
    for g, (_, dil) in enumerate(D_GROUPS):
        o, l = _banded_attention(pd3, g, dil, _attn_bias_tables(rel_bias, g, dil, seq // dil))
        outs.append(o.reshape(bsz * seq, D_BRANCH))
        lses.append(l.reshape(bsz * seq, D_BRANCH))
    return outs, lses


def kernel(x, norm1_g, w_in, hgrn_lb_logits, hgrn_norm_g, lru_conv_w, lru_conv_b, lru_wa, lru_ba, lru_wx, lru_bx,
           lru_lambda, hy_conv_w, hy_conv_b, hy_w1, hy_b1, hy_freq, hy_w2, hy_b2, hy_w3, hy_bias, rel_bias,
           w_branch, w_gate, b_gate, w_out, norm2_g, w_ff1, w_ff3, w_ff2, final_g):
    bsz, seq, _ = x.shape
    n = bsz * seq
    lb_soft = jax.nn.softmax(hgrn_lb_logits.astype(F32), axis=0)
    lower_bounds = jnp.cumsum(lb_soft, axis=0) - lb_soft[0]
    x2 = x.reshape(n, D_MODEL)
    flat = lambda a: a.reshape(n, D_BRANCH)
    for l in range(DEPTH):
        pa, pb, pc, pd = _inproj(x2, norm1_g[l], w_in[l].astype(BF16))
        oa_f, oa_b = _hgrn(pa.reshape(bsz, seq, IN_A), lower_bounds[l])
        yb = _rglru(pb.reshape(bsz, seq, IN_B), lru_conv_w[l], lru_conv_b[l], lru_wa[l], lru_ba[l],
                    lru_wx[l], lru_bx[l], lru_lambda[l])
        yc = _hyena(pc.reshape(bsz, seq, IN_C), hy_conv_w[l], hy_conv_b[l], hy_w1[l], hy_b1[l], hy_freq[l],
                    hy_w2[l], hy_b2[l], hy_w3[l], hy_bias[l])
        od, ld = _dilated_attention(pd.reshape(bsz, seq, IN_D), rel_bias)
        x2 = _merge(x2, norm1_g[l], flat(oa_f), flat(oa_b), pa, hgrn_norm_g[l], flat(yb), flat(yc), od, ld,
                    w_gate[l].reshape(D_MODEL, N_BRANCH * D_MODEL).astype(BF16),
                    b_gate[l].reshape(1, N_BRANCH * D_MODEL), w_branch[l].astype(BF16), w_out[l].astype(BF16))
        x2 = _ffn(x2, norm2_g[l], *_ffn_weights(w_ff1[l], w_ff3[l], w_ff2[l]), final_g, l == DEPTH - 1)
    return x2.reshape(bsz, seq, D_MODEL)
```

```python
import functools
import math

import jax
import jax.numpy as jnp
import numpy as np
from jax import lax
from jax.experimental import pallas as pl
from jax.experimental.pallas import tpu as pltpu

F32 = jnp.float32
BF16 = jnp.bfloat16
HI = lax.Precision.HIGHEST

D_MODEL = 1024
DEPTH = 2
EPS = 1e-6
TINY = 1e-30
N_BRANCH = 4
D_BRANCH = 256
A_HEADS = 4
A_DK = 64
A_CHUNK = 64
B_BLOCKS = 4
B_BW = 64
B_CONV = 4
LRU_C = 8.0
C_ORDER = 2
C_CONV = 3
C_EMB = 33
C_HID = 64
C_MIN_DECAY = math.log(1e-2) / 1.5
C_MAX_DECAY = math.log(1e-2) / 0.3
D_GROUPS = ((128, 1), (512, 4), (2048, 16))
D_HEADS_PER_GROUP = 4
D_HEAD_DIM = 64
D_N_HEADS = 12
D_QKV = 768
N_BUCKETS = 32
MAX_DISTANCE = 1024
NEG_BIG = -1e30
D_FF = 2816
IN_A = 5 * D_BRANCH
IN_B = 2 * D_BRANCH
IN_C = 3 * D_BRANCH
IN_D = 3 * D_QKV
IN_WIDTH = IN_A + IN_B + IN_C + IN_D

LANES = 128
SUBLANES = 8
VMEM_LIMIT = 56 * 1024 * 1024


def _cparams(*sem):
    return pltpu.CompilerParams(dimension_semantics=sem, vmem_limit_bytes=VMEM_LIMIT)


def _const_spec(shape):
    nd = len(shape)
    return pl.BlockSpec(shape, lambda *_: (0,) * nd, pipeline_mode=pl.Buffered(1))


def _rms(x, g):
    return x * lax.rsqrt(jnp.mean(x * x, axis=-1, keepdims=True) + EPS) * g


def _sigmoid(x):
    return 1.0 / (1.0 + jnp.exp(-x))


IN_TM = 512
IN_CHUNK = 256


def _inproj_kernel(x_ref, g_ref, w_ref, oa_ref, ob_ref, oc_ref, od_ref):
    h = _rms(x_ref[...], g_ref[...]).astype(BF16)
    off = 0
    for o_ref in (oa_ref, ob_ref, oc_ref, od_ref):
        width = o_ref.shape[-1]
        for c in range(0, width, IN_CHUNK):
            o_ref[:, c:c + IN_CHUNK] = jnp.dot(h, w_ref[:, off + c:off + c + IN_CHUNK],
                                               preferred_element_type=F32)
        off += width


def _inproj(x2, g, w_bf16):
    n = x2.shape[0]
    widths = (IN_A, IN_B, IN_C, IN_D)
    return pl.pallas_call(
        _inproj_kernel,
        grid=(n // IN_TM,),
        in_specs=[pl.BlockSpec((IN_TM, D_MODEL), lambda i: (i, 0)),
                  _const_spec((1, D_MODEL)),
                  _const_spec((D_MODEL, IN_WIDTH))],
        out_specs=[pl.BlockSpec((IN_TM, w), lambda i: (i, 0)) for w in widths],
        out_shape=[jax.ShapeDtypeStruct((n, w), F32) for w in widths],
        compiler_params=_cparams("parallel"),
        name="inproj",
    )(x2, g.reshape(1, D_MODEL), w_bf16)


FF_TM = 1024
FF_CHUNK = 256
FF_NCHUNK = D_FF // FF_CHUNK


def _ffn_kernel(x_ref, g_ref, w1_ref, w3_ref, w2_ref, fg_ref, o_ref, acc_ref, *, final):
    x = x_ref[...]
    h = _rms(x, g_ref[...]).astype(BF16)
    acc_ref[...] = x

    def body(c, carry):
        c0 = pl.multiple_of(c * FF_CHUNK, FF_CHUNK)
        a = jnp.dot(h, w1_ref[:, pl.ds(c0, FF_CHUNK)], preferred_element_type=F32)
        b = jnp.dot(h, w3_ref[:, pl.ds(c0, FF_CHUNK)], preferred_element_type=F32)
        t = (a * _sigmoid(a) * b).astype(BF16)
        acc_ref[...] += jnp.dot(t, w2_ref[pl.ds(c0, FF_CHUNK), :], preferred_element_type=F32)
        return carry

    lax.fori_loop(0, FF_NCHUNK, body, 0)
    y = acc_ref[...]
    if final:
        y = _rms(y, fg_ref[...])
    o_ref[...] = y


def _ffn(x2, g, w1c, w3c, w2c, final_g, final):
    n = x2.shape[0]
    return pl.pallas_call(
        functools.partial(_ffn_kernel, final=final),
        grid=(n // FF_TM,),
        in_specs=[pl.BlockSpec((FF_TM, D_MODEL), lambda i: (i, 0)),
                  _const_spec((1, D_MODEL)),
                  _const_spec((D_MODEL, D_FF)),
                  _const_spec((D_MODEL, D_FF)),
                  _const_spec((D_FF, D_MODEL)),
                  _const_spec((1, D_MODEL))],
        out_specs=pl.BlockSpec((FF_TM, D_MODEL), lambda i: (i, 0)),
        out_shape=jax.ShapeDtypeStruct((n, D_MODEL), F32),
        scratch_shapes=[pltpu.VMEM((FF_TM, D_MODEL), F32)],
        compiler_params=_cparams("parallel"),
        name="ffn_final" if final else "ffn",
    )(x2, g.reshape(1, D_MODEL), w1c, w3c, w2c, final_g.reshape(1, D_MODEL))


def _ffn_weights(w1, w3, w2):
    return w1.astype(BF16), w3.astype(BF16), w2.astype(BF16)


MG_TM = 512


def _head_ones():
    r = np.arange(D_BRANCH)[:, None] // A_DK
    c = np.arange(D_BRANCH)[None, :] // A_DK
    return jnp.asarray((r == c).astype(np.float32) / A_DK, BF16)


def _merge_kernel(x_ref, g1_ref, oaf_ref, oab_ref, ga_ref, hg_ref, hm_ref, yb_ref, yc_ref,
                  o0_ref, o1_ref, o2_ref, l0_ref, l1_ref, l2_ref,
                  wg_ref, bg_ref, wb_ref, wo_ref, out_ref):
    x = x_ref[...]
    h = _rms(x, g1_ref[...]).astype(BF16)
    oa = oaf_ref[...] + oab_ref[...]
    ms = jnp.dot((oa * oa).astype(BF16), hm_ref[...], preferred_element_type=F32)
    ga = ga_ref[...]
    ya = oa * lax.rsqrt(ms + EPS) * hg_ref[...] * (ga * _sigmoid(ga))
    l0, l1, l2 = l0_ref[...], l1_ref[...], l2_ref[...]
    m = jnp.maximum(jnp.maximum(l0, l1), l2)
    e0, e1, e2 = jnp.exp(l0 - m), jnp.exp(l1 - m), jnp.exp(l2 - m)
    yd = (e0 * o0_ref[...] + e1 * o1_ref[...] + e2 * o2_ref[...]) / (e0 + e1 + e2)
    mixed = None
    for j, y in enumerate((ya, yb_ref[...], yc_ref[...], yd)):
        gate = _sigmoid(jnp.dot(h, wg_ref[:, j * D_MODEL:(j + 1) * D_MODEL], preferred_element_type=F32)
                        + bg_ref[:, j * D_MODEL:(j + 1) * D_MODEL])
        t = gate * jnp.dot(y.astype(BF16), wb_ref[j], preferred_element_type=F32)
        mixed = t if mixed is None else mixed + t
    out_ref[...] = x + jnp.dot(mixed.astype(BF16), wo_ref[...], preferred_element_type=F32)


def _merge(x2, g1, oa_f, oa_b, pa, hg, yb, yc, od, ld, wg, bg, wb, wo):
    n = x2.shape[0]
    tile = lambda w: pl.BlockSpec((MG_TM, w), lambda i: (i, 0))
    return pl.pallas_call(
        _merge_kernel,
        grid=(n // MG_TM,),
        in_specs=[tile(D_MODEL), _const_spec((1, D_MODEL)),
                  tile(D_BRANCH), tile(D_BRANCH),
                  pl.BlockSpec((MG_TM, D_BRANCH), lambda i: (i, 4)),
                  _const_spec((1, D_BRANCH)), _const_spec((D_BRANCH, D_BRANCH)),
                  tile(D_BRANCH), tile(D_BRANCH),
                  tile(D_BRANCH), tile(D_BRANCH), tile(D_BRANCH),
                  tile(D_BRANCH), tile(D_BRANCH), tile(D_BRANCH),
                  _const_spec((D_MODEL, N_BRANCH * D_MODEL)), _const_spec((1, N_BRANCH * D_MODEL)),
                  _const_spec((N_BRANCH, D_BRANCH, D_MODEL)), _const_spec((D_MODEL, D_MODEL))],
        out_specs=tile(D_MODEL),
        out_shape=jax.ShapeDtypeStruct((n, D_MODEL), F32),
        compiler_params=_cparams("parallel"),
        name="merge",
    )(x2, g1.reshape(1, D_MODEL), oa_f, oa_b, pa, hg.reshape(1, D_BRANCH), _head_ones(), yb, yc,
      od[0], od[1], od[2], ld[0], ld[1], ld[2], wg, bg, wb, wo)


HG_TS = 256
HG_NCH = HG_TS // A_CHUNK
HG_MID = A_CHUNK // 2


def _hgrn_tables():
    r = np.arange(HG_TS)[:, None]
    c = np.arange(HG_TS)[None, :]
    same = (r // A_CHUNK) == (c // A_CHUNK)
    rr = np.arange(HG_NCH * A_HEADS * A_CHUNK)[:, None]
    same_s = (rr // (A_HEADS * A_CHUNK)) == (c // A_CHUNK)
    t, sidx = rr % A_CHUNK, c % A_CHUNK
    out = []
    for fwd in (True, False):
        order = (r >= c) if fwd else (r <= c)
        order_s = (t >= sidx) if fwd else (t <= sidx)
        out.append((jnp.asarray((same & order).astype(np.float32), BF16),
                    jnp.asarray((same_s & order_s).astype(np.float32), F32)))
    return out


def _hgrn_prep(q, fl, v, lb, cum, smask, fwd):
    lane_head = lax.broadcasted_iota(jnp.int32, (A_CHUNK, D_BRANCH), 1) // A_DK
    hmask = [(lane_head == hh).astype(F32) for hh in range(A_HEADS)]
    blk_r = lax.broadcasted_iota(jnp.int32, (D_BRANCH, D_BRANCH), 0) // A_DK
    blk_c = lax.broadcasted_iota(jnp.int32, (D_BRANCH, D_BRANCH), 1) // A_DK
    blockdiag = (blk_r == blk_c).astype(F32)
    col_chunk = lax.broadcasted_iota(jnp.int32, (D_BRANCH, HG_TS), 1) // A_CHUNK

    sg = _sigmoid(fl)
    f = lb + (1.0 - lb) * sg
    g = jnp.log(jnp.maximum(f, TINY))
    kk = (1.0 - lb) * (1.0 - sg)
    g1 = g.astype(BF16)
    r1 = g - g1.astype(F32)
    g2 = r1.astype(BF16)
    g3 = (r1 - g2.astype(F32)).astype(BF16)
    b = (jnp.dot(cum, g1, preferred_element_type=F32) + jnp.dot(cum, g2, preferred_element_type=F32)
         + jnp.dot(cum, g3, preferred_element_type=F32))
    rows = lambda c: slice(c * A_CHUNK, (c + 1) * A_CHUNK)
    last = (A_CHUNK - 1) if fwd else 0
    bcast = lambda r0: jnp.concatenate(
        [jnp.broadcast_to(b[c * A_CHUNK + r0:c * A_CHUNK + r0 + 1, :], (A_CHUNK, D_BRANCH)) for c in range(HG_NCH)], 0)
    bm = bcast(HG_MID)
    bl = bcast(last)
    qt = q * jnp.exp(b - bm)
    kt = (kk * jnp.exp(bm - b)).astype(BF16)
    qe = (q * jnp.exp(b)).astype(BF16)
    kh = (kk * jnp.exp(bl - b)).astype(BF16)
    qs = jnp.concatenate([qt[rows(c)] * hmask[hh] for c in range(HG_NCH) for hh in range(A_HEADS)],
                         axis=0).astype(BF16)
    s = lax.dot_general(qs, kt, (((1,), (1,)), ((), ())), preferred_element_type=F32)
    s = (s * smask).astype(BF16)
    ost = jnp.dot(s, v.astype(BF16), preferred_element_type=F32)
    o_intra = []
    for c in range(HG_NCH):
        base = c * A_HEADS * A_CHUNK
        o = ost[base:base + A_CHUNK] * hmask[0]
        for hh in range(1, A_HEADS):
            o = o + ost[base + hh * A_CHUNK:base + (hh + 1) * A_CHUNK] * hmask[hh]
        o_intra.append(o)
    vt = v.T
    lhs = jnp.concatenate([jnp.where(col_chunk == c, vt, 0.0) for c in range(HG_NCH)], axis=0).astype(BF16)
    updall = jnp.dot(lhs, kh, preferred_element_type=F32)
    upd = [updall[c * D_BRANCH:(c + 1) * D_BRANCH] * blockdiag for c in range(HG_NCH)]
    decay = [jnp.exp(b[c * A_CHUNK + last:c * A_CHUNK + last + 1, :]) for c in range(HG_NCH)]
    return o_intra, upd, qe, decay


def _hgrn_scan(o_intra, upd, qe, decay, st, fwd):
    rows = lambda c: slice(c * A_CHUNK, (c + 1) * A_CHUNK)
    outs = [None] * HG_NCH
    for c in (range(HG_NCH) if fwd else range(HG_NCH - 1, -1, -1)):
        outs[c] = o_intra[c] + lax.dot_general(qe[rows(c)], st.astype(BF16), (((1,), (1,)), ((), ())),
                                               preferred_element_type=F32)
        st = st * decay[c] + upd[c]
    return jnp.concatenate(outs, axis=0), st


def _hgrn_kernel(qf_ref, ff_ref, vf_ref, qb_ref, fb_ref, vb_ref, lb_ref, cf_ref, mf_ref, cb_ref, mb_ref,
                 of_ref, ob_ref, sf_ref, sb_ref):
    @pl.when(pl.program_id(1) == 0)
    def _():
        sf_ref[...] = jnp.zeros_like(sf_ref)
        sb_ref[...] = jnp.zeros_like(sb_ref)

    lb = lb_ref[...]
    pf = _hgrn_prep(qf_ref[...], ff_ref[...], vf_ref[...], lb, cf_ref[...], mf_ref[...], True)
    pb = _hgrn_prep(qb_ref[...], fb_ref[...], vb_ref[...], lb, cb_ref[...], mb_ref[...], False)
    o_f, st_f = _hgrn_scan(*pf, sf_ref[...], True)
    o_b, st_b = _hgrn_scan(*pb, sb_ref[...], False)
    of_ref[...] = o_f
    ob_ref[...] = o_b
    sf_ref[...] = st_f
    sb_ref[...] = st_b


def _hgrn(pa3, lb):
    bsz, seq, _ = pa3.shape
    nblk = seq // HG_TS
    blk = (None, HG_TS, D_BRANCH)
    up = lambda col: pl.BlockSpec(blk, lambda b, i: (b, i, col))
    down = lambda col: pl.BlockSpec(blk, lambda b, i: (b, nblk - 1 - i, col))
    shp = jax.ShapeDtypeStruct((bsz, seq, D_BRANCH), F32)
    (cum_f, sm_f), (cum_b, sm_b) = _hgrn_tables()
    mshape = (HG_NCH * A_HEADS * A_CHUNK, HG_TS)
    return pl.pallas_call(
        _hgrn_kernel,
        grid=(bsz, nblk),
        in_specs=[up(0), up(1), up(3), down(0), down(2), down(3), _const_spec((1, D_BRANCH)),
                  _const_spec((HG_TS, HG_TS)), _const_spec(mshape), _const_spec((HG_TS, HG_TS)), _const_spec(mshape)],
        out_specs=[up(0), down(0)],
        out_shape=[shp, shp],
        scratch_shapes=[pltpu.VMEM((D_BRANCH, D_BRANCH), F32), pltpu.VMEM((D_BRANCH, D_BRANCH), F32)],
        compiler_params=_cparams("parallel", "arbitrary"),
        name="hgrn2",
    )(pa3, pa3, pa3, pa3, pa3, pa3, lb.reshape(1, D_BRANCH), cum_f, sm_f, cum_b, sm_b)


RG_TB = 128
RG_PAD = SUBLANES
RG_LEFT = B_CONV // 2


def _dot3_rhs(x, wh, wl):
    xh = x.astype(BF16)
    xl = (x - xh.astype(F32)).astype(BF16)
    return (jnp.dot(xh, wh, preferred_element_type=F32) + jnp.dot(xl, wh, preferred_element_type=F32)
            + jnp.dot(xh, wl, preferred_element_type=F32))


def _group_scan(a, u, fwd):
    row = lax.broadcasted_iota(jnp.int32, a.shape, 1)
    k = 1
    while k < SUBLANES:
        if fwd:
            keep = row >= k
            us, as_ = pltpu.roll(u, k, 1), pltpu.roll(a, k, 1)
        else:
            keep = row < SUBLANES - k
            us, as_ = pltpu.roll(u, SUBLANES - k, 1), pltpu.roll(a, SUBLANES - k, 1)
        u = a * jnp.where(keep, us, 0.0) + u
        a = a * jnp.where(keep, as_, 1.0)
        k *= 2
    return a, u


def _block_scan(a, u, carry, fwd):
    t = a.shape[0]
    ngrp = t // SUBLANES
    ag, ug = _group_scan(a.reshape(ngrp, SUBLANES, D_BRANCH), u.reshape(ngrp, SUBLANES, D_BRANCH), fwd)
    hs = [None] * ngrp
    for g in (range(ngrp) if fwd else range(ngrp - 1, -1, -1)):
        h = ug[g] + ag[g] * carry
        hs[g] = h
        carry = h[SUBLANES - 1:SUBLANES, :] if fwd else h[0:1, :]
    return jnp.concatenate(hs, axis=0), carry


def _gelu_tanh(x):
    return 0.5 * x * (1.0 + jnp.tanh(math.sqrt(2.0 / math.pi) * (x + 0.044715 * (x * x * x))))


def _rglru_kernel(x_ref, gt_ref, cw_ref, cb_ref, wh_ref, wl_ref, bg_ref, lam_ref, o_ref, xp_ref, xc_ref):
    seq = x_ref.shape[0]
    nblk = seq // RG_TB
    xp_ref[0:RG_PAD, :] = jnp.zeros((RG_PAD, D_BRANCH), F32)
    xp_ref[RG_PAD + seq:2 * RG_PAD + seq, :] = jnp.zeros((RG_PAD, D_BRANCH), F32)
    xp_ref[RG_PAD:RG_PAD + seq, :] = x_ref[...]
    nl = -lam_ref[...]
    sp = jnp.maximum(nl, 0.0) + jnp.log(1.0 + jnp.exp(-jnp.abs(nl)))

    def block(i, carry, dirn):
        r0 = pl.multiple_of(i * RG_TB, RG_TB)
        if dirn == 0:
            win = xp_ref[pl.ds(r0, RG_TB + 2 * RG_PAD), :]
            xc = cb_ref[...]
            for j in range(B_CONV):
                s0 = RG_PAD + j - RG_LEFT
                xc = xc + cw_ref[j:j + 1, :] * win[s0:s0 + RG_TB, :]
            xc_ref[pl.ds(r0, RG_TB), :] = xc
        else:
            xc = xc_ref[pl.ds(r0, RG_TB), :]
        cols = slice(dirn * 2 * D_BRANCH, (dirn + 1) * 2 * D_BRANCH)
        gates = _dot3_rhs(xc, wh_ref[:, cols], wl_ref[:, cols]) + bg_ref[:, cols]
        r = _sigmoid(gates[:, :D_BRANCH])
        ig = _sigmoid(gates[:, D_BRANCH:])
        log_a = -LRU_C * r * sp[dirn:dirn + 1, :]
        a = jnp.exp(log_a)
        u = jnp.sqrt(jnp.maximum(-jnp.tanh(log_a) * (a * a + 1.0), 0.0)) * ig * xc
        h, carry = _block_scan(a, u, carry, dirn == 0)
        if dirn == 0:
            o_ref[pl.ds(r0, RG_TB), :] = h
        else:
            o_ref[pl.ds(r0, RG_TB), :] = (o_ref[pl.ds(r0, RG_TB), :] + h) * _gelu_tanh(gt_ref[pl.ds(r0, RG_TB), :])
        return carry

    zero = jnp.zeros((1, D_BRANCH), F32)
    lax.fori_loop(0, nblk, lambda i, c: block(i, c, 0), zero)
    lax.fori_loop(0, nblk, lambda i, c: block(nblk - 1 - i, c, 1), zero)


def _blockdiag(w):
    eye = jnp.eye(B_BLOCKS, dtype=w.dtype)
    return jnp.einsum('ncd,nm->ncmd', w, eye).reshape(D_BRANCH, D_BRANCH)


def _rglru(pb3, cw, cb, wa, ba, wx, bx, lam):
    bsz, seq, _ = pb3.shape
    wg = jnp.concatenate([_blockdiag(wa[0]), _blockdiag(wx[0]), _blockdiag(wa[1]), _blockdiag(wx[1])], axis=1)
    bg = jnp.concatenate([ba[0], bx[0], ba[1], bx[1]]).reshape(1, 4 * D_BRANCH)
    wh = wg.astype(BF16)
    wl = (wg - wh.astype(F32)).astype(BF16)
    blk = (None, seq, D_BRANCH)
    return pl.pallas_call(
        _rglru_kernel,
        grid=(bsz,),
        in_specs=[pl.BlockSpec(blk, lambda b: (b, 0, 0)),
                  pl.BlockSpec(blk, lambda b: (b, 0, 1)),
                  _const_spec((B_CONV, D_BRANCH)), _const_spec((1, D_BRANCH)),
                  _const_spec((D_BRANCH, 4 * D_BRANCH)), _const_spec((D_BRANCH, 4 * D_BRANCH)),
                  _const_spec((1, 4 * D_BRANCH)), _const_spec((2, D_BRANCH))],
        out_specs=pl.BlockSpec(blk, lambda b: (b, 0, 0)),
        out_shape=jax.ShapeDtypeStruct((bsz, seq, D_BRANCH), F32),
        scratch_shapes=[pltpu.VMEM((seq + 2 * RG_PAD, D_BRANCH), F32), pltpu.VMEM((seq, D_BRANCH), F32)],
        compiler_params=_cparams("parallel"),
        name="rglru",
    )(pb3, pb3, cw, cb.reshape(1, D_BRANCH), wh, wl, bg, lam)


FFT_IN = 128
FFT_NT = FFT_IN // SUBLANES
HY_TB = 256
HY_PAD = SUBLANES


def _split_np(a):
    a = np.asarray(a, np.float32)
    hi = a.astype(jnp.bfloat16)
    lo = (a - hi.astype(np.float32)).astype(jnp.bfloat16)
    return jnp.asarray(hi), jnp.asarray(lo)


def _dot3(mh, ml, x):
    xh = x.astype(BF16)
    xl = (x - xh.astype(F32)).astype(BF16)
    return (jnp.dot(mh, xh, preferred_element_type=F32) + jnp.dot(mh, xl, preferred_element_type=F32)
            + jnp.dot(ml, xh, preferred_element_type=F32))


def _dotp(mh, ml, x, passes):
    if passes == 1:
        return jnp.dot(mh, x.astype(BF16), preferred_element_type=F32)
    return _dot3(mh, ml, x)


def _pack_pair(re, im):
    rb = lax.bitcast_convert_type(re.astype(BF16).astype(F32), jnp.uint32)
    ib = lax.bitcast_convert_type(im.astype(BF16).astype(F32), jnp.uint32)
    return rb | (ib >> 16)


def _unpack_pair(w):
    re = lax.bitcast_convert_type(w & jnp.uint32(0xFFFF0000), F32)
    im = lax.bitcast_convert_type(w << 16, F32)
    return re.astype(BF16), im.astype(BF16)


def _fft_tables(seq):
    n = 2 * seq
    n1 = n // FFT_IN
    half = n1 // 2
    eye = np.eye(SUBLANES)
    a = 2.0 * np.pi * np.outer(np.arange(n1), np.arange(half)) / n1
    gr, gi = np.cos(a), -np.sin(a)
    blk = np.stack([np.stack([gr, -gi], axis=1), np.stack([gi, gr], axis=1)], axis=0)
    m_out = np.einsum('rkis,cd->rkcisd', blk, eye).reshape(2 * n1 * SUBLANES, 2 * half * SUBLANES)
    ir, ii = gr.T / n, -gi.T / n
    blk = np.stack([np.stack([ir, -ii], axis=1), np.stack([ii, ir], axis=1)], axis=0)
    m_inv = np.einsum('otrk,cd->otcrkd', blk, eye).reshape(2 * half * SUBLANES, 2 * n1 * SUBLANES)
    a = 2.0 * np.pi * np.outer(np.arange(FFT_IN), np.arange(FFT_IN)) / FFT_IN
    fr, fi = np.cos(a), -np.sin(a)
    w_fwd = np.block([[fr, -fi], [fi, fr]])
    w_inv = np.block([[fr, fi], [-fi, fr]])
    s_in = SUBLANES * np.arange(FFT_NT)[:, None, None] + np.arange(SUBLANES)[None, None, :]
    th = 2.0 * np.pi * s_in * np.arange(n1)[None, :, None] / n
    tw = (jnp.asarray(np.cos(th)[..., None], F32), jnp.asarray(-np.sin(th)[..., None], F32))
    th = 2.0 * np.pi * np.outer(np.arange(n1), np.arange(FFT_IN)) / n
    tw_in = (jnp.asarray(np.cos(th)[..., None], F32), jnp.asarray(-np.sin(th)[..., None], F32))
    return dict(m_out=_split_np(m_out), m_inv=_split_np(m_inv), w_fwd=_split_np(w_fwd), w_inv=_split_np(w_inv),
                tw=tw, tw_in=tw_in, n1=n1, half=half)


FFT_PP = 2


def _tw_spec(n1):
    return pl.BlockSpec((None, n1, SUBLANES, 1), lambda t, q: (t, 0, 0, 0))


def _fft_outer_kernel(mh_ref, ml_ref, twr_ref, twi_ref, z_ref, v_ref, *, complex_in, passes, packed):
    n1 = v_ref.shape[1]
    tr, ti = twr_ref[...], twi_ref[...]
    for pp in range(FFT_PP):
        zz = z_ref[:, pp] if complex_in else z_ref[pp]
        rows_in = math.prod(zz.shape[:-1])
        v = _dotp(mh_ref[:, :rows_in], ml_ref[:, :rows_in], zz.reshape(rows_in, D_BRANCH), passes)
        vr = v[:n1 * SUBLANES].reshape(n1, SUBLANES, D_BRANCH)
        vi = v[n1 * SUBLANES:].reshape(n1, SUBLANES, D_BRANCH)
        wr, wi = vr * tr - vi * ti, vr * ti + vi * tr
        if packed:
            v_ref[pp] = _pack_pair(wr, wi)
        else:
            v_ref[pp, :, 0] = wr
            v_ref[pp, :, 1] = wi


def _fft_outer(tb, z, complex_in, passes, packed):
    n1, half = tb["n1"], tb["half"]
    if complex_in:
        p = z.shape[1]
        zspec = pl.BlockSpec((2, FFT_PP, half, None, SUBLANES, D_BRANCH), lambda t, q: (0, q, 0, t, 0, 0))
    else:
        p = z.shape[0]
        zspec = pl.BlockSpec((FFT_PP, half, None, SUBLANES, D_BRANCH), lambda t, q: (q, 0, t, 0, 0))
    mshape = (2 * n1 * SUBLANES, 2 * half * SUBLANES)
    if packed:
        out_spec = pl.BlockSpec((FFT_PP, n1, None, SUBLANES, D_BRANCH), lambda t, q: (q, 0, t, 0, 0))
        out_shape = jax.ShapeDtypeStruct((p, n1, FFT_NT, SUBLANES, D_BRANCH), jnp.uint32)
    else:
        out_spec = pl.BlockSpec((FFT_PP, n1, 2, None, SUBLANES, D_BRANCH), lambda t, q: (q, 0, 0, t, 0, 0))
        out_shape = jax.ShapeDtypeStruct((p, n1, 2, FFT_NT, SUBLANES, D_BRANCH), F32)
    return pl.pallas_call(
        functools.partial(_fft_outer_kernel, complex_in=complex_in, passes=passes, packed=packed),
        grid=(FFT_NT, p // FFT_PP),
        in_specs=[_const_spec(mshape), _const_spec(mshape), _tw_spec(n1), _tw_spec(n1), zspec],
        out_specs=out_spec,
        out_shape=out_shape,
        compiler_params=_cparams("arbitrary", "arbitrary"),
        name="fft_outer_c" if complex_in else "fft_outer_r",
    )(*tb["m_out"], *tb["tw"], z)


FFT_KB = 8


def _fft_filt_kernel(wh_ref, wl_ref, vf_ref, vb_ref, h_ref):
    for kb in range(FFT_KB):
        zf = _dot3(wh_ref[...], wl_ref[...], vf_ref[kb].reshape(2 * FFT_IN, D_BRANCH))
        zb = _dot3(wh_ref[...], wl_ref[...], vb_ref[kb].reshape(2 * FFT_IN, D_BRANCH))
        h_ref[kb, 0] = zf[:FFT_IN] + zb[:FFT_IN]
        h_ref[kb, 1] = zf[FFT_IN:] - zb[FFT_IN:]


def _fft_filt(tb, v):
    n1 = v.shape[1]
    wspec = _const_spec((2 * FFT_IN, 2 * FFT_IN))
    vblk = (None, FFT_KB, 2, FFT_NT, SUBLANES, D_BRANCH)
    return pl.pallas_call(
        _fft_filt_kernel,
        grid=(C_ORDER, n1 // FFT_KB),
        in_specs=[wspec, wspec,
                  pl.BlockSpec(vblk, lambda o, k: (2 * o, k, 0, 0, 0, 0)),
                  pl.BlockSpec(vblk, lambda o, k: (2 * o + 1, k, 0, 0, 0, 0))],
        out_specs=pl.BlockSpec((None, FFT_KB, 2, FFT_IN, D_BRANCH), lambda o, k: (o, k, 0, 0, 0)),
        out_shape=jax.ShapeDtypeStruct((C_ORDER, n1, 2, FFT_IN, D_BRANCH), F32),
        compiler_params=_cparams("parallel", "parallel"),
        name="fft_filt",
    )(*tb["w_fwd"], v, v)


def _fft_mid_kernel(wf_ref, wi_ref, twr_ref, twi_ref, v_ref, h_ref, d_ref):
    for kb in range(FFT_KB):
        vr, vi = _unpack_pair(v_ref[kb].reshape(FFT_IN, D_BRANCH))
        z = jnp.dot(wf_ref[...], jnp.concatenate([vr, vi], axis=0), preferred_element_type=F32)
        zr, zi = z[:FFT_IN], z[FFT_IN:]
        hr, hi = h_ref[kb, 0], h_ref[kb, 1]
        pr = (zr * hr - zi * hi).astype(BF16)
        pi = (zr * hi + zi * hr).astype(BF16)
        d = jnp.dot(wi_ref[...], jnp.concatenate([pr, pi], axis=0), preferred_element_type=F32)
        dr, di = d[:FFT_IN], d[FFT_IN:]
        tr, ti = twr_ref[kb], twi_ref[kb]
        er, ei = dr * tr + di * ti, di * tr - dr * ti
        d_ref[kb] = _pack_pair(er, ei).reshape(d_ref.shape[1:])


def _fft_mid(tb, v, hspec, order):
    p, n1 = v.shape[:2]
    vspec = pl.BlockSpec((None, FFT_KB, FFT_NT, SUBLANES, D_BRANCH), lambda k, q: (q, k, 0, 0, 0))
    wspec = _const_spec((2 * FFT_IN, 2 * FFT_IN))
    tspec = pl.BlockSpec((FFT_KB, FFT_IN, 1), lambda k, q: (k, 0, 0))
    return pl.pallas_call(
        _fft_mid_kernel,
        grid=(n1 // FFT_KB, p),
        in_specs=[wspec, wspec, tspec, tspec, vspec,
                  pl.BlockSpec((None, FFT_KB, 2, FFT_IN, D_BRANCH), lambda k, q: (order, k, 0, 0, 0))],
        out_specs=vspec,
        out_shape=jax.ShapeDtypeStruct(v.shape, jnp.uint32),
        compiler_params=_cparams("parallel", "arbitrary"),
        name="fft_mid",
    )(tb["w_fwd"][0], tb["w_inv"][0], *tb["tw_in"], v, hspec)


def _ifft_outer_kernel(mh_ref, d_ref, u_ref, x_ref, b_ref, o_ref):
    n1 = d_ref.shape[1]
    for pp in range(FFT_PP):
        er, ei = _unpack_pair(d_ref[pp])
        e = jnp.concatenate([er.reshape(n1 * SUBLANES, D_BRANCH), ei.reshape(n1 * SUBLANES, D_BRANCH)], axis=0)
        y = jnp.dot(mh_ref[...], e, preferred_element_type=F32).reshape((2,) + o_ref.shape[2:])
        o_ref[:, pp] = x_ref[:, pp] * (y + u_ref[:, pp] * b_ref[...])


def _ifft_outer(tb, d, u, ucol, x, xcol, bias):
    n1, half = tb["n1"], tb["half"]
    p = d.shape[0]
    mshape = (2 * half * SUBLANES, 2 * n1 * SUBLANES)
    io = lambda col: pl.BlockSpec((2, FFT_PP, half, None, SUBLANES, D_BRANCH), lambda t, q: (0, q, 0, t, 0, col))
    return pl.pallas_call(
        _ifft_outer_kernel,
        grid=(FFT_NT, p // FFT_PP),
        in_specs=[_const_spec(mshape),
                  pl.BlockSpec((FFT_PP, n1, None, SUBLANES, D_BRANCH), lambda t, q: (q, 0, t, 0, 0)),
                  io(ucol), io(xcol), _const_spec((1, D_BRANCH))],
        out_specs=io(0),
        out_shape=jax.ShapeDtypeStruct((2, p, half, FFT_NT, SUBLANES, D_BRANCH), F32),
        compiler_params=_cparams("arbitrary", "arbitrary"),
        name="ifft_outer",
    )(tb["m_inv"][0], d, u, x, bias.reshape(1, D_BRANCH))


def _hyconv_kernel(x_ref, w_ref, b_ref, o_ref, xp_ref):
    seq = x_ref.shape[0]
    xp_ref[0:HY_PAD, :] = jnp.zeros((HY_PAD, D_BRANCH), F32)
    xp_ref[HY_PAD + seq:2 * HY_PAD + seq, :] = jnp.zeros((HY_PAD, D_BRANCH), F32)
    xp_ref[HY_PAD:HY_PAD + seq, :] = x_ref[...]

    def body(i, carry):
        r0 = pl.multiple_of(i * HY_TB, HY_TB)
        win = xp_ref[pl.ds(r0, HY_TB + 2 * HY_PAD), :]
        y = b_ref[...]
        for j in range(C_CONV):
            s0 = HY_PAD + j - C_CONV // 2
            y = y + w_ref[j:j + 1, :] * win[s0:s0 + HY_TB, :]
        o_ref[pl.ds(r0, HY_TB), :] = y
        return carry

    lax.fori_loop(0, seq // HY_TB, body, 0)


def _hyconv(pc3, cw, cb):
    bsz, seq, width = pc3.shape
    nb = width // D_BRANCH
    blk = pl.BlockSpec((None, seq, D_BRANCH), lambda b, j: (b, 0, j))
    return pl.pallas_call(
        _hyconv_kernel,
        grid=(bsz, nb),
        in_specs=[blk, pl.BlockSpec((C_CONV, D_BRANCH), lambda b, j: (0, j)),
                  pl.BlockSpec((1, D_BRANCH), lambda b, j: (0, j))],
        out_specs=blk,
        out_shape=jax.ShapeDtypeStruct(pc3.shape, F32),
        scratch_shapes=[pltpu.VMEM((seq + 2 * HY_PAD, D_BRANCH), F32)],
        compiler_params=_cparams("parallel", "parallel"),
        name="hyconv",
    )(pc3, cw, cb.reshape(1, width))


def _hyfilt_kernel(z_ref, w1_ref, b1_ref, fr_ref, w2_ref, b2_ref, w3_ref, dec_ref, o_ref, h_ref):
    seq = z_ref.shape[0]
    nblk = seq // HY_TB
    fr = fr_ref[...]

    @pl.when(pl.program_id(0) == 0)
    def _():
        def hidden(i, carry):
            r0 = pl.multiple_of(i * HY_TB, HY_TB)
            zb = z_ref[pl.ds(r0, HY_TB), :]
            h = jnp.sin(fr * (jnp.dot(zb, w1_ref[...], precision=HI, preferred_element_type=F32) + b1_ref[...]))
            h = jnp.sin(fr * (jnp.dot(h, w2_ref[...], precision=HI, preferred_element_type=F32) + b2_ref[...]))
            h_ref[pl.ds(r0, HY_TB), :] = h
            return carry

        lax.fori_loop(0, nblk, hidden, 0)

    def body(i, ss):
        r0 = pl.multiple_of(i * HY_TB, HY_TB)
        hf = jnp.dot(h_ref[pl.ds(r0, HY_TB), :], w3_ref[...], precision=HI, preferred_element_type=F32)
        hf = hf * jnp.exp(-z_ref[pl.ds(r0, HY_TB), 0:1] * dec_ref[...])
        o_ref[pl.ds(r0, HY_TB), :] = hf
        return ss + jnp.sum(hf * hf, axis=0, keepdims=True)

    ss = lax.fori_loop(0, nblk, body, jnp.zeros((1, D_BRANCH), F32))
    scale = lax.rsqrt(ss + EPS)

    def norm(i, carry):
        r0 = pl.multiple_of(i * HY_TB, HY_TB)
        o_ref[pl.ds(r0, HY_TB), :] = o_ref[pl.ds(r0, HY_TB), :] * scale
        return carry

    lax.fori_loop(0, nblk, norm, 0)


def _hyfilt(seq, w1, b1, freq, w2, b2, w3):
    t = jnp.linspace(0.0, 1.0, seq, dtype=F32)[:, None]
    bands = (C_EMB - 1) // 2
    w = 2.0 * math.pi * jnp.arange(seq, dtype=F32)[:, None] / seq
    fr = jnp.linspace(1e-4, bands - 1, bands, dtype=F32)[None]
    z = jnp.concatenate([t, jnp.cos(fr * w), -jnp.sin(fr * w)], axis=-1)
    z = jnp.pad(z, ((0, 0), (0, LANES - C_EMB)))
    padm = lambda a, r, c: jnp.pad(a.astype(F32), ((0, r - a.shape[0]), (0, c - a.shape[1])))
    row = lambda a: padm(a.reshape(1, -1), 1, LANES)
    dec = jnp.abs(jnp.linspace(C_MIN_DECAY, C_MAX_DECAY, D_BRANCH, dtype=F32)).reshape(1, D_BRANCH)
    nset = C_ORDER * 2
    return pl.pallas_call(
        _hyfilt_kernel,
        grid=(nset,),
        in_specs=[_const_spec((seq, LANES)), _const_spec((LANES, LANES)), _const_spec((1, LANES)),
                  _const_spec((1, LANES)), _const_spec((LANES, LANES)), _const_spec((1, LANES)),
                  pl.BlockSpec((LANES, D_BRANCH), lambda j: (0, j)), _const_spec((1, D_BRANCH))],
        out_specs=pl.BlockSpec((None, seq, D_BRANCH), lambda j: (j, 0, 0)),
        out_shape=jax.ShapeDtypeStruct((nset, seq, D_BRANCH), F32),
        scratch_shapes=[pltpu.VMEM((seq, LANES), F32)],
        compiler_params=_cparams("arbitrary"),
        name="hyfilt",
    )(z, padm(w1, LANES, LANES), row(b1), row(freq), padm(w2, LANES, LANES), row(b2),
      padm(w3, LANES, nset * D_BRANCH), dec)


def _hyena(pc3, cw, cb, w1, b1, freq, w2, b2, w3, bias):
    bsz, seq, width = pc3.shape
    tb = _fft_tables(seq)
    half = tb["half"]
    npair = bsz // 2
    uc = _hyconv(pc3, cw, cb)
    uc6 = uc.reshape(2, npair, half, FFT_NT, SUBLANES, width)
    filt = _hyfilt(seq, w1, b1, freq, w2, b2, w3).reshape(C_ORDER * 2, half, FFT_NT, SUBLANES, D_BRANCH)
    hspec = _fft_filt(tb, _fft_outer(tb, filt, False, 3, False))
    src = uc6
    for order in range(C_ORDER):
        d = _fft_mid(tb, _fft_outer(tb, src, True, 1, True), hspec, order)
        src = _ifft_outer(tb, d, src, 0, uc6, order + 1, bias[order])
    return src.reshape(bsz, seq, D_BRANCH)


AT_HALF = 64
AT_TQ = 512
AT_SUB = 128


def _t5_bucket(rel):
    half = N_BUCKETS // 2
    max_exact = half // 2
    n = np.abs(rel)
    large = max_exact + (np.log(np.maximum(n, 1) / max_exact) / math.log(MAX_DISTANCE / max_exact)
                         * (half - max_exact)).astype(np.int64)
    large = np.minimum(large, half - 1)
    return (rel > 0).astype(np.int64) * half + np.where(n < max_exact, n, large)


def _attn_geometry(n):
    tq = min(AT_TQ, n)
    sub = min(AT_SUB, tq)
    if sub + 2 * AT_HALF >= n:
        sub = tq
    win = min(sub + 2 * AT_HALF, n)
    return tq, sub, win, n // tq, n // sub


def _attn_bias_tables(rel_bias, g, dil, n):
    _, sub, win, _, nsb = _attn_geometry(n)
    hs = slice(g * D_HEADS_PER_GROUP, (g + 1) * D_HEADS_PER_GROUP)
    offsets = np.arange(-AT_HALF, AT_HALF + 1) * dil
    onehot = np.zeros((2 * AT_HALF + 1, N_BUCKETS), np.float32)
    onehot[np.arange(2 * AT_HALF + 1), _t5_bucket(offsets)] = 1.0
    band = jnp.dot(rel_bias.astype(F32)[:, hs].T, jnp.asarray(onehot).T, precision=HI)
    nband = 2 * AT_HALF + 1
    lv = sub + win - 1
    tables = []
    for i in sorted({0, min(1, nsb - 1), nsb - 1}):
        ws = int(np.clip(i * sub - AT_HALF, 0, n - win))
        lo = (sub - 1) - (ws - i * sub) - AT_HALF
        v = jnp.pad(band, ((0, 0), (lo, lv - lo - nband)), constant_values=NEG_BIG)
        flat = jnp.tile(v, (1, sub + 1))[:, sub - 1:sub - 1 + sub * (lv - 1)]
        tables.append(flat.reshape(D_HEADS_PER_GROUP, sub, lv - 1)[:, :, :win])
    return jnp.stack(tables)


def _attn_kernel(q_ref, k_ref, v_ref, bias_ref, o_ref, l_ref, *, n, dil):
    tq, sub, win, _, nsb = _attn_geometry(n)
    ncase = bias_ref.shape[0]
    width = q_ref.shape[-1]
    nh = width // D_HEAD_DIM
    hp = pl.program_id(1)
    lane_head = lax.broadcasted_iota(jnp.int32, (sub, width), 1) // D_HEAD_DIM
    hmask = [lane_head == hh for hh in range(nh)]
    whole = win == n
    for r in range(dil):
        if whole:
            kw = k_ref[pl.ds(r, win, stride=dil), :].astype(BF16)
            vw = v_ref[pl.ds(r, win, stride=dil), :].astype(BF16)
        for j in range(tq // sub):
            sidx = pl.program_id(2) * (tq // sub) + j
            case = jnp.minimum(jnp.where(sidx == nsb - 1, ncase - 1, jnp.minimum(sidx, 1)), ncase - 1)
            bias = bias_ref[case, pl.ds(hp * nh, nh)].reshape(nh * sub, win)
            if not whole:
                ws = pl.multiple_of(jnp.clip(sidx * sub - AT_HALF, 0, n - win), AT_HALF)
                kw = k_ref[pl.ds(ws * dil + r, win, stride=dil), :].astype(BF16)
                vw = v_ref[pl.ds(ws * dil + r, win, stride=dil), :].astype(BF16)
            q = q_ref[pl.ds(j * sub * dil + r, sub, stride=dil), :] * (D_HEAD_DIM ** -0.5)
            qs = jnp.concatenate([jnp.where(hmask[hh], q, 0.0) for hh in range(nh)], axis=0).astype(BF16)
            s = lax.dot_general(qs, kw, (((1,), (1,)), ((), ())), preferred_element_type=F32) + bias
            m = jnp.max(s, axis=-1, keepdims=True)
            p = jnp.exp(s - m)
            l = jnp.sum(p, axis=-1, keepdims=True)
            o_all = jnp.dot(p.astype(BF16), vw, preferred_element_type=F32) / l
            lse = m + jnp.log(l)
            o_acc = jnp.zeros((sub, width), F32)
            l_acc = jnp.zeros((sub, width), F32)
            for hh in range(nh):
                o_acc = jnp.where(hmask[hh], o_all[hh * sub:(hh + 1) * sub], o_acc)
                l_acc = jnp.where(hmask[hh], lse[hh * sub:(hh + 1) * sub], l_acc)
            o_ref[pl.ds(j * sub * dil + r, sub, stride=dil), :] = o_acc
            l_ref[pl.ds(j * sub * dil + r, sub, stride=dil), :] = l_acc


def _banded_attention(pd3, g, dil, bias):
    bsz, seq, width = pd3.shape
    n = seq // dil
    tq, sub, win, nq, _ = _attn_geometry(n)
    rows = dil * tq
    bw = LANES if dil > 1 else D_BRANCH
    per = D_BRANCH // bw
    third = width // 3 // bw
    out_spec = pl.BlockSpec((None, rows, bw), lambda b, h, i: (b, i, h))
    shp = jax.ShapeDtypeStruct((bsz, seq, D_BRANCH), F32)
    return pl.pallas_call(
        functools.partial(_attn_kernel, n=n, dil=dil),
        grid=(bsz, per, nq),
        in_specs=[pl.BlockSpec((None, rows, bw), lambda b, h, i: (b, i, g * per + h)),
                  pl.BlockSpec((None, seq, bw), lambda b, h, i: (b, 0, third + g * per + h)),
                  pl.BlockSpec((None, seq, bw), lambda b, h, i: (b, 0, 2 * third + g * per + h)),
                  _const_spec(bias.shape)],
        out_specs=[out_spec, out_spec],
        out_shape=[shp, shp],
        compiler_params=_cparams("parallel", "parallel", "arbitrary"),
        name=f"attn_d{dil}",
    )(pd3, pd3, pd3, bias)


def _dilated_attention(pd3, rel_bias):
    bsz, seq, _ = pd3.shape
    outs, lses = [], []
    for g, (_, dil) in enumerate(D_GROUPS):
        o, l = _banded_attention(pd3, g, dil, _attn_bias_tables(rel_bias, g, dil, seq // dil))
        outs.append(o.reshape(bsz * seq, D_BRANCH))
        lses.append(l.reshape(bsz * seq, D_BRANCH))
    return outs, lses


def kernel(x, norm1_g, w_in, hgrn_lb_logits, hgrn_norm_g, lru_conv_w, lru_conv_b, lru_wa, lru_ba, lru_wx, lru_bx,
           lru_lambda, hy_conv_w, hy_conv_b, hy_w1, hy_b1, hy_freq, hy_w2, hy_b2, hy_w3, hy_bias, rel_bias,
           w_branch, w_gate, b_gate, w_out, norm2_g, w_ff1, w_ff3, w_ff2, final_g):
    bsz, seq, _ = x.shape
    n = bsz * seq
    lb_soft = jax.nn.softmax(hgrn_lb_logits.astype(F32), axis=0)
    lower_bounds = jnp.cumsum(lb_soft, axis=0) - lb_soft[0]
    x2 = x.reshape(n, D_MODEL)
    flat = lambda a: a.reshape(n, D_BRANCH)
    for l in range(DEPTH):
        pa, pb, pc, pd = _inproj(x2, norm1_g[l], w_in[l].astype(BF16))
        oa_f, oa_b = _hgrn(pa.reshape(bsz, seq, IN_A), lower_bounds[l])
        yb = _rglru(pb.reshape(bsz, seq, IN_B), lru_conv_w[l], lru_conv_b[l], lru_wa[l], lru_ba[l],
                    lru_wx[l], lru_bx[l], lru_lambda[l])
        yc = _hyena(pc.reshape(bsz, seq, IN_C), hy_conv_w[l], hy_conv_b[l], hy_w1[l], hy_b1[l], hy_freq[l],
                    hy_w2[l], hy_b2[l], hy_w3[l], hy_bias[l])
        od, ld = _dilated_attention(pd.reshape(bsz, seq, IN_D), rel_bias)
        x2 = _merge(x2, norm1_g[l], flat(oa_f), flat(oa_b), pa, hgrn_norm_g[l], flat(yb), flat(yc), od, ld,
                    w_gate[l].reshape(D_MODEL, N_BRANCH * D_MODEL).astype(BF16),
                    b_gate[l].reshape(1, N_BRANCH * D_MODEL), w_branch[l].astype(BF16), w_out[l].astype(BF16))
        x2 = _ffn(x2, norm2_g[l], *_ffn_weights(w_ff1[l], w_ff3[l], w_ff2[l]), final_g, l == DEPTH - 1)
    return x2.reshape(bsz, seq, D_MODEL)
```

```python
import functools
import math

import jax
import jax.numpy as jnp
import numpy as np
from jax import lax
from jax.experimental import pallas as pl
from jax.experimental.pallas import tpu as pltpu

F32 = jnp.float32
BF16 = jnp.bfloat16
HI = lax.Precision.HIGHEST

D_MODEL = 1024
DEPTH = 2
EPS = 1e-6
TINY = 1e-30
N_BRANCH = 4
D_BRANCH = 256
A_HEADS = 4
A_DK = 64
A_CHUNK = 64
B_BLOCKS = 4
B_BW = 64
B_CONV = 4
LRU_C = 8.0
C_ORDER = 2
C_CONV = 3
C_EMB = 33
C_HID = 64
C_MIN_DECAY = math.log(1e-2) / 1.5
C_MAX_DECAY = math.log(1e-2) / 0.3
D_GROUPS = ((128, 1), (512, 4), (2048, 16))
D_HEADS_PER_GROUP = 4
D_HEAD_DIM = 64
D_N_HEADS = 12
D_QKV = 768
N_BUCKETS = 32
MAX_DISTANCE = 1024
NEG_BIG = -1e30
D_FF = 2816
IN_A = 5 * D_BRANCH
IN_B = 2 * D_BRANCH
IN_C = 3 * D_BRANCH
IN_D = 3 * D_QKV
IN_WIDTH = IN_A + IN_B + IN_C + IN_D

LANES = 128
SUBLANES = 8
VMEM_LIMIT = 56 * 1024 * 1024


def _cparams(*sem):
    return pltpu.CompilerParams(dimension_semantics=sem, vmem_limit_bytes=VMEM_LIMIT)


def _const_spec(shape):
    nd = len(shape)
    return pl.BlockSpec(shape, lambda *_: (0,) * nd, pipeline_mode=pl.Buffered(1))


def _rms(x, g):
    return x * lax.rsqrt(jnp.mean(x * x, axis=-1, keepdims=True) + EPS) * g


def _sigmoid(x):
    return 1.0 / (1.0 + jnp.exp(-x))


IN_TM = 512
IN_CHUNK = 256
IN_CONV_ROWS = 256


def _inproj_kernel(x_ref, xp_ref, xn_ref, g_ref, w_ref, cw_ref, cb_ref, oa_ref, ob_ref, oc_ref, od_ref, cs_ref,
                   *, tiles_per_seq):
    i = pl.program_id(0)
    g = g_ref[...]
    h = _rms(x_ref[...], g).astype(BF16)
    off = 0
    for o_ref in (oa_ref, ob_ref, oc_ref, od_ref):
        width = o_ref.shape[-1]
        for c in range(0, width, IN_CHUNK):
            y = jnp.dot(h, w_ref[:, off + c:off + c + IN_CHUNK], preferred_element_type=F32)
            if o_ref is oc_ref:
                cs_ref[SUBLANES:SUBLANES + IN_TM, c:c + IN_CHUNK] = y
            else:
                o_ref[:, c:c + IN_CHUNK] = y
        off += width
    off_c = oa_ref.shape[-1] + ob_ref.shape[-1]
    hh = _rms(jnp.concatenate([xp_ref[...], xn_ref[...]], axis=0), g).astype(BF16)
    halo = jnp.dot(hh, w_ref[:, off_c:off_c + IN_C], preferred_element_type=F32)
    keep_prev = jnp.where(i % tiles_per_seq != 0, 1.0, 0.0)
    keep_next = jnp.where(i % tiles_per_seq != tiles_per_seq - 1, 1.0, 0.0)
    cs_ref[0:SUBLANES, :] = halo[0:SUBLANES] * keep_prev
    cs_ref[SUBLANES + IN_TM:2 * SUBLANES + IN_TM, :] = halo[SUBLANES:2 * SUBLANES] * keep_next
    left = C_CONV // 2
    for r0 in range(0, IN_TM, IN_CONV_ROWS):
        for c in range(0, IN_C, IN_CHUNK):
            win = cs_ref[r0:r0 + IN_CONV_ROWS + 2 * SUBLANES, c:c + IN_CHUNK]
            y = cb_ref[:, c:c + IN_CHUNK]
            for j in range(C_CONV):
                s0 = SUBLANES + j - left
                y = y + cw_ref[j:j + 1, c:c + IN_CHUNK] * win[s0:s0 + IN_CONV_ROWS, :]
            oc_ref[r0:r0 + IN_CONV_ROWS, c:c + IN_CHUNK] = y


def _inproj(x2, g, w_bf16, cw, cb, seq):
    n = x2.shape[0]
    widths = (IN_A, IN_B, IN_C, IN_D)
    per8 = IN_TM // SUBLANES
    last8 = n // SUBLANES - 1
    return pl.pallas_call(
        functools.partial(_inproj_kernel, tiles_per_seq=seq // IN_TM),
        grid=(n // IN_TM,),
        in_specs=[pl.BlockSpec((IN_TM, D_MODEL), lambda i: (i, 0)),
                  pl.BlockSpec((SUBLANES, D_MODEL), lambda i: (jnp.maximum(i * per8 - 1, 0), 0)),
                  pl.BlockSpec((SUBLANES, D_MODEL), lambda i: (jnp.minimum((i + 1) * per8, last8), 0)),
                  _const_spec((1, D_MODEL)),
                  _const_spec((D_MODEL, IN_WIDTH)),
                  _const_spec((C_CONV, IN_C)), _const_spec((1, IN_C))],
        out_specs=[pl.BlockSpec((IN_TM, w), lambda i: (i, 0)) for w in widths],
        out_shape=[jax.ShapeDtypeStruct((n, w), F32) for w in widths],
        scratch_shapes=[pltpu.VMEM((IN_TM + 2 * SUBLANES, IN_C), F32)],
        compiler_params=_cparams("parallel"),
        name="inproj",
    )(x2, x2, x2, g.reshape(1, D_MODEL), w_bf16, cw, cb.reshape(1, IN_C))


FF_TM = 1024
FF_CHUNK = 256
FF_NCHUNK = D_FF // FF_CHUNK


def _ffn_kernel(x_ref, g_ref, w1_ref, w3_ref, w2_ref, fg_ref, o_ref, acc_ref, *, final):
    x = x_ref[...]
    h = _rms(x, g_ref[...]).astype(BF16)
    acc_ref[...] = x

    def body(c, carry):
        c0 = pl.multiple_of(c * FF_CHUNK, FF_CHUNK)
        a = jnp.dot(h, w1_ref[:, pl.ds(c0, FF_CHUNK)], preferred_element_type=F32)
        b = jnp.dot(h, w3_ref[:, pl.ds(c0, FF_CHUNK)], preferred_element_type=F32)
        t = (a * _sigmoid(a) * b).astype(BF16)
        acc_ref[...] += jnp.dot(t, w2_ref[pl.ds(c0, FF_CHUNK), :], preferred_element_type=F32)
        return carry

    lax.fori_loop(0, FF_NCHUNK, body, 0)
    y = acc_ref[...]
    if final:
        y = _rms(y, fg_ref[...])
    o_ref[...] = y


def _ffn(x2, g, w1c, w3c, w2c, final_g, final):
    n = x2.shape[0]
    return pl.pallas_call(
        functools.partial(_ffn_kernel, final=final),
        grid=(n // FF_TM,),
        in_specs=[pl.BlockSpec((FF_TM, D_MODEL), lambda i: (i, 0)),
                  _const_spec((1, D_MODEL)),
                  _const_spec((D_MODEL, D_FF)),
                  _const_spec((D_MODEL, D_FF)),
                  _const_spec((D_FF, D_MODEL)),
                  _const_spec((1, D_MODEL))],
        out_specs=pl.BlockSpec((FF_TM, D_MODEL), lambda i: (i, 0)),
        out_shape=jax.ShapeDtypeStruct((n, D_MODEL), F32),
        scratch_shapes=[pltpu.VMEM((FF_TM, D_MODEL), F32)],
        compiler_params=_cparams("parallel"),
        name="ffn_final" if final else "ffn",
    )(x2, g.reshape(1, D_MODEL), w1c, w3c, w2c, final_g.reshape(1, D_MODEL))


def _ffn_weights(w1, w3, w2):
    return w1.astype(BF16), w3.astype(BF16), w2.astype(BF16)


MG_TM = 512


def _head_ones():
    r = np.arange(D_BRANCH)[:, None] // A_DK
    c = np.arange(D_BRANCH)[None, :] // A_DK
    return jnp.asarray((r == c).astype(np.float32) / A_DK, BF16)


def _merge_kernel(x_ref, g1_ref, oaf_ref, oab_ref, ga_ref, hg_ref, hm_ref, yb_ref, yc_ref,
                  o0_ref, o1_ref, o2_ref, l0_ref, l1_ref, l2_ref,
                  wg_ref, bg_ref, wb_ref, wo_ref, out_ref):
    x = x_ref[...]
    h = _rms(x, g1_ref[...]).astype(BF16)
    oa = oaf_ref[...] + oab_ref[...]
    ms = jnp.dot((oa * oa).astype(BF16), hm_ref[...], preferred_element_type=F32)
    ga = ga_ref[...]
    ya = oa * lax.rsqrt(ms + EPS) * hg_ref[...] * (ga * _sigmoid(ga))
    l0, l1, l2 = l0_ref[...], l1_ref[...], l2_ref[...]
    m = jnp.maximum(jnp.maximum(l0, l1), l2)
    e0, e1, e2 = jnp.exp(l0 - m), jnp.exp(l1 - m), jnp.exp(l2 - m)
    yd = (e0 * o0_ref[...] + e1 * o1_ref[...] + e2 * o2_ref[...]) / (e0 + e1 + e2)
    mixed = None
    for j, y in enumerate((ya, yb_ref[...], yc_ref[...], yd)):
        gate = _sigmoid(jnp.dot(h, wg_ref[:, j * D_MODEL:(j + 1) * D_MODEL], preferred_element_type=F32)
                        + bg_ref[:, j * D_MODEL:(j + 1) * D_MODEL])
        t = gate * jnp.dot(y.astype(BF16), wb_ref[j], preferred_element_type=F32)
        mixed = t if mixed is None else mixed + t
    out_ref[...] = x + jnp.dot(mixed.astype(BF16), wo_ref[...], preferred_element_type=F32)


def _merge(x2, g1, oa_f, oa_b, pa, hg, yb, yc, od, ld, wg, bg, wb, wo):
    n = x2.shape[0]
    tile = lambda w: pl.BlockSpec((MG_TM, w), lambda i: (i, 0))
    return pl.pallas_call(
        _merge_kernel,
        grid=(n // MG_TM,),
        in_specs=[tile(D_MODEL), _const_spec((1, D_MODEL)),
                  tile(D_BRANCH), tile(D_BRANCH),
                  pl.BlockSpec((MG_TM, D_BRANCH), lambda i: (i, 4)),
                  _const_spec((1, D_BRANCH)), _const_spec((D_BRANCH, D_BRANCH)),
                  tile(D_BRANCH), tile(D_BRANCH),
                  tile(D_BRANCH), tile(D_BRANCH), tile(D_BRANCH),
                  tile(D_BRANCH), tile(D_BRANCH), tile(D_BRANCH),
                  _const_spec((D_MODEL, N_BRANCH * D_MODEL)), _const_spec((1, N_BRANCH * D_MODEL)),
                  _const_spec((N_BRANCH, D_BRANCH, D_MODEL)), _const_spec((D_MODEL, D_MODEL))],
        out_specs=tile(D_MODEL),
        out_shape=jax.ShapeDtypeStruct((n, D_MODEL), F32),
        compiler_params=_cparams("parallel"),
        name="merge",
    )(x2, g1.reshape(1, D_MODEL), oa_f, oa_b, pa, hg.reshape(1, D_BRANCH), _head_ones(), yb, yc,
      od[0], od[1], od[2], ld[0], ld[1], ld[2], wg, bg, wb, wo)


HG_TS = 256
HG_NCH = HG_TS // A_CHUNK
HG_MID = A_CHUNK // 2


def _hgrn_tables():
    r = np.arange(HG_TS)[:, None]
    c = np.arange(HG_TS)[None, :]
    same = (r // A_CHUNK) == (c // A_CHUNK)
    rr = np.arange(HG_NCH * A_HEADS * A_CHUNK)[:, None]
    same_s = (rr // (A_HEADS * A_CHUNK)) == (c // A_CHUNK)
    t, sidx = rr % A_CHUNK, c % A_CHUNK
    out = []
    for fwd in (True, False):
        order = (r >= c) if fwd else (r <= c)
        order_s = (t >= sidx) if fwd else (t <= sidx)
        out.append((jnp.asarray((same & order).astype(np.float32), BF16),
                    jnp.asarray((same_s & order_s).astype(np.float32), F32)))
    return out


def _hgrn_prep(q, fl, v, lb, cum, smask, fwd):
    lane_head = lax.broadcasted_iota(jnp.int32, (A_CHUNK, D_BRANCH), 1) // A_DK
    hmask = [(lane_head == hh).astype(F32) for hh in range(A_HEADS)]
    blk_r = lax.broadcasted_iota(jnp.int32, (D_BRANCH, D_BRANCH), 0) // A_DK
    blk_c = lax.broadcasted_iota(jnp.int32, (D_BRANCH, D_BRANCH), 1) // A_DK
    blockdiag = (blk_r == blk_c).astype(F32)
    col_chunk = lax.broadcasted_iota(jnp.int32, (D_BRANCH, HG_TS), 1) // A_CHUNK

    sg = _sigmoid(fl)
    f = lb + (1.0 - lb) * sg
    g = jnp.log(jnp.maximum(f, TINY))
    kk = (1.0 - lb) * (1.0 - sg)
    g1 = g.astype(BF16)
    r1 = g - g1.astype(F32)
    g2 = r1.astype(BF16)
    g3 = (r1 - g2.astype(F32)).astype(BF16)
    b = (jnp.dot(cum, g1, preferred_element_type=F32) + jnp.dot(cum, g2, preferred_element_type=F32)
         + jnp.dot(cum, g3, preferred_element_type=F32))
    rows = lambda c: slice(c * A_CHUNK, (c + 1) * A_CHUNK)
    last = (A_CHUNK - 1) if fwd else 0
    bcast = lambda r0: jnp.concatenate(
        [jnp.broadcast_to(b[c * A_CHUNK + r0:c * A_CHUNK + r0 + 1, :], (A_CHUNK, D_BRANCH)) for c in range(HG_NCH)], 0)
    bm = bcast(HG_MID)
    bl = bcast(last)
    qt = q * jnp.exp(b - bm)
    kt = (kk * jnp.exp(bm - b)).astype(BF16)
    qe = (q * jnp.exp(b)).astype(BF16)
    kh = (kk * jnp.exp(bl - b)).astype(BF16)
    qs = jnp.concatenate([qt[rows(c)] * hmask[hh] for c in range(HG_NCH) for hh in range(A_HEADS)],
                         axis=0).astype(BF16)
    s = lax.dot_general(qs, kt, (((1,), (1,)), ((), ())), preferred_element_type=F32)
    s = (s * smask).astype(BF16)
    ost = jnp.dot(s, v.astype(BF16), preferred_element_type=F32)
    o_intra = []
    for c in range(HG_NCH):
        base = c * A_HEADS * A_CHUNK
        o = ost[base:base + A_CHUNK] * hmask[0]
        for hh in range(1, A_HEADS):
            o = o + ost[base + hh * A_CHUNK:base + (hh + 1) * A_CHUNK] * hmask[hh]
        o_intra.append(o)
    vt = v.T
    lhs = jnp.concatenate([jnp.where(col_chunk == c, vt, 0.0) for c in range(HG_NCH)], axis=0).astype(BF16)
    updall = jnp.dot(lhs, kh, preferred_element_type=F32)
    upd = [updall[c * D_BRANCH:(c + 1) * D_BRANCH] * blockdiag for c in range(HG_NCH)]
    decay = [jnp.exp(b[c * A_CHUNK + last:c * A_CHUNK + last + 1, :]) for c in range(HG_NCH)]
    return o_intra, upd, qe, decay


def _hgrn_scan(o_intra, upd, qe, decay, st, fwd):
    rows = lambda c: slice(c * A_CHUNK, (c + 1) * A_CHUNK)
    outs = [None] * HG_NCH
    for c in (range(HG_NCH) if fwd else range(HG_NCH - 1, -1, -1)):
        outs[c] = o_intra[c] + lax.dot_general(qe[rows(c)], st.astype(BF16), (((1,), (1,)), ((), ())),
                                               preferred_element_type=F32)
        st = st * decay[c] + upd[c]
    return jnp.concatenate(outs, axis=0), st


def _hgrn_kernel(qf_ref, ff_ref, vf_ref, qb_ref, fb_ref, vb_ref, lb_ref, cf_ref, mf_ref, cb_ref, mb_ref,
                 of_ref, ob_ref, sf_ref, sb_ref):
    @pl.when(pl.program_id(1) == 0)
    def _():
        sf_ref[...] = jnp.zeros_like(sf_ref)
        sb_ref[...] = jnp.zeros_like(sb_ref)

    lb = lb_ref[...]
    pf = _hgrn_prep(qf_ref[...], ff_ref[...], vf_ref[...], lb, cf_ref[...], mf_ref[...], True)
    pb = _hgrn_prep(qb_ref[...], fb_ref[...], vb_ref[...], lb, cb_ref[...], mb_ref[...], False)
    o_f, st_f = _hgrn_scan(*pf, sf_ref[...], True)
    o_b, st_b = _hgrn_scan(*pb, sb_ref[...], False)
    of_ref[...] = o_f
    ob_ref[...] = o_b
    sf_ref[...] = st_f
    sb_ref[...] = st_b


def _hgrn(pa3, lb):
    bsz, seq, _ = pa3.shape
    nblk = seq // HG_TS
    blk = (None, HG_TS, D_BRANCH)
    up = lambda col: pl.BlockSpec(blk, lambda b, i: (b, i, col))
    down = lambda col: pl.BlockSpec(blk, lambda b, i: (b, nblk - 1 - i, col))
    shp = jax.ShapeDtypeStruct((bsz, seq, D_BRANCH), F32)
    (cum_f, sm_f), (cum_b, sm_b) = _hgrn_tables()
    mshape = (HG_NCH * A_HEADS * A_CHUNK, HG_TS)
    return pl.pallas_call(
        _hgrn_kernel,
        grid=(bsz, nblk),
        in_specs=[up(0), up(1), up(3), down(0), down(2), down(3), _const_spec((1, D_BRANCH)),
                  _const_spec((HG_TS, HG_TS)), _const_spec(mshape), _const_spec((HG_TS, HG_TS)), _const_spec(mshape)],
        out_specs=[up(0), down(0)],
        out_shape=[shp, shp],
        scratch_shapes=[pltpu.VMEM((D_BRANCH, D_BRANCH), F32), pltpu.VMEM((D_BRANCH, D_BRANCH), F32)],
        compiler_params=_cparams("parallel", "arbitrary"),
        name="hgrn2",
    )(pa3, pa3, pa3, pa3, pa3, pa3, lb.reshape(1, D_BRANCH), cum_f, sm_f, cum_b, sm_b)


RG_TB = 128
RG_PAD = SUBLANES
RG_LEFT = B_CONV // 2


def _dot3_rhs(x, wh, wl):
    xh = x.astype(BF16)
    xl = (x - xh.astype(F32)).astype(BF16)
    return (jnp.dot(xh, wh, preferred_element_type=F32) + jnp.dot(xl, wh, preferred_element_type=F32)
            + jnp.dot(xh, wl, preferred_element_type=F32))


def _group_scan(a, u, fwd):
    row = lax.broadcasted_iota(jnp.int32, a.shape, 1)
    k = 1
    while k < SUBLANES:
        if fwd:
            keep = row >= k
            us, as_ = pltpu.roll(u, k, 1), pltpu.roll(a, k, 1)
        else:
            keep = row < SUBLANES - k
            us, as_ = pltpu.roll(u, SUBLANES - k, 1), pltpu.roll(a, SUBLANES - k, 1)
        u = a * jnp.where(keep, us, 0.0) + u
        a = a * jnp.where(keep, as_, 1.0)
        k *= 2
    return a, u


def _block_scan(a, u, carry, fwd):
    t = a.shape[0]
    ngrp = t // SUBLANES
    ag, ug = _group_scan(a.reshape(ngrp, SUBLANES, D_BRANCH), u.reshape(ngrp, SUBLANES, D_BRANCH), fwd)
    hs = [None] * ngrp
    for g in (range(ngrp) if fwd else range(ngrp - 1, -1, -1)):
        h = ug[g] + ag[g] * carry
        hs[g] = h
        carry = h[SUBLANES - 1:SUBLANES, :] if fwd else h[0:1, :]
    return jnp.concatenate(hs, axis=0), carry


def _gelu_tanh(x):
    return 0.5 * x * (1.0 + jnp.tanh(math.sqrt(2.0 / math.pi) * (x + 0.044715 * (x * x * x))))


def _rglru_kernel(x_ref, gt_ref, cw_ref, cb_ref, wh_ref, wl_ref, bg_ref, lam_ref, o_ref, xp_ref, xc_ref):
    seq = x_ref.shape[0]
    nblk = seq // RG_TB
    xp_ref[0:RG_PAD, :] = jnp.zeros((RG_PAD, D_BRANCH), F32)
    xp_ref[RG_PAD + seq:2 * RG_PAD + seq, :] = jnp.zeros((RG_PAD, D_BRANCH), F32)
    xp_ref[RG_PAD:RG_PAD + seq, :] = x_ref[...]
    nl = -lam_ref[...]
    sp = jnp.maximum(nl, 0.0) + jnp.log(1.0 + jnp.exp(-jnp.abs(nl)))

    def block(i, carry, dirn):
        r0 = pl.multiple_of(i * RG_TB, RG_TB)
        if dirn == 0:
            win = xp_ref[pl.ds(r0, RG_TB + 2 * RG_PAD), :]
            xc = cb_ref[...]
            for j in range(B_CONV):
                s0 = RG_PAD + j - RG_LEFT
                xc = xc + cw_ref[j:j + 1, :] * win[s0:s0 + RG_TB, :]
            xc_ref[pl.ds(r0, RG_TB), :] = xc
        else:
            xc = xc_ref[pl.ds(r0, RG_TB), :]
        cols = slice(dirn * 2 * D_BRANCH, (dirn + 1) * 2 * D_BRANCH)
        gates = _dot3_rhs(xc, wh_ref[:, cols], wl_ref[:, cols]) + bg_ref[:, cols]
        r = _sigmoid(gates[:, :D_BRANCH])
        ig = _sigmoid(gates[:, D_BRANCH:])
        log_a = -LRU_C * r * sp[dirn:dirn + 1, :]
        a = jnp.exp(log_a)
        u = jnp.sqrt(jnp.maximum(-jnp.tanh(log_a) * (a * a + 1.0), 0.0)) * ig * xc
        h, carry = _block_scan(a, u, carry, dirn == 0)
        if dirn == 0:
            o_ref[pl.ds(r0, RG_TB), :] = h
        else:
            o_ref[pl.ds(r0, RG_TB), :] = (o_ref[pl.ds(r0, RG_TB), :] + h) * _gelu_tanh(gt_ref[pl.ds(r0, RG_TB), :])
        return carry

    zero = jnp.zeros((1, D_BRANCH), F32)
    lax.fori_loop(0, nblk, lambda i, c: block(i, c, 0), zero)
    lax.fori_loop(0, nblk, lambda i, c: block(nblk - 1 - i, c, 1), zero)


def _blockdiag(w):
    eye = jnp.eye(B_BLOCKS, dtype=w.dtype)
    return jnp.einsum('ncd,nm->ncmd', w, eye).reshape(D_BRANCH, D_BRANCH)


def _rglru(pb3, cw, cb, wa, ba, wx, bx, lam):
    bsz, seq, _ = pb3.shape
    wg = jnp.concatenate([_blockdiag(wa[0]), _blockdiag(wx[0]), _blockdiag(wa[1]), _blockdiag(wx[1])], axis=1)
    bg = jnp.concatenate([ba[0], bx[0], ba[1], bx[1]]).reshape(1, 4 * D_BRANCH)
    wh = wg.astype(BF16)
    wl = (wg - wh.astype(F32)).astype(BF16)
    blk = (None, seq, D_BRANCH)
    return pl.pallas_call(
        _rglru_kernel,
        grid=(bsz,),
        in_specs=[pl.BlockSpec(blk, lambda b: (b, 0, 0)),
                  pl.BlockSpec(blk, lambda b: (b, 0, 1)),
                  _const_spec((B_CONV, D_BRANCH)), _const_spec((1, D_BRANCH)),
                  _const_spec((D_BRANCH, 4 * D_BRANCH)), _const_spec((D_BRANCH, 4 * D_BRANCH)),
                  _const_spec((1, 4 * D_BRANCH)), _const_spec((2, D_BRANCH))],
        out_specs=pl.BlockSpec(blk, lambda b: (b, 0, 0)),
        out_shape=jax.ShapeDtypeStruct((bsz, seq, D_BRANCH), F32),
        scratch_shapes=[pltpu.VMEM((seq + 2 * RG_PAD, D_BRANCH), F32), pltpu.VMEM((seq, D_BRANCH), F32)],
        compiler_params=_cparams("parallel"),
        name="rglru",
    )(pb3, pb3, cw, cb.reshape(1, D_BRANCH), wh, wl, bg, lam)


FFT_IN = 128
FFT_NT = FFT_IN // SUBLANES
HY_TB = 256


def _split_np(a):
    a = np.asarray(a, np.float32)
    hi = a.astype(jnp.bfloat16)
    lo = (a - hi.astype(np.float32)).astype(jnp.bfloat16)
    return jnp.asarray(hi), jnp.asarray(lo)


def _dot3(mh, ml, x):
    xh = x.astype(BF16)
    xl = (x - xh.astype(F32)).astype(BF16)
    return (jnp.dot(mh, xh, preferred_element_type=F32) + jnp.dot(mh, xl, preferred_element_type=F32)
            + jnp.dot(ml, xh, preferred_element_type=F32))


def _dotp(mh, ml, x, passes):
    if passes == 1:
        return jnp.dot(mh, x.astype(BF16), preferred_element_type=F32)
    return _dot3(mh, ml, x)


def _pack_pair(re, im):
    rb = lax.bitcast_convert_type(re.astype(BF16).astype(F32), jnp.uint32)
    ib = lax.bitcast_convert_type(im.astype(BF16).astype(F32), jnp.uint32)
    return rb | (ib >> 16)


def _unpack_pair(w):
    re = lax.bitcast_convert_type(w & jnp.uint32(0xFFFF0000), F32)
    im = lax.bitcast_convert_type(w << 16, F32)
    return re.astype(BF16), im.astype(BF16)


def _fft_tables(seq):
    n = 2 * seq
    n1 = n // FFT_IN
    half = n1 // 2
    eye = np.eye(SUBLANES)
    a = 2.0 * np.pi * np.outer(np.arange(n1), np.arange(half)) / n1
    gr, gi = np.cos(a), -np.sin(a)
    blk = np.stack([np.stack([gr, -gi], axis=1), np.stack([gi, gr], axis=1)], axis=0)
    m_out = np.einsum('rkis,cd->rkcisd', blk, eye).reshape(2 * n1 * SUBLANES, 2 * half * SUBLANES)
    ir, ii = gr.T / n, -gi.T / n
    blk = np.stack([np.stack([ir, -ii], axis=1), np.stack([ii, ir], axis=1)], axis=0)
    m_inv = np.einsum('otrk,cd->otcrkd', blk, eye).reshape(2 * half * SUBLANES, 2 * n1 * SUBLANES)
    a = 2.0 * np.pi * np.outer(np.arange(FFT_IN), np.arange(FFT_IN)) / FFT_IN
    fr, fi = np.cos(a), -np.sin(a)
    w_fwd = np.block([[fr, -fi], [fi, fr]])
    w_inv = np.block([[fr, fi], [-fi, fr]])
    s_in = SUBLANES * np.arange(FFT_NT)[:, None, None] + np.arange(SUBLANES)[None, None, :]
    th = 2.0 * np.pi * s_in * np.arange(n1)[None, :, None] / n
    tw = (jnp.asarray(np.cos(th)[..., None], F32), jnp.asarray(-np.sin(th)[..., None], F32))
    th = 2.0 * np.pi * np.outer(np.arange(n1), np.arange(FFT_IN)) / n
    tw_in = (jnp.asarray(np.cos(th)[..., None], F32), jnp.asarray(-np.sin(th)[..., None], F32))
    return dict(m_out=_split_np(m_out), m_inv=_split_np(m_inv), w_fwd=_split_np(w_fwd), w_inv=_split_np(w_inv),
                tw=tw, tw_in=tw_in, n1=n1, half=half)


FFT_PP = 2


def _tw_spec(n1):
    return pl.BlockSpec((None, n1, SUBLANES, 1), lambda t, q: (t, 0, 0, 0))


def _fft_outer_kernel(mh_ref, ml_ref, twr_ref, twi_ref, z_ref, v_ref, *, complex_in, passes, packed):
    n1 = v_ref.shape[1]
    tr, ti = twr_ref[...], twi_ref[...]
    for pp in range(FFT_PP):
        zz = z_ref[:, pp] if complex_in else z_ref[pp]
        rows_in = math.prod(zz.shape[:-1])
        v = _dotp(mh_ref[:, :rows_in], ml_ref[:, :rows_in], zz.reshape(rows_in, D_BRANCH), passes)
        vr = v[:n1 * SUBLANES].reshape(n1, SUBLANES, D_BRANCH)
        vi = v[n1 * SUBLANES:].reshape(n1, SUBLANES, D_BRANCH)
        wr, wi = vr * tr - vi * ti, vr * ti + vi * tr
        if packed:
            v_ref[pp] = _pack_pair(wr, wi)
        else:
            v_ref[pp, :, 0] = wr
            v_ref[pp, :, 1] = wi


def _fft_outer(tb, z, complex_in, passes, packed):
    n1, half = tb["n1"], tb["half"]
    if complex_in:
        p = z.shape[1]
        zspec = pl.BlockSpec((2, FFT_PP, half, None, SUBLANES, D_BRANCH), lambda t, q: (0, q, 0, t, 0, 0))
    else:
        p = z.shape[0]
        zspec = pl.BlockSpec((FFT_PP, half, None, SUBLANES, D_BRANCH), lambda t, q: (q, 0, t, 0, 0))
    mshape = (2 * n1 * SUBLANES, 2 * half * SUBLANES)
    if packed:
        out_spec = pl.BlockSpec((FFT_PP, n1, None, SUBLANES, D_BRANCH), lambda t, q: (q, 0, t, 0, 0))
        out_shape = jax.ShapeDtypeStruct((p, n1, FFT_NT, SUBLANES, D_BRANCH), jnp.uint32)
    else:
        out_spec = pl.BlockSpec((FFT_PP, n1, 2, None, SUBLANES, D_BRANCH), lambda t, q: (q, 0, 0, t, 0, 0))
        out_shape = jax.ShapeDtypeStruct((p, n1, 2, FFT_NT, SUBLANES, D_BRANCH), F32)
    return pl.pallas_call(
        functools.partial(_fft_outer_kernel, complex_in=complex_in, passes=passes, packed=packed),
        grid=(FFT_NT, p // FFT_PP),
        in_specs=[_const_spec(mshape), _const_spec(mshape), _tw_spec(n1), _tw_spec(n1), zspec],
        out_specs=out_spec,
        out_shape=out_shape,
        compiler_params=_cparams("arbitrary", "arbitrary"),
        name="fft_outer_c" if complex_in else "fft_outer_r",
    )(*tb["m_out"], *tb["tw"], z)


FFT_KB = 8


def _fft_filt_kernel(wh_ref, wl_ref, vf_ref, vb_ref, h_ref):
    for kb in range(FFT_KB):
        zf = _dot3(wh_ref[...], wl_ref[...], vf_ref[kb].reshape(2 * FFT_IN, D_BRANCH))
        zb = _dot3(wh_ref[...], wl_ref[...], vb_ref[kb].reshape(2 * FFT_IN, D_BRANCH))
        h_ref[kb, 0] = zf[:FFT_IN] + zb[:FFT_IN]
        h_ref[kb, 1] = zf[FFT_IN:] - zb[FFT_IN:]


def _fft_filt(tb, v):
    n1 = v.shape[1]
    wspec = _const_spec((2 * FFT_IN, 2 * FFT_IN))
    vblk = (None, FFT_KB, 2, FFT_NT, SUBLANES, D_BRANCH)
    return pl.pallas_call(
        _fft_filt_kernel,
        grid=(C_ORDER, n1 // FFT_KB),
        in_specs=[wspec, wspec,
                  pl.BlockSpec(vblk, lambda o, k: (2 * o, k, 0, 0, 0, 0)),
                  pl.BlockSpec(vblk, lambda o, k: (2 * o + 1, k, 0, 0, 0, 0))],
        out_specs=pl.BlockSpec((None, FFT_KB, 2, FFT_IN, D_BRANCH), lambda o, k: (o, k, 0, 0, 0)),
        out_shape=jax.ShapeDtypeStruct((C_ORDER, n1, 2, FFT_IN, D_BRANCH), F32),
        compiler_params=_cparams("parallel", "parallel"),
        name="fft_filt",
    )(*tb["w_fwd"], v, v)


def _fft_mid_kernel(wf_ref, wi_ref, twr_ref, twi_ref, v_ref, h_ref, d_ref):
    for kb in range(FFT_KB):
        vr, vi = _unpack_pair(v_ref[kb].reshape(FFT_IN, D_BRANCH))
        z = jnp.dot(wf_ref[...], jnp.concatenate([vr, vi], axis=0), preferred_element_type=F32)
        zr, zi = z[:FFT_IN], z[FFT_IN:]
        hr, hi = h_ref[kb, 0], h_ref[kb, 1]
        pr = (zr * hr - zi * hi).astype(BF16)
        pi = (zr * hi + zi * hr).astype(BF16)
        d = jnp.dot(wi_ref[...], jnp.concatenate([pr, pi], axis=0), preferred_element_type=F32)
        dr, di = d[:FFT_IN], d[FFT_IN:]
        tr, ti = twr_ref[kb], twi_ref[kb]
        er, ei = dr * tr + di * ti, di * tr - dr * ti
        d_ref[kb] = _pack_pair(er, ei).reshape(d_ref.shape[1:])


def _fft_mid(tb, v, hspec, order):
    p, n1 = v.shape[:2]
    vspec = pl.BlockSpec((None, FFT_KB, FFT_NT, SUBLANES, D_BRANCH), lambda k, q: (q, k, 0, 0, 0))
    wspec = _const_spec((2 * FFT_IN, 2 * FFT_IN))
    tspec = pl.BlockSpec((FFT_KB, FFT_IN, 1), lambda k, q: (k, 0, 0))
    return pl.pallas_call(
        _fft_mid_kernel,
        grid=(n1 // FFT_KB, p),
        in_specs=[wspec, wspec, tspec, tspec, vspec,
                  pl.BlockSpec((None, FFT_KB, 2, FFT_IN, D_BRANCH), lambda k, q: (order, k, 0, 0, 0))],
        out_specs=vspec,
        out_shape=jax.ShapeDtypeStruct(v.shape, jnp.uint32),
        compiler_params=_cparams("parallel", "arbitrary"),
        name="fft_mid",
    )(tb["w_fwd"][0], tb["w_inv"][0], *tb["tw_in"], v, hspec)


def _ifft_outer_kernel(mh_ref, d_ref, u_ref, x_ref, b_ref, o_ref):
    n1 = d_ref.shape[1]
    for pp in range(FFT_PP):
        er, ei = _unpack_pair(d_ref[pp])
        e = jnp.concatenate([er.reshape(n1 * SUBLANES, D_BRANCH), ei.reshape(n1 * SUBLANES, D_BRANCH)], axis=0)
        y = jnp.dot(mh_ref[...], e, preferred_element_type=F32).reshape((2,) + o_ref.shape[2:])
        o_ref[:, pp] = x_ref[:, pp] * (y + u_ref[:, pp] * b_ref[...])


def _ifft_outer(tb, d, u, ucol, x, xcol, bias):
    n1, half = tb["n1"], tb["half"]
    p = d.shape[0]
    mshape = (2 * half * SUBLANES, 2 * n1 * SUBLANES)
    io = lambda col: pl.BlockSpec((2, FFT_PP, half, None, SUBLANES, D_BRANCH), lambda t, q: (0, q, 0, t, 0, col))
    return pl.pallas_call(
        _ifft_outer_kernel,
        grid=(FFT_NT, p // FFT_PP),
        in_specs=[_const_spec(mshape),
                  pl.BlockSpec((FFT_PP, n1, None, SUBLANES, D_BRANCH), lambda t, q: (q, 0, t, 0, 0)),
                  io(ucol), io(xcol), _const_spec((1, D_BRANCH))],
        out_specs=io(0),
        out_shape=jax.ShapeDtypeStruct((2, p, half, FFT_NT, SUBLANES, D_BRANCH), F32),
        compiler_params=_cparams("arbitrary", "arbitrary"),
        name="ifft_outer",
    )(tb["m_inv"][0], d, u, x, bias.reshape(1, D_BRANCH))


def _hyfilt_kernel(z_ref, w1_ref, b1_ref, fr_ref, w2_ref, b2_ref, w3_ref, dec_ref, o_ref, h_ref):
    seq = z_ref.shape[0]
    nblk = seq // HY_TB
    fr = fr_ref[...]

    @pl.when(pl.program_id(0) == 0)
    def _():
        def hidden(i, carry):
            r0 = pl.multiple_of(i * HY_TB, HY_TB)
            zb = z_ref[pl.ds(r0, HY_TB), :]
            h = jnp.sin(fr * (jnp.dot(zb, w1_ref[...], precision=HI, preferred_element_type=F32) + b1_ref[...]))
            h = jnp.sin(fr * (jnp.dot(h, w2_ref[...], precision=HI, preferred_element_type=F32) + b2_ref[...]))
            h_ref[pl.ds(r0, HY_TB), :] = h
            return carry

        lax.fori_loop(0, nblk, hidden, 0)

    def body(i, ss):
        r0 = pl.multiple_of(i * HY_TB, HY_TB)
        hf = jnp.dot(h_ref[pl.ds(r0, HY_TB), :], w3_ref[...], precision=HI, preferred_element_type=F32)
        hf = hf * jnp.exp(-z_ref[pl.ds(r0, HY_TB), 0:1] * dec_ref[...])
        o_ref[pl.ds(r0, HY_TB), :] = hf
        return ss + jnp.sum(hf * hf, axis=0, keepdims=True)

    ss = lax.fori_loop(0, nblk, body, jnp.zeros((1, D_BRANCH), F32))
    scale = lax.rsqrt(ss + EPS)

    def norm(i, carry):
        r0 = pl.multiple_of(i * HY_TB, HY_TB)
        o_ref[pl.ds(r0, HY_TB), :] = o_ref[pl.ds(r0, HY_TB), :] * scale
        return carry

    lax.fori_loop(0, nblk, norm, 0)


def _hyfilt(seq, w1, b1, freq, w2, b2, w3):
    t = jnp.linspace(0.0, 1.0, seq, dtype=F32)[:, None]
    bands = (C_EMB - 1) // 2
    w = 2.0 * math.pi * jnp.arange(seq, dtype=F32)[:, None] / seq
    fr = jnp.linspace(1e-4, bands - 1, bands, dtype=F32)[None]
    z = jnp.concatenate([t, jnp.cos(fr * w), -jnp.sin(fr * w)], axis=-1)
    z = jnp.pad(z, ((0, 0), (0, LANES - C_EMB)))
    padm = lambda a, r, c: jnp.pad(a.astype(F32), ((0, r - a.shape[0]), (0, c - a.shape[1])))
    row = lambda a: padm(a.reshape(1, -1), 1, LANES)
    dec = jnp.abs(jnp.linspace(C_MIN_DECAY, C_MAX_DECAY, D_BRANCH, dtype=F32)).reshape(1, D_BRANCH)
    nset = C_ORDER * 2
    return pl.pallas_call(
        _hyfilt_kernel,
        grid=(nset,),
        in_specs=[_const_spec((seq, LANES)), _const_spec((LANES, LANES)), _const_spec((1, LANES)),
                  _const_spec((1, LANES)), _const_spec((LANES, LANES)), _const_spec((1, LANES)),
                  pl.BlockSpec((LANES, D_BRANCH), lambda j: (0, j)), _const_spec((1, D_BRANCH))],
        out_specs=pl.BlockSpec((None, seq, D_BRANCH), lambda j: (j, 0, 0)),
        out_shape=jax.ShapeDtypeStruct((nset, seq, D_BRANCH), F32),
        scratch_shapes=[pltpu.VMEM((seq, LANES), F32)],
        compiler_params=_cparams("arbitrary"),
        name="hyfilt",
    )(z, padm(w1, LANES, LANES), row(b1), row(freq), padm(w2, LANES, LANES), row(b2),
      padm(w3, LANES, nset * D_BRANCH), dec)


def _hyena(uc3, w1, b1, freq, w2, b2, w3, bias):
    bsz, seq, width = uc3.shape
    tb = _fft_tables(seq)
    half = tb["half"]
    npair = bsz // 2
    uc6 = uc3.reshape(2, npair, half, FFT_NT, SUBLANES, width)
    filt = _hyfilt(seq, w1, b1, freq, w2, b2, w3).reshape(C_ORDER * 2, half, FFT_NT, SUBLANES, D_BRANCH)
    hspec = _fft_filt(tb, _fft_outer(tb, filt, False, 3, False))
    src = uc6
    for order in range(C_ORDER):
        d = _fft_mid(tb, _fft_outer(tb, src, True, 1, True), hspec, order)
        src = _ifft_outer(tb, d, src, 0, uc6, order + 1, bias[order])
    return src.reshape(bsz, seq, D_BRANCH)


AT_HALF = 64
AT_TQ = 512
AT_SUB = 128


def _t5_bucket(rel):
    half = N_BUCKETS // 2
    max_exact = half // 2
    n = np.abs(rel)
    large = max_exact + (np.log(np.maximum(n, 1) / max_exact) / math.log(MAX_DISTANCE / max_exact)
                         * (half - max_exact)).astype(np.int64)
    large = np.minimum(large, half - 1)
    return (rel > 0).astype(np.int64) * half + np.where(n < max_exact, n, large)


def _attn_geometry(n):
    tq = min(AT_TQ, n)
    sub = min(AT_SUB, tq)
    if sub + 2 * AT_HALF >= n:
        sub = tq
    win = min(sub + 2 * AT_HALF, n)
    return tq, sub, win, n // tq, n // sub


def _attn_bias_tables(rel_bias, g, dil, n):
    _, sub, win, _, nsb = _attn_geometry(n)
    hs = slice(g * D_HEADS_PER_GROUP, (g + 1) * D_HEADS_PER_GROUP)
    offsets = np.arange(-AT_HALF, AT_HALF + 1) * dil
    onehot = np.zeros((2 * AT_HALF + 1, N_BUCKETS), np.float32)
    onehot[np.arange(2 * AT_HALF + 1), _t5_bucket(offsets)] = 1.0
    band = jnp.dot(rel_bias.astype(F32)[:, hs].T, jnp.asarray(onehot).T, precision=HI)
    nband = 2 * AT_HALF + 1
    lv = sub + win - 1
    tables = []
    for i in sorted({0, min(1, nsb - 1), nsb - 1}):
        ws = int(np.clip(i * sub - AT_HALF, 0, n - win))
        lo = (sub - 1) - (ws - i * sub) - AT_HALF
        v = jnp.pad(band, ((0, 0), (lo, lv - lo - nband)), constant_values=NEG_BIG)
        flat = jnp.tile(v, (1, sub + 1))[:, sub - 1:sub - 1 + sub * (lv - 1)]
        tables.append(flat.reshape(D_HEADS_PER_GROUP, sub, lv - 1)[:, :, :win])
    return jnp.stack(tables)


def _attn_kernel(q_ref, k_ref, v_ref, bias_ref, o_ref, l_ref, *, n, dil):
    tq, sub, win, _, nsb = _attn_geometry(n)
    ncase = bias_ref.shape[0]
    width = q_ref.shape[-1]
    nh = width // D_HEAD_DIM
    hp = pl.program_id(1)
    lane_head = lax.broadcasted_iota(jnp.int32, (sub, width), 1) // D_HEAD_DIM
    hmask = [lane_head == hh for hh in range(nh)]
    whole = win == n
    for r in range(dil):
        if whole:
            kw = k_ref[pl.ds(r, win, stride=dil), :].astype(BF16)
            vw = v_ref[pl.ds(r, win, stride=dil), :].astype(BF16)
        for j in range(tq // sub):
            sidx = pl.program_id(2) * (tq // sub) + j
            case = jnp.minimum(jnp.where(sidx == nsb - 1, ncase - 1, jnp.minimum(sidx, 1)), ncase - 1)
            bias = bias_ref[case, pl.ds(hp * nh, nh)].reshape(nh * sub, win)
            if not whole:
                ws = pl.multiple_of(jnp.clip(sidx * sub - AT_HALF, 0, n - win), AT_HALF)
                kw = k_ref[pl.ds(ws * dil + r, win, stride=dil), :].astype(BF16)
                vw = v_ref[pl.ds(ws * dil + r, win, stride=dil), :].astype(BF16)
            q = q_ref[pl.ds(j * sub * dil + r, sub, stride=dil), :] * (D_HEAD_DIM ** -0.5)
            qs = jnp.concatenate([jnp.where(hmask[hh], q, 0.0) for hh in range(nh)], axis=0).astype(BF16)
            s = lax.dot_general(qs, kw, (((1,), (1,)), ((), ())), preferred_element_type=F32) + bias
            m = jnp.max(s, axis=-1, keepdims=True)
            p = jnp.exp(s - m)
            l = jnp.sum(p, axis=-1, keepdims=True)
            o_all = jnp.dot(p.astype(BF16), vw, preferred_element_type=F32) / l
            lse = m + jnp.log(l)
            o_acc = jnp.zeros((sub, width), F32)
            l_acc = jnp.zeros((sub, width), F32)
            for hh in range(nh):
                o_acc = jnp.where(hmask[hh], o_all[hh * sub:(hh + 1) * sub], o_acc)
                l_acc = jnp.where(hmask[hh], lse[hh * sub:(hh + 1) * sub], l_acc)
            o_ref[pl.ds(j * sub * dil + r, sub, stride=dil), :] = o_acc
            l_ref[pl.ds(j * sub * dil + r, sub, stride=dil), :] = l_acc


def _banded_attention(pd3, g, dil, bias):
    bsz, seq, width = pd3.shape
    n = seq // dil
    tq, sub, win, nq, _ = _attn_geometry(n)
    rows = dil * tq
    bw = LANES if dil > 1 else D_BRANCH
    per = D_BRANCH // bw
    third = width // 3 // bw
    out_spec = pl.BlockSpec((None, rows, bw), lambda b, h, i: (b, i, h))
    shp = jax.ShapeDtypeStruct((bsz, seq, D_BRANCH), F32)
    return pl.pallas_call(
        functools.partial(_attn_kernel, n=n, dil=dil),
        grid=(bsz, per, nq),
        in_specs=[pl.BlockSpec((None, rows, bw), lambda b, h, i: (b, i, g * per + h)),
                  pl.BlockSpec((None, seq, bw), lambda b, h, i: (b, 0, third + g * per + h)),
                  pl.BlockSpec((None, seq, bw), lambda b, h, i: (b, 0, 2 * third + g * per + h)),
                  _const_spec(bias.shape)],
        out_specs=[out_spec, out_spec],
        out_shape=[shp, shp],
        compiler_params=_cparams("parallel", "parallel", "arbitrary"),
        name=f"attn_d{dil}",
    )(pd3, pd3, pd3, bias)


def _dilated_attention(pd3, rel_bias):
    bsz, seq, _ = pd3.shape
    outs, lses = [], []
    for g, (_, dil) in enumerate(D_GROUPS):
        o, l = _banded_attention(pd3, g, dil, _attn_bias_tables(rel_bias, g, dil, seq // dil))
        outs.append(o.reshape(bsz * seq, D_BRANCH))
        lses.append(l.reshape(bsz * seq, D_BRANCH))
    return outs, lses


def kernel(x, norm1_g, w_in, hgrn_lb_logits, hgrn_norm_g, lru_conv_w, lru_conv_b, lru_wa, lru_ba, lru_wx, lru_bx,
           lru_lambda, hy_conv_w, hy_conv_b, hy_w1, hy_b1, hy_freq, hy_w2, hy_b2, hy_w3, hy_bias, rel_bias,
           w_branch, w_gate, b_gate, w_out, norm2_g, w_ff1, w_ff3, w_ff2, final_g):
    bsz, seq, _ = x.shape
    n = bsz * seq
    lb_soft = jax.nn.softmax(hgrn_lb_logits.astype(F32), axis=0)
    lower_bounds = jnp.cumsum(lb_soft, axis=0) - lb_soft[0]
    x2 = x.reshape(n, D_MODEL)
    flat = lambda a: a.reshape(n, D_BRANCH)
    for l in range(DEPTH):
        pa, pb, uc, pd = _inproj(x2, norm1_g[l], w_in[l].astype(BF16), hy_conv_w[l], hy_conv_b[l], seq)
        oa_f, oa_b = _hgrn(pa.reshape(bsz, seq, IN_A), lower_bounds[l])
        yb = _rglru(pb.reshape(bsz, seq, IN_B), lru_conv_w[l], lru_conv_b[l], lru_wa[l], lru_ba[l],
                    lru_wx[l], lru_bx[l], lru_lambda[l])
        yc = _hyena(uc.reshape(bsz, seq, IN_C), hy_w1[l], hy_b1[l], hy_freq[l],
                    hy_w2[l], hy_b2[l], hy_w3[l], hy_bias[l])
        od, ld = _dilated_attention(pd.reshape(bsz, seq, IN_D), rel_bias)
        x2 = _merge(x2, norm1_g[l], flat(oa_f), flat(oa_b), pa, hgrn_norm_g[l], flat(yb), flat(yc), od, ld,
                    w_gate[l].reshape(D_MODEL, N_BRANCH * D_MODEL).astype(BF16),
                    b_gate[l].reshape(1, N_BRANCH * D_MODEL), w_branch[l].astype(BF16), w_out[l].astype(BF16))
        x2 = _ffn(x2, norm2_g[l], *_ffn_weights(w_ff1[l], w_ff3[l], w_ff2[l]), final_g, l == DEPTH - 1)
    return x2.reshape(bsz, seq, D_MODEL)
```

```python
import functools
import math

import jax
import jax.numpy as jnp
import numpy as np
from jax import lax
from jax.experimental import pallas as pl
from jax.experimental.pallas import tpu as pltpu

F32 = jnp.float32
BF16 = jnp.bfloat16
HI = lax.Precision.HIGHEST

D_MODEL = 1024
DEPTH = 2
EPS = 1e-6
TINY = 1e-30
N_BRANCH = 4
D_BRANCH = 256
A_HEADS = 4
A_DK = 64
A_CHUNK = 64
B_BLOCKS = 4
B_BW = 64
B_CONV = 4
LRU_C = 8.0
C_ORDER = 2
C_CONV = 3
C_EMB = 33
C_MIN_DECAY = math.log(1e-2) / 1.5
C_MAX_DECAY = math.log(1e-2) / 0.3
D_GROUPS = ((128, 1), (512, 4), (2048, 16))
D_HEADS_PER_GROUP = 4
D_HEAD_DIM = 64
D_QKV = 768
N_BUCKETS = 32
MAX_DISTANCE = 1024
NEG_BIG = -1e30
D_FF = 2816
IN_A = 5 * D_BRANCH
IN_B = 2 * D_BRANCH
IN_C = 3 * D_BRANCH
IN_D = 3 * D_QKV
IN_WIDTH = IN_A + IN_B + IN_C + IN_D

LANES = 128
SUBLANES = 8
VMEM_LIMIT = 56 * 1024 * 1024
BF16_BITS = 16
BF16_HIGH_MASK = 0xFFFF0000


def _cparams(*sem):
    return pltpu.CompilerParams(dimension_semantics=sem, vmem_limit_bytes=VMEM_LIMIT)


def _const_spec(shape):
    nd = len(shape)
    return pl.BlockSpec(shape, lambda *_: (0,) * nd, pipeline_mode=pl.Buffered(1))


def _rms(x, g):
    return x * lax.rsqrt(jnp.mean(x * x, axis=-1, keepdims=True) + EPS) * g


def _sigmoid(x):
    return 1.0 / (1.0 + jnp.exp(-x))


IN_TM = 512
IN_CHUNK = 256
IN_CONV_ROWS = 256


def _inproj_kernel(x_ref, xp_ref, xn_ref, g_ref, w_ref, cw_ref, cb_ref, oa_ref, ob_ref, oc_ref, od_ref, cs_ref,
                   *, tiles_per_seq):
    i = pl.program_id(0)
    g = g_ref[...]
    h = _rms(x_ref[...], g).astype(BF16)
    off = 0
    for o_ref in (oa_ref, ob_ref, oc_ref, od_ref):
        width = o_ref.shape[-1]
        for c in range(0, width, IN_CHUNK):
            y = jnp.dot(h, w_ref[:, off + c:off + c + IN_CHUNK], preferred_element_type=F32)
            if o_ref is oc_ref:
                cs_ref[SUBLANES:SUBLANES + IN_TM, c:c + IN_CHUNK] = y
            else:
                o_ref[:, c:c + IN_CHUNK] = y
        off += width
    off_c = oa_ref.shape[-1] + ob_ref.shape[-1]
    hh = _rms(jnp.concatenate([xp_ref[...], xn_ref[...]], axis=0), g).astype(BF16)
    halo = jnp.dot(hh, w_ref[:, off_c:off_c + IN_C], preferred_element_type=F32)
    keep_prev = jnp.where(i % tiles_per_seq != 0, 1.0, 0.0)
    keep_next = jnp.where(i % tiles_per_seq != tiles_per_seq - 1, 1.0, 0.0)
    cs_ref[0:SUBLANES, :] = halo[0:SUBLANES] * keep_prev
    cs_ref[SUBLANES + IN_TM:2 * SUBLANES + IN_TM, :] = halo[SUBLANES:2 * SUBLANES] * keep_next
    left = C_CONV // 2
    for r0 in range(0, IN_TM, IN_CONV_ROWS):
        for c in range(0, IN_C, IN_CHUNK):
            win = cs_ref[r0:r0 + IN_CONV_ROWS + 2 * SUBLANES, c:c + IN_CHUNK]
            y = cb_ref[:, c:c + IN_CHUNK]
            for j in range(C_CONV):
                s0 = SUBLANES + j - left
                y = y + cw_ref[j:j + 1, c:c + IN_CHUNK] * win[s0:s0 + IN_CONV_ROWS, :]
            oc_ref[r0:r0 + IN_CONV_ROWS, c:c + IN_CHUNK] = y


def _inproj(x2, g, w_bf16, cw, cb, seq):
    n = x2.shape[0]
    widths = (IN_A, IN_B, IN_C, IN_D)
    per8 = IN_TM // SUBLANES
    last8 = n // SUBLANES - 1
    return pl.pallas_call(
        functools.partial(_inproj_kernel, tiles_per_seq=seq // IN_TM),
        grid=(n // IN_TM,),
        in_specs=[pl.BlockSpec((IN_TM, D_MODEL), lambda i: (i, 0)),
                  pl.BlockSpec((SUBLANES, D_MODEL), lambda i: (jnp.maximum(i * per8 - 1, 0), 0)),
                  pl.BlockSpec((SUBLANES, D_MODEL), lambda i: (jnp.minimum((i + 1) * per8, last8), 0)),
                  _const_spec((1, D_MODEL)),
                  _const_spec((D_MODEL, IN_WIDTH)),
                  _const_spec((C_CONV, IN_C)), _const_spec((1, IN_C))],
        out_specs=[pl.BlockSpec((IN_TM, w), lambda i: (i, 0)) for w in widths],
        out_shape=[jax.ShapeDtypeStruct((n, w), F32) for w in widths],
        scratch_shapes=[pltpu.VMEM((IN_TM + 2 * SUBLANES, IN_C), F32)],
        compiler_params=_cparams("parallel"),
        name="inproj",
    )(x2, x2, x2, g.reshape(1, D_MODEL), w_bf16, cw, cb.reshape(1, IN_C))


FF_TM = 1024
FF_CHUNK = 256
FF_NCHUNK = D_FF // FF_CHUNK


def _ffn_kernel(x_ref, g_ref, w1_ref, w3_ref, w2_ref, fg_ref, o_ref, acc_ref, *, final):
    x = x_ref[...]
    h = _rms(x, g_ref[...]).astype(BF16)
    acc_ref[...] = x

    def body(c, carry):
        c0 = pl.multiple_of(c * FF_CHUNK, FF_CHUNK)
        a = jnp.dot(h, w1_ref[:, pl.ds(c0, FF_CHUNK)], preferred_element_type=F32)
        b = jnp.dot(h, w3_ref[:, pl.ds(c0, FF_CHUNK)], preferred_element_type=F32)
        t = (a * _sigmoid(a) * b).astype(BF16)
        acc_ref[...] += jnp.dot(t, w2_ref[pl.ds(c0, FF_CHUNK), :], preferred_element_type=F32)
        return carry

    lax.fori_loop(0, FF_NCHUNK, body, 0)
    y = acc_ref[...]
    if final:
        y = _rms(y, fg_ref[...])
    o_ref[...] = y


def _ffn(x2, g, w1c, w3c, w2c, final_g, final):
    n = x2.shape[0]
    return pl.pallas_call(
        functools.partial(_ffn_kernel, final=final),
        grid=(n // FF_TM,),
        in_specs=[pl.BlockSpec((FF_TM, D_MODEL), lambda i: (i, 0)),
                  _const_spec((1, D_MODEL)),
                  _const_spec((D_MODEL, D_FF)),
                  _const_spec((D_MODEL, D_FF)),
                  _const_spec((D_FF, D_MODEL)),
                  _const_spec((1, D_MODEL))],
        out_specs=pl.BlockSpec((FF_TM, D_MODEL), lambda i: (i, 0)),
        out_shape=jax.ShapeDtypeStruct((n, D_MODEL), F32),
        scratch_shapes=[pltpu.VMEM((FF_TM, D_MODEL), F32)],
        compiler_params=_cparams("parallel"),
        name="ffn_final" if final else "ffn",
    )(x2, g.reshape(1, D_MODEL), w1c, w3c, w2c, final_g.reshape(1, D_MODEL))


def _ffn_weights(w1, w3, w2):
    return w1.astype(BF16), w3.astype(BF16), w2.astype(BF16)


MG_TM = 512


def _head_ones():
    r = np.arange(D_BRANCH)[:, None] // A_DK
    c = np.arange(D_BRANCH)[None, :] // A_DK
    return jnp.asarray((r == c).astype(np.float32) / A_DK, BF16)


def _merge_kernel(x_ref, g1_ref, oaf_ref, oab_ref, ga_ref, hg_ref, hm_ref, yb_ref, yc_ref,
                  o0_ref, o1_ref, o2_ref, l0_ref, l1_ref, l2_ref,
                  wg_ref, bg_ref, wb_ref, wo_ref, out_ref):
    x = x_ref[...]
    h = _rms(x, g1_ref[...]).astype(BF16)
    oa = oaf_ref[...] + oab_ref[...]
    ms = jnp.dot((oa * oa).astype(BF16), hm_ref[...], preferred_element_type=F32)
    ga = ga_ref[...]
    ya = oa * lax.rsqrt(ms + EPS) * hg_ref[...] * (ga * _sigmoid(ga))
    l0, l1, l2 = l0_ref[...], l1_ref[...], l2_ref[...]
    m = jnp.maximum(jnp.maximum(l0, l1), l2)
    e0, e1, e2 = jnp.exp(l0 - m), jnp.exp(l1 - m), jnp.exp(l2 - m)
    yd = (e0 * o0_ref[...] + e1 * o1_ref[...] + e2 * o2_ref[...]) / (e0 + e1 + e2)
    mixed = None
    for j, y in enumerate((ya, yb_ref[...], yc_ref[...], yd)):
        gate = _sigmoid(jnp.dot(h, wg_ref[:, j * D_MODEL:(j + 1) * D_MODEL], preferred_element_type=F32)
                        + bg_ref[:, j * D_MODEL:(j + 1) * D_MODEL])
        t = gate * jnp.dot(y.astype(BF16), wb_ref[j], preferred_element_type=F32)
        mixed = t if mixed is None else mixed + t
    out_ref[...] = x + jnp.dot(mixed.astype(BF16), wo_ref[...], preferred_element_type=F32)


def _merge(x2, g1, oa_f, oa_b, pa, hg, yb, yc, od, ld, wg, bg, wb, wo):
    n = x2.shape[0]
    tile = lambda w: pl.BlockSpec((MG_TM, w), lambda i: (i, 0))
    return pl.pallas_call(
        _merge_kernel,
        grid=(n // MG_TM,),
        in_specs=[tile(D_MODEL), _const_spec((1, D_MODEL)),
                  tile(D_BRANCH), tile(D_BRANCH),
                  pl.BlockSpec((MG_TM, D_BRANCH), lambda i: (i, 4)),
                  _const_spec((1, D_BRANCH)), _const_spec((D_BRANCH, D_BRANCH)),
                  tile(D_BRANCH), tile(D_BRANCH),
                  tile(D_BRANCH), tile(D_BRANCH), tile(D_BRANCH),
                  tile(D_BRANCH), tile(D_BRANCH), tile(D_BRANCH),
                  _const_spec((D_MODEL, N_BRANCH * D_MODEL)), _const_spec((1, N_BRANCH * D_MODEL)),
                  _const_spec((N_BRANCH, D_BRANCH, D_MODEL)), _const_spec((D_MODEL, D_MODEL))],
        out_specs=tile(D_MODEL),
        out_shape=jax.ShapeDtypeStruct((n, D_MODEL), F32),
        compiler_params=_cparams("parallel"),
        name="merge",
    )(x2, g1.reshape(1, D_MODEL), oa_f, oa_b, pa, hg.reshape(1, D_BRANCH), _head_ones(), yb, yc,
      od[0], od[1], od[2], ld[0], ld[1], ld[2], wg, bg, wb, wo)


HG_TS = 256
HG_NCH = HG_TS // A_CHUNK
HG_MID = A_CHUNK // 2


def _hgrn_tables():
    r = np.arange(HG_TS)[:, None]
    c = np.arange(HG_TS)[None, :]
    same = (r // A_CHUNK) == (c // A_CHUNK)
    rr = np.arange(HG_NCH * A_HEADS * A_CHUNK)[:, None]
    same_s = (rr // (A_HEADS * A_CHUNK)) == (c // A_CHUNK)
    t, sidx = rr % A_CHUNK, c % A_CHUNK
    out = []
    for fwd in (True, False):
        order = (r >= c) if fwd else (r <= c)
        order_s = (t >= sidx) if fwd else (t <= sidx)
        out.append((jnp.asarray((same & order).astype(np.float32), BF16),
                    jnp.asarray((same_s & order_s).astype(np.float32), F32)))
    return out


def _hgrn_prep(q, fl, v, lb, cum, smask, fwd):
    lane_head = lax.broadcasted_iota(jnp.int32, (A_CHUNK, D_BRANCH), 1) // A_DK
    hmask = [(lane_head == hh).astype(F32) for hh in range(A_HEADS)]
    blk_r = lax.broadcasted_iota(jnp.int32, (D_BRANCH, D_BRANCH), 0) // A_DK
    blk_c = lax.broadcasted_iota(jnp.int32, (D_BRANCH, D_BRANCH), 1) // A_DK
    blockdiag = (blk_r == blk_c).astype(F32)
    col_chunk = lax.broadcasted_iota(jnp.int32, (D_BRANCH, HG_TS), 1) // A_CHUNK

    sg = _sigmoid(fl)
    f = lb + (1.0 - lb) * sg
    g = jnp.log(jnp.maximum(f, TINY))
    kk = (1.0 - lb) * (1.0 - sg)
    g1 = g.astype(BF16)
    r1 = g - g1.astype(F32)
    g2 = r1.astype(BF16)
    g3 = (r1 - g2.astype(F32)).astype(BF16)
    b = (jnp.dot(cum, g1, preferred_element_type=F32) + jnp.dot(cum, g2, preferred_element_type=F32)
         + jnp.dot(cum, g3, preferred_element_type=F32))
    rows = lambda c: slice(c * A_CHUNK, (c + 1) * A_CHUNK)
    last = (A_CHUNK - 1) if fwd else 0
    bcast = lambda r0: jnp.concatenate(
        [jnp.broadcast_to(b[c * A_CHUNK + r0:c * A_CHUNK + r0 + 1, :], (A_CHUNK, D_BRANCH)) for c in range(HG_NCH)], 0)
    bm = bcast(HG_MID)
    bl = bcast(last)
    qt = q * jnp.exp(b - bm)
    kt = (kk * jnp.exp(bm - b)).astype(BF16)
    qe = (q * jnp.exp(b)).astype(BF16)
    kh = (kk * jnp.exp(bl - b)).astype(BF16)
    qs = jnp.concatenate([qt[rows(c)] * hmask[hh] for c in range(HG_NCH) for hh in range(A_HEADS)],
                         axis=0).astype(BF16)
    s = lax.dot_general(qs, kt, (((1,), (1,)), ((), ())), preferred_element_type=F32)
    s = (s * smask).astype(BF16)
    ost = jnp.dot(s, v.astype(BF16), preferred_element_type=F32)
    o_intra = []
    for c in range(HG_NCH):
        base = c * A_HEADS * A_CHUNK
        o = ost[base:base + A_CHUNK] * hmask[0]
        for hh in range(1, A_HEADS):
            o = o + ost[base + hh * A_CHUNK:base + (hh + 1) * A_CHUNK] * hmask[hh]
        o_intra.append(o)
    vt = v.T
    lhs = jnp.concatenate([jnp.where(col_chunk == c, vt, 0.0) for c in range(HG_NCH)], axis=0).astype(BF16)
    updall = jnp.dot(lhs, kh, preferred_element_type=F32)
    upd = [updall[c * D_BRANCH:(c + 1) * D_BRANCH] * blockdiag for c in range(HG_NCH)]
    decay = [jnp.exp(b[c * A_CHUNK + last:c * A_CHUNK + last + 1, :]) for c in range(HG_NCH)]
    return o_intra, upd, qe, decay


def _hgrn_scan(o_intra, upd, qe, decay, st, fwd):
    rows = lambda c: slice(c * A_CHUNK, (c + 1) * A_CHUNK)
    outs = [None] * HG_NCH
    for c in (range(HG_NCH) if fwd else range(HG_NCH - 1, -1, -1)):
        outs[c] = o_intra[c] + lax.dot_general(qe[rows(c)], st.astype(BF16), (((1,), (1,)), ((), ())),
                                               preferred_element_type=F32)
        st = st * decay[c] + upd[c]
    return jnp.concatenate(outs, axis=0), st


def _hgrn_kernel(qf_ref, ff_ref, vf_ref, qb_ref, fb_ref, vb_ref, lb_ref, cf_ref, mf_ref, cb_ref, mb_ref,
                 of_ref, ob_ref, sf_ref, sb_ref):
    @pl.when(pl.program_id(1) == 0)
    def _():
        sf_ref[...] = jnp.zeros_like(sf_ref)
        sb_ref[...] = jnp.zeros_like(sb_ref)

    lb = lb_ref[...]
    pf = _hgrn_prep(qf_ref[...], ff_ref[...], vf_ref[...], lb, cf_ref[...], mf_ref[...], True)
    pb = _hgrn_prep(qb_ref[...], fb_ref[...], vb_ref[...], lb, cb_ref[...], mb_ref[...], False)
    o_f, st_f = _hgrn_scan(*pf, sf_ref[...], True)
    o_b, st_b = _hgrn_scan(*pb, sb_ref[...], False)
    of_ref[...] = o_f
    ob_ref[...] = o_b
    sf_ref[...] = st_f
    sb_ref[...] = st_b


def _hgrn(pa3, lb):
    bsz, seq, _ = pa3.shape
    nblk = seq // HG_TS
    blk = (None, HG_TS, D_BRANCH)
    up = lambda col: pl.BlockSpec(blk, lambda b, i: (b, i, col))
    down = lambda col: pl.BlockSpec(blk, lambda b, i: (b, nblk - 1 - i, col))
    shp = jax.ShapeDtypeStruct((bsz, seq, D_BRANCH), F32)
    (cum_f, sm_f), (cum_b, sm_b) = _hgrn_tables()
    mshape = (HG_NCH * A_HEADS * A_CHUNK, HG_TS)
    return pl.pallas_call(
        _hgrn_kernel,
        grid=(bsz, nblk),
        in_specs=[up(0), up(1), up(3), down(0), down(2), down(3), _const_spec((1, D_BRANCH)),
                  _const_spec((HG_TS, HG_TS)), _const_spec(mshape), _const_spec((HG_TS, HG_TS)), _const_spec(mshape)],
        out_specs=[up(0), down(0)],
        out_shape=[shp, shp],
        scratch_shapes=[pltpu.VMEM((D_BRANCH, D_BRANCH), F32), pltpu.VMEM((D_BRANCH, D_BRANCH), F32)],
        compiler_params=_cparams("parallel", "arbitrary"),
        name="hgrn2",
    )(pa3, pa3, pa3, pa3, pa3, pa3, lb.reshape(1, D_BRANCH), cum_f, sm_f, cum_b, sm_b)


RG_TB = 128
RG_PAD = SUBLANES
RG_LEFT = B_CONV // 2


def _dot3_rhs(x, wh, wl):
    xh = x.astype(BF16)
    xl = (x - xh.astype(F32)).astype(BF16)
    return (jnp.dot(xh, wh, preferred_element_type=F32) + jnp.dot(xl, wh, preferred_element_type=F32)
            + jnp.dot(xh, wl, preferred_element_type=F32))


def _group_scan(a, u, fwd):
    row = lax.broadcasted_iota(jnp.int32, a.shape, 1)
    k = 1
    while k < SUBLANES:
        if fwd:
            keep = row >= k
            us, as_ = pltpu.roll(u, k, 1), pltpu.roll(a, k, 1)
        else:
            keep = row < SUBLANES - k
            us, as_ = pltpu.roll(u, SUBLANES - k, 1), pltpu.roll(a, SUBLANES - k, 1)
        u = a * jnp.where(keep, us, 0.0) + u
        a = a * jnp.where(keep, as_, 1.0)
        k *= 2
    return a, u


def _block_scan(a, u, carry, fwd):
    t = a.shape[0]
    ngrp = t // SUBLANES
    ag, ug = _group_scan(a.reshape(ngrp, SUBLANES, D_BRANCH), u.reshape(ngrp, SUBLANES, D_BRANCH), fwd)
    hs = [None] * ngrp
    for g in (range(ngrp) if fwd else range(ngrp - 1, -1, -1)):
        h = ug[g] + ag[g] * carry
        hs[g] = h
        carry = h[SUBLANES - 1:SUBLANES, :] if fwd else h[0:1, :]
    return jnp.concatenate(hs, axis=0), carry


def _gelu_tanh(x):
    return 0.5 * x * (1.0 + jnp.tanh(math.sqrt(2.0 / math.pi) * (x + 0.044715 * (x * x * x))))


def _rglru_kernel(x_ref, gt_ref, cw_ref, cb_ref, wh_ref, wl_ref, bg_ref, lam_ref, o_ref, xp_ref, xc_ref):
    seq = x_ref.shape[0]
    nblk = seq // RG_TB
    xp_ref[0:RG_PAD, :] = jnp.zeros((RG_PAD, D_BRANCH), F32)
    xp_ref[RG_PAD + seq:2 * RG_PAD + seq, :] = jnp.zeros((RG_PAD, D_BRANCH), F32)
    xp_ref[RG_PAD:RG_PAD + seq, :] = x_ref[...]
    nl = -lam_ref[...]
    sp = jnp.maximum(nl, 0.0) + jnp.log(1.0 + jnp.exp(-jnp.abs(nl)))

    def block(i, carry, dirn):
        r0 = pl.multiple_of(i * RG_TB, RG_TB)
        if dirn == 0:
            win = xp_ref[pl.ds(r0, RG_TB + 2 * RG_PAD), :]
            xc = cb_ref[...]
            for j in range(B_CONV):
                s0 = RG_PAD + j - RG_LEFT
                xc = xc + cw_ref[j:j + 1, :] * win[s0:s0 + RG_TB, :]
            xc_ref[pl.ds(r0, RG_TB), :] = xc
        else:
            xc = xc_ref[pl.ds(r0, RG_TB), :]
        cols = slice(dirn * 2 * D_BRANCH, (dirn + 1) * 2 * D_BRANCH)
        gates = _dot3_rhs(xc, wh_ref[:, cols], wl_ref[:, cols]) + bg_ref[:, cols]
        r = _sigmoid(gates[:, :D_BRANCH])
        ig = _sigmoid(gates[:, D_BRANCH:])
        log_a = -LRU_C * r * sp[dirn:dirn + 1, :]
        a = jnp.exp(log_a)
        u = jnp.sqrt(jnp.maximum(-jnp.tanh(log_a) * (a * a + 1.0), 0.0)) * ig * xc
        h, carry = _block_scan(a, u, carry, dirn == 0)
        if dirn == 0:
            o_ref[pl.ds(r0, RG_TB), :] = h
        else:
            o_ref[pl.ds(r0, RG_TB), :] = (o_ref[pl.ds(r0, RG_TB), :] + h) * _gelu_tanh(gt_ref[pl.ds(r0, RG_TB), :])
        return carry

    zero = jnp.zeros((1, D_BRANCH), F32)
    lax.fori_loop(0, nblk, lambda i, c: block(i, c, 0), zero)
    lax.fori_loop(0, nblk, lambda i, c: block(nblk - 1 - i, c, 1), zero)


def _blockdiag(w):
    eye = jnp.eye(B_BLOCKS, dtype=w.dtype)
    return jnp.einsum('ncd,nm->ncmd', w, eye).reshape(D_BRANCH, D_BRANCH)


def _rglru(pb3, cw, cb, wa, ba, wx, bx, lam):
    bsz, seq, _ = pb3.shape
    wg = jnp.concatenate([_blockdiag(wa[0]), _blockdiag(wx[0]), _blockdiag(wa[1]), _blockdiag(wx[1])], axis=1)
    bg = jnp.concatenate([ba[0], bx[0], ba[1], bx[1]]).reshape(1, 4 * D_BRANCH)
    wh = wg.astype(BF16)
    wl = (wg - wh.astype(F32)).astype(BF16)
    blk = (None, seq, D_BRANCH)
    return pl.pallas_call(
        _rglru_kernel,
        grid=(bsz,),
        in_specs=[pl.BlockSpec(blk, lambda b: (b, 0, 0)),
                  pl.BlockSpec(blk, lambda b: (b, 0, 1)),
                  _const_spec((B_CONV, D_BRANCH)), _const_spec((1, D_BRANCH)),
                  _const_spec((D_BRANCH, 4 * D_BRANCH)), _const_spec((D_BRANCH, 4 * D_BRANCH)),
                  _const_spec((1, 4 * D_BRANCH)), _const_spec((2, D_BRANCH))],
        out_specs=pl.BlockSpec(blk, lambda b: (b, 0, 0)),
        out_shape=jax.ShapeDtypeStruct((bsz, seq, D_BRANCH), F32),
        scratch_shapes=[pltpu.VMEM((seq + 2 * RG_PAD, D_BRANCH), F32), pltpu.VMEM((seq, D_BRANCH), F32)],
        compiler_params=_cparams("parallel"),
        name="rglru",
    )(pb3, pb3, cw, cb.reshape(1, D_BRANCH), wh, wl, bg, lam)


FFT_IN = 128
FFT_NT = FFT_IN // SUBLANES
HY_TB = 256


def _split_np(a):
    a = np.asarray(a, np.float32)
    hi = a.astype(jnp.bfloat16)
    lo = (a - hi.astype(np.float32)).astype(jnp.bfloat16)
    return jnp.asarray(hi), jnp.asarray(lo)


def _dot3(mh, ml, x):
    xh = x.astype(BF16)
    xl = (x - xh.astype(F32)).astype(BF16)
    return (jnp.dot(mh, xh, preferred_element_type=F32) + jnp.dot(mh, xl, preferred_element_type=F32)
            + jnp.dot(ml, xh, preferred_element_type=F32))


def _dotp(mh, ml, x, passes):
    if passes == 1:
        return jnp.dot(mh, x.astype(BF16), preferred_element_type=F32)
    return _dot3(mh, ml, x)


def _pack_pair(re, im):
    rb = lax.bitcast_convert_type(re.astype(BF16).astype(F32), jnp.uint32)
    ib = lax.bitcast_convert_type(im.astype(BF16).astype(F32), jnp.uint32)
    return rb | (ib >> BF16_BITS)


def _unpack_pair(w):
    re = lax.bitcast_convert_type(w & jnp.uint32(BF16_HIGH_MASK), F32)
    im = lax.bitcast_convert_type(w << BF16_BITS, F32)
    return re.astype(BF16), im.astype(BF16)


def _fft_tables(seq):
    n = 2 * seq
    n1 = n // FFT_IN
    half = n1 // 2
    eye = np.eye(SUBLANES)
    a = 2.0 * np.pi * np.outer(np.arange(n1), np.arange(half)) / n1
    gr, gi = np.cos(a), -np.sin(a)
    blk = np.stack([np.stack([gr, -gi], axis=1), np.stack([gi, gr], axis=1)], axis=0)
    m_out = np.einsum('rkis,cd->rkcisd', blk, eye).reshape(2 * n1 * SUBLANES, 2 * half * SUBLANES)
    ir, ii = gr.T / n, -gi.T / n
    blk = np.stack([np.stack([ir, -ii], axis=1), np.stack([ii, ir], axis=1)], axis=0)
    m_inv = np.einsum('otrk,cd->otcrkd', blk, eye).reshape(2 * half * SUBLANES, 2 * n1 * SUBLANES)
    a = 2.0 * np.pi * np.outer(np.arange(FFT_IN), np.arange(FFT_IN)) / FFT_IN
    fr, fi = np.cos(a), -np.sin(a)
    w_fwd = np.block([[fr, -fi], [fi, fr]])
    w_inv = np.block([[fr, fi], [-fi, fr]])
    s_in = SUBLANES * np.arange(FFT_NT)[:, None, None] + np.arange(SUBLANES)[None, None, :]
    th = 2.0 * np.pi * s_in * np.arange(n1)[None, :, None] / n
    tw = (jnp.asarray(np.cos(th)[..., None], F32), jnp.asarray(-np.sin(th)[..., None], F32))
    th = 2.0 * np.pi * np.outer(np.arange(n1), np.arange(FFT_IN)) / n
    tw_in = (jnp.asarray(np.cos(th)[..., None], F32), jnp.asarray(-np.sin(th)[..., None], F32))
    return dict(m_out=_split_np(m_out), m_inv=_split_np(m_inv), w_fwd=_split_np(w_fwd), w_inv=_split_np(w_inv),
                tw=tw, tw_in=tw_in, n1=n1, half=half)


FFT_PP = 2


def _tw_spec(n1):
    return pl.BlockSpec((None, n1, SUBLANES, 1), lambda t, q: (t, 0, 0, 0))


def _fft_outer_kernel(mh_ref, ml_ref, twr_ref, twi_ref, z_ref, v_ref, *, complex_in, passes, packed):
    n1 = v_ref.shape[1]
    tr, ti = twr_ref[...], twi_ref[...]
    for pp in range(FFT_PP):
        zz = z_ref[:, pp] if complex_in else z_ref[pp]
        rows_in = math.prod(zz.shape[:-1])
        v = _dotp(mh_ref[:, :rows_in], ml_ref[:, :rows_in], zz.reshape(rows_in, D_BRANCH), passes)
        vr = v[:n1 * SUBLANES].reshape(n1, SUBLANES, D_BRANCH)
        vi = v[n1 * SUBLANES:].reshape(n1, SUBLANES, D_BRANCH)
        wr, wi = vr * tr - vi * ti, vr * ti + vi * tr
        if packed:
            v_ref[pp] = _pack_pair(wr, wi)
        else:
            v_ref[pp, :, 0] = wr
            v_ref[pp, :, 1] = wi


def _fft_outer(tb, z, complex_in, passes, packed):
    n1, half = tb["n1"], tb["half"]
    if complex_in:
        p = z.shape[1]
        zspec = pl.BlockSpec((2, FFT_PP, half, None, SUBLANES, D_BRANCH), lambda t, q: (0, q, 0, t, 0, 0))
    else:
        p = z.shape[0]
        zspec = pl.BlockSpec((FFT_PP, half, None, SUBLANES, D_BRANCH), lambda t, q: (q, 0, t, 0, 0))
    mshape = (2 * n1 * SUBLANES, 2 * half * SUBLANES)
    if packed:
        out_spec = pl.BlockSpec((FFT_PP, n1, None, SUBLANES, D_BRANCH), lambda t, q: (q, 0, t, 0, 0))
        out_shape = jax.ShapeDtypeStruct((p, n1, FFT_NT, SUBLANES, D_BRANCH), jnp.uint32)
    else:
        out_spec = pl.BlockSpec((FFT_PP, n1, 2, None, SUBLANES, D_BRANCH), lambda t, q: (q, 0, 0, t, 0, 0))
        out_shape = jax.ShapeDtypeStruct((p, n1, 2, FFT_NT, SUBLANES, D_BRANCH), F32)
    return pl.pallas_call(
        functools.partial(_fft_outer_kernel, complex_in=complex_in, passes=passes, packed=packed),
        grid=(FFT_NT, p // FFT_PP),
        in_specs=[_const_spec(mshape), _const_spec(mshape), _tw_spec(n1), _tw_spec(n1), zspec],
        out_specs=out_spec,
        out_shape=out_shape,
        compiler_params=_cparams("arbitrary", "arbitrary"),
        name="fft_outer_c" if complex_in else "fft_outer_r",
    )(*tb["m_out"], *tb["tw"], z)


FFT_KB = 16


def _fft_filt_kernel(wh_ref, wl_ref, vf_ref, vb_ref, h_ref):
    for kb in range(FFT_KB):
        zf = _dot3(wh_ref[...], wl_ref[...], vf_ref[kb].reshape(2 * FFT_IN, D_BRANCH))
        zb = _dot3(wh_ref[...], wl_ref[...], vb_ref[kb].reshape(2 * FFT_IN, D_BRANCH))
        h_ref[kb, 0] = zf[:FFT_IN] + zb[:FFT_IN]
        h_ref[kb, 1] = zf[FFT_IN:] - zb[FFT_IN:]


def _fft_filt(tb, v):
    n1 = v.shape[1]
    wspec = _const_spec((2 * FFT_IN, 2 * FFT_IN))
    vblk = (None, FFT_KB, 2, FFT_NT, SUBLANES, D_BRANCH)
    return pl.pallas_call(
        _fft_filt_kernel,
        grid=(C_ORDER, n1 // FFT_KB),
        in_specs=[wspec, wspec,
                  pl.BlockSpec(vblk, lambda o, k: (2 * o, k, 0, 0, 0, 0)),
                  pl.BlockSpec(vblk, lambda o, k: (2 * o + 1, k, 0, 0, 0, 0))],
        out_specs=pl.BlockSpec((None, FFT_KB, 2, FFT_IN, D_BRANCH), lambda o, k: (o, k, 0, 0, 0)),
        out_shape=jax.ShapeDtypeStruct((C_ORDER, n1, 2, FFT_IN, D_BRANCH), F32),
        compiler_params=_cparams("parallel", "parallel"),
        name="fft_filt",
    )(*tb["w_fwd"], v, v)


def _fft_mid_kernel(wf_ref, wi_ref, twr_ref, twi_ref, v_ref, h_ref, d_ref):
    for kb in range(FFT_KB):
        vr, vi = _unpack_pair(v_ref[kb].reshape(FFT_IN, D_BRANCH))
        z = jnp.dot(wf_ref[...], jnp.concatenate([vr, vi], axis=0), preferred_element_type=F32)
        zr, zi = z[:FFT_IN], z[FFT_IN:]
        hr, hi = h_ref[kb, 0], h_ref[kb, 1]
        pr = (zr * hr - zi * hi).astype(BF16)
        pi = (zr * hi + zi * hr).astype(BF16)
        d = jnp.dot(wi_ref[...], jnp.concatenate([pr, pi], axis=0), preferred_element_type=F32)
        dr, di = d[:FFT_IN], d[FFT_IN:]
        tr, ti = twr_ref[kb], twi_ref[kb]
        er, ei = dr * tr + di * ti, di * tr - dr * ti
        d_ref[kb] = _pack_pair(er, ei).reshape(d_ref.shape[1:])


def _fft_mid(tb, v, hspec, order):
    p, n1 = v.shape[:2]
    vspec = pl.BlockSpec((None, FFT_KB, FFT_NT, SUBLANES, D_BRANCH), lambda k, q: (q, k, 0, 0, 0))
    wspec = _const_spec((2 * FFT_IN, 2 * FFT_IN))
    tspec = pl.BlockSpec((FFT_KB, FFT_IN, 1), lambda k, q: (k, 0, 0))
    return pl.pallas_call(
        _fft_mid_kernel,
        grid=(n1 // FFT_KB, p),
        in_specs=[wspec, wspec, tspec, tspec, vspec,
                  pl.BlockSpec((None, FFT_KB, 2, FFT_IN, D_BRANCH), lambda k, q: (order, k, 0, 0, 0))],
        out_specs=vspec,
        out_shape=jax.ShapeDtypeStruct(v.shape, jnp.uint32),
        compiler_params=_cparams("parallel", "arbitrary"),
        name="fft_mid",
    )(tb["w_fwd"][0], tb["w_inv"][0], *tb["tw_in"], v, hspec)


def _ifft_outer_kernel(mh_ref, d_ref, u_ref, x_ref, b_ref, o_ref):
    n1 = d_ref.shape[1]
    for pp in range(FFT_PP):
        er, ei = _unpack_pair(d_ref[pp])
        e = jnp.concatenate([er.reshape(n1 * SUBLANES, D_BRANCH), ei.reshape(n1 * SUBLANES, D_BRANCH)], axis=0)
        y = jnp.dot(mh_ref[...], e, preferred_element_type=F32).reshape((2,) + o_ref.shape[2:])
        o_ref[:, pp] = x_ref[:, pp] * (y + u_ref[:, pp] * b_ref[...])


def _ifft_outer(tb, d, u, ucol, x, xcol, bias):
    n1, half = tb["n1"], tb["half"]
    p = d.shape[0]
    mshape = (2 * half * SUBLANES, 2 * n1 * SUBLANES)
    io = lambda col: pl.BlockSpec((2, FFT_PP, half, None, SUBLANES, D_BRANCH), lambda t, q: (0, q, 0, t, 0, col))
    return pl.pallas_call(
        _ifft_outer_kernel,
        grid=(FFT_NT, p // FFT_PP),
        in_specs=[_const_spec(mshape),
                  pl.BlockSpec((FFT_PP, n1, None, SUBLANES, D_BRANCH), lambda t, q: (q, 0, t, 0, 0)),
                  io(ucol), io(xcol), _const_spec((1, D_BRANCH))],
        out_specs=io(0),
        out_shape=jax.ShapeDtypeStruct((2, p, half, FFT_NT, SUBLANES, D_BRANCH), F32),
        compiler_params=_cparams("arbitrary", "arbitrary"),
        name="ifft_outer",
    )(tb["m_inv"][0], d, u, x, bias.reshape(1, D_BRANCH))


def _hyfilt_kernel(z_ref, w1_ref, b1_ref, fr_ref, w2_ref, b2_ref, w3_ref, dec_ref, o_ref, h_ref):
    seq = z_ref.shape[0]
    nblk = seq // HY_TB
    fr = fr_ref[...]

    @pl.when(pl.program_id(0) == 0)
    def _():
        def hidden(i, carry):
            r0 = pl.multiple_of(i * HY_TB, HY_TB)
            zb = z_ref[pl.ds(r0, HY_TB), :]
            h = jnp.sin(fr * (jnp.dot(zb, w1_ref[...], precision=HI, preferred_element_type=F32) + b1_ref[...]))
            h = jnp.sin(fr * (jnp.dot(h, w2_ref[...], precision=HI, preferred_element_type=F32) + b2_ref[...]))
            h_ref[pl.ds(r0, HY_TB), :] = h
            return carry

        lax.fori_loop(0, nblk, hidden, 0)

    def body(i, ss):
        r0 = pl.multiple_of(i * HY_TB, HY_TB)
        hf = jnp.dot(h_ref[pl.ds(r0, HY_TB), :], w3_ref[...], precision=HI, preferred_element_type=F32)
        hf = hf * jnp.exp(-z_ref[pl.ds(r0, HY_TB), 0:1] * dec_ref[...])
        o_ref[pl.ds(r0, HY_TB), :] = hf
        return ss + jnp.sum(hf * hf, axis=0, keepdims=True)

    ss = lax.fori_loop(0, nblk, body, jnp.zeros((1, D_BRANCH), F32))
    scale = lax.rsqrt(ss + EPS)

    def norm(i, carry):
        r0 = pl.multiple_of(i * HY_TB, HY_TB)
        o_ref[pl.ds(r0, HY_TB), :] = o_ref[pl.ds(r0, HY_TB), :] * scale
        return carry

    lax.fori_loop(0, nblk, norm, 0)


def _hyfilt(seq, w1, b1, freq, w2, b2, w3):
    t = jnp.linspace(0.0, 1.0, seq, dtype=F32)[:, None]
    bands = (C_EMB - 1) // 2
    w = 2.0 * math.pi * jnp.arange(seq, dtype=F32)[:, None] / seq
    fr = jnp.linspace(1e-4, bands - 1, bands, dtype=F32)[None]
    z = jnp.concatenate([t, jnp.cos(fr * w), -jnp.sin(fr * w)], axis=-1)
    z = jnp.pad(z, ((0, 0), (0, LANES - C_EMB)))
    padm = lambda a, r, c: jnp.pad(a.astype(F32), ((0, r - a.shape[0]), (0, c - a.shape[1])))
    row = lambda a: padm(a.reshape(1, -1), 1, LANES)
    dec = jnp.abs(jnp.linspace(C_MIN_DECAY, C_MAX_DECAY, D_BRANCH, dtype=F32)).reshape(1, D_BRANCH)
    nset = C_ORDER * 2
    return pl.pallas_call(
        _hyfilt_kernel,
        grid=(nset,),
        in_specs=[_const_spec((seq, LANES)), _const_spec((LANES, LANES)), _const_spec((1, LANES)),
                  _const_spec((1, LANES)), _const_spec((LANES, LANES)), _const_spec((1, LANES)),
                  pl.BlockSpec((LANES, D_BRANCH), lambda j: (0, j)), _const_spec((1, D_BRANCH))],
        out_specs=pl.BlockSpec((None, seq, D_BRANCH), lambda j: (j, 0, 0)),
        out_shape=jax.ShapeDtypeStruct((nset, seq, D_BRANCH), F32),
        scratch_shapes=[pltpu.VMEM((seq, LANES), F32)],
        compiler_params=_cparams("arbitrary"),
        name="hyfilt",
    )(z, padm(w1, LANES, LANES), row(b1), row(freq), padm(w2, LANES, LANES), row(b2),
      padm(w3, LANES, nset * D_BRANCH), dec)


def _hyena(uc3, w1, b1, freq, w2, b2, w3, bias):
    bsz, seq, width = uc3.shape
    tb = _fft_tables(seq)
    half = tb["half"]
    npair = bsz // 2
    uc6 = uc3.reshape(2, npair, half, FFT_NT, SUBLANES, width)
    filt = _hyfilt(seq, w1, b1, freq, w2, b2, w3).reshape(C_ORDER * 2, half, FFT_NT, SUBLANES, D_BRANCH)
    hspec = _fft_filt(tb, _fft_outer(tb, filt, False, 3, False))
    src = uc6
    for order in range(C_ORDER):
        d = _fft_mid(tb, _fft_outer(tb, src, True, 1, True), hspec, order)
        src = _ifft_outer(tb, d, src, 0, uc6, order + 1, bias[order])
    return src.reshape(bsz, seq, D_BRANCH)


AT_HALF = 64
AT_TQ = 1024
AT_SUB = 128


def _t5_bucket(rel):
    half = N_BUCKETS // 2
    max_exact = half // 2
    n = np.abs(rel)
    large = max_exact + (np.log(np.maximum(n, 1) / max_exact) / math.log(MAX_DISTANCE / max_exact)
                         * (half - max_exact)).astype(np.int64)
    large = np.minimum(large, half - 1)
    return (rel > 0).astype(np.int64) * half + np.where(n < max_exact, n, large)


def _attn_geometry(n):
    tq = min(AT_TQ, n)
    sub = min(AT_SUB, tq)
    if sub + 2 * AT_HALF >= n:
        sub = tq
    win = min(sub + 2 * AT_HALF, n)
    return tq, sub, win, n // tq, n // sub


def _attn_bias_tables(rel_bias, g, dil, n):
    _, sub, win, _, nsb = _attn_geometry(n)
    hs = slice(g * D_HEADS_PER_GROUP, (g + 1) * D_HEADS_PER_GROUP)
    offsets = np.arange(-AT_HALF, AT_HALF + 1) * dil
    onehot = np.zeros((2 * AT_HALF + 1, N_BUCKETS), np.float32)
    onehot[np.arange(2 * AT_HALF + 1), _t5_bucket(offsets)] = 1.0
    band = jnp.dot(rel_bias.astype(F32)[:, hs].T, jnp.asarray(onehot).T, precision=HI)
    nband = 2 * AT_HALF + 1
    lv = sub + win - 1
    tables = []
    for i in sorted({0, min(1, nsb - 1), nsb - 1}):
        ws = int(np.clip(i * sub - AT_HALF, 0, n - win))
        lo = (sub - 1) - (ws - i * sub) - AT_HALF
        v = jnp.pad(band, ((0, 0), (lo, lv - lo - nband)), constant_values=NEG_BIG)
        flat = jnp.tile(v, (1, sub + 1))[:, sub - 1:sub - 1 + sub * (lv - 1)]
        tables.append(flat.reshape(D_HEADS_PER_GROUP, sub, lv - 1)[:, :, :win])
    return jnp.stack(tables)


def _attn_kernel(q_ref, k_ref, v_ref, bias_ref, o_ref, l_ref, *, n, dil):
    tq, sub, win, _, nsb = _attn_geometry(n)
    ncase = bias_ref.shape[0]
    width = q_ref.shape[-1]
    nh = width // D_HEAD_DIM
    hp = pl.program_id(1)
    lane_head = lax.broadcasted_iota(jnp.int32, (sub, width), 1) // D_HEAD_DIM
    hmask = [lane_head == hh for hh in range(nh)]
    whole = win == n
    for r in range(dil):
        if whole:
            kw = k_ref[pl.ds(r, win, stride=dil), :].astype(BF16)
            vw = v_ref[pl.ds(r, win, stride=dil), :].astype(BF16)
        for j in range(tq // sub):
            sidx = pl.program_id(2) * (tq // sub) + j
            case = jnp.minimum(jnp.where(sidx == nsb - 1, ncase - 1, jnp.minimum(sidx, 1)), ncase - 1)
            bias = bias_ref[case, pl.ds(hp * nh, nh)].reshape(nh * sub, win)
            if not whole:
                ws = pl.multiple_of(jnp.clip(sidx * sub - AT_HALF, 0, n - win), AT_HALF)
                kw = k_ref[pl.ds(ws * dil + r, win, stride=dil), :].astype(BF16)
                vw = v_ref[pl.ds(ws * dil + r, win, stride=dil), :].astype(BF16)
            q = q_ref[pl.ds(j * sub * dil + r, sub, stride=dil), :] * (D_HEAD_DIM ** -0.5)
            qs = jnp.concatenate([jnp.where(hmask[hh], q, 0.0) for hh in range(nh)], axis=0).astype(BF16)
            s = lax.dot_general(qs, kw, (((1,), (1,)), ((), ())), preferred_element_type=F32) + bias
            m = jnp.max(s, axis=-1, keepdims=True)
            p = jnp.exp(s - m)
            l = jnp.sum(p, axis=-1, keepdims=True)
            o_all = jnp.dot(p.astype(BF16), vw, preferred_element_type=F32) / l
            lse = m + jnp.log(l)
            o_acc = jnp.zeros((sub, width), F32)
            l_acc = jnp.zeros((sub, width), F32)
            for hh in range(nh):
                o_acc = jnp.where(hmask[hh], o_all[hh * sub:(hh + 1) * sub], o_acc)
                l_acc = jnp.where(hmask[hh], lse[hh * sub:(hh + 1) * sub], l_acc)
            o_ref[pl.ds(j * sub * dil + r, sub, stride=dil), :] = o_acc
            l_ref[pl.ds(j * sub * dil + r, sub, stride=dil), :] = l_acc


def _banded_attention(pd3, g, dil, bias):
    bsz, seq, width = pd3.shape
    n = seq // dil
    tq, sub, win, nq, _ = _attn_geometry(n)
    rows = dil * tq
    bw = LANES if dil > 1 else D_BRANCH
    per = D_BRANCH // bw
    third = width // 3 // bw
    out_spec = pl.BlockSpec((None, rows, bw), lambda b, h, i: (b, i, h))
    shp = jax.ShapeDtypeStruct((bsz, seq, D_BRANCH), F32)
    return pl.pallas_call(
        functools.partial(_attn_kernel, n=n, dil=dil),
        grid=(bsz, per, nq),
        in_specs=[pl.BlockSpec((None, rows, bw), lambda b, h, i: (b, i, g * per + h)),
                  pl.BlockSpec((None, seq, bw), lambda b, h, i: (b, 0, third + g * per + h)),
                  pl.BlockSpec((None, seq, bw), lambda b, h, i: (b, 0, 2 * third + g * per + h)),
                  _const_spec(bias.shape)],
        out_specs=[out_spec, out_spec],
        out_shape=[shp, shp],
        compiler_params=_cparams("parallel", "parallel", "arbitrary"),
        name=f"attn_d{dil}",
    )(pd3, pd3, pd3, bias)


def _dilated_attention(pd3, rel_bias):
    bsz, seq, _ = pd3.shape
    outs, lses = [], []
    for g, (_, dil) in enumerate(D_GROUPS):
        o, l = _banded_attention(pd3, g, dil, _attn_bias_tables(rel_bias, g, dil, seq // dil))
        outs.append(o.reshape(bsz * seq, D_BRANCH))
        lses.append(l.reshape(bsz * seq, D_BRANCH))
    return outs, lses


def kernel(x, norm1_g, w_in, hgrn_lb_logits, hgrn_norm_g, lru_conv_w, lru_conv_b, lru_wa, lru_ba, lru_wx, lru_bx,
           lru_lambda, hy_conv_w, hy_conv_b, hy_w1, hy_b1, hy_freq, hy_w2, hy_b2, hy_w3, hy_bias, rel_bias,
           w_branch, w_gate, b_gate, w_out, norm2_g, w_ff1, w_ff3, w_ff2, final_g):
    bsz, seq, _ = x.shape
    n = bsz * seq
    lb_soft = jax.nn.softmax(hgrn_lb_logits.astype(F32), axis=0)
    lower_bounds = jnp.cumsum(lb_soft, axis=0) - lb_soft[0]
    x2 = x.reshape(n, D_MODEL)
    flat = lambda a: a.reshape(n, D_BRANCH)
    for l in range(DEPTH):
        pa, pb, uc, pd = _inproj(x2, norm1_g[l], w_in[l].astype(BF16), hy_conv_w[l], hy_conv_b[l], seq)
        oa_f, oa_b = _hgrn(pa.reshape(bsz, seq, IN_A), lower_bounds[l])
        yb = _rglru(pb.reshape(bsz, seq, IN_B), lru_conv_w[l], lru_conv_b[l], lru_wa[l], lru_ba[l],
                    lru_wx[l], lru_bx[l], lru_lambda[l])
        yc = _hyena(uc.reshape(bsz, seq, IN_C), hy_w1[l], hy_b1[l], hy_freq[l],
                    hy_w2[l], hy_b2[l], hy_w3[l], hy_bias[l])
        od, ld = _dilated_attention(pd.reshape(bsz, seq, IN_D), rel_bias)
        x2 = _merge(x2, norm1_g[l], flat(oa_f), flat(oa_b), pa, hgrn_norm_g[l], flat(yb), flat(yc), od, ld,
                    w_gate[l].reshape(D_MODEL, N_BRANCH * D_MODEL).astype(BF16),
                    b_gate[l].reshape(1, N_BRANCH * D_MODEL), w_branch[l].astype(BF16), w_out[l].astype(BF16))
        x2 = _ffn(x2, norm2_g[l], *_ffn_weights(w_ff1[l], w_ff3[l], w_ff2[l]), final_g, l == DEPTH - 1)
    return x2.reshape(bsz, seq, D_MODEL)
```

```python
import functools
import math

import jax
import jax.numpy as jnp
import numpy as np
from jax import lax
from jax.experimental import pallas as pl
from jax.experimental.pallas import tpu as pltpu

F32 = jnp.float32
BF16 = jnp.bfloat16
HI = lax.Precision.HIGHEST

D_MODEL = 1024
DEPTH = 2
EPS = 1e-6
TINY = 1e-30
N_BRANCH = 4
D_BRANCH = 256
A_HEADS = 4
A_DK = 64
A_CHUNK = 64
B_BLOCKS = 4
B_BW = 64
B_CONV = 4
LRU_C = 8.0
C_ORDER = 2
C_CONV = 3
C_EMB = 33
C_MIN_DECAY = math.log(1e-2) / 1.5
C_MAX_DECAY = math.log(1e-2) / 0.3
D_GROUPS = ((128, 1), (512, 4), (2048, 16))
D_HEADS_PER_GROUP = 4
D_HEAD_DIM = 64
D_QKV = 768
N_BUCKETS = 32
MAX_DISTANCE = 1024
NEG_BIG = -1e30
D_FF = 2816
IN_A = 5 * D_BRANCH
IN_B = 2 * D_BRANCH
IN_C = 3 * D_BRANCH
IN_D = 3 * D_QKV
IN_WIDTH = IN_A + IN_B + IN_C + IN_D

LANES = 128
SUBLANES = 8
VMEM_LIMIT = 56 * 1024 * 1024
BF16_BITS = 16
BF16_HIGH_MASK = 0xFFFF0000


def _cparams(*sem):
    return pltpu.CompilerParams(dimension_semantics=sem, vmem_limit_bytes=VMEM_LIMIT)


def _const_spec(shape):
    nd = len(shape)
    return pl.BlockSpec(shape, lambda *_: (0,) * nd, pipeline_mode=pl.Buffered(1))


def _rms(x, g):
    return x * lax.rsqrt(jnp.mean(x * x, axis=-1, keepdims=True) + EPS) * g


def _sigmoid(x):
    return 1.0 / (1.0 + jnp.exp(-x))


IN_TM = 512
IN_CHUNK = 256
IN_CONV_ROWS = 256


def _inproj_kernel(x_ref, xp_ref, xn_ref, g_ref, w_ref, cw_ref, cb_ref, oa_ref, ob_ref, oc_ref, od_ref, cs_ref,
                   *, tiles_per_seq):
    i = pl.program_id(0)
    g = g_ref[...]
    h = _rms(x_ref[...], g).astype(BF16)
    off = 0
    for o_ref in (oa_ref, ob_ref, oc_ref, od_ref):
        width = o_ref.shape[-1]
        for c in range(0, width, IN_CHUNK):
            y = jnp.dot(h, w_ref[:, off + c:off + c + IN_CHUNK], preferred_element_type=F32)
            if o_ref is oc_ref:
                cs_ref[SUBLANES:SUBLANES + IN_TM, c:c + IN_CHUNK] = y
            else:
                o_ref[:, c:c + IN_CHUNK] = y
        off += width
    off_c = oa_ref.shape[-1] + ob_ref.shape[-1]
    hh = _rms(jnp.concatenate([xp_ref[...], xn_ref[...]], axis=0), g).astype(BF16)
    halo = jnp.dot(hh, w_ref[:, off_c:off_c + IN_C], preferred_element_type=F32)
    keep_prev = jnp.where(i % tiles_per_seq != 0, 1.0, 0.0)
    keep_next = jnp.where(i % tiles_per_seq != tiles_per_seq - 1, 1.0, 0.0)
    cs_ref[0:SUBLANES, :] = halo[0:SUBLANES] * keep_prev
    cs_ref[SUBLANES + IN_TM:2 * SUBLANES + IN_TM, :] = halo[SUBLANES:2 * SUBLANES] * keep_next
    left = C_CONV // 2
    for r0 in range(0, IN_TM, IN_CONV_ROWS):
        for c in range(0, IN_C, IN_CHUNK):
            win = cs_ref[r0:r0 + IN_CONV_ROWS + 2 * SUBLANES, c:c + IN_CHUNK]
            y = cb_ref[:, c:c + IN_CHUNK]
            for j in range(C_CONV):
                s0 = SUBLANES + j - left
                y = y + cw_ref[j:j + 1, c:c + IN_CHUNK] * win[s0:s0 + IN_CONV_ROWS, :]
            oc_ref[r0:r0 + IN_CONV_ROWS, c:c + IN_CHUNK] = y


def _inproj(x2, g, w_bf16, cw, cb, seq):
    n = x2.shape[0]
    widths = (IN_A, IN_B, IN_C, IN_D)
    per8 = IN_TM // SUBLANES
    last8 = n // SUBLANES - 1
    return pl.pallas_call(
        functools.partial(_inproj_kernel, tiles_per_seq=seq // IN_TM),
        grid=(n // IN_TM,),
        in_specs=[pl.BlockSpec((IN_TM, D_MODEL), lambda i: (i, 0)),
                  pl.BlockSpec((SUBLANES, D_MODEL), lambda i: (jnp.maximum(i * per8 - 1, 0), 0)),
                  pl.BlockSpec((SUBLANES, D_MODEL), lambda i: (jnp.minimum((i + 1) * per8, last8), 0)),
                  _const_spec((1, D_MODEL)),
                  _const_spec((D_MODEL, IN_WIDTH)),
                  _const_spec((C_CONV, IN_C)), _const_spec((1, IN_C))],
        out_specs=[pl.BlockSpec((IN_TM, w), lambda i: (i, 0)) for w in widths],
        out_shape=[jax.ShapeDtypeStruct((n, w), F32) for w in widths],
        scratch_shapes=[pltpu.VMEM((IN_TM + 2 * SUBLANES, IN_C), F32)],
        compiler_params=_cparams("parallel"),
        name="inproj",
    )(x2, x2, x2, g.reshape(1, D_MODEL), w_bf16, cw, cb.reshape(1, IN_C))


FF_TM = 1024
FF_CHUNK = 256
FF_NCHUNK = D_FF // FF_CHUNK


def _ffn_kernel(x_ref, g_ref, w1_ref, w3_ref, w2_ref, fg_ref, o_ref, acc_ref, *, final):
    x = x_ref[...]
    h = _rms(x, g_ref[...]).astype(BF16)
    acc_ref[...] = x

    def body(c, carry):
        c0 = pl.multiple_of(c * FF_CHUNK, FF_CHUNK)
        a = jnp.dot(h, w1_ref[:, pl.ds(c0, FF_CHUNK)], preferred_element_type=F32)
        b = jnp.dot(h, w3_ref[:, pl.ds(c0, FF_CHUNK)], preferred_element_type=F32)
        t = (a * _sigmoid(a) * b).astype(BF16)
        acc_ref[...] += jnp.dot(t, w2_ref[pl.ds(c0, FF_CHUNK), :], preferred_element_type=F32)
        return carry

    lax.fori_loop(0, FF_NCHUNK, body, 0)
    y = acc_ref[...]
    if final:
        y = _rms(y, fg_ref[...])
    o_ref[...] = y


def _ffn(x2, g, w1c, w3c, w2c, final_g, final):
    n = x2.shape[0]
    return pl.pallas_call(
        functools.partial(_ffn_kernel, final=final),
        grid=(n // FF_TM,),
        in_specs=[pl.BlockSpec((FF_TM, D_MODEL), lambda i: (i, 0)),
                  _const_spec((1, D_MODEL)),
                  _const_spec((D_MODEL, D_FF)),
                  _const_spec((D_MODEL, D_FF)),
                  _const_spec((D_FF, D_MODEL)),
                  _const_spec((1, D_MODEL))],
        out_specs=pl.BlockSpec((FF_TM, D_MODEL), lambda i: (i, 0)),
        out_shape=jax.ShapeDtypeStruct((n, D_MODEL), F32),
        scratch_shapes=[pltpu.VMEM((FF_TM, D_MODEL), F32)],
        compiler_params=_cparams("parallel"),
        name="ffn_final" if final else "ffn",
    )(x2, g.reshape(1, D_MODEL), w1c, w3c, w2c, final_g.reshape(1, D_MODEL))


def _ffn_weights(w1, w3, w2):
    return w1.astype(BF16), w3.astype(BF16), w2.astype(BF16)


MG_TM = 512


def _head_ones():
    r = np.arange(D_BRANCH)[:, None] // A_DK
    c = np.arange(D_BRANCH)[None, :] // A_DK
    return jnp.asarray((r == c).astype(np.float32) / A_DK, BF16)


def _merge_kernel(x_ref, g1_ref, oaf_ref, oab_ref, ga_ref, hg_ref, hm_ref, yb_ref, yc_ref,
                  o0_ref, o1_ref, o2_ref, l0_ref, l1_ref, l2_ref,
                  wg_ref, bg_ref, wb_ref, wo_ref, out_ref):
    x = x_ref[...]
    h = _rms(x, g1_ref[...]).astype(BF16)
    oa = oaf_ref[...] + oab_ref[...]
    ms = jnp.dot((oa * oa).astype(BF16), hm_ref[...], preferred_element_type=F32)
    ga = ga_ref[...]
    ya = oa * lax.rsqrt(ms + EPS) * hg_ref[...] * (ga * _sigmoid(ga))
    l0, l1, l2 = l0_ref[...], l1_ref[...], l2_ref[...]
    m = jnp.maximum(jnp.maximum(l0, l1), l2)
    e0, e1, e2 = jnp.exp(l0 - m), jnp.exp(l1 - m), jnp.exp(l2 - m)
    yd = (e0 * o0_ref[...] + e1 * o1_ref[...] + e2 * o2_ref[...]) / (e0 + e1 + e2)
    mixed = None
    for j, y in enumerate((ya, yb_ref[...], yc_ref[...], yd)):
        gate = _sigmoid(jnp.dot(h, wg_ref[:, j * D_MODEL:(j + 1) * D_MODEL], preferred_element_type=F32)
                        + bg_ref[:, j * D_MODEL:(j + 1) * D_MODEL])
        t = gate * jnp.dot(y.astype(BF16), wb_ref[j], preferred_element_type=F32)
        mixed = t if mixed is None else mixed + t
    out_ref[...] = x + jnp.dot(mixed.astype(BF16), wo_ref[...], preferred_element_type=F32)


def _merge(x2, g1, oa_f, oa_b, pa, hg, yb, yc, od, ld, wg, bg, wb, wo):
    n = x2.shape[0]
    tile = lambda w: pl.BlockSpec((MG_TM, w), lambda i: (i, 0))
    return pl.pallas_call(
        _merge_kernel,
        grid=(n // MG_TM,),
        in_specs=[tile(D_MODEL), _const_spec((1, D_MODEL)),
                  tile(D_BRANCH), tile(D_BRANCH),
                  pl.BlockSpec((MG_TM, D_BRANCH), lambda i: (i, 4)),
                  _const_spec((1, D_BRANCH)), _const_spec((D_BRANCH, D_BRANCH)),
                  tile(D_BRANCH), tile(D_BRANCH),
                  tile(D_BRANCH), tile(D_BRANCH), tile(D_BRANCH),
                  tile(D_BRANCH), tile(D_BRANCH), tile(D_BRANCH),
                  _const_spec((D_MODEL, N_BRANCH * D_MODEL)), _const_spec((1, N_BRANCH * D_MODEL)),
                  _const_spec((N_BRANCH, D_BRANCH, D_MODEL)), _const_spec((D_MODEL, D_MODEL))],
        out_specs=tile(D_MODEL),
        out_shape=jax.ShapeDtypeStruct((n, D_MODEL), F32),
        compiler_params=_cparams("parallel"),
        name="merge",
    )(x2, g1.reshape(1, D_MODEL), oa_f, oa_b, pa, hg.reshape(1, D_BRANCH), _head_ones(), yb, yc,
      od[0], od[1], od[2], ld[0], ld[1], ld[2], wg, bg, wb, wo)


HG_TS = 256
HG_NCH = HG_TS // A_CHUNK
HG_MID = A_CHUNK // 2


def _hgrn_tables():
    r = np.arange(HG_TS)[:, None]
    c = np.arange(HG_TS)[None, :]
    same = (r // A_CHUNK) == (c // A_CHUNK)
    rr = np.arange(HG_NCH * A_HEADS * A_CHUNK)[:, None]
    same_s = (rr // (A_HEADS * A_CHUNK)) == (c // A_CHUNK)
    t, sidx = rr % A_CHUNK, c % A_CHUNK
    out = []
    for fwd in (True, False):
        order = (r >= c) if fwd else (r <= c)
        order_s = (t >= sidx) if fwd else (t <= sidx)
        out.append((jnp.asarray((same & order).astype(np.float32), BF16),
                    jnp.asarray((same_s & order_s).astype(np.float32), F32)))
    return out


def _hgrn_prep(q, fl, v, lb, cum, smask, fwd):
    lane_head = lax.broadcasted_iota(jnp.int32, (A_CHUNK, D_BRANCH), 1) // A_DK
    hmask = [(lane_head == hh).astype(F32) for hh in range(A_HEADS)]
    blk_r = lax.broadcasted_iota(jnp.int32, (D_BRANCH, D_BRANCH), 0) // A_DK
    blk_c = lax.broadcasted_iota(jnp.int32, (D_BRANCH, D_BRANCH), 1) // A_DK
    blockdiag = (blk_r == blk_c).astype(F32)
    col_chunk = lax.broadcasted_iota(jnp.int32, (D_BRANCH, HG_TS), 1) // A_CHUNK

    sg = _sigmoid(fl)
    f = lb + (1.0 - lb) * sg
    g = jnp.log(jnp.maximum(f, TINY))
    kk = (1.0 - lb) * (1.0 - sg)
    g1 = g.astype(BF16)
    r1 = g - g1.astype(F32)
    g2 = r1.astype(BF16)
    g3 = (r1 - g2.astype(F32)).astype(BF16)
    b = (jnp.dot(cum, g1, preferred_element_type=F32) + jnp.dot(cum, g2, preferred_element_type=F32)
         + jnp.dot(cum, g3, preferred_element_type=F32))
    rows = lambda c: slice(c * A_CHUNK, (c + 1) * A_CHUNK)
    last = (A_CHUNK - 1) if fwd else 0
    bcast = lambda r0: jnp.concatenate(
        [jnp.broadcast_to(b[c * A_CHUNK + r0:c * A_CHUNK + r0 + 1, :], (A_CHUNK, D_BRANCH)) for c in range(HG_NCH)], 0)
    bm = bcast(HG_MID)
    bl = bcast(last)
    qt = q * jnp.exp(b - bm)
    kt = (kk * jnp.exp(bm - b)).astype(BF16)
    qe = (q * jnp.exp(b)).astype(BF16)
    kh = (kk * jnp.exp(bl - b)).astype(BF16)
    qs = jnp.concatenate([qt[rows(c)] * hmask[hh] for c in range(HG_NCH) for hh in range(A_HEADS)],
                         axis=0).astype(BF16)
    s = lax.dot_general(qs, kt, (((1,), (1,)), ((), ())), preferred_element_type=F32)
    s = (s * smask).astype(BF16)
    ost = jnp.dot(s, v.astype(BF16), preferred_element_type=F32)
    o_intra = []
    for c in range(HG_NCH):
        base = c * A_HEADS * A_CHUNK
        o = ost[base:base + A_CHUNK] * hmask[0]
        for hh in range(1, A_HEADS):
            o = o + ost[base + hh * A_CHUNK:base + (hh + 1) * A_CHUNK] * hmask[hh]
        o_intra.append(o)
    vt = v.T
    lhs = jnp.concatenate([jnp.where(col_chunk == c, vt, 0.0) for c in range(HG_NCH)], axis=0).astype(BF16)
    updall = jnp.dot(lhs, kh, preferred_element_type=F32)
    upd = [updall[c * D_BRANCH:(c + 1) * D_BRANCH] * blockdiag for c in range(HG_NCH)]
    decay = [jnp.exp(b[c * A_CHUNK + last:c * A_CHUNK + last + 1, :]) for c in range(HG_NCH)]
    return o_intra, upd, qe, decay


def _hgrn_scan(o_intra, upd, qe, decay, st, fwd):
    rows = lambda c: slice(c * A_CHUNK, (c + 1) * A_CHUNK)
    outs = [None] * HG_NCH
    for c in (range(HG_NCH) if fwd else range(HG_NCH - 1, -1, -1)):
        outs[c] = o_intra[c] + lax.dot_general(qe[rows(c)], st.astype(BF16), (((1,), (1,)), ((), ())),
                                               preferred_element_type=F32)
        st = st * decay[c] + upd[c]
    return jnp.concatenate(outs, axis=0), st


def _hgrn_kernel(qf_ref, ff_ref, vf_ref, qb_ref, fb_ref, vb_ref, lb_ref, cf_ref, mf_ref, cb_ref, mb_ref,
                 of_ref, ob_ref, sf_ref, sb_ref):
    @pl.when(pl.program_id(1) == 0)
    def _():
        sf_ref[...] = jnp.zeros_like(sf_ref)
        sb_ref[...] = jnp.zeros_like(sb_ref)

    lb = lb_ref[...]
    pf = _hgrn_prep(qf_ref[...], ff_ref[...], vf_ref[...], lb, cf_ref[...], mf_ref[...], True)
    pb = _hgrn_prep(qb_ref[...], fb_ref[...], vb_ref[...], lb, cb_ref[...], mb_ref[...], False)
    o_f, st_f = _hgrn_scan(*pf, sf_ref[...], True)
    o_b, st_b = _hgrn_scan(*pb, sb_ref[...], False)
    of_ref[...] = o_f
    ob_ref[...] = o_b
    sf_ref[...] = st_f
    sb_ref[...] = st_b


def _hgrn(pa3, lb):
    bsz, seq, _ = pa3.shape
    nblk = seq // HG_TS
    blk = (None, HG_TS, D_BRANCH)
    up = lambda col: pl.BlockSpec(blk, lambda b, i: (b, i, col))
    down = lambda col: pl.BlockSpec(blk, lambda b, i: (b, nblk - 1 - i, col))
    shp = jax.ShapeDtypeStruct((bsz, seq, D_BRANCH), F32)
    (cum_f, sm_f), (cum_b, sm_b) = _hgrn_tables()
    mshape = (HG_NCH * A_HEADS * A_CHUNK, HG_TS)
    return pl.pallas_call(
        _hgrn_kernel,
        grid=(bsz, nblk),
        in_specs=[up(0), up(1), up(3), down(0), down(2), down(3), _const_spec((1, D_BRANCH)),
                  _const_spec((HG_TS, HG_TS)), _const_spec(mshape), _const_spec((HG_TS, HG_TS)), _const_spec(mshape)],
        out_specs=[up(0), down(0)],
        out_shape=[shp, shp],
        scratch_shapes=[pltpu.VMEM((D_BRANCH, D_BRANCH), F32), pltpu.VMEM((D_BRANCH, D_BRANCH), F32)],
        compiler_params=_cparams("parallel", "arbitrary"),
        name="hgrn2",
    )(pa3, pa3, pa3, pa3, pa3, pa3, lb.reshape(1, D_BRANCH), cum_f, sm_f, cum_b, sm_b)


RG_TB = 1024
RG_PAD = SUBLANES
RG_LEFT = B_CONV // 2


def _dot3_rhs(x, wh, wl):
    xh = x.astype(BF16)
    xl = (x - xh.astype(F32)).astype(BF16)
    return (jnp.dot(xh, wh, preferred_element_type=F32) + jnp.dot(xl, wh, preferred_element_type=F32)
            + jnp.dot(xh, wl, preferred_element_type=F32))


def _group_scan(a, u, fwd):
    row = lax.broadcasted_iota(jnp.int32, a.shape, 1)
    k = 1
    while k < SUBLANES:
        if fwd:
            keep = row >= k
            us, as_ = pltpu.roll(u, k, 1), pltpu.roll(a, k, 1)
        else:
            keep = row < SUBLANES - k
            us, as_ = pltpu.roll(u, SUBLANES - k, 1), pltpu.roll(a, SUBLANES - k, 1)
        u = a * jnp.where(keep, us, 0.0) + u
        a = a * jnp.where(keep, as_, 1.0)
        k *= 2
    return a, u


def _block_scan(a, u, carry, fwd):
    t = a.shape[0]
    ngrp = t // SUBLANES
    ag, ug = _group_scan(a.reshape(ngrp, SUBLANES, D_BRANCH), u.reshape(ngrp, SUBLANES, D_BRANCH), fwd)
    hs = [None] * ngrp
    for g in (range(ngrp) if fwd else range(ngrp - 1, -1, -1)):
        h = ug[g] + ag[g] * carry
        hs[g] = h
        carry = h[SUBLANES - 1:SUBLANES, :] if fwd else h[0:1, :]
    return jnp.concatenate(hs, axis=0), carry


def _gelu_tanh(x):
    return 0.5 * x * (1.0 + jnp.tanh(math.sqrt(2.0 / math.pi) * (x + 0.044715 * (x * x * x))))


def _rglru_kernel(x_ref, gt_ref, cw_ref, cb_ref, wh_ref, wl_ref, bg_ref, lam_ref, o_ref, xp_ref, xc_ref):
    seq = x_ref.shape[0]
    nblk = seq // RG_TB
    xp_ref[0:RG_PAD, :] = jnp.zeros((RG_PAD, D_BRANCH), F32)
    xp_ref[RG_PAD + seq:2 * RG_PAD + seq, :] = jnp.zeros((RG_PAD, D_BRANCH), F32)
    xp_ref[RG_PAD:RG_PAD + seq, :] = x_ref[...]
    nl = -lam_ref[...]
    sp = jnp.maximum(nl, 0.0) + jnp.log(1.0 + jnp.exp(-jnp.abs(nl)))

    def block(i, carry, dirn):
        r0 = pl.multiple_of(i * RG_TB, RG_TB)
        if dirn == 0:
            win = xp_ref[pl.ds(r0, RG_TB + 2 * RG_PAD), :]
            xc = cb_ref[...]
            for j in range(B_CONV):
                s0 = RG_PAD + j - RG_LEFT
                xc = xc + cw_ref[j:j + 1, :] * win[s0:s0 + RG_TB, :]
            xc_ref[pl.ds(r0, RG_TB), :] = xc
        else:
            xc = xc_ref[pl.ds(r0, RG_TB), :]
        cols = slice(dirn * 2 * D_BRANCH, (dirn + 1) * 2 * D_BRANCH)
        gates = _dot3_rhs(xc, wh_ref[:, cols], wl_ref[:, cols]) + bg_ref[:, cols]
        r = _sigmoid(gates[:, :D_BRANCH])
        ig = _sigmoid(gates[:, D_BRANCH:])
        log_a = -LRU_C * r * sp[dirn:dirn + 1, :]
        a = jnp.exp(log_a)
        u = jnp.sqrt(jnp.maximum(-jnp.tanh(log_a) * (a * a + 1.0), 0.0)) * ig * xc
        h, carry = _block_scan(a, u, carry, dirn == 0)
        if dirn == 0:
            o_ref[pl.ds(r0, RG_TB), :] = h
        else:
            o_ref[pl.ds(r0, RG_TB), :] = (o_ref[pl.ds(r0, RG_TB), :] + h) * _gelu_tanh(gt_ref[pl.ds(r0, RG_TB), :])
        return carry

    zero = jnp.zeros((1, D_BRANCH), F32)
    lax.fori_loop(0, nblk, lambda i, c: block(i, c, 0), zero)
    lax.fori_loop(0, nblk, lambda i, c: block(nblk - 1 - i, c, 1), zero)


def _blockdiag(w):
    eye = jnp.eye(B_BLOCKS, dtype=w.dtype)
    return jnp.einsum('ncd,nm->ncmd', w, eye).reshape(D_BRANCH, D_BRANCH)


def _rglru(pb3, cw, cb, wa, ba, wx, bx, lam):
    bsz, seq, _ = pb3.shape
    wg = jnp.concatenate([_blockdiag(wa[0]), _blockdiag(wx[0]), _blockdiag(wa[1]), _blockdiag(wx[1])], axis=1)
    bg = jnp.concatenate([ba[0], bx[0], ba[1], bx[1]]).reshape(1, 4 * D_BRANCH)
    wh = wg.astype(BF16)
    wl = (wg - wh.astype(F32)).astype(BF16)
    blk = (None, seq, D_BRANCH)
    return pl.pallas_call(
        _rglru_kernel,
        grid=(bsz,),
        in_specs=[pl.BlockSpec(blk, lambda b: (b, 0, 0)),
                  pl.BlockSpec(blk, lambda b: (b, 0, 1)),
                  _const_spec((B_CONV, D_BRANCH)), _const_spec((1, D_BRANCH)),
                  _const_spec((D_BRANCH, 4 * D_BRANCH)), _const_spec((D_BRANCH, 4 * D_BRANCH)),
                  _const_spec((1, 4 * D_BRANCH)), _const_spec((2, D_BRANCH))],
        out_specs=pl.BlockSpec(blk, lambda b: (b, 0, 0)),
        out_shape=jax.ShapeDtypeStruct((bsz, seq, D_BRANCH), F32),
        scratch_shapes=[pltpu.VMEM((seq + 2 * RG_PAD, D_BRANCH), F32), pltpu.VMEM((seq, D_BRANCH), F32)],
        compiler_params=_cparams("parallel"),
        name="rglru",
    )(pb3, pb3, cw, cb.reshape(1, D_BRANCH), wh, wl, bg, lam)


FFT_IN = 128
FFT_NT = FFT_IN // SUBLANES
HY_TB = 256


def _split_np(a):
    a = np.asarray(a, np.float32)
    hi = a.astype(jnp.bfloat16)
    lo = (a - hi.astype(np.float32)).astype(jnp.bfloat16)
    return jnp.asarray(hi), jnp.asarray(lo)


def _dot3(mh, ml, x):
    xh = x.astype(BF16)
    xl = (x - xh.astype(F32)).astype(BF16)
    return (jnp.dot(mh, xh, preferred_element_type=F32) + jnp.dot(mh, xl, preferred_element_type=F32)
            + jnp.dot(ml, xh, preferred_element_type=F32))


def _dotp(mh, ml, x, passes):
    if passes == 1:
        return jnp.dot(mh, x.astype(BF16), preferred_element_type=F32)
    return _dot3(mh, ml, x)


def _pack_pair(re, im):
    rb = lax.bitcast_convert_type(re.astype(BF16).astype(F32), jnp.uint32)
    ib = lax.bitcast_convert_type(im.astype(BF16).astype(F32), jnp.uint32)
    return rb | (ib >> BF16_BITS)


def _unpack_pair(w):
    re = lax.bitcast_convert_type(w & jnp.uint32(BF16_HIGH_MASK), F32)
    im = lax.bitcast_convert_type(w << BF16_BITS, F32)
    return re.astype(BF16), im.astype(BF16)


def _fft_tables(seq):
    n = 2 * seq
    n1 = n // FFT_IN
    half = n1 // 2
    eye = np.eye(SUBLANES)
    a = 2.0 * np.pi * np.outer(np.arange(n1), np.arange(half)) / n1
    gr, gi = np.cos(a), -np.sin(a)
    blk = np.stack([np.stack([gr, -gi], axis=1), np.stack([gi, gr], axis=1)], axis=0)
    m_out = np.einsum('rkis,cd->rkcisd', blk, eye).reshape(2 * n1 * SUBLANES, 2 * half * SUBLANES)
    ir, ii = gr.T / n, -gi.T / n
    blk = np.stack([np.stack([ir, -ii], axis=1), np.stack([ii, ir], axis=1)], axis=0)
    m_inv = np.einsum('otrk,cd->otcrkd', blk, eye).reshape(2 * half * SUBLANES, 2 * n1 * SUBLANES)
    a = 2.0 * np.pi * np.outer(np.arange(FFT_IN), np.arange(FFT_IN)) / FFT_IN
    fr, fi = np.cos(a), -np.sin(a)
    w_fwd = np.block([[fr, -fi], [fi, fr]])
    w_inv = np.block([[fr, fi], [-fi, fr]])
    s_in = SUBLANES * np.arange(FFT_NT)[:, None, None] + np.arange(SUBLANES)[None, None, :]
    th = 2.0 * np.pi * s_in * np.arange(n1)[None, :, None] / n
    tw = (jnp.asarray(np.cos(th)[..., None], F32), jnp.asarray(-np.sin(th)[..., None], F32))
    th = 2.0 * np.pi * np.outer(np.arange(n1), np.arange(FFT_IN)) / n
    tw_in = (jnp.asarray(np.cos(th)[..., None], F32), jnp.asarray(-np.sin(th)[..., None], F32))
    return dict(m_out=_split_np(m_out), m_inv=_split_np(m_inv), w_fwd=_split_np(w_fwd), w_inv=_split_np(w_inv),
                tw=tw, tw_in=tw_in, n1=n1, half=half)


FFT_PP = 4


def _tw_spec(n1):
    return pl.BlockSpec((None, n1, SUBLANES, 1), lambda t, q: (t, 0, 0, 0))


def _fft_outer_kernel(mh_ref, ml_ref, twr_ref, twi_ref, z_ref, v_ref, *, complex_in, passes, packed):
    n1 = v_ref.shape[1]
    tr, ti = twr_ref[...], twi_ref[...]
    for pp in range(FFT_PP):
        zz = z_ref[:, pp] if complex_in else z_ref[pp]
        rows_in = math.prod(zz.shape[:-1])
        v = _dotp(mh_ref[:, :rows_in], ml_ref[:, :rows_in], zz.reshape(rows_in, D_BRANCH), passes)
        vr = v[:n1 * SUBLANES].reshape(n1, SUBLANES, D_BRANCH)
        vi = v[n1 * SUBLANES:].reshape(n1, SUBLANES, D_BRANCH)
        wr, wi = vr * tr - vi * ti, vr * ti + vi * tr
        if packed:
            v_ref[pp] = _pack_pair(wr, wi)
        else:
            v_ref[pp, :, 0] = wr
            v_ref[pp, :, 1] = wi


def _fft_outer(tb, z, complex_in, passes, packed):
    n1, half = tb["n1"], tb["half"]
    if complex_in:
        p = z.shape[1]
        zspec = pl.BlockSpec((2, FFT_PP, half, None, SUBLANES, D_BRANCH), lambda t, q: (0, q, 0, t, 0, 0))
    else:
        p = z.shape[0]
        zspec = pl.BlockSpec((FFT_PP, half, None, SUBLANES, D_BRANCH), lambda t, q: (q, 0, t, 0, 0))
    mshape = (2 * n1 * SUBLANES, 2 * half * SUBLANES)
    if packed:
        out_spec = pl.BlockSpec((FFT_PP, n1, None, SUBLANES, D_BRANCH), lambda t, q: (q, 0, t, 0, 0))
        out_shape = jax.ShapeDtypeStruct((p, n1, FFT_NT, SUBLANES, D_BRANCH), jnp.uint32)
    else:
        out_spec = pl.BlockSpec((FFT_PP, n1, 2, None, SUBLANES, D_BRANCH), lambda t, q: (q, 0, 0, t, 0, 0))
        out_shape = jax.ShapeDtypeStruct((p, n1, 2, FFT_NT, SUBLANES, D_BRANCH), F32)
    return pl.pallas_call(
        functools.partial(_fft_outer_kernel, complex_in=complex_in, passes=passes, packed=packed),
        grid=(FFT_NT, p // FFT_PP),
        in_specs=[_const_spec(mshape), _const_spec(mshape), _tw_spec(n1), _tw_spec(n1), zspec],
        out_specs=out_spec,
        out_shape=out_shape,
        compiler_params=_cparams("arbitrary", "arbitrary"),
        name="fft_outer_c" if complex_in else "fft_outer_r",
    )(*tb["m_out"], *tb["tw"], z)


FFT_KB = 16


def _fft_filt_kernel(wh_ref, wl_ref, vf_ref, vb_ref, h_ref):
    for kb in range(FFT_KB):
        zf = _dot3(wh_ref[...], wl_ref[...], vf_ref[kb].reshape(2 * FFT_IN, D_BRANCH))
        zb = _dot3(wh_ref[...], wl_ref[...], vb_ref[kb].reshape(2 * FFT_IN, D_BRANCH))
        h_ref[kb, 0] = zf[:FFT_IN] + zb[:FFT_IN]
        h_ref[kb, 1] = zf[FFT_IN:] - zb[FFT_IN:]


def _fft_filt(tb, v):
    n1 = v.shape[1]
    wspec = _const_spec((2 * FFT_IN, 2 * FFT_IN))
    vblk = (None, FFT_KB, 2, FFT_NT, SUBLANES, D_BRANCH)
    return pl.pallas_call(
        _fft_filt_kernel,
        grid=(C_ORDER, n1 // FFT_KB),
        in_specs=[wspec, wspec,
                  pl.BlockSpec(vblk, lambda o, k: (2 * o, k, 0, 0, 0, 0)),
                  pl.BlockSpec(vblk, lambda o, k: (2 * o + 1, k, 0, 0, 0, 0))],
        out_specs=pl.BlockSpec((None, FFT_KB, 2, FFT_IN, D_BRANCH), lambda o, k: (o, k, 0, 0, 0)),
        out_shape=jax.ShapeDtypeStruct((C_ORDER, n1, 2, FFT_IN, D_BRANCH), F32),
        compiler_params=_cparams("parallel", "parallel"),
        name="fft_filt",
    )(*tb["w_fwd"], v, v)


def _fft_mid_kernel(wf_ref, wi_ref, twr_ref, twi_ref, v_ref, h_ref, d_ref):
    for kb in range(FFT_KB):
        vr, vi = _unpack_pair(v_ref[kb].reshape(FFT_IN, D_BRANCH))
        z = jnp.dot(wf_ref[...], jnp.concatenate([vr, vi], axis=0), preferred_element_type=F32)
        zr, zi = z[:FFT_IN], z[FFT_IN:]
        hr, hi = h_ref[kb, 0], h_ref[kb, 1]
        pr = (zr * hr - zi * hi).astype(BF16)
        pi = (zr * hi + zi * hr).astype(BF16)
        d = jnp.dot(wi_ref[...], jnp.concatenate([pr, pi], axis=0), preferred_element_type=F32)
        dr, di = d[:FFT_IN], d[FFT_IN:]
        tr, ti = twr_ref[kb], twi_ref[kb]
        er, ei = dr * tr + di * ti, di * tr - dr * ti
        d_ref[kb] = _pack_pair(er, ei).reshape(d_ref.shape[1:])


def _fft_mid(tb, v, hspec, order):
    p, n1 = v.shape[:2]
    vspec = pl.BlockSpec((None, FFT_KB, FFT_NT, SUBLANES, D_BRANCH), lambda k, q: (q, k, 0, 0, 0))
    wspec = _const_spec((2 * FFT_IN, 2 * FFT_IN))
    tspec = pl.BlockSpec((FFT_KB, FFT_IN, 1), lambda k, q: (k, 0, 0))
    return pl.pallas_call(
        _fft_mid_kernel,
        grid=(n1 // FFT_KB, p),
        in_specs=[wspec, wspec, tspec, tspec, vspec,
                  pl.BlockSpec((None, FFT_KB, 2, FFT_IN, D_BRANCH), lambda k, q: (order, k, 0, 0, 0))],
        out_specs=vspec,
        out_shape=jax.ShapeDtypeStruct(v.shape, jnp.uint32),
        compiler_params=_cparams("parallel", "arbitrary"),
        name="fft_mid",
    )(tb["w_fwd"][0], tb["w_inv"][0], *tb["tw_in"], v, hspec)


def _ifft_outer_kernel(mh_ref, d_ref, u_ref, x_ref, b_ref, o_ref):
    n1 = d_ref.shape[1]
    for pp in range(FFT_PP):
        er, ei = _unpack_pair(d_ref[pp])
        e = jnp.concatenate([er.reshape(n1 * SUBLANES, D_BRANCH), ei.reshape(n1 * SUBLANES, D_BRANCH)], axis=0)
        y = jnp.dot(mh_ref[...], e, preferred_element_type=F32).reshape((2,) + o_ref.shape[2:])
        o_ref[:, pp] = x_ref[:, pp] * (y + u_ref[:, pp] * b_ref[...])


def _ifft_outer(tb, d, u, ucol, x, xcol, bias):
    n1, half = tb["n1"], tb["half"]
    p = d.shape[0]
    mshape = (2 * half * SUBLANES, 2 * n1 * SUBLANES)
    io = lambda col: pl.BlockSpec((2, FFT_PP, half, None, SUBLANES, D_BRANCH), lambda t, q: (0, q, 0, t, 0, col))
    return pl.pallas_call(
        _ifft_outer_kernel,
        grid=(FFT_NT, p // FFT_PP),
        in_specs=[_const_spec(mshape),
                  pl.BlockSpec((FFT_PP, n1, None, SUBLANES, D_BRANCH), lambda t, q: (q, 0, t, 0, 0)),
                  io(ucol), io(xcol), _const_spec((1, D_BRANCH))],
        out_specs=io(0),
        out_shape=jax.ShapeDtypeStruct((2, p, half, FFT_NT, SUBLANES, D_BRANCH), F32),
        compiler_params=_cparams("arbitrary", "arbitrary"),
        name="ifft_outer",
    )(tb["m_inv"][0], d, u, x, bias.reshape(1, D_BRANCH))


def _hyfilt_kernel(z_ref, w1_ref, b1_ref, fr_ref, w2_ref, b2_ref, w3_ref, dec_ref, o_ref, h_ref):
    seq = z_ref.shape[0]
    nblk = seq // HY_TB
    fr = fr_ref[...]

    @pl.when(pl.program_id(0) == 0)
    def _():
        def hidden(i, carry):
            r0 = pl.multiple_of(i * HY_TB, HY_TB)
            zb = z_ref[pl.ds(r0, HY_TB), :]
            h = jnp.sin(fr * (jnp.dot(zb, w1_ref[...], precision=HI, preferred_element_type=F32) + b1_ref[...]))
            h = jnp.sin(fr * (jnp.dot(h, w2_ref[...], precision=HI, preferred_element_type=F32) + b2_ref[...]))
            h_ref[pl.ds(r0, HY_TB), :] = h
            return carry

        lax.fori_loop(0, nblk, hidden, 0)

    def body(i, ss):
        r0 = pl.multiple_of(i * HY_TB, HY_TB)
        hf = jnp.dot(h_ref[pl.ds(r0, HY_TB), :], w3_ref[...], precision=HI, preferred_element_type=F32)
        hf = hf * jnp.exp(-z_ref[pl.ds(r0, HY_TB), 0:1] * dec_ref[...])
        o_ref[pl.ds(r0, HY_TB), :] = hf
        return ss + jnp.sum(hf * hf, axis=0, keepdims=True)

    ss = lax.fori_loop(0, nblk, body, jnp.zeros((1, D_BRANCH), F32))
    scale = lax.rsqrt(ss + EPS)

    def norm(i, carry):
        r0 = pl.multiple_of(i * HY_TB, HY_TB)
        o_ref[pl.ds(r0, HY_TB), :] = o_ref[pl.ds(r0, HY_TB), :] * scale
        return carry

    lax.fori_loop(0, nblk, norm, 0)


def _hyfilt(seq, w1, b1, freq, w2, b2, w3):
    t = jnp.linspace(0.0, 1.0, seq, dtype=F32)[:, None]
    bands = (C_EMB - 1) // 2
    w = 2.0 * math.pi * jnp.arange(seq, dtype=F32)[:, None] / seq
    fr = jnp.linspace(1e-4, bands - 1, bands, dtype=F32)[None]
    z = jnp.concatenate([t, jnp.cos(fr * w), -jnp.sin(fr * w)], axis=-1)
    z = jnp.pad(z, ((0, 0), (0, LANES - C_EMB)))
    padm = lambda a, r, c: jnp.pad(a.astype(F32), ((0, r - a.shape[0]), (0, c - a.shape[1])))
    row = lambda a: padm(a.reshape(1, -1), 1, LANES)
    dec = jnp.abs(jnp.linspace(C_MIN_DECAY, C_MAX_DECAY, D_BRANCH, dtype=F32)).reshape(1, D_BRANCH)
    nset = C_ORDER * 2
    return pl.pallas_call(
        _hyfilt_kernel,
        grid=(nset,),
        in_specs=[_const_spec((seq, LANES)), _const_spec((LANES, LANES)), _const_spec((1, LANES)),
                  _const_spec((1, LANES)), _const_spec((LANES, LANES)), _const_spec((1, LANES)),
                  pl.BlockSpec((LANES, D_BRANCH), lambda j: (0, j)), _const_spec((1, D_BRANCH))],
        out_specs=pl.BlockSpec((None, seq, D_BRANCH), lambda j: (j, 0, 0)),
        out_shape=jax.ShapeDtypeStruct((nset, seq, D_BRANCH), F32),
        scratch_shapes=[pltpu.VMEM((seq, LANES), F32)],
        compiler_params=_cparams("arbitrary"),
        name="hyfilt",
    )(z, padm(w1, LANES, LANES), row(b1), row(freq), padm(w2, LANES, LANES), row(b2),
      padm(w3, LANES, nset * D_BRANCH), dec)


def _hyena(uc3, w1, b1, freq, w2, b2, w3, bias):
    bsz, seq, width = uc3.shape
    tb = _fft_tables(seq)
    half = tb["half"]
    npair = bsz // 2
    uc6 = uc3.reshape(2, npair, half, FFT_NT, SUBLANES, width)
    filt = _hyfilt(seq, w1, b1, freq, w2, b2, w3).reshape(C_ORDER * 2, half, FFT_NT, SUBLANES, D_BRANCH)
    hspec = _fft_filt(tb, _fft_outer(tb, filt, False, 3, False))
    src = uc6
    for order in range(C_ORDER):
        d = _fft_mid(tb, _fft_outer(tb, src, True, 1, True), hspec, order)
        src = _ifft_outer(tb, d, src, 0, uc6, order + 1, bias[order])
    return src.reshape(bsz, seq, D_BRANCH)


AT_HALF = 64
AT_TQ = 1024
AT_SUB = 128


def _t5_bucket(rel):
    half = N_BUCKETS // 2
    max_exact = half // 2
    n = np.abs(rel)
    large = max_exact + (np.log(np.maximum(n, 1) / max_exact) / math.log(MAX_DISTANCE / max_exact)
                         * (half - max_exact)).astype(np.int64)
    large = np.minimum(large, half - 1)
    return (rel > 0).astype(np.int64) * half + np.where(n < max_exact, n, large)


def _attn_geometry(n):
    tq = min(AT_TQ, n)
    sub = min(AT_SUB, tq)
    if sub + 2 * AT_HALF >= n:
        sub = tq
    win = min(sub + 2 * AT_HALF, n)
    return tq, sub, win, n // tq, n // sub


def _attn_bias_tables(rel_bias, g, dil, n):
    _, sub, win, _, nsb = _attn_geometry(n)
    hs = slice(g * D_HEADS_PER_GROUP, (g + 1) * D_HEADS_PER_GROUP)
    offsets = np.arange(-AT_HALF, AT_HALF + 1) * dil
    onehot = np.zeros((2 * AT_HALF + 1, N_BUCKETS), np.float32)
    onehot[np.arange(2 * AT_HALF + 1), _t5_bucket(offsets)] = 1.0
    band = jnp.dot(rel_bias.astype(F32)[:, hs].T, jnp.asarray(onehot).T, precision=HI)
    nband = 2 * AT_HALF + 1
    lv = sub + win - 1
    tables = []
    for i in sorted({0, min(1, nsb - 1), nsb - 1}):
        ws = int(np.clip(i * sub - AT_HALF, 0, n - win))
        lo = (sub - 1) - (ws - i * sub) - AT_HALF
        v = jnp.pad(band, ((0, 0), (lo, lv - lo - nband)), constant_values=NEG_BIG)
        flat = jnp.tile(v, (1, sub + 1))[:, sub - 1:sub - 1 + sub * (lv - 1)]
        tables.append(flat.reshape(D_HEADS_PER_GROUP, sub, lv - 1)[:, :, :win])
    return jnp.stack(tables)


def _attn_kernel(q_ref, k_ref, v_ref, bias_ref, o_ref, l_ref, *, n, dil):
    tq, sub, win, _, nsb = _attn_geometry(n)
    ncase = bias_ref.shape[0]
    width = q_ref.shape[-1]
    nh = width // D_HEAD_DIM
    hp = pl.program_id(1)
    lane_head = lax.broadcasted_iota(jnp.int32, (sub, width), 1) // D_HEAD_DIM
    hmask = [lane_head == hh for hh in range(nh)]
    whole = win == n
    for r in range(dil):
        if whole:
            kw = k_ref[pl.ds(r, win, stride=dil), :].astype(BF16)
            vw = v_ref[pl.ds(r, win, stride=dil), :].astype(BF16)
        for j in range(tq // sub):
            sidx = pl.program_id(2) * (tq // sub) + j
            case = jnp.minimum(jnp.where(sidx == nsb - 1, ncase - 1, jnp.minimum(sidx, 1)), ncase - 1)
            bias = bias_ref[case, pl.ds(hp * nh, nh)].reshape(nh * sub, win)
            if not whole:
                ws = pl.multiple_of(jnp.clip(sidx * sub - AT_HALF, 0, n - win), AT_HALF)
                kw = k_ref[pl.ds(ws * dil + r, win, stride=dil), :].astype(BF16)
                vw = v_ref[pl.ds(ws * dil + r, win, stride=dil), :].astype(BF16)
            q = q_ref[pl.ds(j * sub * dil + r, sub, stride=dil), :] * (D_HEAD_DIM ** -0.5)
            qs = jnp.concatenate([jnp.where(hmask[hh], q, 0.0) for hh in range(nh)], axis=0).astype(BF16)
            s = lax.dot_general(qs, kw, (((1,), (1,)), ((), ())), preferred_element_type=F32) + bias
            m = jnp.max(s, axis=-1, keepdims=True)
            p = jnp.exp(s - m)
            l = jnp.sum(p, axis=-1, keepdims=True)
            o_all = jnp.dot(p.astype(BF16), vw, preferred_element_type=F32) / l
            lse = m + jnp.log(l)
            o_acc = jnp.zeros((sub, width), F32)
            l_acc = jnp.zeros((sub, width), F32)
            for hh in range(nh):
                o_acc = jnp.where(hmask[hh], o_all[hh * sub:(hh + 1) * sub], o_acc)
                l_acc = jnp.where(hmask[hh], lse[hh * sub:(hh + 1) * sub], l_acc)
            o_ref[pl.ds(j * sub * dil + r, sub, stride=dil), :] = o_acc
            l_ref[pl.ds(j * sub * dil + r, sub, stride=dil), :] = l_acc


def _banded_attention(pd3, g, dil, bias):
    bsz, seq, width = pd3.shape
    n = seq // dil
    tq, sub, win, nq, _ = _attn_geometry(n)
    rows = dil * tq
    bw = LANES if dil > 1 else D_BRANCH
    per = D_BRANCH // bw
    third = width // 3 // bw
    out_spec = pl.BlockSpec((None, rows, bw), lambda b, h, i: (b, i, h))
    shp = jax.ShapeDtypeStruct((bsz, seq, D_BRANCH), F32)
    return pl.pallas_call(
        functools.partial(_attn_kernel, n=n, dil=dil),
        grid=(bsz, per, nq),
        in_specs=[pl.BlockSpec((None, rows, bw), lambda b, h, i: (b, i, g * per + h)),
                  pl.BlockSpec((None, seq, bw), lambda b, h, i: (b, 0, third + g * per + h)),
                  pl.BlockSpec((None, seq, bw), lambda b, h, i: (b, 0, 2 * third + g * per + h)),
                  _const_spec(bias.shape)],
        out_specs=[out_spec, out_spec],
        out_shape=[shp, shp],
        compiler_params=_cparams("parallel", "parallel", "arbitrary"),
        name=f"attn_d{dil}",
    )(pd3, pd3, pd3, bias)


def _dilated_attention(pd3, rel_bias):
    bsz, seq, _ = pd3.shape
    outs, lses = [], []
    for g, (_, dil) in enumerate(D_GROUPS):
        o, l = _banded_attention(pd3, g, dil, _attn_bias_tables(rel_bias, g, dil, seq // dil))
        outs.append(o.reshape(bsz * seq, D_BRANCH))
        lses.append(l.reshape(bsz * seq, D_BRANCH))
    return outs, lses


def kernel(x, norm1_g, w_in, hgrn_lb_logits, hgrn_norm_g, lru_conv_w, lru_conv_b, lru_wa, lru_ba, lru_wx, lru_bx,
           lru_lambda, hy_conv_w, hy_conv_b, hy_w1, hy_b1, hy_freq, hy_w2, hy_b2, hy_w3, hy_bias, rel_bias,
           w_branch, w_gate, b_gate, w_out, norm2_g, w_ff1, w_ff3, w_ff2, final_g):
    bsz, seq, _ = x.shape
    n = bsz * seq
    lb_soft = jax.nn.softmax(hgrn_lb_logits.astype(F32), axis=0)
    lower_bounds = jnp.cumsum(lb_soft, axis=0) - lb_soft[0]
    x2 = x.reshape(n, D_MODEL)
    flat = lambda a: a.reshape(n, D_BRANCH)
    for l in range(DEPTH):
        pa, pb, uc, pd = _inproj(x2, norm1_g[l], w_in[l].astype(BF16), hy_conv_w[l], hy_conv_b[l], seq)
        oa_f, oa_b = _hgrn(pa.reshape(bsz, seq, IN_A), lower_bounds[l])
        yb = _rglru(pb.reshape(bsz, seq, IN_B), lru_conv_w[l], lru_conv_b[l], lru_wa[l], lru_ba[l],
                    lru_wx[l], lru_bx[l], lru_lambda[l])
        yc = _hyena(uc.reshape(bsz, seq, IN_C), hy_w1[l], hy_b1[l], hy_freq[l],
                    hy_w2[l], hy_b2[l], hy_w3[l], hy_bias[l])
        od, ld = _dilated_attention(pd.reshape(bsz, seq, IN_D), rel_bias)
        x2 = _merge(x2, norm1_g[l], flat(oa_f), flat(oa_b), pa, hgrn_norm_g[l], flat(yb), flat(yc), od, ld,
                    w_gate[l].reshape(D_MODEL, N_BRANCH * D_MODEL).astype(BF16),
                    b_gate[l].reshape(1, N_BRANCH * D_MODEL), w_branch[l].astype(BF16), w_out[l].astype(BF16))
        x2 = _ffn(x2, norm2_g[l], *_ffn_weights(w_ff1[l], w_ff3[l], w_ff2[l]), final_g, l == DEPTH - 1)
    return x2.reshape(bsz, seq, D_MODEL)
```

```python
import functools
import math

import jax
import jax.numpy as jnp
import numpy as np
from jax import lax
from jax.experimental import pallas as pl
from jax.experimental.pallas import tpu as pltpu

F32 = jnp.float32
BF16 = jnp.bfloat16
HI = lax.Precision.HIGHEST

D_MODEL = 1024
DEPTH = 2
EPS = 1e-6
TINY = 1e-30
N_BRANCH = 4
D_BRANCH = 256
A_HEADS = 4
A_DK = 64
A_CHUNK = 64
B_BLOCKS = 4
B_BW = 64
B_CONV = 4
LRU_C = 8.0
C_ORDER = 2
C_CONV = 3
C_EMB = 33
C_MIN_DECAY = math.log(1e-2) / 1.5
C_MAX_DECAY = math.log(1e-2) / 0.3
D_GROUPS = ((128, 1), (512, 4), (2048, 16))
D_HEADS_PER_GROUP = 4
D_HEAD_DIM = 64
D_QKV = 768
N_BUCKETS = 32
MAX_DISTANCE = 1024
NEG_BIG = -1e30
D_FF = 2816
IN_A = 5 * D_BRANCH
IN_B = 2 * D_BRANCH
IN_C = 3 * D_BRANCH
IN_D = 3 * D_QKV
IN_WIDTH = IN_A + IN_B + IN_C + IN_D

LANES = 128
SUBLANES = 8
VMEM_LIMIT = 56 * 1024 * 1024
BF16_BITS = 16
BF16_HIGH_MASK = 0xFFFF0000


def _cparams(*sem):
    return pltpu.CompilerParams(dimension_semantics=sem, vmem_limit_bytes=VMEM_LIMIT)


def _const_spec(shape):
    nd = len(shape)
    return pl.BlockSpec(shape, lambda *_: (0,) * nd, pipeline_mode=pl.Buffered(1))


def _rms(x, g):
    return x * lax.rsqrt(jnp.mean(x * x, axis=-1, keepdims=True) + EPS) * g


def _sigmoid(x):
    return 1.0 / (1.0 + jnp.exp(-x))


IN_TM = 512
IN_CHUNK = 256
IN_CONV_ROWS = 256


def _inproj_kernel(x_ref, xp_ref, xn_ref, g_ref, w_ref, cw_ref, cb_ref, oa_ref, ob_ref, oc_ref, od_ref, cs_ref,
                   *, tiles_per_seq):
    i = pl.program_id(0)
    g = g_ref[...]
    h = _rms(x_ref[...], g).astype(BF16)
    off = 0
    for o_ref in (oa_ref, ob_ref, oc_ref, od_ref):
        width = o_ref.shape[-1]
        for c in range(0, width, IN_CHUNK):
            y = jnp.dot(h, w_ref[:, off + c:off + c + IN_CHUNK], preferred_element_type=F32)
            if o_ref is oc_ref:
                cs_ref[SUBLANES:SUBLANES + IN_TM, c:c + IN_CHUNK] = y
            else:
                o_ref[:, c:c + IN_CHUNK] = y
        off += width
    off_c = oa_ref.shape[-1] + ob_ref.shape[-1]
    hh = _rms(jnp.concatenate([xp_ref[...], xn_ref[...]], axis=0), g).astype(BF16)
    halo = jnp.dot(hh, w_ref[:, off_c:off_c + IN_C], preferred_element_type=F32)
    keep_prev = jnp.where(i % tiles_per_seq != 0, 1.0, 0.0)
    keep_next = jnp.where(i % tiles_per_seq != tiles_per_seq - 1, 1.0, 0.0)
    cs_ref[0:SUBLANES, :] = halo[0:SUBLANES] * keep_prev
    cs_ref[SUBLANES + IN_TM:2 * SUBLANES + IN_TM, :] = halo[SUBLANES:2 * SUBLANES] * keep_next
    left = C_CONV // 2
    for r0 in range(0, IN_TM, IN_CONV_ROWS):
        for c in range(0, IN_C, IN_CHUNK):
            win = cs_ref[r0:r0 + IN_CONV_ROWS + 2 * SUBLANES, c:c + IN_CHUNK]
            y = cb_ref[:, c:c + IN_CHUNK]
            for j in range(C_CONV):
                s0 = SUBLANES + j - left
                y = y + cw_ref[j:j + 1, c:c + IN_CHUNK] * win[s0:s0 + IN_CONV_ROWS, :]
            oc_ref[r0:r0 + IN_CONV_ROWS, c:c + IN_CHUNK] = y


def _inproj(x2, g, w_bf16, cw, cb, seq):
    n = x2.shape[0]
    widths = (IN_A, IN_B, IN_C, IN_D)
    per8 = IN_TM // SUBLANES
    last8 = n // SUBLANES - 1
    return pl.pallas_call(
        functools.partial(_inproj_kernel, tiles_per_seq=seq // IN_TM),
        grid=(n // IN_TM,),
        in_specs=[pl.BlockSpec((IN_TM, D_MODEL), lambda i: (i, 0)),
                  pl.BlockSpec((SUBLANES, D_MODEL), lambda i: (jnp.maximum(i * per8 - 1, 0), 0)),
                  pl.BlockSpec((SUBLANES, D_MODEL), lambda i: (jnp.minimum((i + 1) * per8, last8), 0)),
                  _const_spec((1, D_MODEL)),
                  _const_spec((D_MODEL, IN_WIDTH)),
                  _const_spec((C_CONV, IN_C)), _const_spec((1, IN_C))],
        out_specs=[pl.BlockSpec((IN_TM, w), lambda i: (i, 0)) for w in widths],
        out_shape=[jax.ShapeDtypeStruct((n, w), F32) for w in widths],
        scratch_shapes=[pltpu.VMEM((IN_TM + 2 * SUBLANES, IN_C), F32)],
        compiler_params=_cparams("parallel"),
        name="inproj",
    )(x2, x2, x2, g.reshape(1, D_MODEL), w_bf16, cw, cb.reshape(1, IN_C))


FF_TM = 1024
FF_CHUNK = 256
FF_NCHUNK = D_FF // FF_CHUNK


def _ffn_kernel(x_ref, g_ref, w1_ref, w3_ref, w2_ref, fg_ref, o_ref, acc_ref, *, final):
    x = x_ref[...]
    h = _rms(x, g_ref[...]).astype(BF16)
    acc_ref[...] = x

    def body(c, carry):
        c0 = pl.multiple_of(c * FF_CHUNK, FF_CHUNK)
        a = jnp.dot(h, w1_ref[:, pl.ds(c0, FF_CHUNK)], preferred_element_type=F32)
        b = jnp.dot(h, w3_ref[:, pl.ds(c0, FF_CHUNK)], preferred_element_type=F32)
        t = (a * _sigmoid(a) * b).astype(BF16)
        acc_ref[...] += jnp.dot(t, w2_ref[pl.ds(c0, FF_CHUNK), :], preferred_element_type=F32)
        return carry

    lax.fori_loop(0, FF_NCHUNK, body, 0)
    y = acc_ref[...]
    if final:
        y = _rms(y, fg_ref[...])
    o_ref[...] = y


def _ffn(x2, g, w1c, w3c, w2c, final_g, final):
    n = x2.shape[0]
    return pl.pallas_call(
        functools.partial(_ffn_kernel, final=final),
        grid=(n // FF_TM,),
        in_specs=[pl.BlockSpec((FF_TM, D_MODEL), lambda i: (i, 0)),
                  _const_spec((1, D_MODEL)),
                  _const_spec((D_MODEL, D_FF)),
                  _const_spec((D_MODEL, D_FF)),
                  _const_spec((D_FF, D_MODEL)),
                  _const_spec((1, D_MODEL))],
        out_specs=pl.BlockSpec((FF_TM, D_MODEL), lambda i: (i, 0)),
        out_shape=jax.ShapeDtypeStruct((n, D_MODEL), F32),
        scratch_shapes=[pltpu.VMEM((FF_TM, D_MODEL), F32)],
        compiler_params=_cparams("parallel"),
        name="ffn_final" if final else "ffn",
    )(x2, g.reshape(1, D_MODEL), w1c, w3c, w2c, final_g.reshape(1, D_MODEL))


def _ffn_weights(w1, w3, w2):
    return w1.astype(BF16), w3.astype(BF16), w2.astype(BF16)


MG_TM = 512


def _head_ones():
    r = np.arange(D_BRANCH)[:, None] // A_DK
    c = np.arange(D_BRANCH)[None, :] // A_DK
    return jnp.asarray((r == c).astype(np.float32) / A_DK, BF16)


def _merge_kernel(x_ref, g1_ref, oaf_ref, oab_ref, ga_ref, hg_ref, hm_ref, yb_ref, yc_ref,
                  o0_ref, o1_ref, o2_ref, l0_ref, l1_ref, l2_ref,
                  wg_ref, bg_ref, wb_ref, wo_ref, out_ref):
    x = x_ref[...]
    h = _rms(x, g1_ref[...]).astype(BF16)
    oa = oaf_ref[...] + oab_ref[...]
    ms = jnp.dot((oa * oa).astype(BF16), hm_ref[...], preferred_element_type=F32)
    ga = ga_ref[...]
    ya = oa * lax.rsqrt(ms + EPS) * hg_ref[...] * (ga * _sigmoid(ga))
    l0, l1, l2 = l0_ref[...], l1_ref[...], l2_ref[...]
    m = jnp.maximum(jnp.maximum(l0, l1), l2)
    e0, e1, e2 = jnp.exp(l0 - m), jnp.exp(l1 - m), jnp.exp(l2 - m)
    yd = (e0 * o0_ref[...] + e1 * o1_ref[...] + e2 * o2_ref[...]) / (e0 + e1 + e2)
    mixed = None
    for j, y in enumerate((ya, yb_ref[...], yc_ref[...], yd)):
        gate = _sigmoid(jnp.dot(h, wg_ref[:, j * D_MODEL:(j + 1) * D_MODEL], preferred_element_type=F32)
                        + bg_ref[:, j * D_MODEL:(j + 1) * D_MODEL])
        t = gate * jnp.dot(y.astype(BF16), wb_ref[j], preferred_element_type=F32)
        mixed = t if mixed is None else mixed + t
    out_ref[...] = x + jnp.dot(mixed.astype(BF16), wo_ref[...], preferred_element_type=F32)


def _merge(x2, g1, oa_f, oa_b, pa, hg, yb, yc, od, ld, wg, bg, wb, wo):
    n = x2.shape[0]
    tile = lambda w: pl.BlockSpec((MG_TM, w), lambda i: (i, 0))
    return pl.pallas_call(
        _merge_kernel,
        grid=(n // MG_TM,),
        in_specs=[tile(D_MODEL), _const_spec((1, D_MODEL)),
                  tile(D_BRANCH), tile(D_BRANCH),
                  pl.BlockSpec((MG_TM, D_BRANCH), lambda i: (i, 4)),
                  _const_spec((1, D_BRANCH)), _const_spec((D_BRANCH, D_BRANCH)),
                  tile(D_BRANCH), tile(D_BRANCH),
                  tile(D_BRANCH), tile(D_BRANCH), tile(D_BRANCH),
                  tile(D_BRANCH), tile(D_BRANCH), tile(D_BRANCH),
                  _const_spec((D_MODEL, N_BRANCH * D_MODEL)), _const_spec((1, N_BRANCH * D_MODEL)),
                  _const_spec((N_BRANCH, D_BRANCH, D_MODEL)), _const_spec((D_MODEL, D_MODEL))],
        out_specs=tile(D_MODEL),
        out_shape=jax.ShapeDtypeStruct((n, D_MODEL), F32),
        compiler_params=_cparams("parallel"),
        name="merge",
    )(x2, g1.reshape(1, D_MODEL), oa_f, oa_b, pa, hg.reshape(1, D_BRANCH), _head_ones(), yb, yc,
      od[0], od[1], od[2], ld[0], ld[1], ld[2], wg, bg, wb, wo)


HG_TS = 256
HG_NCH = HG_TS // A_CHUNK
HG_MID = A_CHUNK // 2
HG_BR = 4


def _hgrn_tables():
    r = np.arange(HG_TS)[:, None]
    c = np.arange(HG_TS)[None, :]
    same = (r // A_CHUNK) == (c // A_CHUNK)
    rr = np.arange(HG_NCH * A_HEADS * A_CHUNK)[:, None]
    same_s = (rr // (A_HEADS * A_CHUNK)) == (c // A_CHUNK)
    t, sidx = rr % A_CHUNK, c % A_CHUNK
    out = []
    for fwd in (True, False):
        order = (r >= c) if fwd else (r <= c)
        order_s = (t >= sidx) if fwd else (t <= sidx)
        out.append((jnp.asarray((same & order).astype(np.float32), BF16),
                    jnp.asarray((same_s & order_s).astype(np.float32), F32)))
    return out


def _hgrn_prep(q, fl, v, lb, cum, smask, fwd):
    lane_head = lax.broadcasted_iota(jnp.int32, (A_CHUNK, D_BRANCH), 1) // A_DK
    hmask = [(lane_head == hh).astype(F32) for hh in range(A_HEADS)]
    blk_r = lax.broadcasted_iota(jnp.int32, (D_BRANCH, D_BRANCH), 0) // A_DK
    blk_c = lax.broadcasted_iota(jnp.int32, (D_BRANCH, D_BRANCH), 1) // A_DK
    blockdiag = (blk_r == blk_c).astype(F32)
    col_chunk = lax.broadcasted_iota(jnp.int32, (D_BRANCH, HG_TS), 1) // A_CHUNK

    sg = _sigmoid(fl)
    f = lb + (1.0 - lb) * sg
    g = jnp.log(jnp.maximum(f, TINY))
    kk = (1.0 - lb) * (1.0 - sg)
    g1 = g.astype(BF16)
    r1 = g - g1.astype(F32)
    g2 = r1.astype(BF16)
    g3 = (r1 - g2.astype(F32)).astype(BF16)
    b = (jnp.dot(cum, g1, preferred_element_type=F32) + jnp.dot(cum, g2, preferred_element_type=F32)
         + jnp.dot(cum, g3, preferred_element_type=F32))
    rows = lambda c: slice(c * A_CHUNK, (c + 1) * A_CHUNK)
    last = (A_CHUNK - 1) if fwd else 0
    bcast = lambda r0: jnp.concatenate(
        [jnp.broadcast_to(b[c * A_CHUNK + r0:c * A_CHUNK + r0 + 1, :], (A_CHUNK, D_BRANCH)) for c in range(HG_NCH)], 0)
    bm = bcast(HG_MID)
    bl = bcast(last)
    qt = q * jnp.exp(b - bm)
    kt = (kk * jnp.exp(bm - b)).astype(BF16)
    qe = (q * jnp.exp(b)).astype(BF16)
    kh = (kk * jnp.exp(bl - b)).astype(BF16)
    qs = jnp.concatenate([qt[rows(c)] * hmask[hh] for c in range(HG_NCH) for hh in range(A_HEADS)],
                         axis=0).astype(BF16)
    s = lax.dot_general(qs, kt, (((1,), (1,)), ((), ())), preferred_element_type=F32)
    s = (s * smask).astype(BF16)
    ost = jnp.dot(s, v.astype(BF16), preferred_element_type=F32)
    o_intra = []
    for c in range(HG_NCH):
        base = c * A_HEADS * A_CHUNK
        o = ost[base:base + A_CHUNK] * hmask[0]
        for hh in range(1, A_HEADS):
            o = o + ost[base + hh * A_CHUNK:base + (hh + 1) * A_CHUNK] * hmask[hh]
        o_intra.append(o)
    vt = v.T
    lhs = jnp.concatenate([jnp.where(col_chunk == c, vt, 0.0) for c in range(HG_NCH)], axis=0).astype(BF16)
    updall = jnp.dot(lhs, kh, preferred_element_type=F32)
    upd = [updall[c * D_BRANCH:(c + 1) * D_BRANCH] * blockdiag for c in range(HG_NCH)]
    decay = [jnp.exp(b[c * A_CHUNK + last:c * A_CHUNK + last + 1, :]) for c in range(HG_NCH)]
    return o_intra, upd, qe, decay


def _hgrn_scan(o_intra, upd, qe, decay, st, fwd):
    rows = lambda c: slice(c * A_CHUNK, (c + 1) * A_CHUNK)
    outs = [None] * HG_NCH
    for c in (range(HG_NCH) if fwd else range(HG_NCH - 1, -1, -1)):
        outs[c] = o_intra[c] + lax.dot_general(qe[rows(c)], st.astype(BF16), (((1,), (1,)), ((), ())),
                                               preferred_element_type=F32)
        st = st * decay[c] + upd[c]
    return jnp.concatenate(outs, axis=0), st


def _hgrn_kernel(qf_ref, ff_ref, vf_ref, qb_ref, fb_ref, vb_ref, lb_ref, cf_ref, mf_ref, cb_ref, mb_ref,
                 of_ref, ob_ref, sf_ref, sb_ref):
    @pl.when(pl.program_id(1) == 0)
    def _():
        sf_ref[...] = jnp.zeros_like(sf_ref)
        sb_ref[...] = jnp.zeros_like(sb_ref)

    lb = lb_ref[...]
    pf = [_hgrn_prep(qf_ref[r], ff_ref[r], vf_ref[r], lb, cf_ref[...], mf_ref[...], True) for r in range(HG_BR)]
    pb = [_hgrn_prep(qb_ref[r], fb_ref[r], vb_ref[r], lb, cb_ref[...], mb_ref[...], False) for r in range(HG_BR)]
    for r in range(HG_BR):
        o_f, st_f = _hgrn_scan(*pf[r], sf_ref[r], True)
        o_b, st_b = _hgrn_scan(*pb[r], sb_ref[r], False)
        of_ref[r] = o_f
        ob_ref[r] = o_b
        sf_ref[r] = st_f
        sb_ref[r] = st_b


def _hgrn(pa3, lb):
    bsz, seq, _ = pa3.shape
    nblk = seq // HG_TS
    blk = (HG_BR, HG_TS, D_BRANCH)
    up = lambda col: pl.BlockSpec(blk, lambda b, i: (b, i, col))
    down = lambda col: pl.BlockSpec(blk, lambda b, i: (b, nblk - 1 - i, col))
    shp = jax.ShapeDtypeStruct((bsz, seq, D_BRANCH), F32)
    (cum_f, sm_f), (cum_b, sm_b) = _hgrn_tables()
    mshape = (HG_NCH * A_HEADS * A_CHUNK, HG_TS)
    state = pltpu.VMEM((HG_BR, D_BRANCH, D_BRANCH), F32)
    return pl.pallas_call(
        _hgrn_kernel,
        grid=(bsz // HG_BR, nblk),
        in_specs=[up(0), up(1), up(3), down(0), down(2), down(3), _const_spec((1, D_BRANCH)),
                  _const_spec((HG_TS, HG_TS)), _const_spec(mshape), _const_spec((HG_TS, HG_TS)), _const_spec(mshape)],
        out_specs=[up(0), down(0)],
        out_shape=[shp, shp],
        scratch_shapes=[state, state],
        compiler_params=_cparams("parallel", "arbitrary"),
        name="hgrn2",
    )(pa3, pa3, pa3, pa3, pa3, pa3, lb.reshape(1, D_BRANCH), cum_f, sm_f, cum_b, sm_b)


RG_TB = 1024
RG_PAD = SUBLANES
RG_LEFT = B_CONV // 2


def _dot3_rhs(x, wh, wl):
    xh = x.astype(BF16)
    xl = (x - xh.astype(F32)).astype(BF16)
    return (jnp.dot(xh, wh, preferred_element_type=F32) + jnp.dot(xl, wh, preferred_element_type=F32)
            + jnp.dot(xh, wl, preferred_element_type=F32))


def _group_scan(a, u, fwd):
    row = lax.broadcasted_iota(jnp.int32, a.shape, 1)
    k = 1
    while k < SUBLANES:
        if fwd:
            keep = row >= k
            us, as_ = pltpu.roll(u, k, 1), pltpu.roll(a, k, 1)
        else:
            keep = row < SUBLANES - k
            us, as_ = pltpu.roll(u, SUBLANES - k, 1), pltpu.roll(a, SUBLANES - k, 1)
        u = a * jnp.where(keep, us, 0.0) + u
        a = a * jnp.where(keep, as_, 1.0)
        k *= 2
    return a, u


def _block_scan(a, u, carry, fwd):
    t = a.shape[0]
    ngrp = t // SUBLANES
    ag, ug = _group_scan(a.reshape(ngrp, SUBLANES, D_BRANCH), u.reshape(ngrp, SUBLANES, D_BRANCH), fwd)
    hs = [None] * ngrp
    for g in (range(ngrp) if fwd else range(ngrp - 1, -1, -1)):
        h = ug[g] + ag[g] * carry
        hs[g] = h
        carry = h[SUBLANES - 1:SUBLANES, :] if fwd else h[0:1, :]
    return jnp.concatenate(hs, axis=0), carry


def _gelu_tanh(x):
    return 0.5 * x * (1.0 + jnp.tanh(math.sqrt(2.0 / math.pi) * (x + 0.044715 * (x * x * x))))


def _rglru_kernel(x_ref, gt_ref, cw_ref, cb_ref, wh_ref, wl_ref, bg_ref, lam_ref, o_ref, xp_ref, xc_ref):
    seq = x_ref.shape[0]
    nblk = seq // RG_TB
    xp_ref[0:RG_PAD, :] = jnp.zeros((RG_PAD, D_BRANCH), F32)
    xp_ref[RG_PAD + seq:2 * RG_PAD + seq, :] = jnp.zeros((RG_PAD, D_BRANCH), F32)
    xp_ref[RG_PAD:RG_PAD + seq, :] = x_ref[...]
    nl = -lam_ref[...]
    sp = jnp.maximum(nl, 0.0) + jnp.log(1.0 + jnp.exp(-jnp.abs(nl)))

    def block(i, carry, dirn):
        r0 = pl.multiple_of(i * RG_TB, RG_TB)
        if dirn == 0:
            win = xp_ref[pl.ds(r0, RG_TB + 2 * RG_PAD), :]
            xc = cb_ref[...]
            for j in range(B_CONV):
                s0 = RG_PAD + j - RG_LEFT
                xc = xc + cw_ref[j:j + 1, :] * win[s0:s0 + RG_TB, :]
            xc_ref[pl.ds(r0, RG_TB), :] = xc
        else:
            xc = xc_ref[pl.ds(r0, RG_TB), :]
        cols = slice(dirn * 2 * D_BRANCH, (dirn + 1) * 2 * D_BRANCH)
        gates = _dot3_rhs(xc, wh_ref[:, cols], wl_ref[:, cols]) + bg_ref[:, cols]
        r = _sigmoid(gates[:, :D_BRANCH])
        ig = _sigmoid(gates[:, D_BRANCH:])
        log_a = -LRU_C * r * sp[dirn:dirn + 1, :]
        a = jnp.exp(log_a)
        u = jnp.sqrt(jnp.maximum(-jnp.tanh(log_a) * (a * a + 1.0), 0.0)) * ig * xc
        h, carry = _block_scan(a, u, carry, dirn == 0)
        if dirn == 0:
            o_ref[pl.ds(r0, RG_TB), :] = h
        else:
            o_ref[pl.ds(r0, RG_TB), :] = (o_ref[pl.ds(r0, RG_TB), :] + h) * _gelu_tanh(gt_ref[pl.ds(r0, RG_TB), :])
        return carry

    zero = jnp.zeros((1, D_BRANCH), F32)
    lax.fori_loop(0, nblk, lambda i, c: block(i, c, 0), zero)
    lax.fori_loop(0, nblk, lambda i, c: block(nblk - 1 - i, c, 1), zero)


def _blockdiag(w):
    eye = jnp.eye(B_BLOCKS, dtype=w.dtype)
    return jnp.einsum('ncd,nm->ncmd', w, eye).reshape(D_BRANCH, D_BRANCH)


def _rglru(pb3, cw, cb, wa, ba, wx, bx, lam):
    bsz, seq, _ = pb3.shape
    wg = jnp.concatenate([_blockdiag(wa[0]), _blockdiag(wx[0]), _blockdiag(wa[1]), _blockdiag(wx[1])], axis=1)
    bg = jnp.concatenate([ba[0], bx[0], ba[1], bx[1]]).reshape(1, 4 * D_BRANCH)
    wh = wg.astype(BF16)
    wl = (wg - wh.astype(F32)).astype(BF16)
    blk = (None, seq, D_BRANCH)
    return pl.pallas_call(
        _rglru_kernel,
        grid=(bsz,),
        in_specs=[pl.BlockSpec(blk, lambda b: (b, 0, 0)),
                  pl.BlockSpec(blk, lambda b: (b, 0, 1)),
                  _const_spec((B_CONV, D_BRANCH)), _const_spec((1, D_BRANCH)),
                  _const_spec((D_BRANCH, 4 * D_BRANCH)), _const_spec((D_BRANCH, 4 * D_BRANCH)),
                  _const_spec((1, 4 * D_BRANCH)), _const_spec((2, D_BRANCH))],
        out_specs=pl.BlockSpec(blk, lambda b: (b, 0, 0)),
        out_shape=jax.ShapeDtypeStruct((bsz, seq, D_BRANCH), F32),
        scratch_shapes=[pltpu.VMEM((seq + 2 * RG_PAD, D_BRANCH), F32), pltpu.VMEM((seq, D_BRANCH), F32)],
        compiler_params=_cparams("parallel"),
        name="rglru",
    )(pb3, pb3, cw, cb.reshape(1, D_BRANCH), wh, wl, bg, lam)


FFT_IN = 128
FFT_NT = FFT_IN // SUBLANES
HY_TB = 1024


def _split_np(a):
    a = np.asarray(a, np.float32)
    hi = a.astype(jnp.bfloat16)
    lo = (a - hi.astype(np.float32)).astype(jnp.bfloat16)
    return jnp.asarray(hi), jnp.asarray(lo)


def _dot3(mh, ml, x):
    xh = x.astype(BF16)
    xl = (x - xh.astype(F32)).astype(BF16)
    return (jnp.dot(mh, xh, preferred_element_type=F32) + jnp.dot(mh, xl, preferred_element_type=F32)
            + jnp.dot(ml, xh, preferred_element_type=F32))


def _dotp(mh, ml, x, passes):
    if passes == 1:
        return jnp.dot(mh, x.astype(BF16), preferred_element_type=F32)
    return _dot3(mh, ml, x)


def _pack_pair(re, im):
    rb = lax.bitcast_convert_type(re.astype(BF16).astype(F32), jnp.uint32)
    ib = lax.bitcast_convert_type(im.astype(BF16).astype(F32), jnp.uint32)
    return rb | (ib >> BF16_BITS)


def _unpack_pair(w):
    re = lax.bitcast_convert_type(w & jnp.uint32(BF16_HIGH_MASK), F32)
    im = lax.bitcast_convert_type(w << BF16_BITS, F32)
    return re.astype(BF16), im.astype(BF16)


def _fft_tables(seq):
    n = 2 * seq
    n1 = n // FFT_IN
    half = n1 // 2
    eye = np.eye(SUBLANES)
    a = 2.0 * np.pi * np.outer(np.arange(n1), np.arange(half)) / n1
    gr, gi = np.cos(a), -np.sin(a)
    blk = np.stack([np.stack([gr, -gi], axis=1), np.stack([gi, gr], axis=1)], axis=0)
    m_out = np.einsum('rkis,cd->rkcisd', blk, eye).reshape(2 * n1 * SUBLANES, 2 * half * SUBLANES)
    ir, ii = gr.T / n, -gi.T / n
    blk = np.stack([np.stack([ir, -ii], axis=1), np.stack([ii, ir], axis=1)], axis=0)
    m_inv = np.einsum('otrk,cd->otcrkd', blk, eye).reshape(2 * half * SUBLANES, 2 * n1 * SUBLANES)
    a = 2.0 * np.pi * np.outer(np.arange(FFT_IN), np.arange(FFT_IN)) / FFT_IN
    fr, fi = np.cos(a), -np.sin(a)
    w_fwd = np.block([[fr, -fi], [fi, fr]])
    w_inv = np.block([[fr, fi], [-fi, fr]])
    s_in = SUBLANES * np.arange(FFT_NT)[:, None, None] + np.arange(SUBLANES)[None, None, :]
    th = 2.0 * np.pi * s_in * np.arange(n1)[None, :, None] / n
    tw = (jnp.asarray(np.cos(th)[..., None], F32), jnp.asarray(-np.sin(th)[..., None], F32))
    th = 2.0 * np.pi * np.outer(np.arange(n1), np.arange(FFT_IN)) / n
    tw_in = (jnp.asarray(np.cos(th)[..., None], F32), jnp.asarray(-np.sin(th)[..., None], F32))
    return dict(m_out=_split_np(m_out), m_inv=_split_np(m_inv), w_fwd=_split_np(w_fwd), w_inv=_split_np(w_inv),
                tw=tw, tw_in=tw_in, n1=n1, half=half)


FFT_PP = 4


def _tw_spec(n1):
    return pl.BlockSpec((None, n1, SUBLANES, 1), lambda t, q: (t, 0, 0, 0))


def _fft_outer_kernel(mh_ref, ml_ref, twr_ref, twi_ref, z_ref, v_ref, *, complex_in, passes, packed):
    n1 = v_ref.shape[1]
    tr, ti = twr_ref[...], twi_ref[...]
    for pp in range(FFT_PP):
        zz = z_ref[:, pp] if complex_in else z_ref[pp]
        rows_in = math.prod(zz.shape[:-1])
        v = _dotp(mh_ref[:, :rows_in], ml_ref[:, :rows_in], zz.reshape(rows_in, D_BRANCH), passes)
        vr = v[:n1 * SUBLANES].reshape(n1, SUBLANES, D_BRANCH)
        vi = v[n1 * SUBLANES:].reshape(n1, SUBLANES, D_BRANCH)
        wr, wi = vr * tr - vi * ti, vr * ti + vi * tr
        if packed:
            v_ref[pp] = _pack_pair(wr, wi)
        else:
            v_ref[pp, :, 0] = wr
            v_ref[pp, :, 1] = wi


def _fft_outer(tb, z, complex_in, passes, packed):
    n1, half = tb["n1"], tb["half"]
    if complex_in:
        p = z.shape[1]
        zspec = pl.BlockSpec((2, FFT_PP, half, None, SUBLANES, D_BRANCH), lambda t, q: (0, q, 0, t, 0, 0))
    else:
        p = z.shape[0]
        zspec = pl.BlockSpec((FFT_PP, half, None, SUBLANES, D_BRANCH), lambda t, q: (q, 0, t, 0, 0))
    mshape = (2 * n1 * SUBLANES, 2 * half * SUBLANES)
    if packed:
        out_spec = pl.BlockSpec((FFT_PP, n1, None, SUBLANES, D_BRANCH), lambda t, q: (q, 0, t, 0, 0))
        out_shape = jax.ShapeDtypeStruct((p, n1, FFT_NT, SUBLANES, D_BRANCH), jnp.uint32)
    else:
        out_spec = pl.BlockSpec((FFT_PP, n1, 2, None, SUBLANES, D_BRANCH), lambda t, q: (q, 0, 0, t, 0, 0))
        out_shape = jax.ShapeDtypeStruct((p, n1, 2, FFT_NT, SUBLANES, D_BRANCH), F32)
    return pl.pallas_call(
        functools.partial(_fft_outer_kernel, complex_in=complex_in, passes=passes, packed=packed),
        grid=(FFT_NT, p // FFT_PP),
        in_specs=[_const_spec(mshape), _const_spec(mshape), _tw_spec(n1), _tw_spec(n1), zspec],
        out_specs=out_spec,
        out_shape=out_shape,
        compiler_params=_cparams("arbitrary", "arbitrary"),
        name="fft_outer_c" if complex_in else "fft_outer_r",
    )(*tb["m_out"], *tb["tw"], z)


FFT_KB = 16


def _fft_filt_kernel(wh_ref, wl_ref, vf_ref, vb_ref, h_ref):
    for kb in range(FFT_KB):
        zf = _dot3(wh_ref[...], wl_ref[...], vf_ref[kb].reshape(2 * FFT_IN, D_BRANCH))
        zb = _dot3(wh_ref[...], wl_ref[...], vb_ref[kb].reshape(2 * FFT_IN, D_BRANCH))
        h_ref[kb, 0] = zf[:FFT_IN] + zb[:FFT_IN]
        h_ref[kb, 1] = zf[FFT_IN:] - zb[FFT_IN:]


def _fft_filt(tb, v):
    n1 = v.shape[1]
    wspec = _const_spec((2 * FFT_IN, 2 * FFT_IN))
    vblk = (None, FFT_KB, 2, FFT_NT, SUBLANES, D_BRANCH)
    return pl.pallas_call(
        _fft_filt_kernel,
        grid=(C_ORDER, n1 // FFT_KB),
        in_specs=[wspec, wspec,
                  pl.BlockSpec(vblk, lambda o, k: (2 * o, k, 0, 0, 0, 0)),
                  pl.BlockSpec(vblk, lambda o, k: (2 * o + 1, k, 0, 0, 0, 0))],
        out_specs=pl.BlockSpec((None, FFT_KB, 2, FFT_IN, D_BRANCH), lambda o, k: (o, k, 0, 0, 0)),
        out_shape=jax.ShapeDtypeStruct((C_ORDER, n1, 2, FFT_IN, D_BRANCH), F32),
        compiler_params=_cparams("parallel", "parallel"),
        name="fft_filt",
    )(*tb["w_fwd"], v, v)


def _fft_mid_kernel(wf_ref, wi_ref, twr_ref, twi_ref, v_ref, h_ref, d_ref):
    for kb in range(FFT_KB):
        vr, vi = _unpack_pair(v_ref[kb].reshape(FFT_IN, D_BRANCH))
        z = jnp.dot(wf_ref[...], jnp.concatenate([vr, vi], axis=0), preferred_element_type=F32)
        zr, zi = z[:FFT_IN], z[FFT_IN:]
        hr, hi = h_ref[kb, 0], h_ref[kb, 1]
        pr = (zr * hr - zi * hi).astype(BF16)
        pi = (zr * hi + zi * hr).astype(BF16)
        d = jnp.dot(wi_ref[...], jnp.concatenate([pr, pi], axis=0), preferred_element_type=F32)
        dr, di = d[:FFT_IN], d[FFT_IN:]
        tr, ti = twr_ref[kb], twi_ref[kb]
        er, ei = dr * tr + di * ti, di * tr - dr * ti
        d_ref[kb] = _pack_pair(er, ei).reshape(d_ref.shape[1:])


def _fft_mid(tb, v, hspec, order):
    p, n1 = v.shape[:2]
    vspec = pl.BlockSpec((None, FFT_KB, FFT_NT, SUBLANES, D_BRANCH), lambda k, q: (q, k, 0, 0, 0))
    wspec = _const_spec((2 * FFT_IN, 2 * FFT_IN))
    tspec = pl.BlockSpec((FFT_KB, FFT_IN, 1), lambda k, q: (k, 0, 0))
    return pl.pallas_call(
        _fft_mid_kernel,
        grid=(n1 // FFT_KB, p),
        in_specs=[wspec, wspec, tspec, tspec, vspec,
                  pl.BlockSpec((None, FFT_KB, 2, FFT_IN, D_BRANCH), lambda k, q: (order, k, 0, 0, 0))],
        out_specs=vspec,
        out_shape=jax.ShapeDtypeStruct(v.shape, jnp.uint32),
        compiler_params=_cparams("parallel", "arbitrary"),
        name="fft_mid",
    )(tb["w_fwd"][0], tb["w_inv"][0], *tb["tw_in"], v, hspec)


def _ifft_outer_kernel(mh_ref, d_ref, u_ref, x_ref, b_ref, o_ref):
    n1 = d_ref.shape[1]
    for pp in range(FFT_PP):
        er, ei = _unpack_pair(d_ref[pp])
        e = jnp.concatenate([er.reshape(n1 * SUBLANES, D_BRANCH), ei.reshape(n1 * SUBLANES, D_BRANCH)], axis=0)
        y = jnp.dot(mh_ref[...], e, preferred_element_type=F32).reshape((2,) + o_ref.shape[2:])
        o_ref[:, pp] = x_ref[:, pp] * (y + u_ref[:, pp] * b_ref[...])


def _ifft_outer(tb, d, u, ucol, x, xcol, bias):
    n1, half = tb["n1"], tb["half"]
    p = d.shape[0]
    mshape = (2 * half * SUBLANES, 2 * n1 * SUBLANES)
    io = lambda col: pl.BlockSpec((2, FFT_PP, half, None, SUBLANES, D_BRANCH), lambda t, q: (0, q, 0, t, 0, col))
    return pl.pallas_call(
        _ifft_outer_kernel,
        grid=(FFT_NT, p // FFT_PP),
        in_specs=[_const_spec(mshape),
                  pl.BlockSpec((FFT_PP, n1, None, SUBLANES, D_BRANCH), lambda t, q: (q, 0, t, 0, 0)),
                  io(ucol), io(xcol), _const_spec((1, D_BRANCH))],
        out_specs=io(0),
        out_shape=jax.ShapeDtypeStruct((2, p, half, FFT_NT, SUBLANES, D_BRANCH), F32),
        compiler_params=_cparams("arbitrary", "arbitrary"),
        name="ifft_outer",
    )(tb["m_inv"][0], d, u, x, bias.reshape(1, D_BRANCH))


def _hyfilt_kernel(z_ref, w1_ref, b1_ref, fr_ref, w2_ref, b2_ref, w3_ref, dec_ref, o_ref, h_ref):
    seq = z_ref.shape[0]
    nblk = seq // HY_TB
    fr = fr_ref[...]

    @pl.when(pl.program_id(0) == 0)
    def _():
        def hidden(i, carry):
            r0 = pl.multiple_of(i * HY_TB, HY_TB)
            zb = z_ref[pl.ds(r0, HY_TB), :]
            h = jnp.sin(fr * (jnp.dot(zb, w1_ref[...], precision=HI, preferred_element_type=F32) + b1_ref[...]))
            h = jnp.sin(fr * (jnp.dot(h, w2_ref[...], precision=HI, preferred_element_type=F32) + b2_ref[...]))
            h_ref[pl.ds(r0, HY_TB), :] = h
            return carry

        lax.fori_loop(0, nblk, hidden, 0)

    def body(i, ss):
        r0 = pl.multiple_of(i * HY_TB, HY_TB)
        hf = jnp.dot(h_ref[pl.ds(r0, HY_TB), :], w3_ref[...], precision=HI, preferred_element_type=F32)
        hf = hf * jnp.exp(-z_ref[pl.ds(r0, HY_TB), 0:1] * dec_ref[...])
        o_ref[pl.ds(r0, HY_TB), :] = hf
        return ss + jnp.sum(hf * hf, axis=0, keepdims=True)

    ss = lax.fori_loop(0, nblk, body, jnp.zeros((1, D_BRANCH), F32))
    scale = lax.rsqrt(ss + EPS)

    def norm(i, carry):
        r0 = pl.multiple_of(i * HY_TB, HY_TB)
        o_ref[pl.ds(r0, HY_TB), :] = o_ref[pl.ds(r0, HY_TB), :] * scale
        return carry

    lax.fori_loop(0, nblk, norm, 0)


def _hyfilt(seq, w1, b1, freq, w2, b2, w3):
    t = jnp.linspace(0.0, 1.0, seq, dtype=F32)[:, None]
    bands = (C_EMB - 1) // 2
    w = 2.0 * math.pi * jnp.arange(seq, dtype=F32)[:, None] / seq
    fr = jnp.linspace(1e-4, bands - 1, bands, dtype=F32)[None]
    z = jnp.concatenate([t, jnp.cos(fr * w), -jnp.sin(fr * w)], axis=-1)
    z = jnp.pad(z, ((0, 0), (0, LANES - C_EMB)))
    padm = lambda a, r, c: jnp.pad(a.astype(F32), ((0, r - a.shape[0]), (0, c - a.shape[1])))
    row = lambda a: padm(a.reshape(1, -1), 1, LANES)
    dec = jnp.abs(jnp.linspace(C_MIN_DECAY, C_MAX_DECAY, D_BRANCH, dtype=F32)).reshape(1, D_BRANCH)
    nset = C_ORDER * 2
    return pl.pallas_call(
        _hyfilt_kernel,
        grid=(nset,),
        in_specs=[_const_spec((seq, LANES)), _const_spec((LANES, LANES)), _const_spec((1, LANES)),
                  _const_spec((1, LANES)), _const_spec((LANES, LANES)), _const_spec((1, LANES)),
                  pl.BlockSpec((LANES, D_BRANCH), lambda j: (0, j)), _const_spec((1, D_BRANCH))],
        out_specs=pl.BlockSpec((None, seq, D_BRANCH), lambda j: (j, 0, 0)),
        out_shape=jax.ShapeDtypeStruct((nset, seq, D_BRANCH), F32),
        scratch_shapes=[pltpu.VMEM((seq, LANES), F32)],
        compiler_params=_cparams("arbitrary"),
        name="hyfilt",
    )(z, padm(w1, LANES, LANES), row(b1), row(freq), padm(w2, LANES, LANES), row(b2),
      padm(w3, LANES, nset * D_BRANCH), dec)


def _hyena(uc3, w1, b1, freq, w2, b2, w3, bias):
    bsz, seq, width = uc3.shape
    tb = _fft_tables(seq)
    half = tb["half"]
    npair = bsz // 2
    uc6 = uc3.reshape(2, npair, half, FFT_NT, SUBLANES, width)
    filt = _hyfilt(seq, w1, b1, freq, w2, b2, w3).reshape(C_ORDER * 2, half, FFT_NT, SUBLANES, D_BRANCH)
    hspec = _fft_filt(tb, _fft_outer(tb, filt, False, 3, False))
    src = uc6
    for order in range(C_ORDER):
        d = _fft_mid(tb, _fft_outer(tb, src, True, 1, True), hspec, order)
        src = _ifft_outer(tb, d, src, 0, uc6, order + 1, bias[order])
    return src.reshape(bsz, seq, D_BRANCH)


AT_HALF = 64
AT_TQ = 1024
AT_SUB = 128


def _t5_bucket(rel):
    half = N_BUCKETS // 2
    max_exact = half // 2
    n = np.abs(rel)
    large = max_exact + (np.log(np.maximum(n, 1) / max_exact) / math.log(MAX_DISTANCE / max_exact)
                         * (half - max_exact)).astype(np.int64)
    large = np.minimum(large, half - 1)
    return (rel > 0).astype(np.int64) * half + np.where(n < max_exact, n, large)


def _attn_geometry(n):
    tq = min(AT_TQ, n)
    sub = min(AT_SUB, tq)
    if sub + 2 * AT_HALF >= n:
        sub = tq
    win = min(sub + 2 * AT_HALF, n)
    return tq, sub, win, n // tq, n // sub


def _attn_bias_tables(rel_bias, g, dil, n):
    _, sub, win, _, nsb = _attn_geometry(n)
    hs = slice(g * D_HEADS_PER_GROUP, (g + 1) * D_HEADS_PER_GROUP)
    offsets = np.arange(-AT_HALF, AT_HALF + 1) * dil
    onehot = np.zeros((2 * AT_HALF + 1, N_BUCKETS), np.float32)
    onehot[np.arange(2 * AT_HALF + 1), _t5_bucket(offsets)] = 1.0
    band = jnp.dot(rel_bias.astype(F32)[:, hs].T, jnp.asarray(onehot).T, precision=HI)
    nband = 2 * AT_HALF + 1
    lv = sub + win - 1
    tables = []
    for i in sorted({0, min(1, nsb - 1), nsb - 1}):
        ws = int(np.clip(i * sub - AT_HALF, 0, n - win))
        lo = (sub - 1) - (ws - i * sub) - AT_HALF
        v = jnp.pad(band, ((0, 0), (lo, lv - lo - nband)), constant_values=NEG_BIG)
        flat = jnp.tile(v, (1, sub + 1))[:, sub - 1:sub - 1 + sub * (lv - 1)]
        tables.append(flat.reshape(D_HEADS_PER_GROUP, sub, lv - 1)[:, :, :win])
    return jnp.stack(tables)


def _attn_kernel(q_ref, k_ref, v_ref, bias_ref, o_ref, l_ref, *, n, dil):
    tq, sub, win, _, nsb = _attn_geometry(n)
    ncase = bias_ref.shape[0]
    width = q_ref.shape[-1]
    nh = width // D_HEAD_DIM
    hp = pl.program_id(1)
    lane_head = lax.broadcasted_iota(jnp.int32, (sub, width), 1) // D_HEAD_DIM
    hmask = [lane_head == hh for hh in range(nh)]
    whole = win == n
    for r in range(dil):
        if whole:
            kw = k_ref[pl.ds(r, win, stride=dil), :].astype(BF16)
            vw = v_ref[pl.ds(r, win, stride=dil), :].astype(BF16)
        for j in range(tq // sub):
            sidx = pl.program_id(2) * (tq // sub) + j
            case = jnp.minimum(jnp.where(sidx == nsb - 1, ncase - 1, jnp.minimum(sidx, 1)), ncase - 1)
            bias = bias_ref[case, pl.ds(hp * nh, nh)].reshape(nh * sub, win)
            if not whole:
                ws = pl.multiple_of(jnp.clip(sidx * sub - AT_HALF, 0, n - win), AT_HALF)
                kw = k_ref[pl.ds(ws * dil + r, win, stride=dil), :].astype(BF16)
                vw = v_ref[pl.ds(ws * dil + r, win, stride=dil), :].astype(BF16)
            q = q_ref[pl.ds(j * sub * dil + r, sub, stride=dil), :] * (D_HEAD_DIM ** -0.5)
            qs = jnp.concatenate([jnp.where(hmask[hh], q, 0.0) for hh in range(nh)], axis=0).astype(BF16)
            s = lax.dot_general(qs, kw, (((1,), (1,)), ((), ())), preferred_element_type=F32) + bias
            m = jnp.max(s, axis=-1, keepdims=True)
            p = jnp.exp(s - m)
            l = jnp.sum(p, axis=-1, keepdims=True)
            o_all = jnp.dot(p.astype(BF16), vw, preferred_element_type=F32) / l
            lse = m + jnp.log(l)
            o_acc = jnp.zeros((sub, width), F32)
            l_acc = jnp.zeros((sub, width), F32)
            for hh in range(nh):
                o_acc = jnp.where(hmask[hh], o_all[hh * sub:(hh + 1) * sub], o_acc)
                l_acc = jnp.where(hmask[hh], lse[hh * sub:(hh + 1) * sub], l_acc)
            o_ref[pl.ds(j * sub * dil + r, sub, stride=dil), :] = o_acc
            l_ref[pl.ds(j * sub * dil + r, sub, stride=dil), :] = l_acc


def _banded_attention(pd3, g, dil, bias):
    bsz, seq, width = pd3.shape
    n = seq // dil
    tq, sub, win, nq, _ = _attn_geometry(n)
    rows = dil * tq
    bw = LANES if dil > 1 else D_BRANCH
    per = D_BRANCH // bw
    third = width // 3 // bw
    out_spec = pl.BlockSpec((None, rows, bw), lambda b, h, i: (b, i, h))
    shp = jax.ShapeDtypeStruct((bsz, seq, D_BRANCH), F32)
    return pl.pallas_call(
        functools.partial(_attn_kernel, n=n, dil=dil),
        grid=(bsz, per, nq),
        in_specs=[pl.BlockSpec((None, rows, bw), lambda b, h, i: (b, i, g * per + h)),
                  pl.BlockSpec((None, seq, bw), lambda b, h, i: (b, 0, third + g * per + h)),
                  pl.BlockSpec((None, seq, bw), lambda b, h, i: (b, 0, 2 * third + g * per + h)),
                  _const_spec(bias.shape)],
        out_specs=[out_spec, out_spec],
        out_shape=[shp, shp],
        compiler_params=_cparams("parallel", "parallel", "arbitrary"),
        name=f"attn_d{dil}",
    )(pd3, pd3, pd3, bias)


def _dilated_attention(pd3, rel_bias):
    bsz, seq, _ = pd3.shape
    outs, lses = [], []
    for g, (_, dil) in enumerate(D_GROUPS):
        o, l = _banded_attention(pd3, g, dil, _attn_bias_tables(rel_bias, g, dil, seq // dil))
        outs.append(o.reshape(bsz * seq, D_BRANCH))
        lses.append(l.reshape(bsz * seq, D_BRANCH))
    return outs, lses


def kernel(x, norm1_g, w_in, hgrn_lb_logits, hgrn_norm_g, lru_conv_w, lru_conv_b, lru_wa, lru_ba, lru_wx, lru_bx,
           lru_lambda, hy_conv_w, hy_conv_b, hy_w1, hy_b1, hy_freq, hy_w2, hy_b2, hy_w3, hy_bias, rel_bias,
           w_branch, w_gate, b_gate, w_out, norm2_g, w_ff1, w_ff3, w_ff2, final_g):
    bsz, seq, _ = x.shape
    n = bsz * seq
    lb_soft = jax.nn.softmax(hgrn_lb_logits.astype(F32), axis=0)
    lower_bounds = jnp.cumsum(lb_soft, axis=0) - lb_soft[0]
    x2 = x.reshape(n, D_MODEL)
    flat = lambda a: a.reshape(n, D_BRANCH)
    for l in range(DEPTH):
        pa, pb, uc, pd = _inproj(x2, norm1_g[l], w_in[l].astype(BF16), hy_conv_w[l], hy_conv_b[l], seq)
        oa_f, oa_b = _hgrn(pa.reshape(bsz, seq, IN_A), lower_bounds[l])
        yb = _rglru(pb.reshape(bsz, seq, IN_B), lru_conv_w[l], lru_conv_b[l], lru_wa[l], lru_ba[l],
                    lru_wx[l], lru_bx[l], lru_lambda[l])
        yc = _hyena(uc.reshape(bsz, seq, IN_C), hy_w1[l], hy_b1[l], hy_freq[l],
                    hy_w2[l], hy_b2[l], hy_w3[l], hy_bias[l])
        od, ld = _dilated_attention(pd.reshape(bsz, seq, IN_D), rel_bias)
        x2 = _merge(x2, norm1_g[l], flat(oa_f), flat(oa_b), pa, hgrn_norm_g[l], flat(yb), flat(yc), od, ld,
                    w_gate[l].reshape(D_MODEL, N_BRANCH * D_MODEL).astype(BF16),
                    b_gate[l].reshape(1, N_BRANCH * D_MODEL), w_branch[l].astype(BF16), w_out[l].astype(BF16))
        x2 = _ffn(x2, norm2_g[l], *_ffn_weights(w_ff1[l], w_ff3[l], w_ff2[l]), final_g, l == DEPTH - 1)
    return x2.reshape(bsz, seq, D_MODEL)
```

```python
import functools
import math

import jax
import jax.numpy as jnp
import numpy as np
from jax import lax
from jax.experimental import pallas as pl
from jax.experimental.pallas import tpu as pltpu

F32 = jnp.float32
BF16 = jnp.bfloat16
HI = lax.Precision.HIGHEST

D_MODEL = 1024
DEPTH = 2
EPS = 1e-6
TINY = 1e-30
N_BRANCH = 4
D_BRANCH = 256
A_HEADS = 4
A_DK = 64
A_CHUNK = 64
B_BLOCKS = 4
B_BW = 64
B_CONV = 4
LRU_C = 8.0
C_ORDER = 2
C_CONV = 3
C_EMB = 33
C_MIN_DECAY = math.log(1e-2) / 1.5
C_MAX_DECAY = math.log(1e-2) / 0.3
D_GROUPS = ((128, 1), (512, 4), (2048, 16))
D_HEADS_PER_GROUP = 4
D_HEAD_DIM = 64
D_QKV = 768
N_BUCKETS = 32
MAX_DISTANCE = 1024
NEG_BIG = -1e30
D_FF = 2816
IN_A = 5 * D_BRANCH
IN_B = 2 * D_BRANCH
IN_C = 3 * D_BRANCH
IN_D = 3 * D_QKV
IN_WIDTH = IN_A + IN_B + IN_C + IN_D

LANES = 128
SUBLANES = 8
VMEM_LIMIT = 56 * 1024 * 1024
BF16_BITS = 16
BF16_HIGH_MASK = 0xFFFF0000


def _cparams(*sem):
    return pltpu.CompilerParams(dimension_semantics=sem, vmem_limit_bytes=VMEM_LIMIT)


def _const_spec(shape):
    nd = len(shape)
    return pl.BlockSpec(shape, lambda *_: (0,) * nd, pipeline_mode=pl.Buffered(1))


def _rms(x, g):
    return x * lax.rsqrt(jnp.mean(x * x, axis=-1, keepdims=True) + EPS) * g


def _sigmoid(x):
    return 1.0 / (1.0 + jnp.exp(-x))


IN_TM = 512
IN_CHUNK = 256
IN_CONV_ROWS = 256


def _inproj_kernel(x_ref, xp_ref, xn_ref, g_ref, w_ref, cw_ref, cb_ref, oa_ref, ob_ref, oc_ref, od_ref, cs_ref,
                   *, tiles_per_seq):
    i = pl.program_id(0)
    g = g_ref[...]
    h = _rms(x_ref[...], g).astype(BF16)
    keep_prev = jnp.where(i % tiles_per_seq != 0, 1.0, 0.0)
    keep_next = jnp.where(i % tiles_per_seq != tiles_per_seq - 1, 1.0, 0.0)
    h_ext = jnp.concatenate([(_rms(xp_ref[...], g) * keep_prev).astype(BF16), h,
                             (_rms(xn_ref[...], g) * keep_next).astype(BF16)], axis=0)
    off = 0
    for o_ref in (oa_ref, ob_ref, oc_ref, od_ref):
        width = o_ref.shape[-1]
        for c in range(0, width, IN_CHUNK):
            w = w_ref[:, off + c:off + c + IN_CHUNK]
            if o_ref is oc_ref:
                cs_ref[:, c:c + IN_CHUNK] = jnp.dot(h_ext, w, preferred_element_type=F32)
            else:
                o_ref[:, c:c + IN_CHUNK] = jnp.dot(h, w, preferred_element_type=F32)
        off += width
    left = C_CONV // 2
    for r0 in range(0, IN_TM, IN_CONV_ROWS):
        for c in range(0, IN_C, IN_CHUNK):
            win = cs_ref[r0:r0 + IN_CONV_ROWS + 2 * SUBLANES, c:c + IN_CHUNK]
            y = cb_ref[:, c:c + IN_CHUNK]
            for j in range(C_CONV):
                s0 = SUBLANES + j - left
                y = y + cw_ref[j:j + 1, c:c + IN_CHUNK] * win[s0:s0 + IN_CONV_ROWS, :]
            oc_ref[r0:r0 + IN_CONV_ROWS, c:c + IN_CHUNK] = y


def _inproj(x2, g, w_bf16, cw, cb, seq):
    n = x2.shape[0]
    widths = (IN_A, IN_B, IN_C, IN_D)
    per8 = IN_TM // SUBLANES
    last8 = n // SUBLANES - 1
    return pl.pallas_call(
        functools.partial(_inproj_kernel, tiles_per_seq=seq // IN_TM),
        grid=(n // IN_TM,),
        in_specs=[pl.BlockSpec((IN_TM, D_MODEL), lambda i: (i, 0)),
                  pl.BlockSpec((SUBLANES, D_MODEL), lambda i: (jnp.maximum(i * per8 - 1, 0), 0)),
                  pl.BlockSpec((SUBLANES, D_MODEL), lambda i: (jnp.minimum((i + 1) * per8, last8), 0)),
                  _const_spec((1, D_MODEL)),
                  _const_spec((D_MODEL, IN_WIDTH)),
                  _const_spec((C_CONV, IN_C)), _const_spec((1, IN_C))],
        out_specs=[pl.BlockSpec((IN_TM, w), lambda i: (i, 0)) for w in widths],
        out_shape=[jax.ShapeDtypeStruct((n, w), F32) for w in widths],
        scratch_shapes=[pltpu.VMEM((IN_TM + 2 * SUBLANES, IN_C), F32)],
        compiler_params=_cparams("parallel"),
        name="inproj",
    )(x2, x2, x2, g.reshape(1, D_MODEL), w_bf16, cw, cb.reshape(1, IN_C))


FF_TM = 1024
FF_CHUNK = 256
FF_NCHUNK = D_FF // FF_CHUNK


def _ffn_kernel(x_ref, g_ref, w1_ref, w3_ref, w2_ref, fg_ref, o_ref, acc_ref, *, final):
    x = x_ref[...]
    h = _rms(x, g_ref[...]).astype(BF16)
    acc_ref[...] = x

    def body(c, carry):
        c0 = pl.multiple_of(c * FF_CHUNK, FF_CHUNK)
        a = jnp.dot(h, w1_ref[:, pl.ds(c0, FF_CHUNK)], preferred_element_type=F32)
        b = jnp.dot(h, w3_ref[:, pl.ds(c0, FF_CHUNK)], preferred_element_type=F32)
        t = (a * _sigmoid(a) * b).astype(BF16)
        acc_ref[...] += jnp.dot(t, w2_ref[pl.ds(c0, FF_CHUNK), :], preferred_element_type=F32)
        return carry

    lax.fori_loop(0, FF_NCHUNK, body, 0)
    y = acc_ref[...]
    if final:
        y = _rms(y, fg_ref[...])
    o_ref[...] = y


def _ffn(x2, g, w1c, w3c, w2c, final_g, final):
    n = x2.shape[0]
    return pl.pallas_call(
        functools.partial(_ffn_kernel, final=final),
        grid=(n // FF_TM,),
        in_specs=[pl.BlockSpec((FF_TM, D_MODEL), lambda i: (i, 0)),
                  _const_spec((1, D_MODEL)),
                  _const_spec((D_MODEL, D_FF)),
                  _const_spec((D_MODEL, D_FF)),
                  _const_spec((D_FF, D_MODEL)),
                  _const_spec((1, D_MODEL))],
        out_specs=pl.BlockSpec((FF_TM, D_MODEL), lambda i: (i, 0)),
        out_shape=jax.ShapeDtypeStruct((n, D_MODEL), F32),
        scratch_shapes=[pltpu.VMEM((FF_TM, D_MODEL), F32)],
        compiler_params=_cparams("parallel"),
        name="ffn_final" if final else "ffn",
    )(x2, g.reshape(1, D_MODEL), w1c, w3c, w2c, final_g.reshape(1, D_MODEL))


def _ffn_weights(w1, w3, w2):
    return w1.astype(BF16), w3.astype(BF16), w2.astype(BF16)


MG_TM = 512


def _head_ones():
    r = np.arange(D_BRANCH)[:, None] // A_DK
    c = np.arange(D_BRANCH)[None, :] // A_DK
    return jnp.asarray((r == c).astype(np.float32) / A_DK, BF16)


def _merge_kernel(x_ref, g1_ref, oaf_ref, oab_ref, ga_ref, hg_ref, hm_ref, yb_ref, yc_ref,
                  o0_ref, o1_ref, o2_ref, l0_ref, l1_ref, l2_ref,
                  wg_ref, bg_ref, wb_ref, wo_ref, out_ref):
    x = x_ref[...]
    h = _rms(x, g1_ref[...]).astype(BF16)
    oa = oaf_ref[...] + oab_ref[...]
    ms = jnp.dot((oa * oa).astype(BF16), hm_ref[...], preferred_element_type=F32)
    ga = ga_ref[...]
    ya = oa * lax.rsqrt(ms + EPS) * hg_ref[...] * (ga * _sigmoid(ga))
    l0, l1, l2 = l0_ref[...], l1_ref[...], l2_ref[...]
    m = jnp.maximum(jnp.maximum(l0, l1), l2)
    e0, e1, e2 = jnp.exp(l0 - m), jnp.exp(l1 - m), jnp.exp(l2 - m)
    yd = (e0 * o0_ref[...] + e1 * o1_ref[...] + e2 * o2_ref[...]) / (e0 + e1 + e2)
    mixed = None
    for j, y in enumerate((ya, yb_ref[...], yc_ref[...], yd)):
        gate = _sigmoid(jnp.dot(h, wg_ref[:, j * D_MODEL:(j + 1) * D_MODEL], preferred_element_type=F32)
                        + bg_ref[:, j * D_MODEL:(j + 1) * D_MODEL])
        t = gate * jnp.dot(y.astype(BF16), wb_ref[j], preferred_element_type=F32)
        mixed = t if mixed is None else mixed + t
    out_ref[...] = x + jnp.dot(mixed.astype(BF16), wo_ref[...], preferred_element_type=F32)


def _merge(x2, g1, oa_f, oa_b, pa, hg, yb, yc, od, ld, wg, bg, wb, wo):
    n = x2.shape[0]
    tile = lambda w: pl.BlockSpec((MG_TM, w), lambda i: (i, 0))
    return pl.pallas_call(
        _merge_kernel,
        grid=(n // MG_TM,),
        in_specs=[tile(D_MODEL), _const_spec((1, D_MODEL)),
                  tile(D_BRANCH), tile(D_BRANCH),
                  pl.BlockSpec((MG_TM, D_BRANCH), lambda i: (i, 4)),
                  _const_spec((1, D_BRANCH)), _const_spec((D_BRANCH, D_BRANCH)),
                  tile(D_BRANCH), tile(D_BRANCH),
                  tile(D_BRANCH), tile(D_BRANCH), tile(D_BRANCH),
                  tile(D_BRANCH), tile(D_BRANCH), tile(D_BRANCH),
                  _const_spec((D_MODEL, N_BRANCH * D_MODEL)), _const_spec((1, N_BRANCH * D_MODEL)),
                  _const_spec((N_BRANCH, D_BRANCH, D_MODEL)), _const_spec((D_MODEL, D_MODEL))],
        out_specs=tile(D_MODEL),
        out_shape=jax.ShapeDtypeStruct((n, D_MODEL), F32),
        compiler_params=_cparams("parallel"),
        name="merge",
    )(x2, g1.reshape(1, D_MODEL), oa_f, oa_b, pa, hg.reshape(1, D_BRANCH), _head_ones(), yb, yc,
      od[0], od[1], od[2], ld[0], ld[1], ld[2], wg, bg, wb, wo)


HG_TS = 256
HG_NCH = HG_TS // A_CHUNK
HG_MID = A_CHUNK // 2
HG_BR = 4


def _hgrn_tables():
    r = np.arange(HG_TS)[:, None]
    c = np.arange(HG_TS)[None, :]
    same = (r // A_CHUNK) == (c // A_CHUNK)
    rr = np.arange(HG_NCH * A_HEADS * A_CHUNK)[:, None]
    same_s = (rr // (A_HEADS * A_CHUNK)) == (c // A_CHUNK)
    t, sidx = rr % A_CHUNK, c % A_CHUNK
    out = []
    for fwd in (True, False):
        order = (r >= c) if fwd else (r <= c)
        order_s = (t >= sidx) if fwd else (t <= sidx)
        out.append((jnp.asarray((same & order).astype(np.float32), BF16),
                    jnp.asarray((same_s & order_s).astype(np.float32), F32)))
    return out


def _hgrn_prep(q, fl, v, lb, cum, smask, fwd):
    lane_head = lax.broadcasted_iota(jnp.int32, (A_CHUNK, D_BRANCH), 1) // A_DK
    hmask = [(lane_head == hh).astype(F32) for hh in range(A_HEADS)]
    blk_r = lax.broadcasted_iota(jnp.int32, (D_BRANCH, D_BRANCH), 0) // A_DK
    blk_c = lax.broadcasted_iota(jnp.int32, (D_BRANCH, D_BRANCH), 1) // A_DK
    blockdiag = (blk_r == blk_c).astype(F32)
    col_chunk = lax.broadcasted_iota(jnp.int32, (D_BRANCH, HG_TS), 1) // A_CHUNK

    sg = _sigmoid(fl)
    f = lb + (1.0 - lb) * sg
    g = jnp.log(jnp.maximum(f, TINY))
    kk = (1.0 - lb) * (1.0 - sg)
    g1 = g.astype(BF16)
    r1 = g - g1.astype(F32)
    g2 = r1.astype(BF16)
    g3 = (r1 - g2.astype(F32)).astype(BF16)
    b = (jnp.dot(cum, g1, preferred_element_type=F32) + jnp.dot(cum, g2, preferred_element_type=F32)
         + jnp.dot(cum, g3, preferred_element_type=F32))
    rows = lambda c: slice(c * A_CHUNK, (c + 1) * A_CHUNK)
    last = (A_CHUNK - 1) if fwd else 0
    bcast = lambda r0: jnp.concatenate(
        [jnp.broadcast_to(b[c * A_CHUNK + r0:c * A_CHUNK + r0 + 1, :], (A_CHUNK, D_BRANCH)) for c in range(HG_NCH)], 0)
    bm = bcast(HG_MID)
    bl = bcast(last)
    qt = q * jnp.exp(b - bm)
    kt = (kk * jnp.exp(bm - b)).astype(BF16)
    qe = (q * jnp.exp(b)).astype(BF16)
    kh = (kk * jnp.exp(bl - b)).astype(BF16)
    qs = jnp.concatenate([qt[rows(c)] * hmask[hh] for c in range(HG_NCH) for hh in range(A_HEADS)],
                         axis=0).astype(BF16)
    s = lax.dot_general(qs, kt, (((1,), (1,)), ((), ())), preferred_element_type=F32)
    s = (s * smask).astype(BF16)
    ost = jnp.dot(s, v.astype(BF16), preferred_element_type=F32)
    o_intra = []
    for c in range(HG_NCH):
        base = c * A_HEADS * A_CHUNK
        o = ost[base:base + A_CHUNK] * hmask[0]
        for hh in range(1, A_HEADS):
            o = o + ost[base + hh * A_CHUNK:base + (hh + 1) * A_CHUNK] * hmask[hh]
        o_intra.append(o)
    vt = v.T
    lhs = jnp.concatenate([jnp.where(col_chunk == c, vt, 0.0) for c in range(HG_NCH)], axis=0).astype(BF16)
    updall = jnp.dot(lhs, kh, preferred_element_type=F32)
    upd = [updall[c * D_BRANCH:(c + 1) * D_BRANCH] * blockdiag for c in range(HG_NCH)]
    decay = [jnp.exp(b[c * A_CHUNK + last:c * A_CHUNK + last + 1, :]) for c in range(HG_NCH)]
    return o_intra, upd, qe, decay


def _hgrn_scan(o_intra, upd, qe, decay, st, fwd):
    rows = lambda c: slice(c * A_CHUNK, (c + 1) * A_CHUNK)
    outs = [None] * HG_NCH
    for c in (range(HG_NCH) if fwd else range(HG_NCH - 1, -1, -1)):
        outs[c] = o_intra[c] + lax.dot_general(qe[rows(c)], st.astype(BF16), (((1,), (1,)), ((), ())),
                                               preferred_element_type=F32)
        st = st * decay[c] + upd[c]
    return jnp.concatenate(outs, axis=0), st


def _hgrn_kernel(qf_ref, ff_ref, vf_ref, qb_ref, fb_ref, vb_ref, lb_ref, cf_ref, mf_ref, cb_ref, mb_ref,
                 of_ref, ob_ref, sf_ref, sb_ref):
    @pl.when(pl.program_id(1) == 0)
    def _():
        sf_ref[...] = jnp.zeros_like(sf_ref)
        sb_ref[...] = jnp.zeros_like(sb_ref)

    lb = lb_ref[...]
    pf = [_hgrn_prep(qf_ref[r], ff_ref[r], vf_ref[r], lb, cf_ref[...], mf_ref[...], True) for r in range(HG_BR)]
    pb = [_hgrn_prep(qb_ref[r], fb_ref[r], vb_ref[r], lb, cb_ref[...], mb_ref[...], False) for r in range(HG_BR)]
    for r in range(HG_BR):
        o_f, st_f = _hgrn_scan(*pf[r], sf_ref[r], True)
        o_b, st_b = _hgrn_scan(*pb[r], sb_ref[r], False)
        of_ref[r] = o_f
        ob_ref[r] = o_b
        sf_ref[r] = st_f
        sb_ref[r] = st_b


def _hgrn(pa3, lb):
    bsz, seq, _ = pa3.shape
    nblk = seq // HG_TS
    blk = (HG_BR, HG_TS, D_BRANCH)
    up = lambda col: pl.BlockSpec(blk, lambda b, i: (b, i, col))
    down = lambda col: pl.BlockSpec(blk, lambda b, i: (b, nblk - 1 - i, col))
    shp = jax.ShapeDtypeStruct((bsz, seq, D_BRANCH), F32)
    (cum_f, sm_f), (cum_b, sm_b) = _hgrn_tables()
    mshape = (HG_NCH * A_HEADS * A_CHUNK, HG_TS)
    state = pltpu.VMEM((HG_BR, D_BRANCH, D_BRANCH), F32)
    return pl.pallas_call(
        _hgrn_kernel,
        grid=(bsz // HG_BR, nblk),
        in_specs=[up(0), up(1), up(3), down(0), down(2), down(3), _const_spec((1, D_BRANCH)),
                  _const_spec((HG_TS, HG_TS)), _const_spec(mshape), _const_spec((HG_TS, HG_TS)), _const_spec(mshape)],
        out_specs=[up(0), down(0)],
        out_shape=[shp, shp],
        scratch_shapes=[state, state],
        compiler_params=_cparams("parallel", "arbitrary"),
        name="hgrn2",
    )(pa3, pa3, pa3, pa3, pa3, pa3, lb.reshape(1, D_BRANCH), cum_f, sm_f, cum_b, sm_b)


RG_TB = 1024
RG_PAD = SUBLANES
RG_LEFT = B_CONV // 2


def _dot3_rhs(x, wh, wl):
    xh = x.astype(BF16)
    xl = (x - xh.astype(F32)).astype(BF16)
    return (jnp.dot(xh, wh, preferred_element_type=F32) + jnp.dot(xl, wh, preferred_element_type=F32)
            + jnp.dot(xh, wl, preferred_element_type=F32))


def _group_scan(a, u, fwd):
    row = lax.broadcasted_iota(jnp.int32, a.shape, 1)
    k = 1
    while k < SUBLANES:
        if fwd:
            keep = row >= k
            us, as_ = pltpu.roll(u, k, 1), pltpu.roll(a, k, 1)
        else:
            keep = row < SUBLANES - k
            us, as_ = pltpu.roll(u, SUBLANES - k, 1), pltpu.roll(a, SUBLANES - k, 1)
        u = a * jnp.where(keep, us, 0.0) + u
        a = a * jnp.where(keep, as_, 1.0)
        k *= 2
    return a, u


def _block_scan(a, u, carry, fwd):
    t = a.shape[0]
    ngrp = t // SUBLANES
    ag, ug = _group_scan(a.reshape(ngrp, SUBLANES, D_BRANCH), u.reshape(ngrp, SUBLANES, D_BRANCH), fwd)
    hs = [None] * ngrp
    for g in (range(ngrp) if fwd else range(ngrp - 1, -1, -1)):
        h = ug[g] + ag[g] * carry
        hs[g] = h
        carry = h[SUBLANES - 1:SUBLANES, :] if fwd else h[0:1, :]
    return jnp.concatenate(hs, axis=0), carry


def _gelu_tanh(x):
    return 0.5 * x * (1.0 + jnp.tanh(math.sqrt(2.0 / math.pi) * (x + 0.044715 * (x * x * x))))


def _rglru_kernel(x_ref, gt_ref, cw_ref, cb_ref, wh_ref, wl_ref, bg_ref, lam_ref, o_ref, xp_ref, xc_ref):
    seq = x_ref.shape[0]
    nblk = seq // RG_TB
    xp_ref[0:RG_PAD, :] = jnp.zeros((RG_PAD, D_BRANCH), F32)
    xp_ref[RG_PAD + seq:2 * RG_PAD + seq, :] = jnp.zeros((RG_PAD, D_BRANCH), F32)
    xp_ref[RG_PAD:RG_PAD + seq, :] = x_ref[...]
    nl = -lam_ref[...]
    sp = jnp.maximum(nl, 0.0) + jnp.log(1.0 + jnp.exp(-jnp.abs(nl)))

    def block(i, carry, dirn):
        r0 = pl.multiple_of(i * RG_TB, RG_TB)
        if dirn == 0:
            win = xp_ref[pl.ds(r0, RG_TB + 2 * RG_PAD), :]
            xc = cb_ref[...]
            for j in range(B_CONV):
                s0 = RG_PAD + j - RG_LEFT
                xc = xc + cw_ref[j:j + 1, :] * win[s0:s0 + RG_TB, :]
            xc_ref[pl.ds(r0, RG_TB), :] = xc
        else:
            xc = xc_ref[pl.ds(r0, RG_TB), :]
        cols = slice(dirn * 2 * D_BRANCH, (dirn + 1) * 2 * D_BRANCH)
        gates = _dot3_rhs(xc, wh_ref[:, cols], wl_ref[:, cols]) + bg_ref[:, cols]
        r = _sigmoid(gates[:, :D_BRANCH])
        ig = _sigmoid(gates[:, D_BRANCH:])
        log_a = -LRU_C * r * sp[dirn:dirn + 1, :]
        a = jnp.exp(log_a)
        u = jnp.sqrt(jnp.maximum(-jnp.tanh(log_a) * (a * a + 1.0), 0.0)) * ig * xc
        h, carry = _block_scan(a, u, carry, dirn == 0)
        if dirn == 0:
            o_ref[pl.ds(r0, RG_TB), :] = h
        else:
            o_ref[pl.ds(r0, RG_TB), :] = (o_ref[pl.ds(r0, RG_TB), :] + h) * _gelu_tanh(gt_ref[pl.ds(r0, RG_TB), :])
        return carry

    zero = jnp.zeros((1, D_BRANCH), F32)
    lax.fori_loop(0, nblk, lambda i, c: block(i, c, 0), zero)
    lax.fori_loop(0, nblk, lambda i, c: block(nblk - 1 - i, c, 1), zero)


def _blockdiag(w):
    eye = jnp.eye(B_BLOCKS, dtype=w.dtype)
    return jnp.einsum('ncd,nm->ncmd', w, eye).reshape(D_BRANCH, D_BRANCH)


def _rglru(pb3, cw, cb, wa, ba, wx, bx, lam):
    bsz, seq, _ = pb3.shape
    wg = jnp.concatenate([_blockdiag(wa[0]), _blockdiag(wx[0]), _blockdiag(wa[1]), _blockdiag(wx[1])], axis=1)
    bg = jnp.concatenate([ba[0], bx[0], ba[1], bx[1]]).reshape(1, 4 * D_BRANCH)
    wh = wg.astype(BF16)
    wl = (wg - wh.astype(F32)).astype(BF16)
    blk = (None, seq, D_BRANCH)
    return pl.pallas_call(
        _rglru_kernel,
        grid=(bsz,),
        in_specs=[pl.BlockSpec(blk, lambda b: (b, 0, 0)),
                  pl.BlockSpec(blk, lambda b: (b, 0, 1)),
                  _const_spec((B_CONV, D_BRANCH)), _const_spec((1, D_BRANCH)),
                  _const_spec((D_BRANCH, 4 * D_BRANCH)), _const_spec((D_BRANCH, 4 * D_BRANCH)),
                  _const_spec((1, 4 * D_BRANCH)), _const_spec((2, D_BRANCH))],
        out_specs=pl.BlockSpec(blk, lambda b: (b, 0, 0)),
        out_shape=jax.ShapeDtypeStruct((bsz, seq, D_BRANCH), F32),
        scratch_shapes=[pltpu.VMEM((seq + 2 * RG_PAD, D_BRANCH), F32), pltpu.VMEM((seq, D_BRANCH), F32)],
        compiler_params=_cparams("parallel"),
        name="rglru",
    )(pb3, pb3, cw, cb.reshape(1, D_BRANCH), wh, wl, bg, lam)


FFT_IN = 128
FFT_NT = FFT_IN // SUBLANES
HY_TB = 1024


def _split_np(a):
    a = np.asarray(a, np.float32)
    hi = a.astype(jnp.bfloat16)
    lo = (a - hi.astype(np.float32)).astype(jnp.bfloat16)
    return jnp.asarray(hi), jnp.asarray(lo)


def _dot3(mh, ml, x):
    xh = x.astype(BF16)
    xl = (x - xh.astype(F32)).astype(BF16)
    return (jnp.dot(mh, xh, preferred_element_type=F32) + jnp.dot(mh, xl, preferred_element_type=F32)
            + jnp.dot(ml, xh, preferred_element_type=F32))


def _dotp(mh, ml, x, passes):
    if passes == 1:
        return jnp.dot(mh, x.astype(BF16), preferred_element_type=F32)
    return _dot3(mh, ml, x)


def _pack_pair(re, im):
    rb = lax.bitcast_convert_type(re.astype(BF16).astype(F32), jnp.uint32)
    ib = lax.bitcast_convert_type(im.astype(BF16).astype(F32), jnp.uint32)
    return rb | (ib >> BF16_BITS)


def _unpack_pair(w):
    re = lax.bitcast_convert_type(w & jnp.uint32(BF16_HIGH_MASK), F32)
    im = lax.bitcast_convert_type(w << BF16_BITS, F32)
    return re.astype(BF16), im.astype(BF16)


def _fft_tables(seq):
    n = 2 * seq
    n1 = n // FFT_IN
    half = n1 // 2
    eye = np.eye(SUBLANES)
    a = 2.0 * np.pi * np.outer(np.arange(n1), np.arange(half)) / n1
    gr, gi = np.cos(a), -np.sin(a)
    blk = np.stack([np.stack([gr, -gi], axis=1), np.stack([gi, gr], axis=1)], axis=0)
    m_out = np.einsum('rkis,cd->rkcisd', blk, eye).reshape(2 * n1 * SUBLANES, 2 * half * SUBLANES)
    ir, ii = gr.T / n, -gi.T / n
    blk = np.stack([np.stack([ir, -ii], axis=1), np.stack([ii, ir], axis=1)], axis=0)
    m_inv = np.einsum('otrk,cd->otcrkd', blk, eye).reshape(2 * half * SUBLANES, 2 * n1 * SUBLANES)
    a = 2.0 * np.pi * np.outer(np.arange(FFT_IN), np.arange(FFT_IN)) / FFT_IN
    fr, fi = np.cos(a), -np.sin(a)
    w_fwd = np.block([[fr, -fi], [fi, fr]])
    w_inv = np.block([[fr, fi], [-fi, fr]])
    s_in = SUBLANES * np.arange(FFT_NT)[:, None, None] + np.arange(SUBLANES)[None, None, :]
    th = 2.0 * np.pi * s_in * np.arange(n1)[None, :, None] / n
    tw = (jnp.asarray(np.cos(th)[..., None], F32), jnp.asarray(-np.sin(th)[..., None], F32))
    th = 2.0 * np.pi * np.outer(np.arange(n1), np.arange(FFT_IN)) / n
    tw_in = (jnp.asarray(np.cos(th)[..., None], F32), jnp.asarray(-np.sin(th)[..., None], F32))
    return dict(m_out=_split_np(m_out), m_inv=_split_np(m_inv), w_fwd=_split_np(w_fwd), w_inv=_split_np(w_inv),
                tw=tw, tw_in=tw_in, n1=n1, half=half)


FFT_PP = 4


def _tw_spec(n1):
    return pl.BlockSpec((None, n1, SUBLANES, 1), lambda t, q: (t, 0, 0, 0))


def _fft_outer_kernel(mh_ref, ml_ref, twr_ref, twi_ref, z_ref, v_ref, *, complex_in, passes, packed):
    n1 = v_ref.shape[1]
    tr, ti = twr_ref[...], twi_ref[...]
    for pp in range(FFT_PP):
        zz = z_ref[:, pp] if complex_in else z_ref[pp]
        rows_in = math.prod(zz.shape[:-1])
        v = _dotp(mh_ref[:, :rows_in], ml_ref[:, :rows_in], zz.reshape(rows_in, D_BRANCH), passes)
        vr = v[:n1 * SUBLANES].reshape(n1, SUBLANES, D_BRANCH)
        vi = v[n1 * SUBLANES:].reshape(n1, SUBLANES, D_BRANCH)
        wr, wi = vr * tr - vi * ti, vr * ti + vi * tr
        if packed:
            v_ref[pp] = _pack_pair(wr, wi)
        else:
            v_ref[pp, :, 0] = wr
            v_ref[pp, :, 1] = wi


def _fft_outer(tb, z, complex_in, passes, packed):
    n1, half = tb["n1"], tb["half"]
    if complex_in:
        p = z.shape[1]
        zspec = pl.BlockSpec((2, FFT_PP, half, None, SUBLANES, D_BRANCH), lambda t, q: (0, q, 0, t, 0, 0))
    else:
        p = z.shape[0]
        zspec = pl.BlockSpec((FFT_PP, half, None, SUBLANES, D_BRANCH), lambda t, q: (q, 0, t, 0, 0))
    mshape = (2 * n1 * SUBLANES, 2 * half * SUBLANES)
    if packed:
        out_spec = pl.BlockSpec((FFT_PP, n1, None, SUBLANES, D_BRANCH), lambda t, q: (q, 0, t, 0, 0))
        out_shape = jax.ShapeDtypeStruct((p, n1, FFT_NT, SUBLANES, D_BRANCH), jnp.uint32)
    else:
        out_spec = pl.BlockSpec((FFT_PP, n1, 2, None, SUBLANES, D_BRANCH), lambda t, q: (q, 0, 0, t, 0, 0))
        out_shape = jax.ShapeDtypeStruct((p, n1, 2, FFT_NT, SUBLANES, D_BRANCH), F32)
    return pl.pallas_call(
        functools.partial(_fft_outer_kernel, complex_in=complex_in, passes=passes, packed=packed),
        grid=(FFT_NT, p // FFT_PP),
        in_specs=[_const_spec(mshape), _const_spec(mshape), _tw_spec(n1), _tw_spec(n1), zspec],
        out_specs=out_spec,
        out_shape=out_shape,
        compiler_params=_cparams("arbitrary", "arbitrary"),
        name="fft_outer_c" if complex_in else "fft_outer_r",
    )(*tb["m_out"], *tb["tw"], z)


FFT_KB = 16


def _fft_filt_kernel(wh_ref, wl_ref, vf_ref, vb_ref, h_ref):
    for kb in range(FFT_KB):
        zf = _dot3(wh_ref[...], wl_ref[...], vf_ref[kb].reshape(2 * FFT_IN, D_BRANCH))
        zb = _dot3(wh_ref[...], wl_ref[...], vb_ref[kb].reshape(2 * FFT_IN, D_BRANCH))
        h_ref[kb, 0] = zf[:FFT_IN] + zb[:FFT_IN]
        h_ref[kb, 1] = zf[FFT_IN:] - zb[FFT_IN:]


def _fft_filt(tb, v):
    n1 = v.shape[1]
    wspec = _const_spec((2 * FFT_IN, 2 * FFT_IN))
    vblk = (None, FFT_KB, 2, FFT_NT, SUBLANES, D_BRANCH)
    return pl.pallas_call(
        _fft_filt_kernel,
        grid=(C_ORDER, n1 // FFT_KB),
        in_specs=[wspec, wspec,
                  pl.BlockSpec(vblk, lambda o, k: (2 * o, k, 0, 0, 0, 0)),
                  pl.BlockSpec(vblk, lambda o, k: (2 * o + 1, k, 0, 0, 0, 0))],
        out_specs=pl.BlockSpec((None, FFT_KB, 2, FFT_IN, D_BRANCH), lambda o, k: (o, k, 0, 0, 0)),
        out_shape=jax.ShapeDtypeStruct((C_ORDER, n1, 2, FFT_IN, D_BRANCH), F32),
        compiler_params=_cparams("parallel", "parallel"),
        name="fft_filt",
    )(*tb["w_fwd"], v, v)


def _fft_mid_kernel(wf_ref, wi_ref, twr_ref, twi_ref, v_ref, h_ref, d_ref):
    for kb in range(FFT_KB):
        vr, vi = _unpack_pair(v_ref[kb].reshape(FFT_IN, D_BRANCH))
        z = jnp.dot(wf_ref[...], jnp.concatenate([vr, vi], axis=0), preferred_element_type=F32)
        zr, zi = z[:FFT_IN], z[FFT_IN:]
        hr, hi = h_ref[kb, 0], h_ref[kb, 1]
        pr = (zr * hr - zi * hi).astype(BF16)
        pi = (zr * hi + zi * hr).astype(BF16)
        d = jnp.dot(wi_ref[...], jnp.concatenate([pr, pi], axis=0), preferred_element_type=F32)
        dr, di = d[:FFT_IN], d[FFT_IN:]
        tr, ti = twr_ref[kb], twi_ref[kb]
        er, ei = dr * tr + di * ti, di * tr - dr * ti
        d_ref[kb] = _pack_pair(er, ei).reshape(d_ref.shape[1:])


def _fft_mid(tb, v, hspec, order):
    p, n1 = v.shape[:2]
    vspec = pl.BlockSpec((None, FFT_KB, FFT_NT, SUBLANES, D_BRANCH), lambda k, q: (q, k, 0, 0, 0))
    wspec = _const_spec((2 * FFT_IN, 2 * FFT_IN))
    tspec = pl.BlockSpec((FFT_KB, FFT_IN, 1), lambda k, q: (k, 0, 0))
    return pl.pallas_call(
        _fft_mid_kernel,
        grid=(n1 // FFT_KB, p),
        in_specs=[wspec, wspec, tspec, tspec, vspec,
                  pl.BlockSpec((None, FFT_KB, 2, FFT_IN, D_BRANCH), lambda k, q: (order, k, 0, 0, 0))],
        out_specs=vspec,
        out_shape=jax.ShapeDtypeStruct(v.shape, jnp.uint32),
        compiler_params=_cparams("parallel", "arbitrary"),
        name="fft_mid",
    )(tb["w_fwd"][0], tb["w_inv"][0], *tb["tw_in"], v, hspec)


def _ifft_outer_kernel(mh_ref, d_ref, u_ref, x_ref, b_ref, o_ref):
    n1 = d_ref.shape[1]
    for pp in range(FFT_PP):
        er, ei = _unpack_pair(d_ref[pp])
        e = jnp.concatenate([er.reshape(n1 * SUBLANES, D_BRANCH), ei.reshape(n1 * SUBLANES, D_BRANCH)], axis=0)
        y = jnp.dot(mh_ref[...], e, preferred_element_type=F32).reshape((2,) + o_ref.shape[2:])
        o_ref[:, pp] = x_ref[:, pp] * (y + u_ref[:, pp] * b_ref[...])


def _ifft_outer(tb, d, u, ucol, x, xcol, bias):
    n1, half = tb["n1"], tb["half"]
    p = d.shape[0]
    mshape = (2 * half * SUBLANES, 2 * n1 * SUBLANES)
    io = lambda col: pl.BlockSpec((2, FFT_PP, half, None, SUBLANES, D_BRANCH), lambda t, q: (0, q, 0, t, 0, col))
    return pl.pallas_call(
        _ifft_outer_kernel,
        grid=(FFT_NT, p // FFT_PP),
        in_specs=[_const_spec(mshape),
                  pl.BlockSpec((FFT_PP, n1, None, SUBLANES, D_BRANCH), lambda t, q: (q, 0, t, 0, 0)),
                  io(ucol), io(xcol), _const_spec((1, D_BRANCH))],
        out_specs=io(0),
        out_shape=jax.ShapeDtypeStruct((2, p, half, FFT_NT, SUBLANES, D_BRANCH), F32),
        compiler_params=_cparams("arbitrary", "arbitrary"),
        name="ifft_outer",
    )(tb["m_inv"][0], d, u, x, bias.reshape(1, D_BRANCH))


def _hyfilt_kernel(z_ref, w1_ref, b1_ref, fr_ref, w2_ref, b2_ref, w3_ref, dec_ref, o_ref, h_ref):
    seq = z_ref.shape[0]
    nblk = seq // HY_TB
    fr = fr_ref[...]

    @pl.when(pl.program_id(0) == 0)
    def _():
        def hidden(i, carry):
            r0 = pl.multiple_of(i * HY_TB, HY_TB)
            zb = z_ref[pl.ds(r0, HY_TB), :]
            h = jnp.sin(fr * (jnp.dot(zb, w1_ref[...], precision=HI, preferred_element_type=F32) + b1_ref[...]))
            h = jnp.sin(fr * (jnp.dot(h, w2_ref[...], precision=HI, preferred_element_type=F32) + b2_ref[...]))
            h_ref[pl.ds(r0, HY_TB), :] = h
            return carry

        lax.fori_loop(0, nblk, hidden, 0)

    def body(i, ss):
        r0 = pl.multiple_of(i * HY_TB, HY_TB)
        hf = jnp.dot(h_ref[pl.ds(r0, HY_TB), :], w3_ref[...], precision=HI, preferred_element_type=F32)
        hf = hf * jnp.exp(-z_ref[pl.ds(r0, HY_TB), 0:1] * dec_ref[...])
        o_ref[pl.ds(r0, HY_TB), :] = hf
        return ss + jnp.sum(hf * hf, axis=0, keepdims=True)

    ss = lax.fori_loop(0, nblk, body, jnp.zeros((1, D_BRANCH), F32))
    scale = lax.rsqrt(ss + EPS)

    def norm(i, carry):
        r0 = pl.multiple_of(i * HY_TB, HY_TB)
        o_ref[pl.ds(r0, HY_TB), :] = o_ref[pl.ds(r0, HY_TB), :] * scale
        return carry

    lax.fori_loop(0, nblk, norm, 0)


def _hyfilt(seq, w1, b1, freq, w2, b2, w3):
    t = jnp.linspace(0.0, 1.0, seq, dtype=F32)[:, None]
    bands = (C_EMB - 1) // 2
    w = 2.0 * math.pi * jnp.arange(seq, dtype=F32)[:, None] / seq
    fr = jnp.linspace(1e-4, bands - 1, bands, dtype=F32)[None]
    z = jnp.concatenate([t, jnp.cos(fr * w), -jnp.sin(fr * w)], axis=-1)
    z = jnp.pad(z, ((0, 0), (0, LANES - C_EMB)))
    padm = lambda a, r, c: jnp.pad(a.astype(F32), ((0, r - a.shape[0]), (0, c - a.shape[1])))
    row = lambda a: padm(a.reshape(1, -1), 1, LANES)
    dec = jnp.abs(jnp.linspace(C_MIN_DECAY, C_MAX_DECAY, D_BRANCH, dtype=F32)).reshape(1, D_BRANCH)
    nset = C_ORDER * 2
    return pl.pallas_call(
        _hyfilt_kernel,
        grid=(nset,),
        in_specs=[_const_spec((seq, LANES)), _const_spec((LANES, LANES)), _const_spec((1, LANES)),
                  _const_spec((1, LANES)), _const_spec((LANES, LANES)), _const_spec((1, LANES)),
                  pl.BlockSpec((LANES, D_BRANCH), lambda j: (0, j)), _const_spec((1, D_BRANCH))],
        out_specs=pl.BlockSpec((None, seq, D_BRANCH), lambda j: (j, 0, 0)),
        out_shape=jax.ShapeDtypeStruct((nset, seq, D_BRANCH), F32),
        scratch_shapes=[pltpu.VMEM((seq, LANES), F32)],
        compiler_params=_cparams("arbitrary"),
        name="hyfilt",
    )(z, padm(w1, LANES, LANES), row(b1), row(freq), padm(w2, LANES, LANES), row(b2),
      padm(w3, LANES, nset * D_BRANCH), dec)


def _hyena(uc3, w1, b1, freq, w2, b2, w3, bias):
    bsz, seq, width = uc3.shape
    tb = _fft_tables(seq)
    half = tb["half"]
    npair = bsz // 2
    uc6 = uc3.reshape(2, npair, half, FFT_NT, SUBLANES, width)
    filt = _hyfilt(seq, w1, b1, freq, w2, b2, w3).reshape(C_ORDER * 2, half, FFT_NT, SUBLANES, D_BRANCH)
    hspec = _fft_filt(tb, _fft_outer(tb, filt, False, 3, False))
    src = uc6
    for order in range(C_ORDER):
        d = _fft_mid(tb, _fft_outer(tb, src, True, 1, True), hspec, order)
        src = _ifft_outer(tb, d, src, 0, uc6, order + 1, bias[order])
    return src.reshape(bsz, seq, D_BRANCH)


AT_HALF = 64
AT_TQ = 2048
AT_SUB = 128


def _t5_bucket(rel):
    half = N_BUCKETS // 2
    max_exact = half // 2
    n = np.abs(rel)
    large = max_exact + (np.log(np.maximum(n, 1) / max_exact) / math.log(MAX_DISTANCE / max_exact)
                         * (half - max_exact)).astype(np.int64)
    large = np.minimum(large, half - 1)
    return (rel > 0).astype(np.int64) * half + np.where(n < max_exact, n, large)


def _attn_geometry(n):
    tq = min(AT_TQ, n)
    sub = min(AT_SUB, tq)
    if sub + 2 * AT_HALF >= n:
        sub = tq
    win = min(sub + 2 * AT_HALF, n)
    return tq, sub, win, n // tq, n // sub


def _attn_bias_tables(rel_bias, g, dil, n):
    _, sub, win, _, nsb = _attn_geometry(n)
    hs = slice(g * D_HEADS_PER_GROUP, (g + 1) * D_HEADS_PER_GROUP)
    offsets = np.arange(-AT_HALF, AT_HALF + 1) * dil
    onehot = np.zeros((2 * AT_HALF + 1, N_BUCKETS), np.float32)
    onehot[np.arange(2 * AT_HALF + 1), _t5_bucket(offsets)] = 1.0
    band = jnp.dot(rel_bias.astype(F32)[:, hs].T, jnp.asarray(onehot).T, precision=HI)
    nband = 2 * AT_HALF + 1
    lv = sub + win - 1
    tables = []
    for i in sorted({0, min(1, nsb - 1), nsb - 1}):
        ws = int(np.clip(i * sub - AT_HALF, 0, n - win))
        lo = (sub - 1) - (ws - i * sub) - AT_HALF
        v = jnp.pad(band, ((0, 0), (lo, lv - lo - nband)), constant_values=NEG_BIG)
        flat = jnp.tile(v, (1, sub + 1))[:, sub - 1:sub - 1 + sub * (lv - 1)]
        tables.append(flat.reshape(D_HEADS_PER_GROUP, sub, lv - 1)[:, :, :win])
    return jnp.stack(tables)


def _attn_kernel(q_ref, k_ref, v_ref, bias_ref, o_ref, l_ref, *, n, dil):
    tq, sub, win, _, nsb = _attn_geometry(n)
    ncase = bias_ref.shape[0]
    width = q_ref.shape[-1]
    nh = width // D_HEAD_DIM
    hp = pl.program_id(1)
    lane_head = lax.broadcasted_iota(jnp.int32, (sub, width), 1) // D_HEAD_DIM
    hmask = [lane_head == hh for hh in range(nh)]
    whole = win == n
    for r in range(dil):
        if whole:
            kw = k_ref[pl.ds(r, win, stride=dil), :].astype(BF16)
            vw = v_ref[pl.ds(r, win, stride=dil), :].astype(BF16)
        for j in range(tq // sub):
            sidx = pl.program_id(2) * (tq // sub) + j
            case = jnp.minimum(jnp.where(sidx == nsb - 1, ncase - 1, jnp.minimum(sidx, 1)), ncase - 1)
            bias = bias_ref[case, pl.ds(hp * nh, nh)].reshape(nh * sub, win)
            if not whole:
                ws = pl.multiple_of(jnp.clip(sidx * sub - AT_HALF, 0, n - win), AT_HALF)
                kw = k_ref[pl.ds(ws * dil + r, win, stride=dil), :].astype(BF16)
                vw = v_ref[pl.ds(ws * dil + r, win, stride=dil), :].astype(BF16)
            q = q_ref[pl.ds(j * sub * dil + r, sub, stride=dil), :] * (D_HEAD_DIM ** -0.5)
            qs = jnp.concatenate([jnp.where(hmask[hh], q, 0.0) for hh in range(nh)], axis=0).astype(BF16)
            s = lax.dot_general(qs, kw, (((1,), (1,)), ((), ())), preferred_element_type=F32) + bias
            m = jnp.max(s, axis=-1, keepdims=True)
            p = jnp.exp(s - m)
            l = jnp.sum(p, axis=-1, keepdims=True)
            o_all = jnp.dot(p.astype(BF16), vw, preferred_element_type=F32) / l
            lse = m + jnp.log(l)
            o_acc = jnp.zeros((sub, width), F32)
            l_acc = jnp.zeros((sub, width), F32)
            for hh in range(nh):
                o_acc = jnp.where(hmask[hh], o_all[hh * sub:(hh + 1) * sub], o_acc)
                l_acc = jnp.where(hmask[hh], lse[hh * sub:(hh + 1) * sub], l_acc)
            o_ref[pl.ds(j * sub * dil + r, sub, stride=dil), :] = o_acc
            l_ref[pl.ds(j * sub * dil + r, sub, stride=dil), :] = l_acc


def _banded_attention(pd3, g, dil, bias):
    bsz, seq, width = pd3.shape
    n = seq // dil
    tq, sub, win, nq, _ = _attn_geometry(n)
    rows = dil * tq
    bw = LANES if dil > 1 else D_BRANCH
    per = D_BRANCH // bw
    third = width // 3 // bw
    out_spec = pl.BlockSpec((None, rows, bw), lambda b, h, i: (b, i, h))
    shp = jax.ShapeDtypeStruct((bsz, seq, D_BRANCH), F32)
    return pl.pallas_call(
        functools.partial(_attn_kernel, n=n, dil=dil),
        grid=(bsz, per, nq),
        in_specs=[pl.BlockSpec((None, rows, bw), lambda b, h, i: (b, i, g * per + h)),
                  pl.BlockSpec((None, seq, bw), lambda b, h, i: (b, 0, third + g * per + h)),
                  pl.BlockSpec((None, seq, bw), lambda b, h, i: (b, 0, 2 * third + g * per + h)),
                  _const_spec(bias.shape)],
        out_specs=[out_spec, out_spec],
        out_shape=[shp, shp],
        compiler_params=_cparams("parallel", "parallel", "arbitrary"),
        name=f"attn_d{dil}",
    )(pd3, pd3, pd3, bias)


def _dilated_attention(pd3, rel_bias):
    bsz, seq, _ = pd3.shape
    outs, lses = [], []
    for g, (_, dil) in enumerate(D_GROUPS):
        o, l = _banded_attention(pd3, g, dil, _attn_bias_tables(rel_bias, g, dil, seq // dil))
        outs.append(o.reshape(bsz * seq, D_BRANCH))
        lses.append(l.reshape(bsz * seq, D_BRANCH))
    return outs, lses


def kernel(x, norm1_g, w_in, hgrn_lb_logits, hgrn_norm_g, lru_conv_w, lru_conv_b, lru_wa, lru_ba, lru_wx, lru_bx,
           lru_lambda, hy_conv_w, hy_conv_b, hy_w1, hy_b1, hy_freq, hy_w2, hy_b2, hy_w3, hy_bias, rel_bias,
           w_branch, w_gate, b_gate, w_out, norm2_g, w_ff1, w_ff3, w_ff2, final_g):
    bsz, seq, _ = x.shape
    n = bsz * seq
    lb_soft = jax.nn.softmax(hgrn_lb_logits.astype(F32), axis=0)
    lower_bounds = jnp.cumsum(lb_soft, axis=0) - lb_soft[0]
    x2 = x.reshape(n, D_MODEL)
    flat = lambda a: a.reshape(n, D_BRANCH)
    for l in range(DEPTH):
        pa, pb, uc, pd = _inproj(x2, norm1_g[l], w_in[l].astype(BF16), hy_conv_w[l], hy_conv_b[l], seq)
        oa_f, oa_b = _hgrn(pa.reshape(bsz, seq, IN_A), lower_bounds[l])
        yb = _rglru(pb.reshape(bsz, seq, IN_B), lru_conv_w[l], lru_conv_b[l], lru_wa[l], lru_ba[l],
                    lru_wx[l], lru_bx[l], lru_lambda[l])
        yc = _hyena(uc.reshape(bsz, seq, IN_C), hy_w1[l], hy_b1[l], hy_freq[l],
                    hy_w2[l], hy_b2[l], hy_w3[l], hy_bias[l])
        od, ld = _dilated_attention(pd.reshape(bsz, seq, IN_D), rel_bias)
        x2 = _merge(x2, norm1_g[l], flat(oa_f), flat(oa_b), pa, hgrn_norm_g[l], flat(yb), flat(yc), od, ld,
                    w_gate[l].reshape(D_MODEL, N_BRANCH * D_MODEL).astype(BF16),
                    b_gate[l].reshape(1, N_BRANCH * D_MODEL), w_branch[l].astype(BF16), w_out[l].astype(BF16))
        x2 = _ffn(x2, norm2_g[l], *_ffn_weights(w_ff1[l], w_ff3[l], w_ff2[l]), final_g, l == DEPTH - 1)
    return x2.reshape(bsz, seq, D_MODEL)
```

```python
import functools
import math

import jax
import jax.numpy as jnp
import numpy as np
from jax import lax
from jax.experimental import pallas as pl
from jax.experimental.pallas import tpu as pltpu

F32 = jnp.float32
BF16 = jnp.bfloat16
HI = lax.Precision.HIGHEST

D_MODEL = 1024
DEPTH = 2
EPS = 1e-6
TINY = 1e-30
N_BRANCH = 4
D_BRANCH = 256
A_HEADS = 4
A_DK = 64
A_CHUNK = 64
B_BLOCKS = 4
B_BW = 64
B_CONV = 4
LRU_C = 8.0
C_ORDER = 2
C_CONV = 3
C_EMB = 33
C_MIN_DECAY = math.log(1e-2) / 1.5
C_MAX_DECAY = math.log(1e-2) / 0.3
D_GROUPS = ((128, 1), (512, 4), (2048, 16))
D_HEADS_PER_GROUP = 4
D_HEAD_DIM = 64
D_QKV = 768
N_BUCKETS = 32
MAX_DISTANCE = 1024
NEG_BIG = -1e30
D_FF = 2816
IN_A = 5 * D_BRANCH
IN_B = 2 * D_BRANCH
IN_C = 3 * D_BRANCH
IN_D = 3 * D_QKV
IN_WIDTH = IN_A + IN_B + IN_C + IN_D

LANES = 128
SUBLANES = 8
VMEM_LIMIT = 56 * 1024 * 1024
BF16_BITS = 16
BF16_HIGH_MASK = 0xFFFF0000


def _cparams(*sem):
    return pltpu.CompilerParams(dimension_semantics=sem, vmem_limit_bytes=VMEM_LIMIT)


def _const_spec(shape):
    nd = len(shape)
    return pl.BlockSpec(shape, lambda *_: (0,) * nd, pipeline_mode=pl.Buffered(1))


def _rms(x, g):
    return x * lax.rsqrt(jnp.mean(x * x, axis=-1, keepdims=True) + EPS) * g


def _sigmoid(x):
    return 1.0 / (1.0 + jnp.exp(-x))


IN_TM = 512
IN_CHUNK = 256
IN_CONV_ROWS = 256


def _inproj_kernel(x_ref, xp_ref, xn_ref, g_ref, w_ref, cw_ref, cb_ref, oa_ref, ob_ref, oc_ref, od_ref, cs_ref,
                   *, tiles_per_seq):
    i = pl.program_id(0)
    g = g_ref[...]
    h = _rms(x_ref[...], g).astype(BF16)
    keep_prev = jnp.where(i % tiles_per_seq != 0, 1.0, 0.0)
    keep_next = jnp.where(i % tiles_per_seq != tiles_per_seq - 1, 1.0, 0.0)
    h_ext = jnp.concatenate([(_rms(xp_ref[...], g) * keep_prev).astype(BF16), h,
                             (_rms(xn_ref[...], g) * keep_next).astype(BF16)], axis=0)
    off = 0
    for o_ref in (oa_ref, ob_ref, oc_ref, od_ref):
        width = o_ref.shape[-1]
        for c in range(0, width, IN_CHUNK):
            w = w_ref[:, off + c:off + c + IN_CHUNK]
            if o_ref is oc_ref:
                cs_ref[:, c:c + IN_CHUNK] = jnp.dot(h_ext, w, preferred_element_type=F32)
            else:
                o_ref[:, c:c + IN_CHUNK] = jnp.dot(h, w, preferred_element_type=F32)
        off += width
    left = C_CONV // 2
    for r0 in range(0, IN_TM, IN_CONV_ROWS):
        for c in range(0, IN_C, IN_CHUNK):
            win = cs_ref[r0:r0 + IN_CONV_ROWS + 2 * SUBLANES, c:c + IN_CHUNK]
            y = cb_ref[:, c:c + IN_CHUNK]
            for j in range(C_CONV):
                s0 = SUBLANES + j - left
                y = y + cw_ref[j:j + 1, c:c + IN_CHUNK] * win[s0:s0 + IN_CONV_ROWS, :]
            oc_ref[r0:r0 + IN_CONV_ROWS, c:c + IN_CHUNK] = y


def _inproj(x2, g, w_bf16, cw, cb, seq):
    n = x2.shape[0]
    widths = (IN_A, IN_B, IN_C, IN_D)
    per8 = IN_TM // SUBLANES
    last8 = n // SUBLANES - 1
    return pl.pallas_call(
        functools.partial(_inproj_kernel, tiles_per_seq=seq // IN_TM),
        grid=(n // IN_TM,),
        in_specs=[pl.BlockSpec((IN_TM, D_MODEL), lambda i: (i, 0)),
                  pl.BlockSpec((SUBLANES, D_MODEL), lambda i: (jnp.maximum(i * per8 - 1, 0), 0)),
                  pl.BlockSpec((SUBLANES, D_MODEL), lambda i: (jnp.minimum((i + 1) * per8, last8), 0)),
                  _const_spec((1, D_MODEL)),
                  _const_spec((D_MODEL, IN_WIDTH)),
                  _const_spec((C_CONV, IN_C)), _const_spec((1, IN_C))],
        out_specs=[pl.BlockSpec((IN_TM, w), lambda i: (i, 0)) for w in widths],
        out_shape=[jax.ShapeDtypeStruct((n, w), F32) for w in widths],
        scratch_shapes=[pltpu.VMEM((IN_TM + 2 * SUBLANES, IN_C), F32)],
        compiler_params=_cparams("parallel"),
        name="inproj",
    )(x2, x2, x2, g.reshape(1, D_MODEL), w_bf16, cw, cb.reshape(1, IN_C))


FF_TM = 1024
FF_CHUNK = 256
FF_NCHUNK = D_FF // FF_CHUNK


def _ffn_kernel(x_ref, g_ref, w1_ref, w3_ref, w2_ref, fg_ref, o_ref, acc_ref, *, final):
    x = x_ref[...]
    h = _rms(x, g_ref[...]).astype(BF16)
    acc_ref[...] = x

    def body(c, carry):
        c0 = pl.multiple_of(c * FF_CHUNK, FF_CHUNK)
        a = jnp.dot(h, w1_ref[:, pl.ds(c0, FF_CHUNK)], preferred_element_type=F32)
        b = jnp.dot(h, w3_ref[:, pl.ds(c0, FF_CHUNK)], preferred_element_type=F32)
        t = (a * _sigmoid(a) * b).astype(BF16)
        acc_ref[...] += jnp.dot(t, w2_ref[pl.ds(c0, FF_CHUNK), :], preferred_element_type=F32)
        return carry

    lax.fori_loop(0, FF_NCHUNK, body, 0)
    y = acc_ref[...]
    if final:
        y = _rms(y, fg_ref[...])
    o_ref[...] = y


def _ffn(x2, g, w1c, w3c, w2c, final_g, final):
    n = x2.shape[0]
    return pl.pallas_call(
        functools.partial(_ffn_kernel, final=final),
        grid=(n // FF_TM,),
        in_specs=[pl.BlockSpec((FF_TM, D_MODEL), lambda i: (i, 0)),
                  _const_spec((1, D_MODEL)),
                  _const_spec((D_MODEL, D_FF)),
                  _const_spec((D_MODEL, D_FF)),
                  _const_spec((D_FF, D_MODEL)),
                  _const_spec((1, D_MODEL))],
        out_specs=pl.BlockSpec((FF_TM, D_MODEL), lambda i: (i, 0)),
        out_shape=jax.ShapeDtypeStruct((n, D_MODEL), F32),
        scratch_shapes=[pltpu.VMEM((FF_TM, D_MODEL), F32)],
        compiler_params=_cparams("parallel"),
        name="ffn_final" if final else "ffn",
    )(x2, g.reshape(1, D_MODEL), w1c, w3c, w2c, final_g.reshape(1, D_MODEL))


def _ffn_weights(w1, w3, w2):
    return w1.astype(BF16), w3.astype(BF16), w2.astype(BF16)


MG_TM = 512
MG_CHUNK = 256


def _head_ones():
    r = np.arange(D_BRANCH)[:, None] // A_DK
    c = np.arange(D_BRANCH)[None, :] // A_DK
    return jnp.asarray((r == c).astype(np.float32) / A_DK, BF16)


def _merge_kernel(x_ref, g1_ref, oaf_ref, oab_ref, ga_ref, hg_ref, hm_ref, yb_ref, yc_ref,
                  o0_ref, o1_ref, o2_ref, l0_ref, l1_ref, l2_ref,
                  wg_ref, bg_ref, wb_ref, wo_ref, out_ref):
    x = x_ref[...]
    h = _rms(x, g1_ref[...]).astype(BF16)
    oa = oaf_ref[...] + oab_ref[...]
    ms = jnp.dot((oa * oa).astype(BF16), hm_ref[...], preferred_element_type=F32)
    ga = ga_ref[...]
    ya = oa * lax.rsqrt(ms + EPS) * hg_ref[...] * (ga * _sigmoid(ga))
    l0, l1, l2 = l0_ref[...], l1_ref[...], l2_ref[...]
    m = jnp.maximum(jnp.maximum(l0, l1), l2)
    e0, e1, e2 = jnp.exp(l0 - m), jnp.exp(l1 - m), jnp.exp(l2 - m)
    yd = (e0 * o0_ref[...] + e1 * o1_ref[...] + e2 * o2_ref[...]) / (e0 + e1 + e2)
    ys = [y.astype(BF16) for y in (ya, yb_ref[...], yc_ref[...], yd)]
    cols = []
    for c in range(0, D_MODEL, MG_CHUNK):
        mixed = None
        for j in range(N_BRANCH):
            lo = j * D_MODEL + c
            gate = _sigmoid(jnp.dot(h, wg_ref[:, lo:lo + MG_CHUNK], preferred_element_type=F32)
                            + bg_ref[:, lo:lo + MG_CHUNK])
            t = gate * jnp.dot(ys[j], wb_ref[j, :, c:c + MG_CHUNK], preferred_element_type=F32)
            mixed = t if mixed is None else mixed + t
        cols.append(mixed.astype(BF16))
    mixed = jnp.concatenate(cols, axis=1)
    out_ref[...] = x + jnp.dot(mixed, wo_ref[...], preferred_element_type=F32)


def _merge(x2, g1, oa_f, oa_b, pa, hg, yb, yc, od, ld, wg, bg, wb, wo):
    n = x2.shape[0]
    tile = lambda w: pl.BlockSpec((MG_TM, w), lambda i: (i, 0))
    return pl.pallas_call(
        _merge_kernel,
        grid=(n // MG_TM,),
        in_specs=[tile(D_MODEL), _const_spec((1, D_MODEL)),
                  tile(D_BRANCH), tile(D_BRANCH),
                  pl.BlockSpec((MG_TM, D_BRANCH), lambda i: (i, 4)),
                  _const_spec((1, D_BRANCH)), _const_spec((D_BRANCH, D_BRANCH)),
                  tile(D_BRANCH), tile(D_BRANCH),
                  tile(D_BRANCH), tile(D_BRANCH), tile(D_BRANCH),
                  tile(D_BRANCH), tile(D_BRANCH), tile(D_BRANCH),
                  _const_spec((D_MODEL, N_BRANCH * D_MODEL)), _const_spec((1, N_BRANCH * D_MODEL)),
                  _const_spec((N_BRANCH, D_BRANCH, D_MODEL)), _const_spec((D_MODEL, D_MODEL))],
        out_specs=tile(D_MODEL),
        out_shape=jax.ShapeDtypeStruct((n, D_MODEL), F32),
        compiler_params=_cparams("parallel"),
        name="merge",
    )(x2, g1.reshape(1, D_MODEL), oa_f, oa_b, pa, hg.reshape(1, D_BRANCH), _head_ones(), yb, yc,
      od[0], od[1], od[2], ld[0], ld[1], ld[2], wg, bg, wb, wo)


HG_TS = 256
HG_NCH = HG_TS // A_CHUNK
HG_MID = A_CHUNK // 2
HG_BR = 4


def _hgrn_tables():
    r = np.arange(HG_TS)[:, None]
    c = np.arange(HG_TS)[None, :]
    same = (r // A_CHUNK) == (c // A_CHUNK)
    rr = np.arange(HG_NCH * A_HEADS * A_CHUNK)[:, None]
    same_s = (rr // (A_HEADS * A_CHUNK)) == (c // A_CHUNK)
    t, sidx = rr % A_CHUNK, c % A_CHUNK
    out = []
    for fwd in (True, False):
        order = (r >= c) if fwd else (r <= c)
        order_s = (t >= sidx) if fwd else (t <= sidx)
        out.append((jnp.asarray((same & order).astype(np.float32), BF16),
                    jnp.asarray((same_s & order_s).astype(np.float32), F32)))
    return out


def _hgrn_prep(q, fl, v, lb, cum, smask, fwd):
    lane_head = lax.broadcasted_iota(jnp.int32, (A_CHUNK, D_BRANCH), 1) // A_DK
    hmask = [(lane_head == hh).astype(F32) for hh in range(A_HEADS)]
    blk_r = lax.broadcasted_iota(jnp.int32, (D_BRANCH, D_BRANCH), 0) // A_DK
    blk_c = lax.broadcasted_iota(jnp.int32, (D_BRANCH, D_BRANCH), 1) // A_DK
    blockdiag = (blk_r == blk_c).astype(F32)
    col_chunk = lax.broadcasted_iota(jnp.int32, (D_BRANCH, HG_TS), 1) // A_CHUNK

    sg = _sigmoid(fl)
    f = lb + (1.0 - lb) * sg
    g = jnp.log(jnp.maximum(f, TINY))
    kk = (1.0 - lb) * (1.0 - sg)
    g1 = g.astype(BF16)
    r1 = g - g1.astype(F32)
    g2 = r1.astype(BF16)
    g3 = (r1 - g2.astype(F32)).astype(BF16)
    b = (jnp.dot(cum, g1, preferred_element_type=F32) + jnp.dot(cum, g2, preferred_element_type=F32)
         + jnp.dot(cum, g3, preferred_element_type=F32))
    rows = lambda c: slice(c * A_CHUNK, (c + 1) * A_CHUNK)
    last = (A_CHUNK - 1) if fwd else 0
    bcast = lambda r0: jnp.concatenate(
        [jnp.broadcast_to(b[c * A_CHUNK + r0:c * A_CHUNK + r0 + 1, :], (A_CHUNK, D_BRANCH)) for c in range(HG_NCH)], 0)
    bm = bcast(HG_MID)
    bl = bcast(last)
    qt = q * jnp.exp(b - bm)
    kt = (kk * jnp.exp(bm - b)).astype(BF16)
    qe = (q * jnp.exp(b)).astype(BF16)
    kh = (kk * jnp.exp(bl - b)).astype(BF16)
    qs = jnp.concatenate([qt[rows(c)] * hmask[hh] for c in range(HG_NCH) for hh in range(A_HEADS)],
                         axis=0).astype(BF16)
    s = lax.dot_general(qs, kt, (((1,), (1,)), ((), ())), preferred_element_type=F32)
    s = (s * smask).astype(BF16)
    ost = jnp.dot(s, v.astype(BF16), preferred_element_type=F32)
    o_intra = []
    for c in range(HG_NCH):
        base = c * A_HEADS * A_CHUNK
        o = ost[base:base + A_CHUNK] * hmask[0]
        for hh in range(1, A_HEADS):
            o = o + ost[base + hh * A_CHUNK:base + (hh + 1) * A_CHUNK] * hmask[hh]
        o_intra.append(o)
    vt = v.T
    lhs = jnp.concatenate([jnp.where(col_chunk == c, vt, 0.0) for c in range(HG_NCH)], axis=0).astype(BF16)
    updall = jnp.dot(lhs, kh, preferred_element_type=F32)
    upd = [updall[c * D_BRANCH:(c + 1) * D_BRANCH] * blockdiag for c in range(HG_NCH)]
    decay = [jnp.exp(b[c * A_CHUNK + last:c * A_CHUNK + last + 1, :]) for c in range(HG_NCH)]
    return o_intra, upd, qe, decay


def _hgrn_scan(o_intra, upd, qe, decay, st, fwd):
    rows = lambda c: slice(c * A_CHUNK, (c + 1) * A_CHUNK)
    outs = [None] * HG_NCH
    for c in (range(HG_NCH) if fwd else range(HG_NCH - 1, -1, -1)):
        outs[c] = o_intra[c] + lax.dot_general(qe[rows(c)], st.astype(BF16), (((1,), (1,)), ((), ())),
                                               preferred_element_type=F32)
        st = st * decay[c] + upd[c]
    return jnp.concatenate(outs, axis=0), st


def _hgrn_kernel(qf_ref, ff_ref, vf_ref, qb_ref, fb_ref, vb_ref, lb_ref, cf_ref, mf_ref, cb_ref, mb_ref,
                 of_ref, ob_ref, sf_ref, sb_ref):
    @pl.when(pl.program_id(1) == 0)
    def _():
        sf_ref[...] = jnp.zeros_like(sf_ref)
        sb_ref[...] = jnp.zeros_like(sb_ref)

    lb = lb_ref[...]
    pf = [_hgrn_prep(qf_ref[r], ff_ref[r], vf_ref[r], lb, cf_ref[...], mf_ref[...], True) for r in range(HG_BR)]
    pb = [_hgrn_prep(qb_ref[r], fb_ref[r], vb_ref[r], lb, cb_ref[...], mb_ref[...], False) for r in range(HG_BR)]
    for r in range(HG_BR):
        o_f, st_f = _hgrn_scan(*pf[r], sf_ref[r], True)
        o_b, st_b = _hgrn_scan(*pb[r], sb_ref[r], False)
        of_ref[r] = o_f
        ob_ref[r] = o_b
        sf_ref[r] = st_f
        sb_ref[r] = st_b


def _hgrn(pa3, lb):
    bsz, seq, _ = pa3.shape
    nblk = seq // HG_TS
    blk = (HG_BR, HG_TS, D_BRANCH)
    up = lambda col: pl.BlockSpec(blk, lambda b, i: (b, i, col))
    down = lambda col: pl.BlockSpec(blk, lambda b, i: (b, nblk - 1 - i, col))
    shp = jax.ShapeDtypeStruct((bsz, seq, D_BRANCH), F32)
    (cum_f, sm_f), (cum_b, sm_b) = _hgrn_tables()
    mshape = (HG_NCH * A_HEADS * A_CHUNK, HG_TS)
    state = pltpu.VMEM((HG_BR, D_BRANCH, D_BRANCH), F32)
    return pl.pallas_call(
        _hgrn_kernel,
        grid=(bsz // HG_BR, nblk),
        in_specs=[up(0), up(1), up(3), down(0), down(2), down(3), _const_spec((1, D_BRANCH)),
                  _const_spec((HG_TS, HG_TS)), _const_spec(mshape), _const_spec((HG_TS, HG_TS)), _const_spec(mshape)],
        out_specs=[up(0), down(0)],
        out_shape=[shp, shp],
        scratch_shapes=[state, state],
        compiler_params=_cparams("parallel", "arbitrary"),
        name="hgrn2",
    )(pa3, pa3, pa3, pa3, pa3, pa3, lb.reshape(1, D_BRANCH), cum_f, sm_f, cum_b, sm_b)


RG_TB = 1024
RG_PAD = SUBLANES
RG_LEFT = B_CONV // 2


def _dot3_rhs(x, wh, wl):
    xh = x.astype(BF16)
    xl = (x - xh.astype(F32)).astype(BF16)
    return (jnp.dot(xh, wh, preferred_element_type=F32) + jnp.dot(xl, wh, preferred_element_type=F32)
            + jnp.dot(xh, wl, preferred_element_type=F32))


def _group_scan(a, u, fwd):
    row = lax.broadcasted_iota(jnp.int32, a.shape, 1)
    k = 1
    while k < SUBLANES:
        if fwd:
            keep = row >= k
            us, as_ = pltpu.roll(u, k, 1), pltpu.roll(a, k, 1)
        else:
            keep = row < SUBLANES - k
            us, as_ = pltpu.roll(u, SUBLANES - k, 1), pltpu.roll(a, SUBLANES - k, 1)
        u = a * jnp.where(keep, us, 0.0) + u
        a = a * jnp.where(keep, as_, 1.0)
        k *= 2
    return a, u


def _block_scan(a, u, carry, fwd):
    t = a.shape[0]
    ngrp = t // SUBLANES
    ag, ug = _group_scan(a.reshape(ngrp, SUBLANES, D_BRANCH), u.reshape(ngrp, SUBLANES, D_BRANCH), fwd)
    hs = [None] * ngrp
    for g in (range(ngrp) if fwd else range(ngrp - 1, -1, -1)):
        h = ug[g] + ag[g] * carry
        hs[g] = h
        carry = h[SUBLANES - 1:SUBLANES, :] if fwd else h[0:1, :]
    return jnp.concatenate(hs, axis=0), carry


def _gelu_tanh(x):
    return 0.5 * x * (1.0 + jnp.tanh(math.sqrt(2.0 / math.pi) * (x + 0.044715 * (x * x * x))))


def _rglru_kernel(x_ref, gt_ref, cw_ref, cb_ref, wh_ref, wl_ref, bg_ref, lam_ref, o_ref, xp_ref, xc_ref):
    seq = x_ref.shape[0]
    nblk = seq // RG_TB
    xp_ref[0:RG_PAD, :] = jnp.zeros((RG_PAD, D_BRANCH), F32)
    xp_ref[RG_PAD + seq:2 * RG_PAD + seq, :] = jnp.zeros((RG_PAD, D_BRANCH), F32)
    xp_ref[RG_PAD:RG_PAD + seq, :] = x_ref[...]
    nl = -lam_ref[...]
    sp = jnp.maximum(nl, 0.0) + jnp.log(1.0 + jnp.exp(-jnp.abs(nl)))

    def block(i, carry, dirn):
        r0 = pl.multiple_of(i * RG_TB, RG_TB)
        if dirn == 0:
            win = xp_ref[pl.ds(r0, RG_TB + 2 * RG_PAD), :]
            xc = cb_ref[...]
            for j in range(B_CONV):
                s0 = RG_PAD + j - RG_LEFT
                xc = xc + cw_ref[j:j + 1, :] * win[s0:s0 + RG_TB, :]
            xc_ref[pl.ds(r0, RG_TB), :] = xc
        else:
            xc = xc_ref[pl.ds(r0, RG_TB), :]
        cols = slice(dirn * 2 * D_BRANCH, (dirn + 1) * 2 * D_BRANCH)
        gates = _dot3_rhs(xc, wh_ref[:, cols], wl_ref[:, cols]) + bg_ref[:, cols]
        r = _sigmoid(gates[:, :D_BRANCH])
        ig = _sigmoid(gates[:, D_BRANCH:])
        log_a = -LRU_C * r * sp[dirn:dirn + 1, :]
        a = jnp.exp(log_a)
        u = jnp.sqrt(jnp.maximum(-jnp.tanh(log_a) * (a * a + 1.0), 0.0)) * ig * xc
        h, carry = _block_scan(a, u, carry, dirn == 0)
        if dirn == 0:
            o_ref[pl.ds(r0, RG_TB), :] = h
        else:
            o_ref[pl.ds(r0, RG_TB), :] = (o_ref[pl.ds(r0, RG_TB), :] + h) * _gelu_tanh(gt_ref[pl.ds(r0, RG_TB), :])
        return carry

    zero = jnp.zeros((1, D_BRANCH), F32)
    lax.fori_loop(0, nblk, lambda i, c: block(i, c, 0), zero)
    lax.fori_loop(0, nblk, lambda i, c: block(nblk - 1 - i, c, 1), zero)


def _blockdiag(w):
    eye = jnp.eye(B_BLOCKS, dtype=w.dtype)
    return jnp.einsum('ncd,nm->ncmd', w, eye).reshape(D_BRANCH, D_BRANCH)


def _rglru(pb3, cw, cb, wa, ba, wx, bx, lam):
    bsz, seq, _ = pb3.shape
    wg = jnp.concatenate([_blockdiag(wa[0]), _blockdiag(wx[0]), _blockdiag(wa[1]), _blockdiag(wx[1])], axis=1)
    bg = jnp.concatenate([ba[0], bx[0], ba[1], bx[1]]).reshape(1, 4 * D_BRANCH)
    wh = wg.astype(BF16)
    wl = (wg - wh.astype(F32)).astype(BF16)
    blk = (None, seq, D_BRANCH)
    return pl.pallas_call(
        _rglru_kernel,
        grid=(bsz,),
        in_specs=[pl.BlockSpec(blk, lambda b: (b, 0, 0)),
                  pl.BlockSpec(blk, lambda b: (b, 0, 1)),
                  _const_spec((B_CONV, D_BRANCH)), _const_spec((1, D_BRANCH)),
                  _const_spec((D_BRANCH, 4 * D_BRANCH)), _const_spec((D_BRANCH, 4 * D_BRANCH)),
                  _const_spec((1, 4 * D_BRANCH)), _const_spec((2, D_BRANCH))],
        out_specs=pl.BlockSpec(blk, lambda b: (b, 0, 0)),
        out_shape=jax.ShapeDtypeStruct((bsz, seq, D_BRANCH), F32),
        scratch_shapes=[pltpu.VMEM((seq + 2 * RG_PAD, D_BRANCH), F32), pltpu.VMEM((seq, D_BRANCH), F32)],
        compiler_params=_cparams("parallel"),
        name="rglru",
    )(pb3, pb3, cw, cb.reshape(1, D_BRANCH), wh, wl, bg, lam)


FFT_IN = 128
FFT_NT = FFT_IN // SUBLANES
HY_TB = 1024


def _split_np(a):
    a = np.asarray(a, np.float32)
    hi = a.astype(jnp.bfloat16)
    lo = (a - hi.astype(np.float32)).astype(jnp.bfloat16)
    return jnp.asarray(hi), jnp.asarray(lo)


def _dot3(mh, ml, x):
    xh = x.astype(BF16)
    xl = (x - xh.astype(F32)).astype(BF16)
    return (jnp.dot(mh, xh, preferred_element_type=F32) + jnp.dot(mh, xl, preferred_element_type=F32)
            + jnp.dot(ml, xh, preferred_element_type=F32))


def _dotp(mh, ml, x, passes):
    if passes == 1:
        return jnp.dot(mh, x.astype(BF16), preferred_element_type=F32)
    return _dot3(mh, ml, x)


def _pack_pair(re, im):
    rb = lax.bitcast_convert_type(re.astype(BF16).astype(F32), jnp.uint32)
    ib = lax.bitcast_convert_type(im.astype(BF16).astype(F32), jnp.uint32)
    return rb | (ib >> BF16_BITS)


def _unpack_pair(w):
    re = lax.bitcast_convert_type(w & jnp.uint32(BF16_HIGH_MASK), F32)
    im = lax.bitcast_convert_type(w << BF16_BITS, F32)
    return re.astype(BF16), im.astype(BF16)


def _fft_tables(seq):
    n = 2 * seq
    n1 = n // FFT_IN
    half = n1 // 2
    eye = np.eye(SUBLANES)
    a = 2.0 * np.pi * np.outer(np.arange(n1), np.arange(half)) / n1
    gr, gi = np.cos(a), -np.sin(a)
    blk = np.stack([np.stack([gr, -gi], axis=1), np.stack([gi, gr], axis=1)], axis=0)
    m_out = np.einsum('rkis,cd->rkcisd', blk, eye).reshape(2 * n1 * SUBLANES, 2 * half * SUBLANES)
    ir, ii = gr.T / n, -gi.T / n
    blk = np.stack([np.stack([ir, -ii], axis=1), np.stack([ii, ir], axis=1)], axis=0)
    m_inv = np.einsum('otrk,cd->otcrkd', blk, eye).reshape(2 * half * SUBLANES, 2 * n1 * SUBLANES)
    a = 2.0 * np.pi * np.outer(np.arange(FFT_IN), np.arange(FFT_IN)) / FFT_IN
    fr, fi = np.cos(a), -np.sin(a)
    w_fwd = np.block([[fr, -fi], [fi, fr]])
    w_inv = np.block([[fr, fi], [-fi, fr]])
    s_in = SUBLANES * np.arange(FFT_NT)[:, None, None] + np.arange(SUBLANES)[None, None, :]
    th = 2.0 * np.pi * s_in * np.arange(n1)[None, :, None] / n
    tw = (jnp.asarray(np.cos(th)[..., None], F32), jnp.asarray(-np.sin(th)[..., None], F32))
    th = 2.0 * np.pi * np.outer(np.arange(n1), np.arange(FFT_IN)) / n
    tw_in = (jnp.asarray(np.cos(th)[..., None], F32), jnp.asarray(-np.sin(th)[..., None], F32))
    return dict(m_out=_split_np(m_out), m_inv=_split_np(m_inv), w_fwd=_split_np(w_fwd), w_inv=_split_np(w_inv),
                tw=tw, tw_in=tw_in, n1=n1, half=half)


FFT_PP = 4


def _tw_spec(n1):
    return pl.BlockSpec((None, n1, SUBLANES, 1), lambda t, q: (t, 0, 0, 0))


def _fft_outer_kernel(mh_ref, ml_ref, twr_ref, twi_ref, z_ref, v_ref, *, complex_in, passes, packed):
    n1 = v_ref.shape[1]
    tr, ti = twr_ref[...], twi_ref[...]
    for pp in range(FFT_PP):
        zz = z_ref[:, pp] if complex_in else z_ref[pp]
        rows_in = math.prod(zz.shape[:-1])
        v = _dotp(mh_ref[:, :rows_in], ml_ref[:, :rows_in], zz.reshape(rows_in, D_BRANCH), passes)
        vr = v[:n1 * SUBLANES].reshape(n1, SUBLANES, D_BRANCH)
        vi = v[n1 * SUBLANES:].reshape(n1, SUBLANES, D_BRANCH)
        wr, wi = vr * tr - vi * ti, vr * ti + vi * tr
        if packed:
            v_ref[pp] = _pack_pair(wr, wi)
        else:
            v_ref[pp, :, 0] = wr
            v_ref[pp, :, 1] = wi


def _fft_outer(tb, z, complex_in, passes, packed):
    n1, half = tb["n1"], tb["half"]
    if complex_in:
        p = z.shape[1]
        zspec = pl.BlockSpec((2, FFT_PP, half, None, SUBLANES, D_BRANCH), lambda t, q: (0, q, 0, t, 0, 0))
    else:
        p = z.shape[0]
        zspec = pl.BlockSpec((FFT_PP, half, None, SUBLANES, D_BRANCH), lambda t, q: (q, 0, t, 0, 0))
    mshape = (2 * n1 * SUBLANES, 2 * half * SUBLANES)
    if packed:
        out_spec = pl.BlockSpec((FFT_PP, n1, None, SUBLANES, D_BRANCH), lambda t, q: (q, 0, t, 0, 0))
        out_shape = jax.ShapeDtypeStruct((p, n1, FFT_NT, SUBLANES, D_BRANCH), jnp.uint32)
    else:
        out_spec = pl.BlockSpec((FFT_PP, n1, 2, None, SUBLANES, D_BRANCH), lambda t, q: (q, 0, 0, t, 0, 0))
        out_shape = jax.ShapeDtypeStruct((p, n1, 2, FFT_NT, SUBLANES, D_BRANCH), F32)
    return pl.pallas_call(
        functools.partial(_fft_outer_kernel, complex_in=complex_in, passes=passes, packed=packed),
        grid=(FFT_NT, p // FFT_PP),
        in_specs=[_const_spec(mshape), _const_spec(mshape), _tw_spec(n1), _tw_spec(n1), zspec],
        out_specs=out_spec,
        out_shape=out_shape,
        compiler_params=_cparams("arbitrary", "arbitrary"),
        name="fft_outer_c" if complex_in else "fft_outer_r",
    )(*tb["m_out"], *tb["tw"], z)


FFT_KB = 16


def _fft_filt_kernel(wh_ref, wl_ref, vf_ref, vb_ref, h_ref):
    for kb in range(FFT_KB):
        zf = _dot3(wh_ref[...], wl_ref[...], vf_ref[kb].reshape(2 * FFT_IN, D_BRANCH))
        zb = _dot3(wh_ref[...], wl_ref[...], vb_ref[kb].reshape(2 * FFT_IN, D_BRANCH))
        h_ref[kb, 0] = zf[:FFT_IN] + zb[:FFT_IN]
        h_ref[kb, 1] = zf[FFT_IN:] - zb[FFT_IN:]


def _fft_filt(tb, v):
    n1 = v.shape[1]
    wspec = _const_spec((2 * FFT_IN, 2 * FFT_IN))
    vblk = (None, FFT_KB, 2, FFT_NT, SUBLANES, D_BRANCH)
    return pl.pallas_call(
        _fft_filt_kernel,
        grid=(C_ORDER, n1 // FFT_KB),
        in_specs=[wspec, wspec,
                  pl.BlockSpec(vblk, lambda o, k: (2 * o, k, 0, 0, 0, 0)),
                  pl.BlockSpec(vblk, lambda o, k: (2 * o + 1, k, 0, 0, 0, 0))],
        out_specs=pl.BlockSpec((None, FFT_KB, 2, FFT_IN, D_BRANCH), lambda o, k: (o, k, 0, 0, 0)),
        out_shape=jax.ShapeDtypeStruct((C_ORDER, n1, 2, FFT_IN, D_BRANCH), F32),
        compiler_params=_cparams("parallel", "parallel"),
        name="fft_filt",
    )(*tb["w_fwd"], v, v)


def _fft_mid_kernel(wf_ref, wi_ref, twr_ref, twi_ref, v_ref, h_ref, d_ref):
    for kb in range(FFT_KB):
        vr, vi = _unpack_pair(v_ref[kb].reshape(FFT_IN, D_BRANCH))
        z = jnp.dot(wf_ref[...], jnp.concatenate([vr, vi], axis=0), preferred_element_type=F32)
        zr, zi = z[:FFT_IN], z[FFT_IN:]
        hr, hi = h_ref[kb, 0], h_ref[kb, 1]
        pr = (zr * hr - zi * hi).astype(BF16)
        pi = (zr * hi + zi * hr).astype(BF16)
        d = jnp.dot(wi_ref[...], jnp.concatenate([pr, pi], axis=0), preferred_element_type=F32)
        dr, di = d[:FFT_IN], d[FFT_IN:]
        tr, ti = twr_ref[kb], twi_ref[kb]
        er, ei = dr * tr + di * ti, di * tr - dr * ti
        d_ref[kb] = _pack_pair(er, ei).reshape(d_ref.shape[1:])


def _fft_mid(tb, v, hspec, order):
    p, n1 = v.shape[:2]
    vspec = pl.BlockSpec((None, FFT_KB, FFT_NT, SUBLANES, D_BRANCH), lambda k, q: (q, k, 0, 0, 0))
    wspec = _const_spec((2 * FFT_IN, 2 * FFT_IN))
    tspec = pl.BlockSpec((FFT_KB, FFT_IN, 1), lambda k, q: (k, 0, 0))
    return pl.pallas_call(
        _fft_mid_kernel,
        grid=(n1 // FFT_KB, p),
        in_specs=[wspec, wspec, tspec, tspec, vspec,
                  pl.BlockSpec((None, FFT_KB, 2, FFT_IN, D_BRANCH), lambda k, q: (order, k, 0, 0, 0))],
        out_specs=vspec,
        out_shape=jax.ShapeDtypeStruct(v.shape, jnp.uint32),
        compiler_params=_cparams("parallel", "arbitrary"),
        name="fft_mid",
    )(tb["w_fwd"][0], tb["w_inv"][0], *tb["tw_in"], v, hspec)


def _ifft_outer_kernel(mh_ref, d_ref, u_ref, x_ref, b_ref, o_ref):
    n1 = d_ref.shape[1]
    for pp in range(FFT_PP):
        er, ei = _unpack_pair(d_ref[pp])
        e = jnp.concatenate([er.reshape(n1 * SUBLANES, D_BRANCH), ei.reshape(n1 * SUBLANES, D_BRANCH)], axis=0)
        y = jnp.dot(mh_ref[...], e, preferred_element_type=F32).reshape((2,) + o_ref.shape[2:])
        o_ref[:, pp] = x_ref[:, pp] * (y + u_ref[:, pp] * b_ref[...])


def _ifft_outer(tb, d, u, ucol, x, xcol, bias):
    n1, half = tb["n1"], tb["half"]
    p = d.shape[0]
    mshape = (2 * half * SUBLANES, 2 * n1 * SUBLANES)
    io = lambda col: pl.BlockSpec((2, FFT_PP, half, None, SUBLANES, D_BRANCH), lambda t, q: (0, q, 0, t, 0, col))
    return pl.pallas_call(
        _ifft_outer_kernel,
        grid=(FFT_NT, p // FFT_PP),
        in_specs=[_const_spec(mshape),
                  pl.BlockSpec((FFT_PP, n1, None, SUBLANES, D_BRANCH), lambda t, q: (q, 0, t, 0, 0)),
                  io(ucol), io(xcol), _const_spec((1, D_BRANCH))],
        out_specs=io(0),
        out_shape=jax.ShapeDtypeStruct((2, p, half, FFT_NT, SUBLANES, D_BRANCH), F32),
        compiler_params=_cparams("arbitrary", "arbitrary"),
        name="ifft_outer",
    )(tb["m_inv"][0], d, u, x, bias.reshape(1, D_BRANCH))


def _hyfilt_kernel(z_ref, w1_ref, b1_ref, fr_ref, w2_ref, b2_ref, w3_ref, dec_ref, o_ref, h_ref):
    seq = z_ref.shape[0]
    nblk = seq // HY_TB
    fr = fr_ref[...]

    @pl.when(pl.program_id(0) == 0)
    def _():
        def hidden(i, carry):
            r0 = pl.multiple_of(i * HY_TB, HY_TB)
            zb = z_ref[pl.ds(r0, HY_TB), :]
            h = jnp.sin(fr * (jnp.dot(zb, w1_ref[...], precision=HI, preferred_element_type=F32) + b1_ref[...]))
            h = jnp.sin(fr * (jnp.dot(h, w2_ref[...], precision=HI, preferred_element_type=F32) + b2_ref[...]))
            h_ref[pl.ds(r0, HY_TB), :] = h
            return carry

        lax.fori_loop(0, nblk, hidden, 0)

    def body(i, ss):
        r0 = pl.multiple_of(i * HY_TB, HY_TB)
        hf = jnp.dot(h_ref[pl.ds(r0, HY_TB), :], w3_ref[...], precision=HI, preferred_element_type=F32)
        hf = hf * jnp.exp(-z_ref[pl.ds(r0, HY_TB), 0:1] * dec_ref[...])
        o_ref[pl.ds(r0, HY_TB), :] = hf
        return ss + jnp.sum(hf * hf, axis=0, keepdims=True)

    ss = lax.fori_loop(0, nblk, body, jnp.zeros((1, D_BRANCH), F32))
    scale = lax.rsqrt(ss + EPS)

    def norm(i, carry):
        r0 = pl.multiple_of(i * HY_TB, HY_TB)
        o_ref[pl.ds(r0, HY_TB), :] = o_ref[pl.ds(r0, HY_TB), :] * scale
        return carry

    lax.fori_loop(0, nblk, norm, 0)


def _hyfilt(seq, w1, b1, freq, w2, b2, w3):
    t = jnp.linspace(0.0, 1.0, seq, dtype=F32)[:, None]
    bands = (C_EMB - 1) // 2
    w = 2.0 * math.pi * jnp.arange(seq, dtype=F32)[:, None] / seq
    fr = jnp.linspace(1e-4, bands - 1, bands, dtype=F32)[None]
    z = jnp.concatenate([t, jnp.cos(fr * w), -jnp.sin(fr * w)], axis=-1)
    z = jnp.pad(z, ((0, 0), (0, LANES - C_EMB)))
    padm = lambda a, r, c: jnp.pad(a.astype(F32), ((0, r - a.shape[0]), (0, c - a.shape[1])))
    row = lambda a: padm(a.reshape(1, -1), 1, LANES)
    dec = jnp.abs(jnp.linspace(C_MIN_DECAY, C_MAX_DECAY, D_BRANCH, dtype=F32)).reshape(1, D_BRANCH)
    nset = C_ORDER * 2
    return pl.pallas_call(
        _hyfilt_kernel,
        grid=(nset,),
        in_specs=[_const_spec((seq, LANES)), _const_spec((LANES, LANES)), _const_spec((1, LANES)),
                  _const_spec((1, LANES)), _const_spec((LANES, LANES)), _const_spec((1, LANES)),
                  pl.BlockSpec((LANES, D_BRANCH), lambda j: (0, j)), _const_spec((1, D_BRANCH))],
        out_specs=pl.BlockSpec((None, seq, D_BRANCH), lambda j: (j, 0, 0)),
        out_shape=jax.ShapeDtypeStruct((nset, seq, D_BRANCH), F32),
        scratch_shapes=[pltpu.VMEM((seq, LANES), F32)],
        compiler_params=_cparams("arbitrary"),
        name="hyfilt",
    )(z, padm(w1, LANES, LANES), row(b1), row(freq), padm(w2, LANES, LANES), row(b2),
      padm(w3, LANES, nset * D_BRANCH), dec)


def _hyena(uc3, w1, b1, freq, w2, b2, w3, bias):
    bsz, seq, width = uc3.shape
    tb = _fft_tables(seq)
    half = tb["half"]
    npair = bsz // 2
    uc6 = uc3.reshape(2, npair, half, FFT_NT, SUBLANES, width)
    filt = _hyfilt(seq, w1, b1, freq, w2, b2, w3).reshape(C_ORDER * 2, half, FFT_NT, SUBLANES, D_BRANCH)
    hspec = _fft_filt(tb, _fft_outer(tb, filt, False, 3, False))
    src = uc6
    for order in range(C_ORDER):
        d = _fft_mid(tb, _fft_outer(tb, src, True, 1, True), hspec, order)
        src = _ifft_outer(tb, d, src, 0, uc6, order + 1, bias[order])
    return src.reshape(bsz, seq, D_BRANCH)


AT_HALF = 64
AT_TQ = 2048
AT_SUB = 128


def _t5_bucket(rel):
    half = N_BUCKETS // 2
    max_exact = half // 2
    n = np.abs(rel)
    large = max_exact + (np.log(np.maximum(n, 1) / max_exact) / math.log(MAX_DISTANCE / max_exact)
                         * (half - max_exact)).astype(np.int64)
    large = np.minimum(large, half - 1)
    return (rel > 0).astype(np.int64) * half + np.where(n < max_exact, n, large)


def _attn_geometry(n):
    tq = min(AT_TQ, n)
    sub = min(AT_SUB, tq)
    if sub + 2 * AT_HALF >= n:
        sub = tq
    win = min(sub + 2 * AT_HALF, n)
    return tq, sub, win, n // tq, n // sub


def _attn_bias_tables(rel_bias, g, dil, n):
    _, sub, win, _, nsb = _attn_geometry(n)
    hs = slice(g * D_HEADS_PER_GROUP, (g + 1) * D_HEADS_PER_GROUP)
    offsets = np.arange(-AT_HALF, AT_HALF + 1) * dil
    onehot = np.zeros((2 * AT_HALF + 1, N_BUCKETS), np.float32)
    onehot[np.arange(2 * AT_HALF + 1), _t5_bucket(offsets)] = 1.0
    band = jnp.dot(rel_bias.astype(F32)[:, hs].T, jnp.asarray(onehot).T, precision=HI)
    nband = 2 * AT_HALF + 1
    lv = sub + win - 1
    tables = []
    for i in sorted({0, min(1, nsb - 1), nsb - 1}):
        ws = int(np.clip(i * sub - AT_HALF, 0, n - win))
        lo = (sub - 1) - (ws - i * sub) - AT_HALF
        v = jnp.pad(band, ((0, 0), (lo, lv - lo - nband)), constant_values=NEG_BIG)
        flat = jnp.tile(v, (1, sub + 1))[:, sub - 1:sub - 1 + sub * (lv - 1)]
        tables.append(flat.reshape(D_HEADS_PER_GROUP, sub, lv - 1)[:, :, :win])
    return jnp.stack(tables)


def _attn_kernel(q_ref, k_ref, v_ref, bias_ref, o_ref, l_ref, *, n, dil):
    tq, sub, win, _, nsb = _attn_geometry(n)
    ncase = bias_ref.shape[0]
    width = q_ref.shape[-1]
    nh = width // D_HEAD_DIM
    hp = pl.program_id(1)
    lane_head = lax.broadcasted_iota(jnp.int32, (sub, width), 1) // D_HEAD_DIM
    hmask = [lane_head == hh for hh in range(nh)]
    whole = win == n
    for r in range(dil):
        if whole:
            kw = k_ref[pl.ds(r, win, stride=dil), :].astype(BF16)
            vw = v_ref[pl.ds(r, win, stride=dil), :].astype(BF16)
        for j in range(tq // sub):
            sidx = pl.program_id(2) * (tq // sub) + j
            case = jnp.minimum(jnp.where(sidx == nsb - 1, ncase - 1, jnp.minimum(sidx, 1)), ncase - 1)
            bias = bias_ref[case, pl.ds(hp * nh, nh)].reshape(nh * sub, win)
            if not whole:
                ws = pl.multiple_of(jnp.clip(sidx * sub - AT_HALF, 0, n - win), AT_HALF)
                kw = k_ref[pl.ds(ws * dil + r, win, stride=dil), :].astype(BF16)
                vw = v_ref[pl.ds(ws * dil + r, win, stride=dil), :].astype(BF16)
            q = q_ref[pl.ds(j * sub * dil + r, sub, stride=dil), :] * (D_HEAD_DIM ** -0.5)
            qs = jnp.concatenate([jnp.where(hmask[hh], q, 0.0) for hh in range(nh)], axis=0).astype(BF16)
            s = lax.dot_general(qs, kw, (((1,), (1,)), ((), ())), preferred_element_type=F32) + bias
            m = jnp.max(s, axis=-1, keepdims=True)
            p = jnp.exp(s - m)
            l = jnp.sum(p, axis=-1, keepdims=True)
            o_all = jnp.dot(p.astype(BF16), vw, preferred_element_type=F32) / l
            lse = m + jnp.log(l)
            o_acc = jnp.zeros((sub, width), F32)
            l_acc = jnp.zeros((sub, width), F32)
            for hh in range(nh):
                o_acc = jnp.where(hmask[hh], o_all[hh * sub:(hh + 1) * sub], o_acc)
                l_acc = jnp.where(hmask[hh], lse[hh * sub:(hh + 1) * sub], l_acc)
            o_ref[pl.ds(j * sub * dil + r, sub, stride=dil), :] = o_acc
            l_ref[pl.ds(j * sub * dil + r, sub, stride=dil), :] = l_acc


def _banded_attention(pd3, g, dil, bias):
    bsz, seq, width = pd3.shape
    n = seq // dil
    tq, sub, win, nq, _ = _attn_geometry(n)
    rows = dil * tq
    bw = LANES if dil > 1 else D_BRANCH
    per = D_BRANCH // bw
    third = width // 3 // bw
    out_spec = pl.BlockSpec((None, rows, bw), lambda b, h, i: (b, i, h))
    shp = jax.ShapeDtypeStruct((bsz, seq, D_BRANCH), F32)
    return pl.pallas_call(
        functools.partial(_attn_kernel, n=n, dil=dil),
        grid=(bsz, per, nq),
        in_specs=[pl.BlockSpec((None, rows, bw), lambda b, h, i: (b, i, g * per + h)),
                  pl.BlockSpec((None, seq, bw), lambda b, h, i: (b, 0, third + g * per + h)),
                  pl.BlockSpec((None, seq, bw), lambda b, h, i: (b, 0, 2 * third + g * per + h)),
                  _const_spec(bias.shape)],
        out_specs=[out_spec, out_spec],
        out_shape=[shp, shp],
        compiler_params=_cparams("parallel", "parallel", "arbitrary"),
        name=f"attn_d{dil}",
    )(pd3, pd3, pd3, bias)


def _dilated_attention(pd3, rel_bias):
    bsz, seq, _ = pd3.shape
    outs, lses = [], []
    for g, (_, dil) in enumerate(D_GROUPS):
        o, l = _banded_attention(pd3, g, dil, _attn_bias_tables(rel_bias, g, dil, seq // dil))
        outs.append(o.reshape(bsz * seq, D_BRANCH))
        lses.append(l.reshape(bsz * seq, D_BRANCH))
    return outs, lses


def kernel(x, norm1_g, w_in, hgrn_lb_logits, hgrn_norm_g, lru_conv_w, lru_conv_b, lru_wa, lru_ba, lru_wx, lru_bx,
           lru_lambda, hy_conv_w, hy_conv_b, hy_w1, hy_b1, hy_freq, hy_w2, hy_b2, hy_w3, hy_bias, rel_bias,
           w_branch, w_gate, b_gate, w_out, norm2_g, w_ff1, w_ff3, w_ff2, final_g):
    bsz, seq, _ = x.shape
    n = bsz * seq
    lb_soft = jax.nn.softmax(hgrn_lb_logits.astype(F32), axis=0)
    lower_bounds = jnp.cumsum(lb_soft, axis=0) - lb_soft[0]
    x2 = x.reshape(n, D_MODEL)
    flat = lambda a: a.reshape(n, D_BRANCH)
    for l in range(DEPTH):
        pa, pb, uc, pd = _inproj(x2, norm1_g[l], w_in[l].astype(BF16), hy_conv_w[l], hy_conv_b[l], seq)
        oa_f, oa_b = _hgrn(pa.reshape(bsz, seq, IN_A), lower_bounds[l])
        yb = _rglru(pb.reshape(bsz, seq, IN_B), lru_conv_w[l], lru_conv_b[l], lru_wa[l], lru_ba[l],
                    lru_wx[l], lru_bx[l], lru_lambda[l])
        yc = _hyena(uc.reshape(bsz, seq, IN_C), hy_w1[l], hy_b1[l], hy_freq[l],
                    hy_w2[l], hy_b2[l], hy_w3[l], hy_bias[l])
        od, ld = _dilated_attention(pd.reshape(bsz, seq, IN_D), rel_bias)
        x2 = _merge(x2, norm1_g[l], flat(oa_f), flat(oa_b), pa, hgrn_norm_g[l], flat(yb), flat(yc), od, ld,
                    w_gate[l].reshape(D_MODEL, N_BRANCH * D_MODEL).astype(BF16),
                    b_gate[l].reshape(1, N_BRANCH * D_MODEL), w_branch[l].astype(BF16), w_out[l].astype(BF16))
        x2 = _ffn(x2, norm2_g[l], *_ffn_weights(w_ff1[l], w_ff3[l], w_ff2[l]), final_g, l == DEPTH - 1)
    return x2.reshape(bsz, seq, D_MODEL)
```

```python
import functools
import math

import jax
import jax.numpy as jnp
import numpy as np
from jax import lax
from jax.experimental import pallas as pl
from jax.experimental.pallas import tpu as pltpu

F32 = jnp.float32
BF16 = jnp.bfloat16
HI = lax.Precision.HIGHEST

D_MODEL = 1024
DEPTH = 2
EPS = 1e-6
TINY = 1e-30
N_BRANCH = 4
D_BRANCH = 256
A_HEADS = 4
A_DK = 64
A_CHUNK = 64
B_BLOCKS = 4
B_BW = 64
B_CONV = 4
LRU_C = 8.0
C_ORDER = 2
C_CONV = 3
C_EMB = 33
C_MIN_DECAY = math.log(1e-2) / 1.5
C_MAX_DECAY = math.log(1e-2) / 0.3
D_GROUPS = ((128, 1), (512, 4), (2048, 16))
D_HEADS_PER_GROUP = 4
D_HEAD_DIM = 64
D_QKV = 768
N_BUCKETS = 32
MAX_DISTANCE = 1024
NEG_BIG = -1e30
D_FF = 2816
IN_A = 5 * D_BRANCH
IN_B = 2 * D_BRANCH
IN_C = 3 * D_BRANCH
IN_D = 3 * D_QKV
IN_WIDTH = IN_A + IN_B + IN_C + IN_D

LANES = 128
SUBLANES = 8
VMEM_LIMIT = 56 * 1024 * 1024
BF16_BITS = 16
BF16_HIGH_MASK = 0xFFFF0000


def _cparams(*sem):
    return pltpu.CompilerParams(dimension_semantics=sem, vmem_limit_bytes=VMEM_LIMIT)


def _const_spec(shape):
    nd = len(shape)
    return pl.BlockSpec(shape, lambda *_: (0,) * nd, pipeline_mode=pl.Buffered(1))


def _rms(x, g):
    return x * lax.rsqrt(jnp.mean(x * x, axis=-1, keepdims=True) + EPS) * g


def _sigmoid(x):
    return 1.0 / (1.0 + jnp.exp(-x))


IN_TM = 512
IN_CHUNK = 256
IN_CONV_ROWS = 256


def _inproj_kernel(x_ref, xp_ref, xn_ref, g_ref, w_ref, cw_ref, cb_ref, oa_ref, ob_ref, oc_ref, od_ref, cs_ref,
                   *, tiles_per_seq):
    i = pl.program_id(0)
    g = g_ref[...]
    h = _rms(x_ref[...], g).astype(BF16)
    keep_prev = jnp.where(i % tiles_per_seq != 0, 1.0, 0.0)
    keep_next = jnp.where(i % tiles_per_seq != tiles_per_seq - 1, 1.0, 0.0)
    h_ext = jnp.concatenate([(_rms(xp_ref[...], g) * keep_prev).astype(BF16), h,
                             (_rms(xn_ref[...], g) * keep_next).astype(BF16)], axis=0)
    off = 0
    for o_ref in (oa_ref, ob_ref, oc_ref, od_ref):
        width = o_ref.shape[-1]
        for c in range(0, width, IN_CHUNK):
            w = w_ref[:, off + c:off + c + IN_CHUNK]
            if o_ref is oc_ref:
                cs_ref[:, c:c + IN_CHUNK] = jnp.dot(h_ext, w, preferred_element_type=F32)
            else:
                o_ref[:, c:c + IN_CHUNK] = jnp.dot(h, w, preferred_element_type=F32)
        off += width
    left = C_CONV // 2
    for r0 in range(0, IN_TM, IN_CONV_ROWS):
        for c in range(0, IN_C, IN_CHUNK):
            win = cs_ref[r0:r0 + IN_CONV_ROWS + 2 * SUBLANES, c:c + IN_CHUNK]
            y = cb_ref[:, c:c + IN_CHUNK]
            for j in range(C_CONV):
                s0 = SUBLANES + j - left
                y = y + cw_ref[j:j + 1, c:c + IN_CHUNK] * win[s0:s0 + IN_CONV_ROWS, :]
            oc_ref[r0:r0 + IN_CONV_ROWS, c:c + IN_CHUNK] = y


def _inproj(x2, g, w_bf16, cw, cb, seq):
    n = x2.shape[0]
    widths = (IN_A, IN_B, IN_C, IN_D)
    per8 = IN_TM // SUBLANES
    last8 = n // SUBLANES - 1
    return pl.pallas_call(
        functools.partial(_inproj_kernel, tiles_per_seq=seq // IN_TM),
        grid=(n // IN_TM,),
        in_specs=[pl.BlockSpec((IN_TM, D_MODEL), lambda i: (i, 0)),
                  pl.BlockSpec((SUBLANES, D_MODEL), lambda i: (jnp.maximum(i * per8 - 1, 0), 0)),
                  pl.BlockSpec((SUBLANES, D_MODEL), lambda i: (jnp.minimum((i + 1) * per8, last8), 0)),
                  _const_spec((1, D_MODEL)),
                  _const_spec((D_MODEL, IN_WIDTH)),
                  _const_spec((C_CONV, IN_C)), _const_spec((1, IN_C))],
        out_specs=[pl.BlockSpec((IN_TM, w), lambda i: (i, 0)) for w in widths],
        out_shape=[jax.ShapeDtypeStruct((n, w), F32) for w in widths],
        scratch_shapes=[pltpu.VMEM((IN_TM + 2 * SUBLANES, IN_C), F32)],
        compiler_params=_cparams("parallel"),
        name="inproj",
    )(x2, x2, x2, g.reshape(1, D_MODEL), w_bf16, cw, cb.reshape(1, IN_C))


FF_TM = 1024
FF_CHUNK = 256
FF_NCHUNK = D_FF // FF_CHUNK


def _ffn_kernel(x_ref, g_ref, w1_ref, w3_ref, w2_ref, fg_ref, o_ref, acc_ref, *, final):
    h = _rms(x_ref[...], g_ref[...]).astype(BF16)

    def chunk(c):
        c0 = pl.multiple_of(c * FF_CHUNK, FF_CHUNK)
        a = jnp.dot(h, w1_ref[:, pl.ds(c0, FF_CHUNK)], preferred_element_type=F32)
        b = jnp.dot(h, w3_ref[:, pl.ds(c0, FF_CHUNK)], preferred_element_type=F32)
        t = (a * _sigmoid(a) * b).astype(BF16)
        return jnp.dot(t, w2_ref[pl.ds(c0, FF_CHUNK), :], preferred_element_type=F32)

    acc_ref[...] = chunk(0)

    def body(c, carry):
        acc_ref[...] += chunk(c)
        return carry

    lax.fori_loop(1, FF_NCHUNK, body, 0)
    y = x_ref[...] + acc_ref[...]
    if final:
        y = _rms(y, fg_ref[...])
    o_ref[...] = y


def _ffn(x2, g, w1c, w3c, w2c, final_g, final):
    n = x2.shape[0]
    return pl.pallas_call(
        functools.partial(_ffn_kernel, final=final),
        grid=(n // FF_TM,),
        in_specs=[pl.BlockSpec((FF_TM, D_MODEL), lambda i: (i, 0)),
                  _const_spec((1, D_MODEL)),
                  _const_spec((D_MODEL, D_FF)),
                  _const_spec((D_MODEL, D_FF)),
                  _const_spec((D_FF, D_MODEL)),
                  _const_spec((1, D_MODEL))],
        out_specs=pl.BlockSpec((FF_TM, D_MODEL), lambda i: (i, 0)),
        out_shape=jax.ShapeDtypeStruct((n, D_MODEL), F32),
        scratch_shapes=[pltpu.VMEM((FF_TM, D_MODEL), F32)],
        compiler_params=_cparams("parallel"),
        name="ffn_final" if final else "ffn",
    )(x2, g.reshape(1, D_MODEL), w1c, w3c, w2c, final_g.reshape(1, D_MODEL))


def _ffn_weights(w1, w3, w2):
    return w1.astype(BF16), w3.astype(BF16), w2.astype(BF16)


MG_TM = 512
MG_CHUNK = 256


def _head_ones():
    r = np.arange(D_BRANCH)[:, None] // A_DK
    c = np.arange(D_BRANCH)[None, :] // A_DK
    return jnp.asarray((r == c).astype(np.float32) / A_DK, BF16)


def _merge_kernel(x_ref, g1_ref, oaf_ref, oab_ref, ga_ref, hg_ref, hm_ref, yb_ref, yc_ref,
                  o0_ref, o1_ref, o2_ref, l0_ref, l1_ref, l2_ref,
                  wg_ref, bg_ref, wb_ref, wo_ref, out_ref):
    x = x_ref[...]
    h = _rms(x, g1_ref[...]).astype(BF16)
    oa = oaf_ref[...] + oab_ref[...]
    ms = jnp.dot((oa * oa).astype(BF16), hm_ref[...], preferred_element_type=F32)
    ga = ga_ref[...]
    ya = oa * lax.rsqrt(ms + EPS) * hg_ref[...] * (ga * _sigmoid(ga))
    l0, l1, l2 = l0_ref[...], l1_ref[...], l2_ref[...]
    m = jnp.maximum(jnp.maximum(l0, l1), l2)
    e0, e1, e2 = jnp.exp(l0 - m), jnp.exp(l1 - m), jnp.exp(l2 - m)
    yd = (e0 * o0_ref[...] + e1 * o1_ref[...] + e2 * o2_ref[...]) / (e0 + e1 + e2)
    ys = [y.astype(BF16) for y in (ya, yb_ref[...], yc_ref[...], yd)]
    cols = []
    for c in range(0, D_MODEL, MG_CHUNK):
        mixed = None
        for j in range(N_BRANCH):
            lo = j * D_MODEL + c
            gate = _sigmoid(jnp.dot(h, wg_ref[:, lo:lo + MG_CHUNK], preferred_element_type=F32)
                            + bg_ref[:, lo:lo + MG_CHUNK])
            t = gate * jnp.dot(ys[j], wb_ref[j, :, c:c + MG_CHUNK], preferred_element_type=F32)
            mixed = t if mixed is None else mixed + t
        cols.append(mixed.astype(BF16))
    mixed = jnp.concatenate(cols, axis=1)
    out_ref[...] = x + jnp.dot(mixed, wo_ref[...], preferred_element_type=F32)


def _merge(x2, g1, oa_f, oa_b, pa, hg, yb, yc, od, ld, wg, bg, wb, wo):
    n = x2.shape[0]
    tile = lambda w: pl.BlockSpec((MG_TM, w), lambda i: (i, 0))
    return pl.pallas_call(
        _merge_kernel,
        grid=(n // MG_TM,),
        in_specs=[tile(D_MODEL), _const_spec((1, D_MODEL)),
                  tile(D_BRANCH), tile(D_BRANCH),
                  pl.BlockSpec((MG_TM, D_BRANCH), lambda i: (i, 4)),
                  _const_spec((1, D_BRANCH)), _const_spec((D_BRANCH, D_BRANCH)),
                  tile(D_BRANCH), tile(D_BRANCH),
                  tile(D_BRANCH), tile(D_BRANCH), tile(D_BRANCH),
                  tile(D_BRANCH), tile(D_BRANCH), tile(D_BRANCH),
                  _const_spec((D_MODEL, N_BRANCH * D_MODEL)), _const_spec((1, N_BRANCH * D_MODEL)),
                  _const_spec((N_BRANCH, D_BRANCH, D_MODEL)), _const_spec((D_MODEL, D_MODEL))],
        out_specs=tile(D_MODEL),
        out_shape=jax.ShapeDtypeStruct((n, D_MODEL), F32),
        compiler_params=_cparams("parallel"),
        name="merge",
    )(x2, g1.reshape(1, D_MODEL), oa_f, oa_b, pa, hg.reshape(1, D_BRANCH), _head_ones(), yb, yc,
      od[0], od[1], od[2], ld[0], ld[1], ld[2], wg, bg, wb, wo)


HG_TS = 256
HG_NCH = HG_TS // A_CHUNK
HG_MID = A_CHUNK // 2
HG_BR = 4


def _hgrn_tables():
    r = np.arange(HG_TS)[:, None]
    c = np.arange(HG_TS)[None, :]
    same = (r // A_CHUNK) == (c // A_CHUNK)
    rr = np.arange(HG_NCH * A_HEADS * A_CHUNK)[:, None]
    same_s = (rr // (A_HEADS * A_CHUNK)) == (c // A_CHUNK)
    t, sidx = rr % A_CHUNK, c % A_CHUNK
    out = []
    for fwd in (True, False):
        order = (r >= c) if fwd else (r <= c)
        order_s = (t >= sidx) if fwd else (t <= sidx)
        out.append((jnp.asarray((same & order).astype(np.float32), BF16),
                    jnp.asarray((same_s & order_s).astype(np.float32), F32)))
    return out


def _hgrn_prep(q, fl, v, lb, cum, smask, fwd):
    lane_head = lax.broadcasted_iota(jnp.int32, (A_CHUNK, D_BRANCH), 1) // A_DK
    hmask = [(lane_head == hh).astype(F32) for hh in range(A_HEADS)]
    blk_r = lax.broadcasted_iota(jnp.int32, (D_BRANCH, D_BRANCH), 0) // A_DK
    blk_c = lax.broadcasted_iota(jnp.int32, (D_BRANCH, D_BRANCH), 1) // A_DK
    blockdiag = (blk_r == blk_c).astype(F32)
    col_chunk = lax.broadcasted_iota(jnp.int32, (D_BRANCH, HG_TS), 1) // A_CHUNK

    sg = _sigmoid(fl)
    f = lb + (1.0 - lb) * sg
    g = jnp.log(jnp.maximum(f, TINY))
    kk = (1.0 - lb) * (1.0 - sg)
    g1 = g.astype(BF16)
    r1 = g - g1.astype(F32)
    g2 = r1.astype(BF16)
    g3 = (r1 - g2.astype(F32)).astype(BF16)
    b = (jnp.dot(cum, g1, preferred_element_type=F32) + jnp.dot(cum, g2, preferred_element_type=F32)
         + jnp.dot(cum, g3, preferred_element_type=F32))
    rows = lambda c: slice(c * A_CHUNK, (c + 1) * A_CHUNK)
    last = (A_CHUNK - 1) if fwd else 0
    bcast = lambda r0: jnp.concatenate(
        [jnp.broadcast_to(b[c * A_CHUNK + r0:c * A_CHUNK + r0 + 1, :], (A_CHUNK, D_BRANCH)) for c in range(HG_NCH)], 0)
    bm = bcast(HG_MID)
    bl = bcast(last)
    qt = q * jnp.exp(b - bm)
    kt = (kk * jnp.exp(bm - b)).astype(BF16)
    qe = (q * jnp.exp(b)).astype(BF16)
    kh = (kk * jnp.exp(bl - b)).astype(BF16)
    qs = jnp.concatenate([qt[rows(c)] * hmask[hh] for c in range(HG_NCH) for hh in range(A_HEADS)],
                         axis=0).astype(BF16)
    s = lax.dot_general(qs, kt, (((1,), (1,)), ((), ())), preferred_element_type=F32)
    s = (s * smask).astype(BF16)
    ost = jnp.dot(s, v.astype(BF16), preferred_element_type=F32)
    o_intra = []
    for c in range(HG_NCH):
        base = c * A_HEADS * A_CHUNK
        o = ost[base:base + A_CHUNK] * hmask[0]
        for hh in range(1, A_HEADS):
            o = o + ost[base + hh * A_CHUNK:base + (hh + 1) * A_CHUNK] * hmask[hh]
        o_intra.append(o)
    vt = v.T
    lhs = jnp.concatenate([jnp.where(col_chunk == c, vt, 0.0) for c in range(HG_NCH)], axis=0).astype(BF16)
    updall = jnp.dot(lhs, kh, preferred_element_type=F32)
    upd = [updall[c * D_BRANCH:(c + 1) * D_BRANCH] * blockdiag for c in range(HG_NCH)]
    decay = [jnp.exp(b[c * A_CHUNK + last:c * A_CHUNK + last + 1, :]) for c in range(HG_NCH)]
    return o_intra, upd, qe, decay


def _hgrn_scan(o_intra, upd, qe, decay, st, fwd):
    rows = lambda c: slice(c * A_CHUNK, (c + 1) * A_CHUNK)
    outs = [None] * HG_NCH
    for c in (range(HG_NCH) if fwd else range(HG_NCH - 1, -1, -1)):
        outs[c] = o_intra[c] + lax.dot_general(qe[rows(c)], st.astype(BF16), (((1,), (1,)), ((), ())),
                                               preferred_element_type=F32)
        st = st * decay[c] + upd[c]
    return jnp.concatenate(outs, axis=0), st


def _hgrn_kernel(qf_ref, ff_ref, vf_ref, qb_ref, fb_ref, vb_ref, lb_ref, cf_ref, mf_ref, cb_ref, mb_ref,
                 of_ref, ob_ref, sf_ref, sb_ref):
    @pl.when(pl.program_id(1) == 0)
    def _():
        sf_ref[...] = jnp.zeros_like(sf_ref)
        sb_ref[...] = jnp.zeros_like(sb_ref)

    lb = lb_ref[...]
    pf = [_hgrn_prep(qf_ref[r], ff_ref[r], vf_ref[r], lb, cf_ref[...], mf_ref[...], True) for r in range(HG_BR)]
    pb = [_hgrn_prep(qb_ref[r], fb_ref[r], vb_ref[r], lb, cb_ref[...], mb_ref[...], False) for r in range(HG_BR)]
    for r in range(HG_BR):
        o_f, st_f = _hgrn_scan(*pf[r], sf_ref[r], True)
        o_b, st_b = _hgrn_scan(*pb[r], sb_ref[r], False)
        of_ref[r] = o_f
        ob_ref[r] = o_b
        sf_ref[r] = st_f
        sb_ref[r] = st_b


def _hgrn(pa3, lb):
    bsz, seq, _ = pa3.shape
    nblk = seq // HG_TS
    blk = (HG_BR, HG_TS, D_BRANCH)
    up = lambda col: pl.BlockSpec(blk, lambda b, i: (b, i, col))
    down = lambda col: pl.BlockSpec(blk, lambda b, i: (b, nblk - 1 - i, col))
    shp = jax.ShapeDtypeStruct((bsz, seq, D_BRANCH), F32)
    (cum_f, sm_f), (cum_b, sm_b) = _hgrn_tables()
    mshape = (HG_NCH * A_HEADS * A_CHUNK, HG_TS)
    state = pltpu.VMEM((HG_BR, D_BRANCH, D_BRANCH), F32)
    return pl.pallas_call(
        _hgrn_kernel,
        grid=(bsz // HG_BR, nblk),
        in_specs=[up(0), up(1), up(3), down(0), down(2), down(3), _const_spec((1, D_BRANCH)),
                  _const_spec((HG_TS, HG_TS)), _const_spec(mshape), _const_spec((HG_TS, HG_TS)), _const_spec(mshape)],
        out_specs=[up(0), down(0)],
        out_shape=[shp, shp],
        scratch_shapes=[state, state],
        compiler_params=_cparams("parallel", "arbitrary"),
        name="hgrn2",
    )(pa3, pa3, pa3, pa3, pa3, pa3, lb.reshape(1, D_BRANCH), cum_f, sm_f, cum_b, sm_b)


RG_TB = 1024
RG_PAD = SUBLANES
RG_LEFT = B_CONV // 2


def _dot3_rhs(x, wh, wl):
    xh = x.astype(BF16)
    xl = (x - xh.astype(F32)).astype(BF16)
    return (jnp.dot(xh, wh, preferred_element_type=F32) + jnp.dot(xl, wh, preferred_element_type=F32)
            + jnp.dot(xh, wl, preferred_element_type=F32))


def _group_scan(a, u, fwd):
    row = lax.broadcasted_iota(jnp.int32, a.shape, 1)
    k = 1
    while k < SUBLANES:
        if fwd:
            keep = row >= k
            us, as_ = pltpu.roll(u, k, 1), pltpu.roll(a, k, 1)
        else:
            keep = row < SUBLANES - k
            us, as_ = pltpu.roll(u, SUBLANES - k, 1), pltpu.roll(a, SUBLANES - k, 1)
        u = a * jnp.where(keep, us, 0.0) + u
        a = a * jnp.where(keep, as_, 1.0)
        k *= 2
    return a, u


def _block_scan(a, u, carry, fwd):
    t = a.shape[0]
    ngrp = t // SUBLANES
    ag, ug = _group_scan(a.reshape(ngrp, SUBLANES, D_BRANCH), u.reshape(ngrp, SUBLANES, D_BRANCH), fwd)
    hs = [None] * ngrp
    for g in (range(ngrp) if fwd else range(ngrp - 1, -1, -1)):
        h = ug[g] + ag[g] * carry
        hs[g] = h
        carry = h[SUBLANES - 1:SUBLANES, :] if fwd else h[0:1, :]
    return jnp.concatenate(hs, axis=0), carry


def _gelu_tanh(x):
    return 0.5 * x * (1.0 + jnp.tanh(math.sqrt(2.0 / math.pi) * (x + 0.044715 * (x * x * x))))


def _rglru_kernel(x_ref, gt_ref, cw_ref, cb_ref, wh_ref, wl_ref, bg_ref, lam_ref, o_ref, xp_ref, xc_ref):
    seq = x_ref.shape[0]
    nblk = seq // RG_TB
    xp_ref[0:RG_PAD, :] = jnp.zeros((RG_PAD, D_BRANCH), F32)
    xp_ref[RG_PAD + seq:2 * RG_PAD + seq, :] = jnp.zeros((RG_PAD, D_BRANCH), F32)
    xp_ref[RG_PAD:RG_PAD + seq, :] = x_ref[...]
    nl = -lam_ref[...]
    sp = jnp.maximum(nl, 0.0) + jnp.log(1.0 + jnp.exp(-jnp.abs(nl)))

    def block(i, carry, dirn):
        r0 = pl.multiple_of(i * RG_TB, RG_TB)
        if dirn == 0:
            win = xp_ref[pl.ds(r0, RG_TB + 2 * RG_PAD), :]
            xc = cb_ref[...]
            for j in range(B_CONV):
                s0 = RG_PAD + j - RG_LEFT
                xc = xc + cw_ref[j:j + 1, :] * win[s0:s0 + RG_TB, :]
            xc_ref[pl.ds(r0, RG_TB), :] = xc
        else:
            xc = xc_ref[pl.ds(r0, RG_TB), :]
        cols = slice(dirn * 2 * D_BRANCH, (dirn + 1) * 2 * D_BRANCH)
        gates = _dot3_rhs(xc, wh_ref[:, cols], wl_ref[:, cols]) + bg_ref[:, cols]
        r = _sigmoid(gates[:, :D_BRANCH])
        ig = _sigmoid(gates[:, D_BRANCH:])
        log_a = -LRU_C * r * sp[dirn:dirn + 1, :]
        a = jnp.exp(log_a)
        u = jnp.sqrt(jnp.maximum(-jnp.tanh(log_a) * (a * a + 1.0), 0.0)) * ig * xc
        h, carry = _block_scan(a, u, carry, dirn == 0)
        if dirn == 0:
            o_ref[pl.ds(r0, RG_TB), :] = h
        else:
            o_ref[pl.ds(r0, RG_TB), :] = (o_ref[pl.ds(r0, RG_TB), :] + h) * _gelu_tanh(gt_ref[pl.ds(r0, RG_TB), :])
        return carry

    zero = jnp.zeros((1, D_BRANCH), F32)
    lax.fori_loop(0, nblk, lambda i, c: block(i, c, 0), zero)
    lax.fori_loop(0, nblk, lambda i, c: block(nblk - 1 - i, c, 1), zero)


def _blockdiag(w):
    eye = jnp.eye(B_BLOCKS, dtype=w.dtype)
    return jnp.einsum('ncd,nm->ncmd', w, eye).reshape(D_BRANCH, D_BRANCH)


def _rglru(pb3, cw, cb, wa, ba, wx, bx, lam):
    bsz, seq, _ = pb3.shape
    wg = jnp.concatenate([_blockdiag(wa[0]), _blockdiag(wx[0]), _blockdiag(wa[1]), _blockdiag(wx[1])], axis=1)
    bg = jnp.concatenate([ba[0], bx[0], ba[1], bx[1]]).reshape(1, 4 * D_BRANCH)
    wh = wg.astype(BF16)
    wl = (wg - wh.astype(F32)).astype(BF16)
    blk = (None, seq, D_BRANCH)
    return pl.pallas_call(
        _rglru_kernel,
        grid=(bsz,),
        in_specs=[pl.BlockSpec(blk, lambda b: (b, 0, 0)),
                  pl.BlockSpec(blk, lambda b: (b, 0, 1)),
                  _const_spec((B_CONV, D_BRANCH)), _const_spec((1, D_BRANCH)),
                  _const_spec((D_BRANCH, 4 * D_BRANCH)), _const_spec((D_BRANCH, 4 * D_BRANCH)),
                  _const_spec((1, 4 * D_BRANCH)), _const_spec((2, D_BRANCH))],
        out_specs=pl.BlockSpec(blk, lambda b: (b, 0, 0)),
        out_shape=jax.ShapeDtypeStruct((bsz, seq, D_BRANCH), F32),
        scratch_shapes=[pltpu.VMEM((seq + 2 * RG_PAD, D_BRANCH), F32), pltpu.VMEM((seq, D_BRANCH), F32)],
        compiler_params=_cparams("parallel"),
        name="rglru",
    )(pb3, pb3, cw, cb.reshape(1, D_BRANCH), wh, wl, bg, lam)


FFT_IN = 128
FFT_NT = FFT_IN // SUBLANES
HY_TB = 1024


def _split_np(a):
    a = np.asarray(a, np.float32)
    hi = a.astype(jnp.bfloat16)
    lo = (a - hi.astype(np.float32)).astype(jnp.bfloat16)
    return jnp.asarray(hi), jnp.asarray(lo)


def _dot3(mh, ml, x):
    xh = x.astype(BF16)
    xl = (x - xh.astype(F32)).astype(BF16)
    return (jnp.dot(mh, xh, preferred_element_type=F32) + jnp.dot(mh, xl, preferred_element_type=F32)
            + jnp.dot(ml, xh, preferred_element_type=F32))


def _dotp(mh, ml, x, passes):
    if passes == 1:
        return jnp.dot(mh, x.astype(BF16), preferred_element_type=F32)
    return _dot3(mh, ml, x)


def _pack_pair(re, im):
    rb = lax.bitcast_convert_type(re.astype(BF16).astype(F32), jnp.uint32)
    ib = lax.bitcast_convert_type(im.astype(BF16).astype(F32), jnp.uint32)
    return rb | (ib >> BF16_BITS)


def _unpack_pair(w):
    re = lax.bitcast_convert_type(w & jnp.uint32(BF16_HIGH_MASK), F32)
    im = lax.bitcast_convert_type(w << BF16_BITS, F32)
    return re.astype(BF16), im.astype(BF16)


def _fft_tables(seq):
    n = 2 * seq
    n1 = n // FFT_IN
    half = n1 // 2
    eye = np.eye(SUBLANES)
    a = 2.0 * np.pi * np.outer(np.arange(n1), np.arange(half)) / n1
    gr, gi = np.cos(a), -np.sin(a)
    blk = np.stack([np.stack([gr, -gi], axis=1), np.stack([gi, gr], axis=1)], axis=0)
    m_out = np.einsum('rkis,cd->rkcisd', blk, eye).reshape(2 * n1 * SUBLANES, 2 * half * SUBLANES)
    ir, ii = gr.T / n, -gi.T / n
    blk = np.stack([np.stack([ir, -ii], axis=1), np.stack([ii, ir], axis=1)], axis=0)
    m_inv = np.einsum('otrk,cd->otcrkd', blk, eye).reshape(2 * half * SUBLANES, 2 * n1 * SUBLANES)
    a = 2.0 * np.pi * np.outer(np.arange(FFT_IN), np.arange(FFT_IN)) / FFT_IN
    fr, fi = np.cos(a), -np.sin(a)
    w_fwd = np.block([[fr, -fi], [fi, fr]])
    w_inv = np.block([[fr, fi], [-fi, fr]])
    s_in = SUBLANES * np.arange(FFT_NT)[:, None, None] + np.arange(SUBLANES)[None, None, :]
    th = 2.0 * np.pi * s_in * np.arange(n1)[None, :, None] / n
    tw = (jnp.asarray(np.cos(th)[..., None], F32), jnp.asarray(-np.sin(th)[..., None], F32))
    th = 2.0 * np.pi * np.outer(np.arange(n1), np.arange(FFT_IN)) / n
    tw_in = (jnp.asarray(np.cos(th)[..., None], F32), jnp.asarray(-np.sin(th)[..., None], F32))
    return dict(m_out=_split_np(m_out), m_inv=_split_np(m_inv), w_fwd=_split_np(w_fwd), w_inv=_split_np(w_inv),
                tw=tw, tw_in=tw_in, n1=n1, half=half)


FFT_PP = 4


def _tw_spec(n1):
    return pl.BlockSpec((None, n1, SUBLANES, 1), lambda t, q: (t, 0, 0, 0))


def _fft_outer_kernel(mh_ref, ml_ref, twr_ref, twi_ref, z_ref, v_ref, *, complex_in, passes, packed):
    n1 = v_ref.shape[1]
    tr, ti = twr_ref[...], twi_ref[...]
    for pp in range(FFT_PP):
        zz = z_ref[:, pp] if complex_in else z_ref[pp]
        rows_in = math.prod(zz.shape[:-1])
        v = _dotp(mh_ref[:, :rows_in], ml_ref[:, :rows_in], zz.reshape(rows_in, D_BRANCH), passes)
        vr = v[:n1 * SUBLANES].reshape(n1, SUBLANES, D_BRANCH)
        vi = v[n1 * SUBLANES:].reshape(n1, SUBLANES, D_BRANCH)
        wr, wi = vr * tr - vi * ti, vr * ti + vi * tr
        if packed:
            v_ref[pp] = _pack_pair(wr, wi)
        else:
            v_ref[pp, :, 0] = wr
            v_ref[pp, :, 1] = wi


def _fft_outer(tb, z, complex_in, passes, packed):
    n1, half = tb["n1"], tb["half"]
    if complex_in:
        p = z.shape[1]
        zspec = pl.BlockSpec((2, FFT_PP, half, None, SUBLANES, D_BRANCH), lambda t, q: (0, q, 0, t, 0, 0))
    else:
        p = z.shape[0]
        zspec = pl.BlockSpec((FFT_PP, half, None, SUBLANES, D_BRANCH), lambda t, q: (q, 0, t, 0, 0))
    mshape = (2 * n1 * SUBLANES, 2 * half * SUBLANES)
    if packed:
        out_spec = pl.BlockSpec((FFT_PP, n1, None, SUBLANES, D_BRANCH), lambda t, q: (q, 0, t, 0, 0))
        out_shape = jax.ShapeDtypeStruct((p, n1, FFT_NT, SUBLANES, D_BRANCH), jnp.uint32)
    else:
        out_spec = pl.BlockSpec((FFT_PP, n1, 2, None, SUBLANES, D_BRANCH), lambda t, q: (q, 0, 0, t, 0, 0))
        out_shape = jax.ShapeDtypeStruct((p, n1, 2, FFT_NT, SUBLANES, D_BRANCH), F32)
    return pl.pallas_call(
        functools.partial(_fft_outer_kernel, complex_in=complex_in, passes=passes, packed=packed),
        grid=(FFT_NT, p // FFT_PP),
        in_specs=[_const_spec(mshape), _const_spec(mshape), _tw_spec(n1), _tw_spec(n1), zspec],
        out_specs=out_spec,
        out_shape=out_shape,
        compiler_params=_cparams("arbitrary", "arbitrary"),
        name="fft_outer_c" if complex_in else "fft_outer_r",
    )(*tb["m_out"], *tb["tw"], z)


FFT_KB = 16


def _fft_filt_kernel(wh_ref, wl_ref, vf_ref, vb_ref, h_ref):
    for kb in range(FFT_KB):
        zf = _dot3(wh_ref[...], wl_ref[...], vf_ref[kb].reshape(2 * FFT_IN, D_BRANCH))
        zb = _dot3(wh_ref[...], wl_ref[...], vb_ref[kb].reshape(2 * FFT_IN, D_BRANCH))
        h_ref[kb, 0] = zf[:FFT_IN] + zb[:FFT_IN]
        h_ref[kb, 1] = zf[FFT_IN:] - zb[FFT_IN:]


def _fft_filt(tb, v):
    n1 = v.shape[1]
    wspec = _const_spec((2 * FFT_IN, 2 * FFT_IN))
    vblk = (None, FFT_KB, 2, FFT_NT, SUBLANES, D_BRANCH)
    return pl.pallas_call(
        _fft_filt_kernel,
        grid=(C_ORDER, n1 // FFT_KB),
        in_specs=[wspec, wspec,
                  pl.BlockSpec(vblk, lambda o, k: (2 * o, k, 0, 0, 0, 0)),
                  pl.BlockSpec(vblk, lambda o, k: (2 * o + 1, k, 0, 0, 0, 0))],
        out_specs=pl.BlockSpec((None, FFT_KB, 2, FFT_IN, D_BRANCH), lambda o, k: (o, k, 0, 0, 0)),
        out_shape=jax.ShapeDtypeStruct((C_ORDER, n1, 2, FFT_IN, D_BRANCH), F32),
        compiler_params=_cparams("parallel", "parallel"),
        name="fft_filt",
    )(*tb["w_fwd"], v, v)


def _fft_mid_kernel(wf_ref, wi_ref, twr_ref, twi_ref, v_ref, h_ref, d_ref):
    for kb in range(FFT_KB):
        vr, vi = _unpack_pair(v_ref[kb].reshape(FFT_IN, D_BRANCH))
        z = jnp.dot(wf_ref[...], jnp.concatenate([vr, vi], axis=0), preferred_element_type=F32)
        zr, zi = z[:FFT_IN], z[FFT_IN:]
        hr, hi = h_ref[kb, 0], h_ref[kb, 1]
        pr = (zr * hr - zi * hi).astype(BF16)
        pi = (zr * hi + zi * hr).astype(BF16)
        d = jnp.dot(wi_ref[...], jnp.concatenate([pr, pi], axis=0), preferred_element_type=F32)
        dr, di = d[:FFT_IN], d[FFT_IN:]
        tr, ti = twr_ref[kb], twi_ref[kb]
        er, ei = dr * tr + di * ti, di * tr - dr * ti
        d_ref[kb] = _pack_pair(er, ei).reshape(d_ref.shape[1:])


def _fft_mid(tb, v, hspec, order):
    p, n1 = v.shape[:2]
    vspec = pl.BlockSpec((None, FFT_KB, FFT_NT, SUBLANES, D_BRANCH), lambda k, q: (q, k, 0, 0, 0))
    wspec = _const_spec((2 * FFT_IN, 2 * FFT_IN))
    tspec = pl.BlockSpec((FFT_KB, FFT_IN, 1), lambda k, q: (k, 0, 0))
    return pl.pallas_call(
        _fft_mid_kernel,
        grid=(n1 // FFT_KB, p),
        in_specs=[wspec, wspec, tspec, tspec, vspec,
                  pl.BlockSpec((None, FFT_KB, 2, FFT_IN, D_BRANCH), lambda k, q: (order, k, 0, 0, 0))],
        out_specs=vspec,
        out_shape=jax.ShapeDtypeStruct(v.shape, jnp.uint32),
        compiler_params=_cparams("parallel", "arbitrary"),
        name="fft_mid",
    )(tb["w_fwd"][0], tb["w_inv"][0], *tb["tw_in"], v, hspec)


def _ifft_outer_kernel(mh_ref, d_ref, u_ref, x_ref, b_ref, o_ref):
    n1 = d_ref.shape[1]
    for pp in range(FFT_PP):
        er, ei = _unpack_pair(d_ref[pp])
        e = jnp.concatenate([er.reshape(n1 * SUBLANES, D_BRANCH), ei.reshape(n1 * SUBLANES, D_BRANCH)], axis=0)
        y = jnp.dot(mh_ref[...], e, preferred_element_type=F32).reshape((2,) + o_ref.shape[2:])
        o_ref[:, pp] = x_ref[:, pp] * (y + u_ref[:, pp] * b_ref[...])


def _ifft_outer(tb, d, u, ucol, x, xcol, bias):
    n1, half = tb["n1"], tb["half"]
    p = d.shape[0]
    mshape = (2 * half * SUBLANES, 2 * n1 * SUBLANES)
    io = lambda col: pl.BlockSpec((2, FFT_PP, half, None, SUBLANES, D_BRANCH), lambda t, q: (0, q, 0, t, 0, col))
    return pl.pallas_call(
        _ifft_outer_kernel,
        grid=(FFT_NT, p // FFT_PP),
        in_specs=[_const_spec(mshape),
                  pl.BlockSpec((FFT_PP, n1, None, SUBLANES, D_BRANCH), lambda t, q: (q, 0, t, 0, 0)),
                  io(ucol), io(xcol), _const_spec((1, D_BRANCH))],
        out_specs=io(0),
        out_shape=jax.ShapeDtypeStruct((2, p, half, FFT_NT, SUBLANES, D_BRANCH), F32),
        compiler_params=_cparams("arbitrary", "arbitrary"),
        name="ifft_outer",
    )(tb["m_inv"][0], d, u, x, bias.reshape(1, D_BRANCH))


def _hyfilt_kernel(z_ref, w1_ref, b1_ref, fr_ref, w2_ref, b2_ref, w3_ref, dec_ref, o_ref, h_ref):
    seq = z_ref.shape[0]
    nblk = seq // HY_TB
    fr = fr_ref[...]

    @pl.when(pl.program_id(0) == 0)
    def _():
        def hidden(i, carry):
            r0 = pl.multiple_of(i * HY_TB, HY_TB)
            zb = z_ref[pl.ds(r0, HY_TB), :]
            h = jnp.sin(fr * (jnp.dot(zb, w1_ref[...], precision=HI, preferred_element_type=F32) + b1_ref[...]))
            h = jnp.sin(fr * (jnp.dot(h, w2_ref[...], precision=HI, preferred_element_type=F32) + b2_ref[...]))
            h_ref[pl.ds(r0, HY_TB), :] = h
            return carry

        lax.fori_loop(0, nblk, hidden, 0)

    def body(i, ss):
        r0 = pl.multiple_of(i * HY_TB, HY_TB)
        hf = jnp.dot(h_ref[pl.ds(r0, HY_TB), :], w3_ref[...], precision=HI, preferred_element_type=F32)
        hf = hf * jnp.exp(-z_ref[pl.ds(r0, HY_TB), 0:1] * dec_ref[...])
        o_ref[pl.ds(r0, HY_TB), :] = hf
        return ss + jnp.sum(hf * hf, axis=0, keepdims=True)

    ss = lax.fori_loop(0, nblk, body, jnp.zeros((1, D_BRANCH), F32))
    scale = lax.rsqrt(ss + EPS)

    def norm(i, carry):
        r0 = pl.multiple_of(i * HY_TB, HY_TB)
        o_ref[pl.ds(r0, HY_TB), :] = o_ref[pl.ds(r0, HY_TB), :] * scale
        return carry

    lax.fori_loop(0, nblk, norm, 0)


def _hyfilt(seq, w1, b1, freq, w2, b2, w3):
    t = jnp.linspace(0.0, 1.0, seq, dtype=F32)[:, None]
    bands = (C_EMB - 1) // 2
    w = 2.0 * math.pi * jnp.arange(seq, dtype=F32)[:, None] / seq
    fr = jnp.linspace(1e-4, bands - 1, bands, dtype=F32)[None]
    z = jnp.concatenate([t, jnp.cos(fr * w), -jnp.sin(fr * w)], axis=-1)
    z = jnp.pad(z, ((0, 0), (0, LANES - C_EMB)))
    padm = lambda a, r, c: jnp.pad(a.astype(F32), ((0, r - a.shape[0]), (0, c - a.shape[1])))
    row = lambda a: padm(a.reshape(1, -1), 1, LANES)
    dec = jnp.abs(jnp.linspace(C_MIN_DECAY, C_MAX_DECAY, D_BRANCH, dtype=F32)).reshape(1, D_BRANCH)
    nset = C_ORDER * 2
    return pl.pallas_call(
        _hyfilt_kernel,
        grid=(nset,),
        in_specs=[_const_spec((seq, LANES)), _const_spec((LANES, LANES)), _const_spec((1, LANES)),
                  _const_spec((1, LANES)), _const_spec((LANES, LANES)), _const_spec((1, LANES)),
                  pl.BlockSpec((LANES, D_BRANCH), lambda j: (0, j)), _const_spec((1, D_BRANCH))],
        out_specs=pl.BlockSpec((None, seq, D_BRANCH), lambda j: (j, 0, 0)),
        out_shape=jax.ShapeDtypeStruct((nset, seq, D_BRANCH), F32),
        scratch_shapes=[pltpu.VMEM((seq, LANES), F32)],
        compiler_params=_cparams("arbitrary"),
        name="hyfilt",
    )(z, padm(w1, LANES, LANES), row(b1), row(freq), padm(w2, LANES, LANES), row(b2),
      padm(w3, LANES, nset * D_BRANCH), dec)


def _hyena(uc3, w1, b1, freq, w2, b2, w3, bias):
    bsz, seq, width = uc3.shape
    tb = _fft_tables(seq)
    half = tb["half"]
    npair = bsz // 2
    uc6 = uc3.reshape(2, npair, half, FFT_NT, SUBLANES, width)
    filt = _hyfilt(seq, w1, b1, freq, w2, b2, w3).reshape(C_ORDER * 2, half, FFT_NT, SUBLANES, D_BRANCH)
    hspec = _fft_filt(tb, _fft_outer(tb, filt, False, 3, False))
    src = uc6
    for order in range(C_ORDER):
        d = _fft_mid(tb, _fft_outer(tb, src, True, 1, True), hspec, order)
        src = _ifft_outer(tb, d, src, 0, uc6, order + 1, bias[order])
    return src.reshape(bsz, seq, D_BRANCH)


AT_HALF = 64
AT_TQ = 2048
AT_SUB = 128


def _t5_bucket(rel):
    half = N_BUCKETS // 2
    max_exact = half // 2
    n = np.abs(rel)
    large = max_exact + (np.log(np.maximum(n, 1) / max_exact) / math.log(MAX_DISTANCE / max_exact)
                         * (half - max_exact)).astype(np.int64)
    large = np.minimum(large, half - 1)
    return (rel > 0).astype(np.int64) * half + np.where(n < max_exact, n, large)


def _attn_geometry(n):
    tq = min(AT_TQ, n)
    sub = min(AT_SUB, tq)
    if sub + 2 * AT_HALF >= n:
        sub = tq
    win = min(sub + 2 * AT_HALF, n)
    return tq, sub, win, n // tq, n // sub


def _attn_bias_tables(rel_bias, g, dil, n):
    _, sub, win, _, nsb = _attn_geometry(n)
    hs = slice(g * D_HEADS_PER_GROUP, (g + 1) * D_HEADS_PER_GROUP)
    offsets = np.arange(-AT_HALF, AT_HALF + 1) * dil
    onehot = np.zeros((2 * AT_HALF + 1, N_BUCKETS), np.float32)
    onehot[np.arange(2 * AT_HALF + 1), _t5_bucket(offsets)] = 1.0
    band = jnp.dot(rel_bias.astype(F32)[:, hs].T, jnp.asarray(onehot).T, precision=HI)
    nband = 2 * AT_HALF + 1
    lv = sub + win - 1
    tables = []
    for i in sorted({0, min(1, nsb - 1), nsb - 1}):
        ws = int(np.clip(i * sub - AT_HALF, 0, n - win))
        lo = (sub - 1) - (ws - i * sub) - AT_HALF
        v = jnp.pad(band, ((0, 0), (lo, lv - lo - nband)), constant_values=NEG_BIG)
        flat = jnp.tile(v, (1, sub + 1))[:, sub - 1:sub - 1 + sub * (lv - 1)]
        tables.append(flat.reshape(D_HEADS_PER_GROUP, sub, lv - 1)[:, :, :win])
    return jnp.stack(tables)


def _attn_kernel(q_ref, k_ref, v_ref, bias_ref, o_ref, l_ref, *, n, dil):
    tq, sub, win, _, nsb = _attn_geometry(n)
    ncase = bias_ref.shape[0]
    width = q_ref.shape[-1]
    nh = width // D_HEAD_DIM
    hp = pl.program_id(1)
    lane_head = lax.broadcasted_iota(jnp.int32, (sub, width), 1) // D_HEAD_DIM
    hmask = [lane_head == hh for hh in range(nh)]
    whole = win == n
    for r in range(dil):
        if whole:
            kw = k_ref[pl.ds(r, win, stride=dil), :].astype(BF16)
            vw = v_ref[pl.ds(r, win, stride=dil), :].astype(BF16)
        for j in range(tq // sub):
            sidx = pl.program_id(2) * (tq // sub) + j
            case = jnp.minimum(jnp.where(sidx == nsb - 1, ncase - 1, jnp.minimum(sidx, 1)), ncase - 1)
            bias = bias_ref[case, pl.ds(hp * nh, nh)].reshape(nh * sub, win)
            if not whole:
                ws = pl.multiple_of(jnp.clip(sidx * sub - AT_HALF, 0, n - win), AT_HALF)
                kw = k_ref[pl.ds(ws * dil + r, win, stride=dil), :].astype(BF16)
                vw = v_ref[pl.ds(ws * dil + r, win, stride=dil), :].astype(BF16)
            q = q_ref[pl.ds(j * sub * dil + r, sub, stride=dil), :] * (D_HEAD_DIM ** -0.5)
            qs = jnp.concatenate([jnp.where(hmask[hh], q, 0.0) for hh in range(nh)], axis=0).astype(BF16)
            s = lax.dot_general(qs, kw, (((1,), (1,)), ((), ())), preferred_element_type=F32) + bias
            m = jnp.max(s, axis=-1, keepdims=True)
            p = jnp.exp(s - m)
            l = jnp.sum(p, axis=-1, keepdims=True)
            o_all = jnp.dot(p.astype(BF16), vw, preferred_element_type=F32) / l
            lse = m + jnp.log(l)
            o_acc = jnp.zeros((sub, width), F32)
            l_acc = jnp.zeros((sub, width), F32)
            for hh in range(nh):
                o_acc = jnp.where(hmask[hh], o_all[hh * sub:(hh + 1) * sub], o_acc)
                l_acc = jnp.where(hmask[hh], lse[hh * sub:(hh + 1) * sub], l_acc)
            o_ref[pl.ds(j * sub * dil + r, sub, stride=dil), :] = o_acc
            l_ref[pl.ds(j * sub * dil + r, sub, stride=dil), :] = l_acc


def _banded_attention(pd3, g, dil, bias):
    bsz, seq, width = pd3.shape
    n = seq // dil
    tq, sub, win, nq, _ = _attn_geometry(n)
    rows = dil * tq
    bw = LANES if dil > 1 else D_BRANCH
    per = D_BRANCH // bw
    third = width // 3 // bw
    out_spec = pl.BlockSpec((None, rows, bw), lambda b, h, i: (b, i, h))
    shp = jax.ShapeDtypeStruct((bsz, seq, D_BRANCH), F32)
    return pl.pallas_call(
        functools.partial(_attn_kernel, n=n, dil=dil),
        grid=(bsz, per, nq),
        in_specs=[pl.BlockSpec((None, rows, bw), lambda b, h, i: (b, i, g * per + h)),
                  pl.BlockSpec((None, seq, bw), lambda b, h, i: (b, 0, third + g * per + h)),
                  pl.BlockSpec((None, seq, bw), lambda b, h, i: (b, 0, 2 * third + g * per + h)),
                  _const_spec(bias.shape)],
        out_specs=[out_spec, out_spec],
        out_shape=[shp, shp],
        compiler_params=_cparams("parallel", "parallel", "arbitrary"),
        name=f"attn_d{dil}",
    )(pd3, pd3, pd3, bias)


def _dilated_attention(pd3, rel_bias):
    bsz, seq, _ = pd3.shape
    outs, lses = [], []
    for g, (_, dil) in enumerate(D_GROUPS):
        o, l = _banded_attention(pd3, g, dil, _attn_bias_tables(rel_bias, g, dil, seq // dil))
        outs.append(o.reshape(bsz * seq, D_BRANCH))
        lses.append(l.reshape(bsz * seq, D_BRANCH))
    return outs, lses


def kernel(x, norm1_g, w_in, hgrn_lb_logits, hgrn_norm_g, lru_conv_w, lru_conv_b, lru_wa, lru_ba, lru_wx, lru_bx,
           lru_lambda, hy_conv_w, hy_conv_b, hy_w1, hy_b1, hy_freq, hy_w2, hy_b2, hy_w3, hy_bias, rel_bias,
           w_branch, w_gate, b_gate, w_out, norm2_g, w_ff1, w_ff3, w_ff2, final_g):
    bsz, seq, _ = x.shape
    n = bsz * seq
    lb_soft = jax.nn.softmax(hgrn_lb_logits.astype(F32), axis=0)
    lower_bounds = jnp.cumsum(lb_soft, axis=0) - lb_soft[0]
    x2 = x.reshape(n, D_MODEL)
    flat = lambda a: a.reshape(n, D_BRANCH)
    for l in range(DEPTH):
        pa, pb, uc, pd = _inproj(x2, norm1_g[l], w_in[l].astype(BF16), hy_conv_w[l], hy_conv_b[l], seq)
        oa_f, oa_b = _hgrn(pa.reshape(bsz, seq, IN_A), lower_bounds[l])
        yb = _rglru(pb.reshape(bsz, seq, IN_B), lru_conv_w[l], lru_conv_b[l], lru_wa[l], lru_ba[l],
                    lru_wx[l], lru_bx[l], lru_lambda[l])
        yc = _hyena(uc.reshape(bsz, seq, IN_C), hy_w1[l], hy_b1[l], hy_freq[l],
                    hy_w2[l], hy_b2[l], hy_w3[l], hy_bias[l])
        od, ld = _dilated_attention(pd.reshape(bsz, seq, IN_D), rel_bias)
        x2 = _merge(x2, norm1_g[l], flat(oa_f), flat(oa_b), pa, hgrn_norm_g[l], flat(yb), flat(yc), od, ld,
                    w_gate[l].reshape(D_MODEL, N_BRANCH * D_MODEL).astype(BF16),
                    b_gate[l].reshape(1, N_BRANCH * D_MODEL), w_branch[l].astype(BF16), w_out[l].astype(BF16))
        x2 = _ffn(x2, norm2_g[l], *_ffn_weights(w_ff1[l], w_ff3[l], w_ff2[l]), final_g, l == DEPTH - 1)
    return x2.reshape(bsz, seq, D_MODEL)
```
